```python
import math
import jax, jax.numpy as jnp
from jax import lax
import numpy as np

D_MODEL = 2048
BATCH = 2
SEQ = 4096
DEPTH = 2
DEC_BATCH = 2
DEC_SEQ = 16384
PAST_LEN = 128

N_MIXERS = 2
GRID_W = 64
RMS_EPS = 1e-6
HY_WIDTH = D_MODEL
EMB_DIM = 33
FILT_H = 64
N_INNER = 2
DECAY_TARGET = 1e-2
FAST_DECAY_PCT = 0.3
SLOW_DECAY_PCT = 1.5
MIN_DECAY = math.log(DECAY_TARGET) / FAST_DECAY_PCT
MAX_DECAY = math.log(DECAY_TARGET) / SLOW_DECAY_PCT
HEAD_DIM = 128
N_HEADS = D_MODEL // HEAD_DIM
N_KV_HEADS = 4
GROUP = N_HEADS // N_KV_HEADS
ROPE_AXIS_DIM = HEAD_DIM // 2
ROPE_THETA = 10000.0
Q_BLOCK = 128
N_GROUPS = 4
EXPERTS_PER_GROUP = 8
N_EXPERTS = N_GROUPS * EXPERTS_PER_GROUP
TOP_K = 2
D_EXPERT = D_MODEL // 4

kernel_name = "hyena_gqa_axial_hier_moe_encoder"


def rmsnorm(x, g):
    xf = x.astype(jnp.float32)
    y = xf * lax.rsqrt(jnp.mean(xf * xf, axis=-1, keepdims=True) + RMS_EPS)
    return (y * g.astype(jnp.float32)).astype(x.dtype)


def hyena_filter(L, f_w1, f_b1, f_win, f_bin, f_freq, f_w3):
    f32 = jnp.float32
    t = jnp.linspace(0.0, 1.0, L, dtype=f32)[:, None]
    bands = (EMB_DIM - 1) // 2
    w = 2.0 * math.pi * jnp.arange(L, dtype=f32)[:, None] / L
    fb = jnp.linspace(1e-4, bands - 1, bands, dtype=f32)[None, :]
    z = jnp.concatenate([t, jnp.cos(fb * w), -jnp.sin(fb * w)], axis=-1)
    freq = f_freq.astype(f32)
    h = jnp.sin(freq * (z @ f_w1.astype(f32) + f_b1.astype(f32)))
    for j in range(N_INNER):
        h = jnp.sin(freq * (h @ f_win[j].astype(f32) + f_bin[j].astype(f32)))
    h = h @ f_w3.astype(f32)
    deltas = jnp.abs(jnp.linspace(MIN_DECAY, MAX_DECAY, HY_WIDTH, dtype=f32))
    decay = jnp.exp(-t * deltas[None, :])
    h_fwd = h[:, :HY_WIDTH] * decay
    h_bwd = h[:, HY_WIDTH:] * decay
    k = jnp.concatenate([h_fwd, jnp.zeros((1, HY_WIDTH), f32), h_bwd[1:][::-1]], axis=0)
    return k / jnp.sum(jnp.abs(k), axis=0, keepdims=True)


def hyena_mixer(x, w_in, conv_w, conv_b, f_w1, f_b1, f_win, f_bin, f_freq, f_w3, skip_b, w_out):
    B, L, _ = x.shape
    u = x @ w_in
    up = jnp.pad(u, ((0, 0), (1, 1), (0, 0)))
    uc = up[:, 0:L] * conv_w[0] + up[:, 1:L + 1] * conv_w[1] + up[:, 2:L + 2] * conv_w[2] + conv_b
    x0, x1, v = jnp.split(uc, 3, axis=-1)
    k = hyena_filter(L, f_w1, f_b1, f_win, f_bin, f_freq, f_w3)
    zv = (v * x1).astype(jnp.float32)
    Z = jnp.fft.rfft(zv, n=2 * L, axis=1)
    K = jnp.fft.rfft(k, n=2 * L, axis=0)
    y = jnp.fft.irfft(Z * K[None], n=2 * L, axis=1)[:, :L] + zv * skip_b.astype(jnp.float32)
    y = y.astype(x.dtype) * x0
    return y @ w_out


def axial_rope(L):
    rows = L // GRID_W
    r, c = jnp.meshgrid(jnp.arange(rows, dtype=jnp.float32), jnp.arange(GRID_W, dtype=jnp.float32), indexing="ij")
    r = r.reshape(L)
    c = c.reshape(L)
    inv = 1.0 / (ROPE_THETA ** (jnp.arange(0, ROPE_AXIS_DIM, 2, dtype=jnp.float32) / ROPE_AXIS_DIM))
    ang = jnp.concatenate([r[:, None] * inv[None], c[:, None] * inv[None]], axis=-1)
    return jnp.cos(ang), jnp.sin(ang)


def apply_rope(x, cos, sin):
    extra = x.ndim - 3
    cos = cos.reshape((cos.shape[0],) + (1,) * extra + (cos.shape[1],))
    sin = sin.reshape((sin.shape[0],) + (1,) * extra + (sin.shape[1],))
    xf = x.astype(jnp.float32).reshape(x.shape[:-1] + (HEAD_DIM // 2, 2))
    a, b = xf[..., 0], xf[..., 1]
    out = jnp.stack([a * cos - b * sin, a * sin + b * cos], axis=-1).reshape(x.shape)
    return out.astype(x.dtype)


def attention_mixer(x, w_qkv, q_gain, k_gain, w_o):
    B, L, _ = x.shape
    qkv = x @ w_qkv
    nq = N_HEADS * HEAD_DIM
    nk = N_KV_HEADS * HEAD_DIM
    q = qkv[..., :nq].reshape(B, L, N_KV_HEADS, GROUP, HEAD_DIM)
    k = qkv[..., nq:nq + nk].reshape(B, L, N_KV_HEADS, HEAD_DIM)
    v = qkv[..., nq + nk:].reshape(B, L, N_KV_HEADS, HEAD_DIM)
    q = rmsnorm(q, q_gain)
    k = rmsnorm(k, k_gain)
    cos, sin = axial_rope(L)
    q = apply_rope(q, cos, sin)
    k = apply_rope(k, cos, sin)
    scale = HEAD_DIM ** -0.5
    nb = L // Q_BLOCK
    qb = jnp.moveaxis(q.reshape(B, nb, Q_BLOCK, N_KV_HEADS, GROUP, HEAD_DIM), 1, 0)

    def block(qi):
        s = jnp.einsum("bqkgd,bskd->bkgqs", qi, k).astype(jnp.float32) * scale
        p = jax.nn.softmax(s, axis=-1)
        return jnp.einsum("bkgqs,bskd->bqkgd", p.astype(v.dtype), v)

    o = lax.map(block, qb)
    o = jnp.moveaxis(o, 0, 1).reshape(B, L, nq)
    return o @ w_o


def hier_moe(x, router_group, router_expert, w_gate, w_up, w_down):
    B, L, D = x.shape
    h = x.reshape(B * L, D)
    hf = h.astype(jnp.float32)
    gl = hf @ router_group.astype(jnp.float32)
    gp = jax.nn.softmax(gl, axis=-1)
    g_idx = jnp.argmax(gl, axis=-1)
    p_g = jnp.take_along_axis(gp, g_idx[:, None], axis=-1)
    el = jnp.einsum("td,gde->tge", hf, router_expert.astype(jnp.float32))
    el_sel = jnp.take_along_axis(el, g_idx[:, None, None], axis=1)[:, 0]
    top_v, top_i = lax.top_k(el_sel, TOP_K)
    wts = jax.nn.softmax(top_v, axis=-1) * p_g
    flat = g_idx[:, None] * EXPERTS_PER_GROUP + top_i
    combine = jnp.sum(jax.nn.one_hot(flat, N_EXPERTS, dtype=jnp.float32) * wts[..., None], axis=1)
    combine = combine.astype(h.dtype)
    y = jnp.zeros_like(h)
    for e in range(N_EXPERTS):
        a = jax.nn.silu(h @ w_gate[e]) * (h @ w_up[e])
        y = y + combine[:, e:e + 1] * (a @ w_down[e])
    return y.reshape(B, L, D)


def trunk(x, norm_mix, norm_ffn, norm_final,
          hy_w_in, hy_conv_w, hy_conv_b, hy_f_w1, hy_f_b1, hy_f_win, hy_f_bin, hy_f_freq, hy_f_w3, hy_skip_b, hy_w_out,
          at_w_qkv, at_q_gain, at_k_gain, at_w_o,
          moe_router_group, moe_router_expert, moe_w_gate, moe_w_up, moe_w_down):
    for i in range(DEPTH):
        h = rmsnorm(x, norm_mix[i])
        j = i // N_MIXERS
        if i % N_MIXERS == 0:
            x = x + hyena_mixer(h, hy_w_in[j], hy_conv_w[j], hy_conv_b[j], hy_f_w1[j], hy_f_b1[j], hy_f_win[j],
                                hy_f_bin[j], hy_f_freq[j], hy_f_w3[j], hy_skip_b[j], hy_w_out[j])
        else:
            x = x + attention_mixer(h, at_w_qkv[j], at_q_gain[j], at_k_gain[j], at_w_o[j])
        h = rmsnorm(x, norm_ffn[i])
        x = x + hier_moe(h, moe_router_group[i], moe_router_expert[i], moe_w_gate[i], moe_w_up[i], moe_w_down[i])
    return rmsnorm(x, norm_final)


def setup_inputs(seed: int = 0) -> dict:
    key = jax.random.key(seed)
    ks = jax.random.split(key, 32)
    D = D_MODEL
    n_hy = (DEPTH + N_MIXERS - 1) // N_MIXERS
    n_at = DEPTH // N_MIXERS
    qkv_w = (N_HEADS + 2 * N_KV_HEADS) * HEAD_DIM

    def nrm(k, shape, fan_in):
        return jax.random.normal(k, shape, jnp.float32) * (fan_in ** -0.5)

    def gain(k, shape):
        return 1.0 + 0.05 * jax.random.normal(k, shape, jnp.float32)

    def small(k, shape):
        return 0.02 * jax.random.normal(k, shape, jnp.float32)

    return {
        "x_prompt": jax.random.normal(ks[0], (BATCH, SEQ, D), jnp.float32),
        "x_sample": jax.random.normal(ks[1], (DEC_BATCH, DEC_SEQ, D), jnp.float32),
        "norm_mix": gain(ks[2], (DEPTH, D)),
        "norm_ffn": gain(ks[3], (DEPTH, D)),
        "norm_final": gain(ks[4], (D,)),
        "hy_w_in": nrm(ks[5], (n_hy, D, 3 * HY_WIDTH), D),
        "hy_conv_w": nrm(ks[6], (n_hy, 3, 3 * HY_WIDTH), 3),
        "hy_conv_b": small(ks[7], (n_hy, 3 * HY_WIDTH)),
        "hy_f_w1": nrm(ks[8], (n_hy, EMB_DIM, FILT_H), EMB_DIM),
        "hy_f_b1": small(ks[9], (n_hy, FILT_H)),
        "hy_f_win": nrm(ks[10], (n_hy, N_INNER, FILT_H, FILT_H), FILT_H),
        "hy_f_bin": small(ks[11], (n_hy, N_INNER, FILT_H)),
        "hy_f_freq": gain(ks[12], (n_hy, FILT_H)),
        "hy_f_w3": nrm(ks[13], (n_hy, FILT_H, 2 * HY_WIDTH), FILT_H),
        "hy_skip_b": gain(ks[14], (n_hy, HY_WIDTH)),
        "hy_w_out": nrm(ks[15], (n_hy, HY_WIDTH, D), HY_WIDTH),
        "at_w_qkv": nrm(ks[16], (n_at, D, qkv_w), D),
        "at_q_gain": gain(ks[17], (n_at, HEAD_DIM)),
        "at_k_gain": gain(ks[18], (n_at, HEAD_DIM)),
        "at_w_o": nrm(ks[19], (n_at, N_HEADS * HEAD_DIM, D), N_HEADS * HEAD_DIM),
        "moe_router_group": nrm(ks[20], (DEPTH, D, N_GROUPS), D),
        "moe_router_expert": nrm(ks[21], (DEPTH, N_GROUPS, D, EXPERTS_PER_GROUP), D),
        "moe_w_gate": nrm(ks[22], (DEPTH, N_EXPERTS, D, D_EXPERT), D),
        "moe_w_up": nrm(ks[23], (DEPTH, N_EXPERTS, D, D_EXPERT), D),
        "moe_w_down": nrm(ks[24], (DEPTH, N_EXPERTS, D_EXPERT, D), D_EXPERT),
    }


def reference(x_prompt, x_sample, norm_mix, norm_ffn, norm_final,
              hy_w_in, hy_conv_w, hy_conv_b, hy_f_w1, hy_f_b1, hy_f_win, hy_f_bin, hy_f_freq, hy_f_w3, hy_skip_b, hy_w_out,
              at_w_qkv, at_q_gain, at_k_gain, at_w_o,
              moe_router_group, moe_router_expert, moe_w_gate, moe_w_up, moe_w_down):
    y_prompt = trunk(x_prompt, norm_mix, norm_ffn, norm_final,
                     hy_w_in, hy_conv_w, hy_conv_b, hy_f_w1, hy_f_b1, hy_f_win, hy_f_bin, hy_f_freq, hy_f_w3, hy_skip_b, hy_w_out,
                     at_w_qkv, at_q_gain, at_k_gain, at_w_o,
                     moe_router_group, moe_router_expert, moe_w_gate, moe_w_up, moe_w_down)
    y_sample = trunk(x_sample, norm_mix, norm_ffn, norm_final,
                     hy_w_in, hy_conv_w, hy_conv_b, hy_f_w1, hy_f_b1, hy_f_win, hy_f_bin, hy_f_freq, hy_f_w3, hy_skip_b, hy_w_out,
                     at_w_qkv, at_q_gain, at_k_gain, at_w_o,
                     moe_router_group, moe_router_expert, moe_w_gate, moe_w_up, moe_w_down)
    return (y_prompt, y_sample)
```

```python
import functools
import math

import jax
import jax.numpy as jnp
import numpy as np
from jax import lax
from jax.experimental import pallas as pl
from jax.experimental.pallas import tpu as pltpu

RMS_EPS = 1e-6
GRID_W = 64
EMB_DIM = 33
DECAY_TARGET = 1e-2
FAST_DECAY_PCT = 0.3
SLOW_DECAY_PCT = 1.5
MIN_DECAY = math.log(DECAY_TARGET) / FAST_DECAY_PCT
MAX_DECAY = math.log(DECAY_TARGET) / SLOW_DECAY_PCT
HEAD_DIM = 128
N_KV_HEADS = 4
ROPE_THETA = 10000.0
N_GROUPS = 4
EXPERTS_PER_GROUP = 8
N_MIXERS = 2

LANES = 128
BF16_SUBLANES = 16
VMEM_LIMIT_BYTES = 52 * 1024 * 1024
DFT_N2 = 128

BF16 = jnp.bfloat16
F32 = jnp.float32


def _params(sem):
    return pltpu.CompilerParams(dimension_semantics=sem, vmem_limit_bytes=VMEM_LIMIT_BYTES)


def _dot(a, b):
    return jnp.dot(a, b, preferred_element_type=F32)


def _norm_rows(x, g):
    ms = jnp.mean(x * x, axis=-1, keepdims=True)
    return x * lax.rsqrt(ms + RMS_EPS) * g


def _tile(n, pref):
    return pref if n % pref == 0 else n


def _matmul_res_kernel(a_ref, w_ref, r_ref, o_ref):
    o_ref[...] = r_ref[...] + _dot(a_ref[...], w_ref[...])


def matmul_residual(a, w, res, *, tm=1024, tn=512):
    T, K = a.shape
    N = w.shape[1]
    tm = _tile(T, tm)
    tn = _tile(N, tn)
    return pl.pallas_call(
        _matmul_res_kernel,
        grid=(T // tm, N // tn),
        in_specs=[pl.BlockSpec((tm, K), lambda i, j: (i, 0)),
                  pl.BlockSpec((K, tn), lambda i, j: (0, j)),
                  pl.BlockSpec((tm, tn), lambda i, j: (i, j))],
        out_specs=pl.BlockSpec((tm, tn), lambda i, j: (i, j)),
        out_shape=jax.ShapeDtypeStruct((T, N), F32),
        compiler_params=_params(("parallel", "arbitrary")),
        name="matmul_residual",
    )(a, w, res)


HALO = BF16_SUBLANES


def _hy_inproj_kernel(x_ref, xp_ref, xn_ref, g_ref, w0_ref, w1_ref, w2_ref,
                      cw0_ref, cw1_ref, cw2_ref, cb0_ref, cb1_ref, cb2_ref,
                      x0_out, zv_out, hn_ref, *, tm, tiles_per_seq):
    i = pl.program_id(0)
    j = pl.program_id(1)

    @pl.when(j == 0)
    def _():
        g = g_ref[...]
        hn_ref[HALO:HALO + tm, :] = _norm_rows(x_ref[...], g).astype(BF16)
        first = (i % tiles_per_seq) == 0
        last = (i % tiles_per_seq) == tiles_per_seq - 1
        hp = _norm_rows(xp_ref[...], g)
        hn_ref[0:HALO, :] = jnp.where(first, 0.0, hp).astype(BF16)
        hx = _norm_rows(xn_ref[...], g)
        hn_ref[HALO + tm:, :] = jnp.where(last, 0.0, hx).astype(BF16)

    h = hn_ref[...]
    rows = tm + 2 * HALO

    def conv(w_ref, cw_ref, cb_ref):
        u = _dot(h, w_ref[...])
        um = pltpu.roll(u, 1, 0)[HALO:HALO + tm]
        up = pltpu.roll(u, rows - 1, 0)[HALO:HALO + tm]
        uc = u[HALO:HALO + tm]
        cw = cw_ref[...]
        return um * cw[0:1] + uc * cw[1:2] + up * cw[2:3] + cb_ref[...]

    x0_out[...] = conv(w0_ref, cw0_ref, cb0_ref)
    x1 = conv(w1_ref, cw1_ref, cb1_ref)
    v = conv(w2_ref, cw2_ref, cb2_ref)
    zv_out[...] = v * x1


def hyena_inproj(x, g, w_in, conv_w, conv_b, seq_len, *, tm=512, tn=512):
    T, D = x.shape
    W = w_in.shape[1] // 3
    tm = _tile(seq_len, tm)
    tn = _tile(W, tn)
    nj = W // tn
    hb = tm // HALO
    n_halo_blocks = T // HALO
    kern = functools.partial(_hy_inproj_kernel, tm=tm, tiles_per_seq=seq_len // tm)

    def wspec(c):
        return pl.BlockSpec((D, tn), lambda i, j, c=c: (0, j + c * nj))

    def cwspec(c):
        return pl.BlockSpec((3, tn), lambda i, j, c=c: (0, j + c * nj))

    def cbspec(c):
        return pl.BlockSpec((1, tn), lambda i, j, c=c: (0, j + c * nj))

    return pl.pallas_call(
        kern,
        grid=(T // tm, nj),
        in_specs=[pl.BlockSpec((tm, D), lambda i, j: (i, 0)),
                  pl.BlockSpec((HALO, D), lambda i, j: (jnp.maximum(i * hb - 1, 0), 0)),
                  pl.BlockSpec((HALO, D), lambda i, j: (jnp.minimum((i + 1) * hb, n_halo_blocks - 1), 0)),
                  pl.BlockSpec((1, D), lambda i, j: (0, 0)),
                  wspec(0), wspec(1), wspec(2),
                  cwspec(0), cwspec(1), cwspec(2),
                  cbspec(0), cbspec(1), cbspec(2)],
        out_specs=[pl.BlockSpec((tm, tn), lambda i, j: (i, j)),
                   pl.BlockSpec((tm, tn), lambda i, j: (i, j))],
        out_shape=[jax.ShapeDtypeStruct((T, W), F32), jax.ShapeDtypeStruct((T, W), F32)],
        scratch_shapes=[pltpu.VMEM((tm + 2 * HALO, D), BF16)],
        compiler_params=_params(("parallel", "arbitrary")),
        name="hyena_inproj",
    )(x, x, x, g, w_in, w_in, w_in, conv_w, conv_w, conv_w, conv_b, conv_b, conv_b)


def _filter_mlp_kernel(z_ref, w1_ref, b1_ref, win_ref, bin_ref, fr_ref, h_ref):
    hp = lax.Precision.HIGHEST
    fr = fr_ref[...]
    h = jnp.sin(fr * (jnp.dot(z_ref[...], w1_ref[...], precision=hp, preferred_element_type=F32) + b1_ref[...]))
    for l in range(win_ref.shape[0]):
        h = jnp.sin(fr * (jnp.dot(h, win_ref[l], precision=hp, preferred_element_type=F32) + bin_ref[l]))
    h_ref[...] = h


def _filter_expand_kernel(h_ref, t_ref, w3_ref, dl_ref, k_ref, s_ref, *, tl, n_fwd_tiles):
    j = pl.program_id(0)
    i = pl.program_id(1)
    taps = jnp.dot(h_ref[...], w3_ref[...], precision=lax.Precision.HIGHEST, preferred_element_type=F32)
    taps = taps * jnp.exp(-t_ref[...] * dl_ref[...])
    row = i * tl + lax.broadcasted_iota(jnp.int32, taps.shape, 0)
    taps = jnp.where(jnp.logical_and(j >= n_fwd_tiles, row == 0), 0.0, taps)
    k_ref[...] = taps.astype(k_ref.dtype)

    @pl.when(i == 0)
    def _():
        s_ref[...] = jnp.zeros_like(s_ref)

    s_ref[...] += jnp.sum(jnp.abs(taps), axis=0, keepdims=True)


def hyena_filter_taps(L, D, f_w1, f_b1, f_win, f_bin, f_freq, f_w3):
    H = f_w1.shape[1]
    t = jnp.linspace(0.0, 1.0, L, dtype=F32)[:, None]
    bands = (EMB_DIM - 1) // 2
    w = 2.0 * math.pi * jnp.arange(L, dtype=F32)[:, None] / L
    fb = jnp.linspace(1e-4, bands - 1, bands, dtype=F32)[None, :]
    z = jnp.concatenate([t, jnp.cos(fb * w), -jnp.sin(fb * w)], axis=-1)
    zp = jnp.pad(z, ((0, 0), (0, LANES - EMB_DIM)))
    w1p = jnp.pad(f_w1, ((0, LANES - EMB_DIM), (0, 0)))
    n_in = f_win.shape[0]
    tl = _tile(L, 512)
    h = pl.pallas_call(
        _filter_mlp_kernel,
        grid=(L // tl,),
        in_specs=[pl.BlockSpec((tl, LANES), lambda i: (i, 0)),
                  pl.BlockSpec((LANES, H), lambda i: (0, 0)),
                  pl.BlockSpec((1, H), lambda i: (0, 0)),
                  pl.BlockSpec((n_in, H, H), lambda i: (0, 0, 0)),
                  pl.BlockSpec((n_in, 1, H), lambda i: (0, 0, 0)),
                  pl.BlockSpec((1, H), lambda i: (0, 0))],
        out_specs=pl.BlockSpec((tl, H), lambda i: (i, 0)),
        out_shape=jax.ShapeDtypeStruct((L, H), F32),
        compiler_params=_params(("parallel",)),
        name="filter_mlp",
    )(zp, w1p, f_b1[None, :], f_win, f_bin[:, None, :], f_freq[None, :])

    deltas = jnp.abs(jnp.linspace(MIN_DECAY, MAX_DECAY, D, dtype=F32))[None, :]
    deltas2 = jnp.concatenate([deltas, deltas], axis=1)
    tn = _tile(D, 1024)
    n_fwd = D // tn
    kern = functools.partial(_filter_expand_kernel, tl=tl, n_fwd_tiles=n_fwd)
    taps, asum = pl.pallas_call(
        kern,
        grid=(2 * n_fwd, L // tl),
        in_specs=[pl.BlockSpec((tl, H), lambda j, i: (i, 0)),
                  pl.BlockSpec((tl, 1), lambda j, i: (i, 0)),
                  pl.BlockSpec((H, tn), lambda j, i: (0, j)),
                  pl.BlockSpec((1, tn), lambda j, i: (0, j))],
        out_specs=[pl.BlockSpec((tl, tn), lambda j, i: (i, j)),
                   pl.BlockSpec((1, tn), lambda j, i: (0, j))],
        out_shape=[jax.ShapeDtypeStruct((L, 2 * D), BF16), jax.ShapeDtypeStruct((1, 2 * D), F32)],
        compiler_params=_params(("parallel", "arbitrary")),
        name="filter_expand",
    )(h, t, f_w3, deltas2)
    return taps, asum


def dft_tables(N1, N2):
    N = N1 * N2
    half = N1 // 2
    k1 = jnp.arange(N1, dtype=jnp.int32)[None, :, None]
    n = (jnp.arange(half, dtype=jnp.int32)[None, None, :] * N2
         + jnp.arange(N2, dtype=jnp.int32)[:, None, None])
    ang = ((k1 * n) % N).astype(F32) * (2.0 * math.pi / N)
    c, s = jnp.cos(ang), jnp.sin(ang)
    fwd = jnp.concatenate([jnp.concatenate([c, s], axis=2),
                           jnp.concatenate([-s, c], axis=2)], axis=1)
    inv = jnp.swapaxes(fwd, 1, 2)
    fwd_real = jnp.concatenate([c, -s], axis=1)
    k2 = jnp.arange(N2, dtype=jnp.int32)
    ang2 = ((k2[:, None] * k2[None, :]) % N2).astype(F32) * (2.0 * math.pi / N2)
    c2, s2 = jnp.cos(ang2), jnp.sin(ang2)
    f2 = jnp.concatenate([jnp.concatenate([c2, s2], axis=1),
                          jnp.concatenate([-s2, c2], axis=1)], axis=0)
    g2 = f2.T
    return fwd.astype(BF16), inv.astype(BF16), fwd_real.astype(BF16), f2.astype(BF16), g2.astype(BF16)


def _fft_s1_kernel(z_ref, m_ref, o_ref):
    rhs = jnp.concatenate([z_ref[0], z_ref[1]], axis=0).astype(BF16)
    o_ref[...] = _dot(m_ref[...], rhs).astype(o_ref.dtype)


def _fft_s1_real_kernel(k_ref, m_ref, o_ref):
    o_ref[...] = _dot(m_ref[...], k_ref[...]).astype(o_ref.dtype)


def _filt_s2_kernel(af_ref, ab_ref, f2_ref, s_ref, k_ref):
    n2 = af_ref.shape[1]
    td = af_ref.shape[2]
    f2 = f2_ref[...]
    xf = _dot(f2, af_ref[...].reshape(2 * n2, td))
    xb = _dot(f2, ab_ref[...].reshape(2 * n2, td))
    s = s_ref[...]
    k_ref[0] = (xf[:n2] + xb[:n2]) * s
    k_ref[1] = (xf[n2:] - xb[n2:]) * s


def _fft_mid_kernel(a_ref, k_ref, f2_ref, g2_ref, c_ref):
    n2 = a_ref.shape[1]
    td = a_ref.shape[2]
    x = _dot(f2_ref[...], a_ref[...].reshape(2 * n2, td))
    xr, xi = x[:n2], x[n2:]
    kr, ki = k_ref[0], k_ref[1]
    y = jnp.concatenate([xr * kr - xi * ki, xr * ki + xi * kr], axis=0).astype(BF16)
    c = _dot(g2_ref[...], y)
    c_ref[...] = c.reshape(2, n2, td).astype(c_ref.dtype)


def _fft_out_kernel(c_ref, m_ref, zv_ref, x0_ref, sk_ref, y_ref):
    y = _dot(m_ref[...], c_ref[...])
    half = y.shape[0] // 2
    sk = sk_ref[...]
    y_ref[0] = ((y[:half] + zv_ref[0] * sk) * x0_ref[0]).astype(y_ref.dtype)
    y_ref[1] = ((y[half:] + zv_ref[1] * sk) * x0_ref[1]).astype(y_ref.dtype)


def hyena_long_conv(zv, x0, taps, asum, skip_b, L):
    D = zv.shape[1]
    N = 2 * L
    N2 = DFT_N2
    N1 = N // N2
    half = N1 // 2
    fwd, inv, fwd_real, f2, g2 = dft_tables(N1, N2)
    td = _tile(D, 2048)
    nd = D // td

    af = pl.pallas_call(
        _fft_s1_real_kernel,
        grid=(N2,),
        in_specs=[pl.BlockSpec((half, 2 * D), lambda j: (0, j)),
                  pl.BlockSpec((None, 2 * N1, half), lambda j: (j, 0, 0))],
        out_specs=pl.BlockSpec((2 * N1, 2 * D), lambda j: (0, j)),
        out_shape=jax.ShapeDtypeStruct((2 * N1, N2 * 2 * D), BF16),
        compiler_params=_params(("parallel",)),
        name="filter_dft_stage1",
    )(taps.reshape(half, N2 * 2 * D), fwd_real)
    scale = 1.0 / ((asum[:, :D] + asum[:, D:]) * N)
    af4 = af.reshape(2, N1, N2, 2 * D)
    spec = pl.pallas_call(
        _filt_s2_kernel,
        grid=(N1, nd),
        in_specs=[pl.BlockSpec((2, None, N2, td), lambda k, j: (0, k, 0, j)),
                  pl.BlockSpec((2, None, N2, td), lambda k, j: (0, k, 0, j + nd)),
                  pl.BlockSpec((2 * N2, 2 * N2), lambda k, j: (0, 0)),
                  pl.BlockSpec((1, td), lambda k, j: (0, j))],
        out_specs=pl.BlockSpec((2, None, N2, td), lambda k, j: (0, k, 0, j)),
        out_shape=jax.ShapeDtypeStruct((2, N1, N2, D), F32),
        compiler_params=_params(("parallel", "parallel")),
        name="filter_dft_stage2",
    )(af4, af4, f2, scale)

    a = pl.pallas_call(
        _fft_s1_kernel,
        grid=(N2,),
        in_specs=[pl.BlockSpec((2, half, D), lambda j: (0, 0, j)),
                  pl.BlockSpec((None, 2 * N1, N1), lambda j: (j, 0, 0))],
        out_specs=pl.BlockSpec((2 * N1, D), lambda j: (0, j)),
        out_shape=jax.ShapeDtypeStruct((2 * N1, N2 * D), BF16),
        compiler_params=_params(("parallel",)),
        name="conv_dft_stage1",
    )(zv.reshape(2, half, N2 * D), fwd)

    c = pl.pallas_call(
        _fft_mid_kernel,
        grid=(N1, nd),
        in_specs=[pl.BlockSpec((2, None, N2, td), lambda k, j: (0, k, 0, j)),
                  pl.BlockSpec((2, None, N2, td), lambda k, j: (0, k, 0, j)),
                  pl.BlockSpec((2 * N2, 2 * N2), lambda k, j: (0, 0)),
                  pl.BlockSpec((2 * N2, 2 * N2), lambda k, j: (0, 0))],
        out_specs=pl.BlockSpec((2, None, N2, td), lambda k, j: (0, k, 0, j)),
        out_shape=jax.ShapeDtypeStruct((2, N1, N2, D), BF16),
        compiler_params=_params(("parallel", "parallel")),
        name="conv_dft_stage2",
    )(a.reshape(2, N1, N2, D), spec, f2, g2)

    y = pl.pallas_call(
        _fft_out_kernel,
        grid=(N2,),
        in_specs=[pl.BlockSpec((2 * N1, D), lambda j: (0, j)),
                  pl.BlockSpec((None, N1, 2 * N1), lambda j: (j, 0, 0)),
                  pl.BlockSpec((2, half, D), lambda j: (0, 0, j)),
                  pl.BlockSpec((2, half, D), lambda j: (0, 0, j)),
                  pl.BlockSpec((1, D), lambda j: (0, 0))],
        out_specs=pl.BlockSpec((2, half, D), lambda j: (0, 0, j)),
        out_shape=jax.ShapeDtypeStruct((2, half, N2 * D), BF16),
        compiler_params=_params(("parallel",)),
        name="conv_dft_inverse",
    )(c.reshape(2 * N1, N2 * D), inv, zv.reshape(2, half, N2 * D), x0.reshape(2, half, N2 * D), skip_b)
    return y.reshape(2 * L, D)


def _qkv_kernel(x_ref, g_ref, w_ref, gain_ref, cos_ref, sin_ref, o_ref, hn_ref, *, n_rope_tiles):
    j = pl.program_id(1)

    @pl.when(j == 0)
    def _():
        hn_ref[...] = _norm_rows(x_ref[...], g_ref[...]).astype(BF16)

    u = _dot(hn_ref[...], w_ref[...])
    tn = u.shape[1]

    @pl.when(j < n_rope_tiles)
    def _():
        gain = gain_ref[...]
        cos = cos_ref[...]
        sin = sin_ref[...]
        even = (lax.broadcasted_iota(jnp.int32, cos.shape, 1) % 2) == 0
        for hh in range(tn // HEAD_DIM):
            xh = u[:, hh * HEAD_DIM:(hh + 1) * HEAD_DIM]
            y = _norm_rows(xh, gain)
            sw = jnp.where(even, pltpu.roll(y, HEAD_DIM - 1, 1), pltpu.roll(y, 1, 1))
            o_ref[:, hh * HEAD_DIM:(hh + 1) * HEAD_DIM] = (y * cos + sw * sin).astype(o_ref.dtype)

    @pl.when(j >= n_rope_tiles)
    def _():
        o_ref[...] = u.astype(o_ref.dtype)


def rope_tables(L):
    rows = L // GRID_W
    r, c = jnp.meshgrid(jnp.arange(rows, dtype=F32), jnp.arange(GRID_W, dtype=F32), indexing="ij")
    r = r.reshape(L)
    c = c.reshape(L)
    axis_dim = HEAD_DIM // 2
    inv = 1.0 / (ROPE_THETA ** (jnp.arange(0, axis_dim, 2, dtype=F32) / axis_dim))
    ang = jnp.concatenate([r[:, None] * inv[None], c[:, None] * inv[None]], axis=-1)
    cos = jnp.repeat(jnp.cos(ang), 2, axis=1)
    sign = jnp.tile(jnp.array([-1.0, 1.0], F32), HEAD_DIM // 2)[None, :]
    sin = jnp.repeat(jnp.sin(ang), 2, axis=1) * sign
    return cos, sin


def qkv_project(x, g, w_qkv, q_gain, k_gain, seq_len, *, tm=512):
    T, D = x.shape
    n_out = w_qkv.shape[1]
    nk = N_KV_HEADS * HEAD_DIM
    nq = n_out - 2 * nk
    tn = nk
    tm = _tile(seq_len, tm)
    n_rope = (nq + nk) // tn
    scale = HEAD_DIM ** -0.5
    gains = jnp.concatenate([jnp.tile((q_gain * scale)[None, None, :], (nq // tn, 1, 1)),
                             k_gain[None, None, :],
                             jnp.ones((1, 1, HEAD_DIM), F32)], axis=0)
    cos, sin = rope_tables(seq_len)
    spt = seq_len // tm
    kern = functools.partial(_qkv_kernel, n_rope_tiles=n_rope)
    return pl.pallas_call(
        kern,
        grid=(T // tm, n_out // tn),
        in_specs=[pl.BlockSpec((tm, D), lambda i, j: (i, 0)),
                  pl.BlockSpec((1, D), lambda i, j: (0, 0)),
                  pl.BlockSpec((D, tn), lambda i, j: (0, j)),
                  pl.BlockSpec((None, 1, HEAD_DIM), lambda i, j: (j, 0, 0)),
                  pl.BlockSpec((tm, HEAD_DIM), lambda i, j: (i % spt, 0)),
                  pl.BlockSpec((tm, HEAD_DIM), lambda i, j: (i % spt, 0))],
        out_specs=pl.BlockSpec((tm, tn), lambda i, j: (i, j)),
        out_shape=jax.ShapeDtypeStruct((T, n_out), BF16),
        scratch_shapes=[pltpu.VMEM((tm, D), BF16)],
        compiler_params=_params(("parallel", "arbitrary")),
        name="qkv_project",
    )(x, g, w_qkv, gains, cos, sin)


def _flash_kernel(q_ref, k_ref, v_ref, o_ref, qs_ref, m_ref, l_ref, acc_ref, *, group, tq):
    ki = pl.program_id(3)

    @pl.when(ki == 0)
    def _():
        m_ref[...] = jnp.full_like(m_ref, -jnp.inf)
        l_ref[...] = jnp.zeros_like(l_ref)
        acc_ref[...] = jnp.zeros_like(acc_ref)
        for gq in range(group):
            qs_ref[gq * tq:(gq + 1) * tq, :] = q_ref[:, gq * HEAD_DIM:(gq + 1) * HEAD_DIM]

    k = k_ref[...]
    s = lax.dot_general(qs_ref[...], k, (((1,), (1,)), ((), ())), preferred_element_type=F32)
    tk = s.shape[1]
    m_prev = m_ref[...]
    m_new = jnp.maximum(m_prev, jnp.max(s, axis=-1, keepdims=True))
    alpha = jnp.exp(m_prev - m_new)
    p = jnp.exp(s - jnp.tile(m_new, (1, tk // LANES)))
    l_ref[...] = alpha * l_ref[...] + jnp.sum(p, axis=-1, keepdims=True)
    acc_ref[...] = acc_ref[...] * alpha + _dot(p.astype(BF16), v_ref[...])
    m_ref[...] = m_new

    @pl.when(ki == pl.num_programs(3) - 1)
    def _():
        o = acc_ref[...] / l_ref[...]
        for gq in range(group):
            o_ref[:, gq * HEAD_DIM:(gq + 1) * HEAD_DIM] = o[gq * tq:(gq + 1) * tq].astype(o_ref.dtype)


def flash_attention(qkv, B, L, n_heads, *, tq=256, tk=512):
    group = n_heads // N_KV_HEADS
    tq = _tile(L, tq)
    tk = _tile(L, tk)
    qw = group * HEAD_DIM
    k_col = n_heads
    v_col = n_heads + N_KV_HEADS
    kern = functools.partial(_flash_kernel, group=group, tq=tq)
    return pl.pallas_call(
        kern,
        grid=(B, N_KV_HEADS, L // tq, L // tk),
        in_specs=[pl.BlockSpec((None, tq, qw), lambda b, h, qi, ki: (b, qi, h)),
                  pl.BlockSpec((None, tk, HEAD_DIM), lambda b, h, qi, ki: (b, ki, k_col + h)),
                  pl.BlockSpec((None, tk, HEAD_DIM), lambda b, h, qi, ki: (b, ki, v_col + h))],
        out_specs=pl.BlockSpec((None, tq, qw), lambda b, h, qi, ki: (b, qi, h)),
        out_shape=jax.ShapeDtypeStruct((B, L, n_heads * HEAD_DIM), BF16),
        scratch_shapes=[pltpu.VMEM((group * tq, HEAD_DIM), BF16),
                        pltpu.VMEM((group * tq, LANES), F32),
                        pltpu.VMEM((group * tq, LANES), F32),
                        pltpu.VMEM((group * tq, HEAD_DIM), F32)],
        compiler_params=_params(("parallel", "parallel", "parallel", "arbitrary")),
        name="flash_attention",
    )(qkv, qkv, qkv)


def _pack_bf16_pairs(x):
    half = x.shape[1] // 2
    bits = lax.bitcast_convert_type(x.astype(BF16).astype(F32), jnp.uint32)
    return (bits[:, :half] >> 16) | (bits[:, half:] & jnp.uint32(0xFFFF0000))


def _unpack_bf16_pairs(p):
    lo = lax.bitcast_convert_type(p << 16, F32).astype(BF16)
    hi = lax.bitcast_convert_type(p & jnp.uint32(0xFFFF0000), F32).astype(BF16)
    return lo, hi


def _router_kernel(xa_ref, xb_ref, g_ref, rw_ref, h_ref, info_ref, *, n_a_tiles):
    i = pl.program_id(0)

    def route(x_ref):
        hf = _norm_rows(x_ref[...], g_ref[...])
        h_ref[...] = _pack_bf16_pairs(hf)
        logits = jnp.dot(hf, rw_ref[...], precision=lax.Precision.HIGHEST, preferred_element_type=F32)
        shape = logits.shape
        lane = lax.broadcasted_iota(jnp.int32, shape, 1)
        lanef = lane.astype(F32)
        neg = -jnp.inf
        big = float(LANES)
        is_g = lane < N_GROUPS
        gl = jnp.where(is_g, logits, neg)
        gmax = jnp.max(gl, axis=-1, keepdims=True)
        gidx = jnp.min(jnp.where(gl == gmax, lanef, big), axis=-1, keepdims=True)
        p_g = 1.0 / jnp.sum(jnp.where(is_g, jnp.exp(logits - gmax), 0.0), axis=-1, keepdims=True)
        lo = N_GROUPS + gidx * EXPERTS_PER_GROUP
        el = jnp.where(lanef >= lo, jnp.where(lanef < lo + EXPERTS_PER_GROUP, logits, neg), neg)
        v1 = jnp.max(el, axis=-1, keepdims=True)
        i1 = jnp.min(jnp.where(el == v1, lanef, big), axis=-1, keepdims=True)
        el2 = jnp.where(lanef == i1, neg, el)
        v2 = jnp.max(el2, axis=-1, keepdims=True)
        i2 = jnp.min(jnp.where(el2 == v2, lanef, big), axis=-1, keepdims=True)
        t = jnp.exp(v2 - v1)
        w1 = p_g / (1.0 + t)
        w2 = p_g * t / (1.0 + t)
        info_ref[...] = jnp.where(lane == 0, i1 - N_GROUPS,
                                  jnp.where(lane == 1, i2 - N_GROUPS,
                                            jnp.where(lane == 2, w1, jnp.where(lane == 3, w2, 0.0))))

    @pl.when(i < n_a_tiles)
    def _():
        route(xa_ref)

    @pl.when(i >= n_a_tiles)
    def _():
        route(xb_ref)


def moe_route(xa, xb, g, router_group, router_expert, *, tm=512):
    Ta, D = xa.shape
    Tb = xb.shape[0]
    tm = _tile(math.gcd(Ta, Tb), tm)
    na, nb = Ta // tm, Tb // tm
    n_e = N_GROUPS * EXPERTS_PER_GROUP
    rw = jnp.concatenate([router_group, jnp.transpose(router_expert, (1, 0, 2)).reshape(D, n_e)], axis=1)
    rw = jnp.pad(rw, ((0, 0), (0, LANES - rw.shape[1])))
    kern = functools.partial(_router_kernel, n_a_tiles=na)
    return pl.pallas_call(
        kern,
        grid=(na + nb,),
        in_specs=[pl.BlockSpec((tm, D), lambda i: (jnp.minimum(i, na - 1), 0)),
                  pl.BlockSpec((tm, D), lambda i: (jnp.maximum(i - na, 0), 0)),
                  pl.BlockSpec((1, D), lambda i: (0, 0)),
                  pl.BlockSpec((D, LANES), lambda i: (0, 0))],
        out_specs=[pl.BlockSpec((tm, D // 2), lambda i: (i, 0)),
                   pl.BlockSpec((tm, LANES), lambda i: (i, 0))],
        out_shape=[jax.ShapeDtypeStruct((Ta + Tb, D // 2), jnp.uint32),
                   jax.ShapeDtypeStruct((Ta + Tb, LANES), F32)],
        compiler_params=_params(("parallel",)),
        name="moe_route",
    )(xa, xb, g, rw)


def moe_plan(info, tm, n_experts):
    T = info.shape[0]
    e = info[:, :2].astype(jnp.int32).reshape(-1)
    n_rows = 2 * T
    n_tiles = n_rows // tm + n_experts
    onehot = (e[:, None] == jnp.arange(n_experts, dtype=jnp.int32)[None, :]).astype(jnp.int32)
    csum = jnp.cumsum(onehot, axis=0)
    rank = jnp.sum(onehot * csum, axis=1) - 1
    counts = csum[-1]
    padded = ((counts + tm - 1) // tm) * tm
    ends = jnp.cumsum(padded)
    off = ends - padded
    pos = off[e] + rank
    r = jnp.arange(n_rows, dtype=jnp.int32)
    src = jnp.zeros((n_tiles * tm,), jnp.int32).at[pos].set(r // 2)
    dst = jnp.zeros((n_tiles * tm,), jnp.int32).at[pos].set((r % 2) * T + r // 2)
    tile_start = jnp.arange(n_tiles, dtype=jnp.int32) * tm
    tile_e = jnp.minimum(jnp.searchsorted(ends, tile_start, side="right"), n_experts - 1).astype(jnp.int32)
    tile_valid = jnp.clip(counts[tile_e] - (tile_start - off[tile_e]), 0, tm).astype(jnp.int32)
    return tile_e, tile_valid, src.reshape(n_tiles, 1, tm), dst.reshape(n_tiles, 1, tm)


def _expert_kernel(te_ref, tv_ref, src_ref, dst_ref, h_hbm, wg_ref, wu_ref, wd_ref, y_hbm,
                   xbuf, ybuf, sem_in, sem_out):
    del te_ref
    i = pl.program_id(0)
    valid = tv_ref[i]

    def gather_copy(r):
        return pltpu.make_async_copy(h_hbm.at[pl.ds(src_ref[0, 0, r], 1)], xbuf.at[pl.ds(r, 1)], sem_in)

    def scatter_copy(r):
        return pltpu.make_async_copy(ybuf.at[pl.ds(r, 1)], y_hbm.at[pl.ds(dst_ref[0, 0, r], 1)], sem_out)

    def start_all(make):
        def body(r, c):
            make(r).start()
            return c
        lax.fori_loop(0, valid, body, 0)

    def wait_all(make):
        def body(r, c):
            make(r).wait()
            return c
        lax.fori_loop(0, valid, body, 0)

    @pl.when(valid > 0)
    def _():
        start_all(gather_copy)
        wait_all(gather_copy)
        rows = lax.broadcasted_iota(jnp.int32, xbuf.shape, 0)
        packed = jnp.where(rows < valid, xbuf[...], jnp.zeros_like(xbuf))
        x_lo, x_hi = _unpack_bf16_pairs(packed)
        half = xbuf.shape[1]
        gate = _dot(x_lo, wg_ref[:half, :]) + _dot(x_hi, wg_ref[half:, :])
        up = _dot(x_lo, wu_ref[:half, :]) + _dot(x_hi, wu_ref[half:, :])
        a = (gate * jax.nn.sigmoid(gate) * up).astype(BF16)
        ybuf[...] = _dot(a, wd_ref[...])
        start_all(scatter_copy)
        wait_all(scatter_copy)


def moe_experts(h, plan, w_gate, w_up, w_down, tm):
    T = h.shape[0]
    D = w_gate.shape[1]
    tile_e, tile_valid, src, dst = plan
    n_tiles = tile_e.shape[0]
    d_e = w_gate.shape[-1]
    grid_spec = pltpu.PrefetchScalarGridSpec(
        num_scalar_prefetch=2,
        grid=(n_tiles,),
        in_specs=[pl.BlockSpec((1, 1, tm), lambda i, te, tv: (i, 0, 0), memory_space=pltpu.SMEM),
                  pl.BlockSpec((1, 1, tm), lambda i, te, tv: (i, 0, 0), memory_space=pltpu.SMEM),
                  pl.BlockSpec(memory_space=pl.ANY),
                  pl.BlockSpec((None, D, d_e), lambda i, te, tv: (te[i], 0, 0)),
                  pl.BlockSpec((None, D, d_e), lambda i, te, tv: (te[i], 0, 0)),
                  pl.BlockSpec((None, d_e, D), lambda i, te, tv: (te[i], 0, 0))],
        out_specs=pl.BlockSpec(memory_space=pl.ANY),
        scratch_shapes=[pltpu.VMEM((tm, D // 2), h.dtype),
                        pltpu.VMEM((tm, D), F32),
                        pltpu.SemaphoreType.DMA,
                        pltpu.SemaphoreType.DMA],
    )
    return pl.pallas_call(
        _expert_kernel,
        grid_spec=grid_spec,
        out_shape=jax.ShapeDtypeStruct((2 * T, D), F32),
        compiler_params=_params(("arbitrary",)),
        name="moe_experts",
    )(tile_e, tile_valid, src, dst, h, w_gate, w_up, w_down)


def _combine_kernel(x_ref, y0_ref, y1_ref, info_ref, g_ref, o_ref, *, final_norm):
    info = info_ref[...]
    y = info[:, 2:3] * y0_ref[...] + info[:, 3:4] * y1_ref[...]
    x = x_ref[...] + y
    if final_norm:
        x = _norm_rows(x, g_ref[...])
    o_ref[...] = x


def moe_combine(x, y, info, row_off, g_final, final_norm, *, tm=512):
    Tg, D = x.shape
    T = info.shape[0]
    tm = _tile(math.gcd(Tg, row_off) if row_off else Tg, tm)
    ob = row_off // tm
    y3 = y.reshape(2, T, D)
    kern = functools.partial(_combine_kernel, final_norm=final_norm)
    return pl.pallas_call(
        kern,
        grid=(Tg // tm,),
        in_specs=[pl.BlockSpec((tm, D), lambda i: (i, 0)),
                  pl.BlockSpec((None, tm, D), lambda i: (0, i + ob, 0)),
                  pl.BlockSpec((None, tm, D), lambda i: (1, i + ob, 0)),
                  pl.BlockSpec((tm, LANES), lambda i: (i + ob, 0)),
                  pl.BlockSpec((1, D), lambda i: (0, 0))],
        out_specs=pl.BlockSpec((tm, D), lambda i: (i, 0)),
        out_shape=jax.ShapeDtypeStruct((Tg, D), F32),
        compiler_params=_params(("parallel",)),
        name="moe_combine",
    )(x, y3, y3, info, g_final)


MOE_TILE = 512


def hier_moe_block(xs, g, router_group, router_expert, w_gate, w_up, w_down, g_final, final_norm):
    xa, xb = xs
    n_experts = w_gate.shape[0]
    h, info = moe_route(xa, xb, g, router_group, router_expert)
    tm = _tile(2 * h.shape[0], MOE_TILE)
    plan = moe_plan(info, tm, n_experts)
    y = moe_experts(h, plan, w_gate, w_up, w_down, tm)
    oa = moe_combine(xa, y, info, 0, g_final, final_norm)
    ob = moe_combine(xb, y, info, xa.shape[0], g_final, final_norm)
    return oa, ob


def hyena_block(x, L, g, w_in, conv_w, conv_b, f_w1, f_b1, f_win, f_bin, f_freq, f_w3, skip_b, w_out):
    D = x.shape[1]
    x0, zv = hyena_inproj(x, g, w_in, conv_w, conv_b[None, :], L)
    taps, asum = hyena_filter_taps(L, w_out.shape[0], f_w1, f_b1, f_win, f_bin, f_freq, f_w3)
    y = hyena_long_conv(zv, x0, taps, asum, skip_b[None, :], L)
    return matmul_residual(y, w_out, x)


def attention_block(x, B, L, g, w_qkv, q_gain, k_gain, w_o):
    D = x.shape[1]
    n_heads = w_o.shape[0] // HEAD_DIM
    qkv = qkv_project(x, g, w_qkv, q_gain, k_gain, L)
    o = flash_attention(qkv.reshape(B, L, qkv.shape[1]), B, L, n_heads)
    return matmul_residual(o.reshape(B * L, n_heads * HEAD_DIM), w_o, x)


def kernel(x_prompt, x_sample, norm_mix, norm_ffn, norm_final, hy_w_in, hy_conv_w, hy_conv_b, hy_f_w1, hy_f_b1, hy_f_win, hy_f_bin, hy_f_freq, hy_f_w3, hy_skip_b, hy_w_out, at_w_qkv, at_q_gain, at_k_gain, at_w_o, moe_router_group, moe_router_expert, moe_w_gate, moe_w_up, moe_w_down):
    D = x_prompt.shape[-1]
    depth = norm_mix.shape[0]
    shapes = [x_prompt.shape, x_sample.shape]
    for s in shapes:
        assert s[0] == 2, "the long convolution packs exactly two sequences per group"
    xs = [x_prompt.reshape(-1, D), x_sample.reshape(-1, D)]
    hy_w_in_b = hy_w_in.astype(BF16)
    hy_w_out_b = hy_w_out.astype(BF16)
    at_w_qkv_b = at_w_qkv.astype(BF16)
    at_w_o_b = at_w_o.astype(BF16)
    w_gate_b = moe_w_gate.astype(BF16)
    w_up_b = moe_w_up.astype(BF16)
    w_down_b = moe_w_down.astype(BF16)
    g_final = norm_final[None, :]
    for i in range(depth):
        j = i // N_MIXERS
        g = norm_mix[i][None, :]
        if i % N_MIXERS == 0:
            xs = [hyena_block(x, s[1], g, hy_w_in_b[j], hy_conv_w[j], hy_conv_b[j], hy_f_w1[j], hy_f_b1[j],
                              hy_f_win[j], hy_f_bin[j], hy_f_freq[j], hy_f_w3[j], hy_skip_b[j], hy_w_out_b[j])
                  for x, s in zip(xs, shapes)]
        else:
            xs = [attention_block(x, s[0], s[1], g, at_w_qkv_b[j], at_q_gain[j], at_k_gain[j], at_w_o_b[j])
                  for x, s in zip(xs, shapes)]
        xs = hier_moe_block(xs, norm_ffn[i][None, :], moe_router_group[i], moe_router_expert[i],
                            w_gate_b[i], w_up_b[i], w_down_b[i], g_final, i == depth - 1)
    return (xs[0].reshape(shapes[0]), xs[1].reshape(shapes[1]))
```

```python
import functools
import math

import jax
import jax.numpy as jnp
import numpy as np
from jax import lax
from jax.experimental import pallas as pl
from jax.experimental.pallas import tpu as pltpu

RMS_EPS = 1e-6
GRID_W = 64
EMB_DIM = 33
DECAY_TARGET = 1e-2
FAST_DECAY_PCT = 0.3
SLOW_DECAY_PCT = 1.5
MIN_DECAY = math.log(DECAY_TARGET) / FAST_DECAY_PCT
MAX_DECAY = math.log(DECAY_TARGET) / SLOW_DECAY_PCT
HEAD_DIM = 128
N_KV_HEADS = 4
ROPE_THETA = 10000.0
N_GROUPS = 4
EXPERTS_PER_GROUP = 8
N_MIXERS = 2

LANES = 128
BF16_SUBLANES = 16
VMEM_LIMIT_BYTES = 52 * 1024 * 1024
DFT_N2 = 128

BF16 = jnp.bfloat16
F32 = jnp.float32


def _params(sem):
    return pltpu.CompilerParams(dimension_semantics=sem, vmem_limit_bytes=VMEM_LIMIT_BYTES)


def _dot(a, b):
    return jnp.dot(a, b, preferred_element_type=F32)


def _norm_rows(x, g):
    ms = jnp.mean(x * x, axis=-1, keepdims=True)
    return x * lax.rsqrt(ms + RMS_EPS) * g


def _tile(n, pref):
    return pref if n % pref == 0 else n


def _matmul_res_kernel(a_ref, w_ref, r_ref, o_ref):
    o_ref[...] = r_ref[...] + _dot(a_ref[...], w_ref[...])


def matmul_residual(a, w, res, *, tm=1024, tn=512):
    T, K = a.shape
    N = w.shape[1]
    tm = _tile(T, tm)
    tn = _tile(N, tn)
    return pl.pallas_call(
        _matmul_res_kernel,
        grid=(T // tm, N // tn),
        in_specs=[pl.BlockSpec((tm, K), lambda i, j: (i, 0)),
                  pl.BlockSpec((K, tn), lambda i, j: (0, j)),
                  pl.BlockSpec((tm, tn), lambda i, j: (i, j))],
        out_specs=pl.BlockSpec((tm, tn), lambda i, j: (i, j)),
        out_shape=jax.ShapeDtypeStruct((T, N), F32),
        compiler_params=_params(("parallel", "arbitrary")),
        name="matmul_residual",
    )(a, w, res)


HALO = BF16_SUBLANES


def _hy_inproj_kernel(x_ref, xp_ref, xn_ref, g_ref, w0_ref, w1_ref, w2_ref,
                      cw0_ref, cw1_ref, cw2_ref, cb0_ref, cb1_ref, cb2_ref,
                      x0_out, zv_out, hn_ref, *, tm, tiles_per_seq):
    i = pl.program_id(0)
    j = pl.program_id(1)

    @pl.when(j == 0)
    def _():
        g = g_ref[...]
        hn_ref[HALO:HALO + tm, :] = _norm_rows(x_ref[...], g).astype(BF16)
        first = (i % tiles_per_seq) == 0
        last = (i % tiles_per_seq) == tiles_per_seq - 1
        hp = _norm_rows(xp_ref[...], g)
        hn_ref[0:HALO, :] = jnp.where(first, 0.0, hp).astype(BF16)
        hx = _norm_rows(xn_ref[...], g)
        hn_ref[HALO + tm:, :] = jnp.where(last, 0.0, hx).astype(BF16)

    h = hn_ref[...]
    rows = tm + 2 * HALO

    def conv(w_ref, cw_ref, cb_ref):
        u = _dot(h, w_ref[...])
        um = pltpu.roll(u, 1, 0)[HALO:HALO + tm]
        up = pltpu.roll(u, rows - 1, 0)[HALO:HALO + tm]
        uc = u[HALO:HALO + tm]
        cw = cw_ref[...]
        return um * cw[0:1] + uc * cw[1:2] + up * cw[2:3] + cb_ref[...]

    x0_out[...] = conv(w0_ref, cw0_ref, cb0_ref)
    x1 = conv(w1_ref, cw1_ref, cb1_ref)
    v = conv(w2_ref, cw2_ref, cb2_ref)
    zv_out[...] = v * x1


def hyena_inproj(x, g, w_in, conv_w, conv_b, seq_len, *, tm=512, tn=512):
    T, D = x.shape
    W = w_in.shape[1] // 3
    tm = _tile(seq_len, tm)
    tn = _tile(W, tn)
    nj = W // tn
    hb = tm // HALO
    n_halo_blocks = T // HALO
    kern = functools.partial(_hy_inproj_kernel, tm=tm, tiles_per_seq=seq_len // tm)

    def wspec(c):
        return pl.BlockSpec((D, tn), lambda i, j, c=c: (0, j + c * nj))

    def cwspec(c):
        return pl.BlockSpec((3, tn), lambda i, j, c=c: (0, j + c * nj))

    def cbspec(c):
        return pl.BlockSpec((1, tn), lambda i, j, c=c: (0, j + c * nj))

    return pl.pallas_call(
        kern,
        grid=(T // tm, nj),
        in_specs=[pl.BlockSpec((tm, D), lambda i, j: (i, 0)),
                  pl.BlockSpec((HALO, D), lambda i, j: (jnp.maximum(i * hb - 1, 0), 0)),
                  pl.BlockSpec((HALO, D), lambda i, j: (jnp.minimum((i + 1) * hb, n_halo_blocks - 1), 0)),
                  pl.BlockSpec((1, D), lambda i, j: (0, 0)),
                  wspec(0), wspec(1), wspec(2),
                  cwspec(0), cwspec(1), cwspec(2),
                  cbspec(0), cbspec(1), cbspec(2)],
        out_specs=[pl.BlockSpec((tm, tn), lambda i, j: (i, j)),
                   pl.BlockSpec((tm, tn), lambda i, j: (i, j))],
        out_shape=[jax.ShapeDtypeStruct((T, W), F32), jax.ShapeDtypeStruct((T, W), F32)],
        scratch_shapes=[pltpu.VMEM((tm + 2 * HALO, D), BF16)],
        compiler_params=_params(("parallel", "arbitrary")),
        name="hyena_inproj",
    )(x, x, x, g, w_in, w_in, w_in, conv_w, conv_w, conv_w, conv_b, conv_b, conv_b)


def _filter_mlp_kernel(z_ref, w1_ref, b1_ref, win_ref, bin_ref, fr_ref, h_ref):
    hp = lax.Precision.HIGHEST
    fr = fr_ref[...]
    h = jnp.sin(fr * (jnp.dot(z_ref[...], w1_ref[...], precision=hp, preferred_element_type=F32) + b1_ref[...]))
    for l in range(win_ref.shape[0]):
        h = jnp.sin(fr * (jnp.dot(h, win_ref[l], precision=hp, preferred_element_type=F32) + bin_ref[l]))
    h_ref[...] = h


def _filter_expand_kernel(h_ref, t_ref, w3_ref, dl_ref, k_ref, s_ref, *, tl, n_fwd_tiles):
    j = pl.program_id(0)
    i = pl.program_id(1)
    taps = jnp.dot(h_ref[...], w3_ref[...], precision=lax.Precision.HIGHEST, preferred_element_type=F32)
    taps = taps * jnp.exp(-t_ref[...] * dl_ref[...])
    row = i * tl + lax.broadcasted_iota(jnp.int32, taps.shape, 0)
    taps = jnp.where(jnp.logical_and(j >= n_fwd_tiles, row == 0), 0.0, taps)
    k_ref[...] = taps.astype(k_ref.dtype)

    @pl.when(i == 0)
    def _():
        s_ref[...] = jnp.zeros_like(s_ref)

    s_ref[...] += jnp.sum(jnp.abs(taps), axis=0, keepdims=True)


def hyena_filter_taps(L, D, f_w1, f_b1, f_win, f_bin, f_freq, f_w3):
    H = f_w1.shape[1]
    t = jnp.linspace(0.0, 1.0, L, dtype=F32)[:, None]
    bands = (EMB_DIM - 1) // 2
    w = 2.0 * math.pi * jnp.arange(L, dtype=F32)[:, None] / L
    fb = jnp.linspace(1e-4, bands - 1, bands, dtype=F32)[None, :]
    z = jnp.concatenate([t, jnp.cos(fb * w), -jnp.sin(fb * w)], axis=-1)
    zp = jnp.pad(z, ((0, 0), (0, LANES - EMB_DIM)))
    w1p = jnp.pad(f_w1, ((0, LANES - EMB_DIM), (0, 0)))
    n_in = f_win.shape[0]
    tl = _tile(L, 512)
    h = pl.pallas_call(
        _filter_mlp_kernel,
        grid=(L // tl,),
        in_specs=[pl.BlockSpec((tl, LANES), lambda i: (i, 0)),
                  pl.BlockSpec((LANES, H), lambda i: (0, 0)),
                  pl.BlockSpec((1, H), lambda i: (0, 0)),
                  pl.BlockSpec((n_in, H, H), lambda i: (0, 0, 0)),
                  pl.BlockSpec((n_in, 1, H), lambda i: (0, 0, 0)),
                  pl.BlockSpec((1, H), lambda i: (0, 0))],
        out_specs=pl.BlockSpec((tl, H), lambda i: (i, 0)),
        out_shape=jax.ShapeDtypeStruct((L, H), F32),
        compiler_params=_params(("parallel",)),
        name="filter_mlp",
    )(zp, w1p, f_b1[None, :], f_win, f_bin[:, None, :], f_freq[None, :])

    deltas = jnp.abs(jnp.linspace(MIN_DECAY, MAX_DECAY, D, dtype=F32))[None, :]
    deltas2 = jnp.concatenate([deltas, deltas], axis=1)
    tn = _tile(D, 1024)
    n_fwd = D // tn
    kern = functools.partial(_filter_expand_kernel, tl=tl, n_fwd_tiles=n_fwd)
    taps, asum = pl.pallas_call(
        kern,
        grid=(2 * n_fwd, L // tl),
        in_specs=[pl.BlockSpec((tl, H), lambda j, i: (i, 0)),
                  pl.BlockSpec((tl, 1), lambda j, i: (i, 0)),
                  pl.BlockSpec((H, tn), lambda j, i: (0, j)),
                  pl.BlockSpec((1, tn), lambda j, i: (0, j))],
        out_specs=[pl.BlockSpec((tl, tn), lambda j, i: (i, j)),
                   pl.BlockSpec((1, tn), lambda j, i: (0, j))],
        out_shape=[jax.ShapeDtypeStruct((L, 2 * D), BF16), jax.ShapeDtypeStruct((1, 2 * D), F32)],
        compiler_params=_params(("parallel", "arbitrary")),
        name="filter_expand",
    )(h, t, f_w3, deltas2)
    return taps, asum


def dft_tables(N1, N2):
    N = N1 * N2
    half = N1 // 2
    k1 = jnp.arange(N1, dtype=jnp.int32)[None, :, None]
    n = (jnp.arange(half, dtype=jnp.int32)[None, None, :] * N2
         + jnp.arange(N2, dtype=jnp.int32)[:, None, None])
    ang = ((k1 * n) % N).astype(F32) * (2.0 * math.pi / N)
    c, s = jnp.cos(ang), jnp.sin(ang)
    fwd = jnp.concatenate([jnp.concatenate([c, s], axis=2),
                           jnp.concatenate([-s, c], axis=2)], axis=1)
    inv = jnp.swapaxes(fwd, 1, 2)
    fwd_real = jnp.concatenate([c, -s], axis=1)
    k2 = jnp.arange(N2, dtype=jnp.int32)
    ang2 = ((k2[:, None] * k2[None, :]) % N2).astype(F32) * (2.0 * math.pi / N2)
    c2, s2 = jnp.cos(ang2), jnp.sin(ang2)
    f2 = jnp.concatenate([jnp.concatenate([c2, s2], axis=1),
                          jnp.concatenate([-s2, c2], axis=1)], axis=0)
    g2 = f2.T
    return fwd.astype(BF16), inv.astype(BF16), fwd_real.astype(BF16), f2.astype(BF16), g2.astype(BF16)


def _fft_s1_kernel(z_ref, m_ref, o_ref):
    rhs = jnp.concatenate([z_ref[0], z_ref[1]], axis=0).astype(BF16)
    o_ref[...] = _dot(m_ref[...], rhs).astype(o_ref.dtype)


def _fft_s1_real_kernel(k_ref, m_ref, o_ref):
    o_ref[...] = _dot(m_ref[...], k_ref[...]).astype(o_ref.dtype)


def _fft_mid_kernel(a_ref, af_ref, ab_ref, s_ref, f2_ref, g2_ref, c_ref):
    n2 = a_ref.shape[1]
    td = a_ref.shape[2]
    f2 = f2_ref[...]
    xf = _dot(f2, af_ref[...].reshape(2 * n2, td))
    xb = _dot(f2, ab_ref[...].reshape(2 * n2, td))
    s = s_ref[...]
    kr = (xf[:n2] + xb[:n2]) * s
    ki = (xf[n2:] - xb[n2:]) * s
    x = _dot(f2, a_ref[...].reshape(2 * n2, td))
    xr, xi = x[:n2], x[n2:]
    y = jnp.concatenate([xr * kr - xi * ki, xr * ki + xi * kr], axis=0).astype(BF16)
    c = _dot(g2_ref[...], y)
    c_ref[...] = c.reshape(2, n2, td).astype(c_ref.dtype)


def _fft_out_kernel(c_ref, m_ref, zv_ref, x0_ref, sk_ref, y_ref):
    y = _dot(m_ref[...], c_ref[...])
    half = y.shape[0] // 2
    sk = sk_ref[...]
    y_ref[0] = ((y[:half] + zv_ref[0] * sk) * x0_ref[0]).astype(y_ref.dtype)
    y_ref[1] = ((y[half:] + zv_ref[1] * sk) * x0_ref[1]).astype(y_ref.dtype)


def hyena_long_conv(zv, x0, taps, asum, skip_b, L):
    D = zv.shape[1]
    N = 2 * L
    N2 = DFT_N2
    N1 = N // N2
    half = N1 // 2
    fwd, inv, fwd_real, f2, g2 = dft_tables(N1, N2)
    td = _tile(D, 2048)
    nd = D // td

    af = pl.pallas_call(
        _fft_s1_real_kernel,
        grid=(N2,),
        in_specs=[pl.BlockSpec((half, 2 * D), lambda j: (0, j)),
                  pl.BlockSpec((None, 2 * N1, half), lambda j: (j, 0, 0))],
        out_specs=pl.BlockSpec((2 * N1, 2 * D), lambda j: (0, j)),
        out_shape=jax.ShapeDtypeStruct((2 * N1, N2 * 2 * D), BF16),
        compiler_params=_params(("parallel",)),
        name="filter_dft_stage1",
    )(taps.reshape(half, N2 * 2 * D), fwd_real)
    scale = 1.0 / ((asum[:, :D] + asum[:, D:]) * N)
    af4 = af.reshape(2, N1, N2, 2 * D)

    a = pl.pallas_call(
        _fft_s1_kernel,
        grid=(N2,),
        in_specs=[pl.BlockSpec((2, half, D), lambda j: (0, 0, j)),
                  pl.BlockSpec((None, 2 * N1, N1), lambda j: (j, 0, 0))],
        out_specs=pl.BlockSpec((2 * N1, D), lambda j: (0, j)),
        out_shape=jax.ShapeDtypeStruct((2 * N1, N2 * D), BF16),
        compiler_params=_params(("parallel",)),
        name="conv_dft_stage1",
    )(zv.reshape(2, half, N2 * D), fwd)

    c = pl.pallas_call(
        _fft_mid_kernel,
        grid=(N1, nd),
        in_specs=[pl.BlockSpec((2, None, N2, td), lambda k, j: (0, k, 0, j)),
                  pl.BlockSpec((2, None, N2, td), lambda k, j: (0, k, 0, j)),
                  pl.BlockSpec((2, None, N2, td), lambda k, j: (0, k, 0, j + nd)),
                  pl.BlockSpec((1, td), lambda k, j: (0, j)),
                  pl.BlockSpec((2 * N2, 2 * N2), lambda k, j: (0, 0)),
                  pl.BlockSpec((2 * N2, 2 * N2), lambda k, j: (0, 0))],
        out_specs=pl.BlockSpec((2, None, N2, td), lambda k, j: (0, k, 0, j)),
        out_shape=jax.ShapeDtypeStruct((2, N1, N2, D), BF16),
        compiler_params=_params(("parallel", "parallel")),
        name="conv_dft_stage2",
    )(a.reshape(2, N1, N2, D), af4, af4, scale, f2, g2)

    y = pl.pallas_call(
        _fft_out_kernel,
        grid=(N2,),
        in_specs=[pl.BlockSpec((2 * N1, D), lambda j: (0, j)),
                  pl.BlockSpec((None, N1, 2 * N1), lambda j: (j, 0, 0)),
                  pl.BlockSpec((2, half, D), lambda j: (0, 0, j)),
                  pl.BlockSpec((2, half, D), lambda j: (0, 0, j)),
                  pl.BlockSpec((1, D), lambda j: (0, 0))],
        out_specs=pl.BlockSpec((2, half, D), lambda j: (0, 0, j)),
        out_shape=jax.ShapeDtypeStruct((2, half, N2 * D), BF16),
        compiler_params=_params(("parallel",)),
        name="conv_dft_inverse",
    )(c.reshape(2 * N1, N2 * D), inv, zv.reshape(2, half, N2 * D), x0.reshape(2, half, N2 * D), skip_b)
    return y.reshape(2 * L, D)


def _qkv_kernel(x_ref, g_ref, w_ref, gain_ref, cos_ref, sin_ref, o_ref, hn_ref, *, n_rope_tiles):
    j = pl.program_id(1)

    @pl.when(j == 0)
    def _():
        hn_ref[...] = _norm_rows(x_ref[...], g_ref[...]).astype(BF16)

    u = _dot(hn_ref[...], w_ref[...])
    tn = u.shape[1]

    @pl.when(j < n_rope_tiles)
    def _():
        gain = gain_ref[...]
        cos = cos_ref[...]
        sin = sin_ref[...]
        even = (lax.broadcasted_iota(jnp.int32, cos.shape, 1) % 2) == 0
        for hh in range(tn // HEAD_DIM):
            xh = u[:, hh * HEAD_DIM:(hh + 1) * HEAD_DIM]
            y = _norm_rows(xh, gain)
            sw = jnp.where(even, pltpu.roll(y, HEAD_DIM - 1, 1), pltpu.roll(y, 1, 1))
            o_ref[:, hh * HEAD_DIM:(hh + 1) * HEAD_DIM] = (y * cos + sw * sin).astype(o_ref.dtype)

    @pl.when(j >= n_rope_tiles)
    def _():
        o_ref[...] = u.astype(o_ref.dtype)


def rope_tables(L):
    rows = L // GRID_W
    r, c = jnp.meshgrid(jnp.arange(rows, dtype=F32), jnp.arange(GRID_W, dtype=F32), indexing="ij")
    r = r.reshape(L)
    c = c.reshape(L)
    axis_dim = HEAD_DIM // 2
    inv = 1.0 / (ROPE_THETA ** (jnp.arange(0, axis_dim, 2, dtype=F32) / axis_dim))
    ang = jnp.concatenate([r[:, None] * inv[None], c[:, None] * inv[None]], axis=-1)
    cos = jnp.repeat(jnp.cos(ang), 2, axis=1)
    sign = jnp.tile(jnp.array([-1.0, 1.0], F32), HEAD_DIM // 2)[None, :]
    sin = jnp.repeat(jnp.sin(ang), 2, axis=1) * sign
    return cos, sin


def qkv_project(x, g, w_qkv, q_gain, k_gain, seq_len, *, tm=512):
    T, D = x.shape
    n_out = w_qkv.shape[1]
    nk = N_KV_HEADS * HEAD_DIM
    nq = n_out - 2 * nk
    tn = nk
    tm = _tile(seq_len, tm)
    n_rope = (nq + nk) // tn
    scale = HEAD_DIM ** -0.5 * math.log2(math.e)
    gains = jnp.concatenate([jnp.tile((q_gain * scale)[None, None, :], (nq // tn, 1, 1)),
                             k_gain[None, None, :],
                             jnp.ones((1, 1, HEAD_DIM), F32)], axis=0)
    cos, sin = rope_tables(seq_len)
    spt = seq_len // tm
    kern = functools.partial(_qkv_kernel, n_rope_tiles=n_rope)
    return pl.pallas_call(
        kern,
        grid=(T // tm, n_out // tn),
        in_specs=[pl.BlockSpec((tm, D), lambda i, j: (i, 0)),
                  pl.BlockSpec((1, D), lambda i, j: (0, 0)),
                  pl.BlockSpec((D, tn), lambda i, j: (0, j)),
                  pl.BlockSpec((None, 1, HEAD_DIM), lambda i, j: (j, 0, 0)),
                  pl.BlockSpec((tm, HEAD_DIM), lambda i, j: (i % spt, 0)),
                  pl.BlockSpec((tm, HEAD_DIM), lambda i, j: (i % spt, 0))],
        out_specs=pl.BlockSpec((tm, tn), lambda i, j: (i, j)),
        out_shape=jax.ShapeDtypeStruct((T, n_out), BF16),
        scratch_shapes=[pltpu.VMEM((tm, D), BF16)],
        compiler_params=_params(("parallel", "arbitrary")),
        name="qkv_project",
    )(x, g, w_qkv, gains, cos, sin)


def _flash_kernel(q_ref, k_ref, v_ref, o_ref, qs_ref, s0_ref, s1_ref, p0_ref, p1_ref, a0_ref, a1_ref,
                  m_ref, acc_ref, *, group, tq, tk, nk):
    s_refs = (s0_ref, s1_ref)
    p_refs = (p0_ref, p1_ref)
    a_refs = (a0_ref, a1_ref)
    for gq in range(group):
        qs_ref[gq * tq:(gq + 1) * tq, :] = q_ref[:, gq * HEAD_DIM:(gq + 1) * HEAD_DIM]
    m_ref[...] = jnp.full_like(m_ref, -jnp.inf)
    acc_ref[...] = jnp.zeros_like(acc_ref)

    def rows_of(j):
        if isinstance(j, int):
            return pl.ds(j * tk, tk)
        return pl.ds(pl.multiple_of(j * tk, tk), tk)

    def scores(j, slot):
        kb = k_ref[rows_of(j), :]
        s_refs[slot][...] = lax.dot_general(qs_ref[...], kb, (((1,), (1,)), ((), ())),
                                            preferred_element_type=F32)

    def softmax(slot):
        s = s_refs[slot][...]
        m_prev = m_ref[...]
        m_new = jnp.maximum(m_prev, jnp.max(s, axis=-1, keepdims=True))
        a_refs[slot][...] = jnp.exp2(m_prev - m_new)
        p_refs[slot][...] = jnp.exp2(s - jnp.tile(m_new, (1, tk // LANES))).astype(BF16)
        m_ref[...] = m_new

    def accumulate(j, slot):
        pv = _dot(p_refs[slot][...], v_ref[rows_of(j), :])
        acc_ref[...] = acc_ref[...] * jnp.tile(a_refs[slot][...], (1, 2)) + pv

    def step(j, slot, first, last):
        if not last:
            scores(j + 1, 1 - slot)
        softmax(slot)
        if not first:
            accumulate(j - 1, 1 - slot)

    scores(0, 0)
    if nk >= 4 and nk % 2 == 0:
        step(0, 0, True, False)

        def pair(t, c):
            j = 2 * t + 1
            step(j, 1, False, False)
            step(j + 1, 0, False, False)
            return c

        lax.fori_loop(0, (nk - 2) // 2, pair, 0)
        step(nk - 1, 1, False, True)
    else:
        for j in range(nk):
            step(j, j % 2, j == 0, j == nk - 1)
    accumulate(nk - 1, (nk - 1) % 2)

    acc = acc_ref[...]
    o = acc[:, :HEAD_DIM] / acc[:, HEAD_DIM:]
    for gq in range(group):
        o_ref[:, gq * HEAD_DIM:(gq + 1) * HEAD_DIM] = o[gq * tq:(gq + 1) * tq].astype(o_ref.dtype)


def flash_attention(qkv, B, L, n_heads, *, tq=256, tk=512):
    group = n_heads // N_KV_HEADS
    tq = _tile(L, tq)
    tk = _tile(L, tk)
    nk = L // tk
    qw = group * HEAD_DIM
    k_col = n_heads
    v_col = (n_heads + N_KV_HEADS) * HEAD_DIM
    rows = group * tq
    v = qkv[:, :, v_col:].reshape(B, L, N_KV_HEADS, HEAD_DIM)
    v_aug = jnp.concatenate([v, jnp.ones_like(v)], axis=-1).reshape(B, L, N_KV_HEADS * 2 * HEAD_DIM)
    kern = functools.partial(_flash_kernel, group=group, tq=tq, tk=tk, nk=nk)
    return pl.pallas_call(
        kern,
        grid=(B, N_KV_HEADS, L // tq),
        in_specs=[pl.BlockSpec((None, tq, qw), lambda b, h, qi: (b, qi, h)),
                  pl.BlockSpec((None, L, HEAD_DIM), lambda b, h, qi: (b, 0, k_col + h)),
                  pl.BlockSpec((None, L, 2 * HEAD_DIM), lambda b, h, qi: (b, 0, h))],
        out_specs=pl.BlockSpec((None, tq, qw), lambda b, h, qi: (b, qi, h)),
        out_shape=jax.ShapeDtypeStruct((B, L, n_heads * HEAD_DIM), BF16),
        scratch_shapes=[pltpu.VMEM((rows, HEAD_DIM), BF16),
                        pltpu.VMEM((rows, tk), F32), pltpu.VMEM((rows, tk), F32),
                        pltpu.VMEM((rows, tk), BF16), pltpu.VMEM((rows, tk), BF16),
                        pltpu.VMEM((rows, LANES), F32), pltpu.VMEM((rows, LANES), F32),
                        pltpu.VMEM((rows, LANES), F32),
                        pltpu.VMEM((rows, 2 * HEAD_DIM), F32)],
        compiler_params=_params(("parallel", "parallel", "parallel")),
        name="flash_attention",
    )(qkv, qkv, v_aug)


def _pack_bf16_pairs(x):
    half = x.shape[1] // 2
    bits = lax.bitcast_convert_type(x.astype(BF16).astype(F32), jnp.uint32)
    return (bits[:, :half] >> 16) | (bits[:, half:] & jnp.uint32(0xFFFF0000))


def _unpack_bf16_pairs(p):
    lo = lax.bitcast_convert_type(p << 16, F32).astype(BF16)
    hi = lax.bitcast_convert_type(p & jnp.uint32(0xFFFF0000), F32).astype(BF16)
    return lo, hi


def _router_kernel(xa_ref, xb_ref, g_ref, rw_ref, h_ref, info_ref, cnt_ref, *, n_a_tiles):
    i = pl.program_id(0)

    def route(x_ref):
        hf = _norm_rows(x_ref[...], g_ref[...])
        h_ref[...] = _pack_bf16_pairs(hf)
        logits = jnp.dot(hf, rw_ref[...], precision=lax.Precision.HIGHEST, preferred_element_type=F32)
        shape = logits.shape
        lane = lax.broadcasted_iota(jnp.int32, shape, 1)
        lanef = lane.astype(F32)
        neg = -jnp.inf
        big = float(LANES)
        is_g = lane < N_GROUPS
        gl = jnp.where(is_g, logits, neg)
        gmax = jnp.max(gl, axis=-1, keepdims=True)
        gidx = jnp.min(jnp.where(gl == gmax, lanef, big), axis=-1, keepdims=True)
        p_g = 1.0 / jnp.sum(jnp.where(is_g, jnp.exp(logits - gmax), 0.0), axis=-1, keepdims=True)
        lo = N_GROUPS + gidx * EXPERTS_PER_GROUP
        el = jnp.where(lanef >= lo, jnp.where(lanef < lo + EXPERTS_PER_GROUP, logits, neg), neg)
        v1 = jnp.max(el, axis=-1, keepdims=True)
        i1 = jnp.min(jnp.where(el == v1, lanef, big), axis=-1, keepdims=True)
        el2 = jnp.where(lanef == i1, neg, el)
        v2 = jnp.max(el2, axis=-1, keepdims=True)
        i2 = jnp.min(jnp.where(el2 == v2, lanef, big), axis=-1, keepdims=True)
        t = jnp.exp(v2 - v1)
        w1 = p_g / (1.0 + t)
        w2 = p_g * t / (1.0 + t)
        e1 = i1 - N_GROUPS
        e2 = i2 - N_GROUPS
        oh1 = jnp.where(lanef == e1, 1.0, 0.0)
        oh2 = jnp.where(lanef == e2, 1.0, 0.0)
        oh = oh1 + oh2
        tm = shape[0]
        tri = jnp.where(lax.broadcasted_iota(jnp.int32, (tm, tm), 1) < lax.broadcasted_iota(jnp.int32, (tm, tm), 0),
                        1.0, 0.0).astype(BF16)
        before = _dot(tri, oh.astype(BF16)) + cnt_ref[...]
        r1 = jnp.sum(before * oh1, axis=-1, keepdims=True)
        r2 = jnp.sum(before * oh2, axis=-1, keepdims=True)
        cnt_ref[...] += jnp.sum(oh, axis=0, keepdims=True)
        vals = (e1, e2, w1, w2, r1, r2)
        info = jnp.zeros(shape, F32)
        for idx, val in enumerate(vals):
            info = jnp.where(lane == idx, val, info)
        info_ref[...] = info

    @pl.when(i == 0)
    def _():
        cnt_ref[...] = jnp.zeros_like(cnt_ref)

    @pl.when(i < n_a_tiles)
    def _():
        route(xa_ref)

    @pl.when(i >= n_a_tiles)
    def _():
        route(xb_ref)


def moe_route(xa, xb, g, router_group, router_expert, *, tm=512):
    Ta, D = xa.shape
    Tb = xb.shape[0]
    tm = _tile(math.gcd(Ta, Tb), tm)
    na, nb = Ta // tm, Tb // tm
    n_e = N_GROUPS * EXPERTS_PER_GROUP
    rw = jnp.concatenate([router_group, jnp.transpose(router_expert, (1, 0, 2)).reshape(D, n_e)], axis=1)
    rw = jnp.pad(rw, ((0, 0), (0, LANES - rw.shape[1])))
    kern = functools.partial(_router_kernel, n_a_tiles=na)
    return pl.pallas_call(
        kern,
        grid=(na + nb,),
        in_specs=[pl.BlockSpec((tm, D), lambda i: (jnp.minimum(i, na - 1), 0)),
                  pl.BlockSpec((tm, D), lambda i: (jnp.maximum(i - na, 0), 0)),
                  pl.BlockSpec((1, D), lambda i: (0, 0)),
                  pl.BlockSpec((D, LANES), lambda i: (0, 0))],
        out_specs=[pl.BlockSpec((tm, D // 2), lambda i: (i, 0)),
                   pl.BlockSpec((tm, LANES), lambda i: (i, 0)),
                   pl.BlockSpec((1, LANES), lambda i: (0, 0))],
        out_shape=[jax.ShapeDtypeStruct((Ta + Tb, D // 2), jnp.uint32),
                   jax.ShapeDtypeStruct((Ta + Tb, LANES), F32),
                   jax.ShapeDtypeStruct((1, LANES), F32)],
        compiler_params=_params(("arbitrary",)),
        name="moe_route",
    )(xa, xb, g, rw)


def moe_plan(info, cnt, tm, n_experts):
    T = info.shape[0]
    n_tiles = (2 * T) // tm + n_experts
    counts = cnt[0, :n_experts].astype(jnp.int32)
    padded = ((counts + tm - 1) // tm) * tm
    ends = jnp.cumsum(padded)
    off = ends - padded
    e = info[:, 0:2].astype(jnp.int32)
    rank = info[:, 4:6].astype(jnp.int32)
    pos = (off[e] + rank).reshape(-1)
    tile_start = jnp.arange(n_tiles, dtype=jnp.int32) * tm
    tile_e = jnp.sum((tile_start[:, None] >= ends[None, :]).astype(jnp.int32), axis=1)
    tile_e = jnp.minimum(tile_e, n_experts - 1)
    tile_valid = jnp.clip(counts[tile_e] - (tile_start - off[tile_e]), 0, tm).astype(jnp.int32)
    n_used = (ends[-1:] // tm).astype(jnp.int32)
    return pos, tile_e, tile_valid, n_used


DMA_UNROLL = 8


def _dispatch_kernel(pos_ref, h_ref, zeros_hbm, hs_hbm, sem):
    del zeros_hbm
    tm = h_ref.shape[0]

    def row_copy(r, slot):
        return pltpu.make_async_copy(h_ref.at[pl.ds(r, 1)], hs_hbm.at[pl.ds(pos_ref[0, 0, 2 * r + slot], 1)], sem)

    def start(r, c):
        row_copy(r, 0).start()
        row_copy(r, 1).start()
        return c

    def wait(r, c):
        row_copy(r, 0).wait()
        row_copy(r, 1).wait()
        return c

    lax.fori_loop(0, tm, start, 0, unroll=DMA_UNROLL)
    lax.fori_loop(0, tm, wait, 0, unroll=DMA_UNROLL)


def moe_dispatch(h, pos, n_sorted_rows, *, tm=512):
    T, C = h.shape
    tm = _tile(T, tm)
    return pl.pallas_call(
        _dispatch_kernel,
        grid=(T // tm,),
        in_specs=[pl.BlockSpec((1, 1, 2 * tm), lambda i: (i, 0, 0), memory_space=pltpu.SMEM),
                  pl.BlockSpec((tm, C), lambda i: (i, 0)),
                  pl.BlockSpec(memory_space=pl.ANY)],
        out_specs=pl.BlockSpec(memory_space=pl.ANY),
        out_shape=jax.ShapeDtypeStruct((n_sorted_rows, C), h.dtype),
        scratch_shapes=[pltpu.SemaphoreType.DMA],
        input_output_aliases={2: 0},
        compiler_params=_params(("arbitrary",)),
        name="moe_dispatch",
    )(pos.reshape(T // tm, 1, 2 * tm), h, jnp.zeros((n_sorted_rows, C), h.dtype))


def _expert_kernel(te_ref, tv_ref, nu_ref, x_ref, wg_ref, wu_ref, wd_ref, y_ref):
    del te_ref, nu_ref
    valid = tv_ref[pl.program_id(0)]

    @pl.when(valid == 0)
    def _():
        y_ref[...] = jnp.zeros_like(y_ref)

    @pl.when(valid > 0)
    def _():
        x_lo, x_hi = _unpack_bf16_pairs(x_ref[...])
        half = x_ref.shape[1]
        gate = _dot(x_lo, wg_ref[:half, :]) + _dot(x_hi, wg_ref[half:, :])
        up = _dot(x_lo, wu_ref[:half, :]) + _dot(x_hi, wu_ref[half:, :])
        a = (gate * jax.nn.sigmoid(gate) * up).astype(BF16)
        y_ref[...] = _dot(a, wd_ref[...])


def moe_experts(hs, tile_e, tile_valid, n_used, w_gate, w_up, w_down, tm):
    P = hs.shape[0]
    D = w_gate.shape[1]
    d_e = w_gate.shape[-1]

    def row_map(i, te, tv, nu):
        return (jnp.minimum(i, nu[0] - 1), 0)

    def out_map(i, te, tv, nu):
        return (i, 0)

    def w_map(i, te, tv, nu):
        return (te[i], 0, 0)

    grid_spec = pltpu.PrefetchScalarGridSpec(
        num_scalar_prefetch=3,
        grid=(P // tm,),
        in_specs=[pl.BlockSpec((tm, D // 2), row_map),
                  pl.BlockSpec((None, D, d_e), w_map),
                  pl.BlockSpec((None, D, d_e), w_map),
                  pl.BlockSpec((None, d_e, D), w_map)],
        out_specs=pl.BlockSpec((tm, D), out_map),
    )
    return pl.pallas_call(
        _expert_kernel,
        grid_spec=grid_spec,
        out_shape=jax.ShapeDtypeStruct((P, D), F32),
        compiler_params=_params(("arbitrary",)),
        name="moe_experts",
    )(tile_e, tile_valid, n_used, hs, w_gate, w_up, w_down)


def _combine_kernel(pos_ref, x_ref, info_ref, g_ref, y_hbm, o_ref, ybuf, sem, *, final_norm):
    tm = x_ref.shape[0]

    def row_copy(r, slot):
        return pltpu.make_async_copy(y_hbm.at[pl.ds(pos_ref[0, 0, 2 * r + slot], 1)],
                                     ybuf.at[slot, pl.ds(r, 1)], sem)

    def start(r, c):
        row_copy(r, 0).start()
        row_copy(r, 1).start()
        return c

    def wait(r, c):
        row_copy(r, 0).wait()
        row_copy(r, 1).wait()
        return c

    lax.fori_loop(0, tm, start, 0, unroll=DMA_UNROLL)
    lax.fori_loop(0, tm, wait, 0, unroll=DMA_UNROLL)
    info = info_ref[...]
    y = info[:, 2:3] * ybuf[0] + info[:, 3:4] * ybuf[1]
    x = x_ref[...] + y
    if final_norm:
        x = _norm_rows(x, g_ref[...])
    o_ref[...] = x


def moe_combine(x, y, pos, info, row_off, g_final, final_norm, *, tm=256):
    Tg, D = x.shape
    T = info.shape[0]
    tm = _tile(math.gcd(Tg, row_off) if row_off else Tg, tm)
    ob = row_off // tm
    kern = functools.partial(_combine_kernel, final_norm=final_norm)
    return pl.pallas_call(
        kern,
        grid=(Tg // tm,),
        in_specs=[pl.BlockSpec((1, 1, 2 * tm), lambda i: (i + ob, 0, 0), memory_space=pltpu.SMEM),
                  pl.BlockSpec((tm, D), lambda i: (i, 0)),
                  pl.BlockSpec((tm, LANES), lambda i: (i + ob, 0)),
                  pl.BlockSpec((1, D), lambda i: (0, 0)),
                  pl.BlockSpec(memory_space=pl.ANY)],
        out_specs=pl.BlockSpec((tm, D), lambda i: (i, 0)),
        out_shape=jax.ShapeDtypeStruct((Tg, D), F32),
        scratch_shapes=[pltpu.VMEM((2, tm, D), F32), pltpu.SemaphoreType.DMA],
        compiler_params=_params(("arbitrary",)),
        name="moe_combine",
    )(pos.reshape(T // tm, 1, 2 * tm), x, info, g_final, y)


MOE_TILE = 512


def hier_moe_block(xs, g, router_group, router_expert, w_gate, w_up, w_down, g_final, final_norm):
    xa, xb = xs
    n_experts = w_gate.shape[0]
    h, info, cnt = moe_route(xa, xb, g, router_group, router_expert)
    T = h.shape[0]
    tm = _tile(2 * T, MOE_TILE)
    pos, tile_e, tile_valid, n_used = moe_plan(info, cnt, tm, n_experts)
    hs = moe_dispatch(h, pos, tile_e.shape[0] * tm)
    y = moe_experts(hs, tile_e, tile_valid, n_used, w_gate, w_up, w_down, tm)
    oa = moe_combine(xa, y, pos, info, 0, g_final, final_norm)
    ob = moe_combine(xb, y, pos, info, xa.shape[0], g_final, final_norm)
    return oa, ob


def hyena_block(x, L, g, w_in, conv_w, conv_b, f_w1, f_b1, f_win, f_bin, f_freq, f_w3, skip_b, w_out):
    D = x.shape[1]
    x0, zv = hyena_inproj(x, g, w_in, conv_w, conv_b[None, :], L)
    taps, asum = hyena_filter_taps(L, w_out.shape[0], f_w1, f_b1, f_win, f_bin, f_freq, f_w3)
    y = hyena_long_conv(zv, x0, taps, asum, skip_b[None, :], L)
    return matmul_residual(y, w_out, x)


def attention_block(x, B, L, g, w_qkv, q_gain, k_gain, w_o):
    D = x.shape[1]
    n_heads = w_o.shape[0] // HEAD_DIM
    qkv = qkv_project(x, g, w_qkv, q_gain, k_gain, L)
    o = flash_attention(qkv.reshape(B, L, qkv.shape[1]), B, L, n_heads)
    return matmul_residual(o.reshape(B * L, n_heads * HEAD_DIM), w_o, x)


def kernel(x_prompt, x_sample, norm_mix, norm_ffn, norm_final, hy_w_in, hy_conv_w, hy_conv_b, hy_f_w1, hy_f_b1, hy_f_win, hy_f_bin, hy_f_freq, hy_f_w3, hy_skip_b, hy_w_out, at_w_qkv, at_q_gain, at_k_gain, at_w_o, moe_router_group, moe_router_expert, moe_w_gate, moe_w_up, moe_w_down):
    D = x_prompt.shape[-1]
    depth = norm_mix.shape[0]
    shapes = [x_prompt.shape, x_sample.shape]
    for s in shapes:
        assert s[0] == 2, "the long convolution packs exactly two sequences per group"
    xs = [x_prompt.reshape(-1, D), x_sample.reshape(-1, D)]
    hy_w_in_b = hy_w_in.astype(BF16)
    hy_w_out_b = hy_w_out.astype(BF16)
    at_w_qkv_b = at_w_qkv.astype(BF16)
    at_w_o_b = at_w_o.astype(BF16)
    w_gate_b = moe_w_gate.astype(BF16)
    w_up_b = moe_w_up.astype(BF16)
    w_down_b = moe_w_down.astype(BF16)
    g_final = norm_final[None, :]
    for i in range(depth):
        j = i // N_MIXERS
        g = norm_mix[i][None, :]
        if i % N_MIXERS == 0:
            xs = [hyena_block(x, s[1], g, hy_w_in_b[j], hy_conv_w[j], hy_conv_b[j], hy_f_w1[j], hy_f_b1[j],
                              hy_f_win[j], hy_f_bin[j], hy_f_freq[j], hy_f_w3[j], hy_skip_b[j], hy_w_out_b[j])
                  for x, s in zip(xs, shapes)]
        else:
            xs = [attention_block(x, s[0], s[1], g, at_w_qkv_b[j], at_q_gain[j], at_k_gain[j], at_w_o_b[j])
                  for x, s in zip(xs, shapes)]
        xs = hier_moe_block(xs, norm_ffn[i][None, :], moe_router_group[i], moe_router_expert[i],
                            w_gate_b[i], w_up_b[i], w_down_b[i], g_final, i == depth - 1)
    return (xs[0].reshape(shapes[0]), xs[1].reshape(shapes[1]))
```

```python
import functools
import math

import jax
import jax.numpy as jnp
import numpy as np
from jax import lax
from jax.experimental import pallas as pl
from jax.experimental.pallas import tpu as pltpu

RMS_EPS = 1e-6
GRID_W = 64
EMB_DIM = 33
DECAY_TARGET = 1e-2
FAST_DECAY_PCT = 0.3
SLOW_DECAY_PCT = 1.5
MIN_DECAY = math.log(DECAY_TARGET) / FAST_DECAY_PCT
MAX_DECAY = math.log(DECAY_TARGET) / SLOW_DECAY_PCT
HEAD_DIM = 128
SCORE_SCALE = HEAD_DIM ** -0.5 * math.log2(math.e)
N_KV_HEADS = 4
ROPE_THETA = 10000.0
N_GROUPS = 4
EXPERTS_PER_GROUP = 8
N_MIXERS = 2

LANES = 128
BF16_SUBLANES = 16
VMEM_LIMIT_BYTES = 52 * 1024 * 1024
DFT_N2 = 128

BF16 = jnp.bfloat16
F32 = jnp.float32


def _params(sem):
    return pltpu.CompilerParams(dimension_semantics=sem, vmem_limit_bytes=VMEM_LIMIT_BYTES)


def _dot(a, b):
    return jnp.dot(a, b, preferred_element_type=F32)


def _norm_rows(x, g):
    ms = jnp.mean(x * x, axis=-1, keepdims=True)
    return x * lax.rsqrt(ms + RMS_EPS) * g


def _tile(n, pref):
    return pref if n % pref == 0 else n


def _matmul_res_kernel(a_ref, w_ref, r_ref, o_ref):
    o_ref[...] = r_ref[...] + _dot(a_ref[...].astype(BF16), w_ref[...])


def matmul_residual(a, w, res, *, tm=1024, tn=512):
    T, K = a.shape
    N = w.shape[1]
    tm = _tile(T, tm)
    tn = _tile(N, tn)
    return pl.pallas_call(
        _matmul_res_kernel,
        grid=(T // tm, N // tn),
        in_specs=[pl.BlockSpec((tm, K), lambda i, j: (i, 0)),
                  pl.BlockSpec((K, tn), lambda i, j: (0, j)),
                  pl.BlockSpec((tm, tn), lambda i, j: (i, j))],
        out_specs=pl.BlockSpec((tm, tn), lambda i, j: (i, j)),
        out_shape=jax.ShapeDtypeStruct((T, N), F32),
        compiler_params=_params(("parallel", "arbitrary")),
        name="matmul_residual",
    )(a, w, res)


HALO = BF16_SUBLANES


def _hy_inproj_kernel(x_ref, xp_ref, xn_ref, g_ref, w0_ref, w1_ref, w2_ref,
                      cw0_ref, cw1_ref, cw2_ref, cb0_ref, cb1_ref, cb2_ref,
                      x0_out, zv_out, hn_ref, *, tm, tiles_per_seq):
    i = pl.program_id(0)
    j = pl.program_id(1)

    @pl.when(j == 0)
    def _():
        g = g_ref[...]
        hn_ref[HALO:HALO + tm, :] = _norm_rows(x_ref[...], g).astype(BF16)
        first = (i % tiles_per_seq) == 0
        last = (i % tiles_per_seq) == tiles_per_seq - 1
        hp = _norm_rows(xp_ref[...], g)
        hn_ref[0:HALO, :] = jnp.where(first, 0.0, hp).astype(BF16)
        hx = _norm_rows(xn_ref[...], g)
        hn_ref[HALO + tm:, :] = jnp.where(last, 0.0, hx).astype(BF16)

    h = hn_ref[...]
    rows = tm + 2 * HALO

    def conv(w_ref, cw_ref, cb_ref):
        u = _dot(h, w_ref[...])
        um = pltpu.roll(u, 1, 0)[HALO:HALO + tm]
        up = pltpu.roll(u, rows - 1, 0)[HALO:HALO + tm]
        uc = u[HALO:HALO + tm]
        cw = cw_ref[...]
        return um * cw[0:1] + uc * cw[1:2] + up * cw[2:3] + cb_ref[...]

    x0_out[...] = conv(w0_ref, cw0_ref, cb0_ref)
    x1 = conv(w1_ref, cw1_ref, cb1_ref)
    v = conv(w2_ref, cw2_ref, cb2_ref)
    zv_out[...] = v * x1


def hyena_inproj(x, g, w_in, conv_w, conv_b, seq_len, *, tm=512, tn=512):
    T, D = x.shape
    W = w_in.shape[1] // 3
    tm = _tile(seq_len, tm)
    tn = _tile(W, tn)
    nj = W // tn
    hb = tm // HALO
    n_halo_blocks = T // HALO
    kern = functools.partial(_hy_inproj_kernel, tm=tm, tiles_per_seq=seq_len // tm)

    def wspec(c):
        return pl.BlockSpec((D, tn), lambda i, j, c=c: (0, j + c * nj))

    def cwspec(c):
        return pl.BlockSpec((3, tn), lambda i, j, c=c: (0, j + c * nj))

    def cbspec(c):
        return pl.BlockSpec((1, tn), lambda i, j, c=c: (0, j + c * nj))

    return pl.pallas_call(
        kern,
        grid=(T // tm, nj),
        in_specs=[pl.BlockSpec((tm, D), lambda i, j: (i, 0)),
                  pl.BlockSpec((HALO, D), lambda i, j: (jnp.maximum(i * hb - 1, 0), 0)),
                  pl.BlockSpec((HALO, D), lambda i, j: (jnp.minimum((i + 1) * hb, n_halo_blocks - 1), 0)),
                  pl.BlockSpec((1, D), lambda i, j: (0, 0)),
                  wspec(0), wspec(1), wspec(2),
                  cwspec(0), cwspec(1), cwspec(2),
                  cbspec(0), cbspec(1), cbspec(2)],
        out_specs=[pl.BlockSpec((tm, tn), lambda i, j: (i, j)),
                   pl.BlockSpec((tm, tn), lambda i, j: (i, j))],
        out_shape=[jax.ShapeDtypeStruct((T, W), F32), jax.ShapeDtypeStruct((T, W), F32)],
        scratch_shapes=[pltpu.VMEM((tm + 2 * HALO, D), BF16)],
        compiler_params=_params(("parallel", "arbitrary")),
        name="hyena_inproj",
    )(x, x, x, g, w_in, w_in, w_in, conv_w, conv_w, conv_w, conv_b, conv_b, conv_b)


def _filter_mlp_kernel(z_ref, w1_ref, b1_ref, win_ref, bin_ref, fr_ref, h_ref):
    hp = lax.Precision.HIGHEST
    fr = fr_ref[...]
    h = jnp.sin(fr * (jnp.dot(z_ref[...], w1_ref[...], precision=hp, preferred_element_type=F32) + b1_ref[...]))
    for l in range(win_ref.shape[0]):
        h = jnp.sin(fr * (jnp.dot(h, win_ref[l], precision=hp, preferred_element_type=F32) + bin_ref[l]))
    h_ref[...] = h


def _filter_expand_kernel(h_ref, t_ref, w3_ref, dl_ref, k_ref, s_ref, *, tl, n_fwd_tiles):
    j = pl.program_id(0)
    i = pl.program_id(1)
    taps = jnp.dot(h_ref[...], w3_ref[...], precision=lax.Precision.HIGHEST, preferred_element_type=F32)
    taps = taps * jnp.exp(-t_ref[...] * dl_ref[...])
    row = i * tl + lax.broadcasted_iota(jnp.int32, taps.shape, 0)
    taps = jnp.where(jnp.logical_and(j >= n_fwd_tiles, row == 0), 0.0, taps)
    k_ref[...] = taps.astype(k_ref.dtype)

    @pl.when(i == 0)
    def _():
        s_ref[...] = jnp.zeros_like(s_ref)

    s_ref[...] += jnp.sum(jnp.abs(taps), axis=0, keepdims=True)


def hyena_filter_taps(L, D, f_w1, f_b1, f_win, f_bin, f_freq, f_w3):
    H = f_w1.shape[1]
    t = jnp.linspace(0.0, 1.0, L, dtype=F32)[:, None]
    bands = (EMB_DIM - 1) // 2
    w = 2.0 * math.pi * jnp.arange(L, dtype=F32)[:, None] / L
    fb = jnp.linspace(1e-4, bands - 1, bands, dtype=F32)[None, :]
    z = jnp.concatenate([t, jnp.cos(fb * w), -jnp.sin(fb * w)], axis=-1)

    def lag_major_to_dft_order(a):
        return a.reshape(L // DFT_N2, DFT_N2, a.shape[1]).transpose(1, 0, 2).reshape(L, a.shape[1])

    z = lag_major_to_dft_order(z)
    t = lag_major_to_dft_order(t)
    zp = jnp.pad(z, ((0, 0), (0, LANES - EMB_DIM)))
    w1p = jnp.pad(f_w1, ((0, LANES - EMB_DIM), (0, 0)))
    n_in = f_win.shape[0]
    tl = _tile(L, 512)
    h = pl.pallas_call(
        _filter_mlp_kernel,
        grid=(L // tl,),
        in_specs=[pl.BlockSpec((tl, LANES), lambda i: (i, 0)),
                  pl.BlockSpec((LANES, H), lambda i: (0, 0)),
                  pl.BlockSpec((1, H), lambda i: (0, 0)),
                  pl.BlockSpec((n_in, H, H), lambda i: (0, 0, 0)),
                  pl.BlockSpec((n_in, 1, H), lambda i: (0, 0, 0)),
                  pl.BlockSpec((1, H), lambda i: (0, 0))],
        out_specs=pl.BlockSpec((tl, H), lambda i: (i, 0)),
        out_shape=jax.ShapeDtypeStruct((L, H), F32),
        compiler_params=_params(("parallel",)),
        name="filter_mlp",
    )(zp, w1p, f_b1[None, :], f_win, f_bin[:, None, :], f_freq[None, :])

    deltas = jnp.abs(jnp.linspace(MIN_DECAY, MAX_DECAY, D, dtype=F32))[None, :]
    deltas2 = jnp.concatenate([deltas, deltas], axis=1)
    tn = _tile(D, 1024)
    n_fwd = D // tn
    kern = functools.partial(_filter_expand_kernel, tl=tl, n_fwd_tiles=n_fwd)
    taps, asum = pl.pallas_call(
        kern,
        grid=(2 * n_fwd, L // tl),
        in_specs=[pl.BlockSpec((tl, H), lambda j, i: (i, 0)),
                  pl.BlockSpec((tl, 1), lambda j, i: (i, 0)),
                  pl.BlockSpec((H, tn), lambda j, i: (0, j)),
                  pl.BlockSpec((1, tn), lambda j, i: (0, j))],
        out_specs=[pl.BlockSpec((tl, tn), lambda j, i: (i, j)),
                   pl.BlockSpec((1, tn), lambda j, i: (0, j))],
        out_shape=[jax.ShapeDtypeStruct((L, 2 * D), F32), jax.ShapeDtypeStruct((1, 2 * D), F32)],
        compiler_params=_params(("parallel", "arbitrary")),
        name="filter_expand",
    )(h, t, f_w3, deltas2)
    return taps, asum


def dft_tables(N1, N2):
    N = N1 * N2
    half = N1 // 2
    k1 = jnp.arange(N1, dtype=jnp.int32)[None, :, None]
    n = (jnp.arange(half, dtype=jnp.int32)[None, None, :] * N2
         + jnp.arange(N2, dtype=jnp.int32)[:, None, None])
    ang = ((k1 * n) % N).astype(F32) * (2.0 * math.pi / N)
    c, s = jnp.cos(ang), jnp.sin(ang)
    fwd = jnp.concatenate([jnp.concatenate([c, s], axis=2),
                           jnp.concatenate([-s, c], axis=2)], axis=1)
    inv = jnp.swapaxes(fwd, 1, 2)
    fwd_real = jnp.concatenate([c, -s], axis=1)
    k2 = jnp.arange(N2, dtype=jnp.int32)
    ang2 = ((k2[:, None] * k2[None, :]) % N2).astype(F32) * (2.0 * math.pi / N2)
    c2, s2 = jnp.cos(ang2), jnp.sin(ang2)
    f2 = jnp.concatenate([jnp.concatenate([c2, s2], axis=1),
                          jnp.concatenate([-s2, c2], axis=1)], axis=0)
    g2 = f2.T
    return fwd.astype(BF16), inv.astype(BF16), fwd_real.astype(BF16), f2.astype(BF16), g2.astype(BF16)


def _pack_complex(re, im):
    rb = lax.bitcast_convert_type(re.astype(BF16).astype(F32), jnp.uint32)
    ib = lax.bitcast_convert_type(im.astype(BF16).astype(F32), jnp.uint32)
    return (rb >> 16) | (ib & jnp.uint32(0xFFFF0000))


def _unpack_complex(p):
    re = lax.bitcast_convert_type(p << 16, F32)
    im = lax.bitcast_convert_type(p & jnp.uint32(0xFFFF0000), F32)
    return jnp.concatenate([re, im], axis=0).astype(BF16)


def _double_buffered(in_copies, out_copies, compute):
    j = pl.program_id(0)
    n = pl.num_programs(0)
    slot = j % 2

    @pl.when(j == 0)
    def _():
        for cp in in_copies(0, 0):
            cp.start()

    @pl.when(j + 1 < n)
    def _():
        for cp in in_copies(j + 1, 1 - slot):
            cp.start()

    for cp in in_copies(j, slot):
        cp.wait()

    @pl.when(j >= 2)
    def _():
        for cp in out_copies(j - 2, slot):
            cp.wait()

    compute(slot)
    for cp in out_copies(j, slot):
        cp.start()

    @pl.when(j == n - 1)
    def _():
        for cp in out_copies(j, slot):
            cp.wait()

        @pl.when(n >= 2)
        def _():
            for cp in out_copies(j - 1, 1 - slot):
                cp.wait()


def _fft_s1_kernel(z_hbm, m_ref, a_hbm, zbuf, obuf, sem_in, sem_out):
    half = zbuf.shape[1] // 2
    n1 = obuf.shape[1]

    def in_copies(step, slot):
        return [pltpu.make_async_copy(z_hbm.at[b, :, step, :], zbuf.at[slot, pl.ds(b * half, half)], sem_in.at[slot])
                for b in range(2)]

    def out_copies(step, slot):
        return [pltpu.make_async_copy(obuf.at[slot], a_hbm.at[:, step, :], sem_out.at[slot])]

    def compute(slot):
        a = _dot(m_ref[...], zbuf[slot].astype(BF16))
        obuf[slot] = _pack_complex(a[:n1], a[n1:])

    _double_buffered(in_copies, out_copies, compute)


def _fft_s1_real_kernel(k_ref, m_ref, a_hbm, obuf, sem_out):
    n1 = obuf.shape[1]

    def out_copies(step, slot):
        return [pltpu.make_async_copy(obuf.at[slot], a_hbm.at[:, step, :], sem_out.at[slot])]

    def compute(slot):
        a = _dot(m_ref[...], k_ref[...].astype(BF16))
        obuf[slot] = _pack_complex(a[:n1], a[n1:])

    _double_buffered(lambda step, slot: [], out_copies, compute)


def _fft_mid_kernel(a_ref, af_ref, ab_ref, s_ref, f2_ref, g2_ref, c_ref):
    n2 = a_ref.shape[0]
    f2 = f2_ref[...]
    xf = _dot(f2, _unpack_complex(af_ref[...]))
    xb = _dot(f2, _unpack_complex(ab_ref[...]))
    s = s_ref[...]
    kr = (xf[:n2] + xb[:n2]) * s
    ki = (xf[n2:] - xb[n2:]) * s
    x = _dot(f2, _unpack_complex(a_ref[...]))
    xr, xi = x[:n2], x[n2:]
    y = jnp.concatenate([xr * kr - xi * ki, xr * ki + xi * kr], axis=0).astype(BF16)
    c = _dot(g2_ref[...], y)
    c_ref[...] = _pack_complex(c[:n2], c[n2:])


def _fft_out_kernel(c_hbm, m_ref, zv_hbm, x0_hbm, sk_ref, y_hbm, cbuf, zbuf, xbuf, ybuf, sem_in, sem_out):
    half = zbuf.shape[1] // 2

    def in_copies(step, slot):
        cps = [pltpu.make_async_copy(c_hbm.at[:, step, :], cbuf.at[slot], sem_in.at[slot])]
        for b in range(2):
            rows = pl.ds(b * half, half)
            cps.append(pltpu.make_async_copy(zv_hbm.at[b, :, step, :], zbuf.at[slot, rows], sem_in.at[slot]))
            cps.append(pltpu.make_async_copy(x0_hbm.at[b, :, step, :], xbuf.at[slot, rows], sem_in.at[slot]))
        return cps

    def out_copies(step, slot):
        return [pltpu.make_async_copy(ybuf.at[slot, pl.ds(b * half, half)], y_hbm.at[b, :, step, :], sem_out.at[slot])
                for b in range(2)]

    def compute(slot):
        y = _dot(m_ref[...], _unpack_complex(cbuf[slot]))
        ybuf[slot] = (y + zbuf[slot] * sk_ref[...]) * xbuf[slot]

    _double_buffered(in_copies, out_copies, compute)


def hyena_long_conv(zv, x0, taps, asum, skip_b, L):
    D = zv.shape[1]
    N = 2 * L
    N2 = DFT_N2
    N1 = N // N2
    half = N1 // 2
    fwd, inv, fwd_real, f2, g2 = dft_tables(N1, N2)
    td2 = _tile(D, 2048)
    nd2 = D // td2
    zv4 = zv.reshape(2, half, N2, D)
    x04 = x0.reshape(2, half, N2, D)
    any_spec = pl.BlockSpec(memory_space=pl.ANY)

    af = pl.pallas_call(
        _fft_s1_real_kernel,
        grid=(N2,),
        in_specs=[pl.BlockSpec((None, half, 2 * D), lambda j: (j, 0, 0)),
                  pl.BlockSpec((None, 2 * N1, half), lambda j: (j, 0, 0))],
        out_specs=any_spec,
        out_shape=jax.ShapeDtypeStruct((N1, N2, 2 * D), jnp.uint32),
        scratch_shapes=[pltpu.VMEM((2, N1, 2 * D), jnp.uint32), pltpu.SemaphoreType.DMA((2,))],
        compiler_params=_params(("arbitrary",)),
        name="filter_dft_stage1",
    )(taps.reshape(N2, half, 2 * D), fwd_real)
    scale = 1.0 / ((asum[:, :D] + asum[:, D:]) * N)

    a = pl.pallas_call(
        _fft_s1_kernel,
        grid=(N2,),
        in_specs=[any_spec,
                  pl.BlockSpec((None, 2 * N1, N1), lambda j: (j, 0, 0))],
        out_specs=any_spec,
        out_shape=jax.ShapeDtypeStruct((N1, N2, D), jnp.uint32),
        scratch_shapes=[pltpu.VMEM((2, N1, D), F32), pltpu.VMEM((2, N1, D), jnp.uint32),
                        pltpu.SemaphoreType.DMA((2,)), pltpu.SemaphoreType.DMA((2,))],
        compiler_params=_params(("arbitrary",)),
        name="conv_dft_stage1",
    )(zv4, fwd)

    c = pl.pallas_call(
        _fft_mid_kernel,
        grid=(N1, nd2),
        in_specs=[pl.BlockSpec((None, N2, td2), lambda k, j: (k, 0, j)),
                  pl.BlockSpec((None, N2, td2), lambda k, j: (k, 0, j)),
                  pl.BlockSpec((None, N2, td2), lambda k, j: (k, 0, j + nd2)),
                  pl.BlockSpec((1, td2), lambda k, j: (0, j)),
                  pl.BlockSpec((2 * N2, 2 * N2), lambda k, j: (0, 0)),
                  pl.BlockSpec((2 * N2, 2 * N2), lambda k, j: (0, 0))],
        out_specs=pl.BlockSpec((None, N2, td2), lambda k, j: (k, 0, j)),
        out_shape=jax.ShapeDtypeStruct((N1, N2, D), jnp.uint32),
        compiler_params=_params(("parallel", "parallel")),
        name="conv_dft_stage2",
    )(a, af, af, scale, f2, g2)

    y = pl.pallas_call(
        _fft_out_kernel,
        grid=(N2,),
        in_specs=[any_spec,
                  pl.BlockSpec((None, N1, 2 * N1), lambda j: (j, 0, 0)),
                  any_spec, any_spec,
                  pl.BlockSpec((1, D), lambda j: (0, 0))],
        out_specs=any_spec,
        out_shape=jax.ShapeDtypeStruct((2, half, N2, D), F32),
        scratch_shapes=[pltpu.VMEM((2, N1, D), jnp.uint32), pltpu.VMEM((2, N1, D), F32),
                        pltpu.VMEM((2, N1, D), F32), pltpu.VMEM((2, N1, D), F32),
                        pltpu.SemaphoreType.DMA((2,)), pltpu.SemaphoreType.DMA((2,))],
        compiler_params=_params(("arbitrary",)),
        name="conv_dft_inverse",
    )(c, inv, zv4, x04, skip_b)
    return y.reshape(2 * L, D)


def _qkv_kernel(x_ref, g_ref, w_ref, gain_ref, cos_ref, sin_ref, o_ref, va_ref, hn_ref, *, n_rope_tiles):
    j = pl.program_id(1)

    @pl.when(j == 0)
    def _():
        hn_ref[...] = _norm_rows(x_ref[...], g_ref[...]).astype(BF16)

    u = _dot(hn_ref[...], w_ref[...])
    tn = u.shape[1]

    @pl.when(j < n_rope_tiles)
    def _():
        gain = gain_ref[...]
        cos = cos_ref[...]
        sin = sin_ref[...]
        even = (lax.broadcasted_iota(jnp.int32, cos.shape, 1) % 2) == 0
        for hh in range(tn // HEAD_DIM):
            xh = u[:, hh * HEAD_DIM:(hh + 1) * HEAD_DIM]
            y = _norm_rows(xh, gain)
            sw = jnp.where(even, pltpu.roll(y, HEAD_DIM - 1, 1), pltpu.roll(y, 1, 1))
            o_ref[:, hh * HEAD_DIM:(hh + 1) * HEAD_DIM] = (y * cos + sw * sin).astype(o_ref.dtype)

    @pl.when(j >= n_rope_tiles)
    def _():
        o_ref[...] = u.astype(o_ref.dtype)
        ones = jnp.ones((u.shape[0], HEAD_DIM), va_ref.dtype)
        for hh in range(tn // HEAD_DIM):
            va_ref[:, 2 * hh * HEAD_DIM:(2 * hh + 1) * HEAD_DIM] = u[:, hh * HEAD_DIM:(hh + 1) * HEAD_DIM].astype(va_ref.dtype)
            va_ref[:, (2 * hh + 1) * HEAD_DIM:(2 * hh + 2) * HEAD_DIM] = ones


def rope_tables(L):
    rows = L // GRID_W
    r, c = jnp.meshgrid(jnp.arange(rows, dtype=F32), jnp.arange(GRID_W, dtype=F32), indexing="ij")
    r = r.reshape(L)
    c = c.reshape(L)
    axis_dim = HEAD_DIM // 2
    inv = 1.0 / (ROPE_THETA ** (jnp.arange(0, axis_dim, 2, dtype=F32) / axis_dim))
    ang = jnp.concatenate([r[:, None] * inv[None], c[:, None] * inv[None]], axis=-1)
    cos = jnp.repeat(jnp.cos(ang), 2, axis=1)
    sign = jnp.tile(jnp.array([-1.0, 1.0], F32), HEAD_DIM // 2)[None, :]
    sin = jnp.repeat(jnp.sin(ang), 2, axis=1) * sign
    return cos, sin


def qkv_project(x, g, w_qkv, q_gain, k_gain, seq_len, *, tm=512):
    T, D = x.shape
    n_out = w_qkv.shape[1]
    nk = N_KV_HEADS * HEAD_DIM
    nq = n_out - 2 * nk
    tn = nk
    tm = _tile(seq_len, tm)
    n_rope = (nq + nk) // tn
    gains = jnp.concatenate([jnp.tile((q_gain * SCORE_SCALE)[None, None, :], (nq // tn, 1, 1)),
                             k_gain[None, None, :],
                             jnp.ones((1, 1, HEAD_DIM), F32)], axis=0)
    cos, sin = rope_tables(seq_len)
    spt = seq_len // tm
    kern = functools.partial(_qkv_kernel, n_rope_tiles=n_rope)
    return pl.pallas_call(
        kern,
        grid=(T // tm, n_out // tn),
        in_specs=[pl.BlockSpec((tm, D), lambda i, j: (i, 0)),
                  pl.BlockSpec((1, D), lambda i, j: (0, 0)),
                  pl.BlockSpec((D, tn), lambda i, j: (0, j)),
                  pl.BlockSpec((None, 1, HEAD_DIM), lambda i, j: (j, 0, 0)),
                  pl.BlockSpec((tm, HEAD_DIM), lambda i, j: (i % spt, 0)),
                  pl.BlockSpec((tm, HEAD_DIM), lambda i, j: (i % spt, 0))],
        out_specs=[pl.BlockSpec((tm, tn), lambda i, j: (i, j)),
                   pl.BlockSpec((tm, 2 * nk), lambda i, j: (i, 0))],
        out_shape=[jax.ShapeDtypeStruct((T, n_out), BF16), jax.ShapeDtypeStruct((T, 2 * nk), BF16)],
        scratch_shapes=[pltpu.VMEM((tm, D), BF16)],
        compiler_params=_params(("parallel", "arbitrary")),
        name="qkv_project",
    )(x, g, w_qkv, gains, cos, sin)


def _flash_kernel(q_ref, k_ref, v_ref, o_ref, qs_ref, s0_ref, s1_ref, p0_ref, p1_ref, a0_ref, a1_ref,
                  m_ref, acc_ref, *, group, tq, tk, nk):
    s_refs = (s0_ref, s1_ref)
    p_refs = (p0_ref, p1_ref)
    a_refs = (a0_ref, a1_ref)
    for gq in range(group):
        qs_ref[gq * tq:(gq + 1) * tq, :] = q_ref[:, gq * HEAD_DIM:(gq + 1) * HEAD_DIM]
    m_ref[...] = jnp.full_like(m_ref, -jnp.inf)
    acc_ref[...] = jnp.zeros_like(acc_ref)

    def rows_of(j):
        if isinstance(j, int):
            return pl.ds(j * tk, tk)
        return pl.ds(pl.multiple_of(j * tk, tk), tk)

    def scores(j, slot):
        kb = k_ref[rows_of(j), :]
        s_refs[slot][...] = lax.dot_general(qs_ref[...], kb, (((1,), (1,)), ((), ())),
                                            preferred_element_type=F32)

    def softmax(slot):
        s = s_refs[slot][...]
        m_prev = m_ref[...]
        m_new = jnp.maximum(m_prev, jnp.max(s, axis=-1, keepdims=True))
        a_refs[slot][...] = jnp.exp2(m_prev - m_new)
        p_refs[slot][...] = jnp.exp2(s - jnp.tile(m_new, (1, tk // LANES))).astype(BF16)
        m_ref[...] = m_new

    def accumulate(j, slot):
        pv = _dot(p_refs[slot][...], v_ref[rows_of(j), :])
        acc_ref[...] = acc_ref[...] * jnp.tile(a_refs[slot][...], (1, 2)) + pv

    def step(j, slot, first, last):
        if not last:
            scores(j + 1, 1 - slot)
        softmax(slot)
        if not first:
            accumulate(j - 1, 1 - slot)

    scores(0, 0)
    if nk >= 4 and nk % 2 == 0:
        step(0, 0, True, False)

        def pair(t, c):
            j = 2 * t + 1
            step(j, 1, False, False)
            step(j + 1, 0, False, False)
            return c

        lax.fori_loop(0, (nk - 2) // 2, pair, 0)
        step(nk - 1, 1, False, True)
    else:
        for j in range(nk):
            step(j, j % 2, j == 0, j == nk - 1)
    accumulate(nk - 1, (nk - 1) % 2)

    acc = acc_ref[...]
    o = acc[:, :HEAD_DIM] / acc[:, HEAD_DIM:]
    for gq in range(group):
        o_ref[:, gq * HEAD_DIM:(gq + 1) * HEAD_DIM] = o[gq * tq:(gq + 1) * tq].astype(o_ref.dtype)


def _flash_bounded_kernel(q_ref, k_ref, v_ref, o_ref, qs_ref, acc_ref, *, group, tq, tk, nk, unroll):
    for gq in range(group):
        qs_ref[gq * tq:(gq + 1) * tq, :] = q_ref[:, gq * HEAD_DIM:(gq + 1) * HEAD_DIM]
    acc_ref[...] = jnp.zeros_like(acc_ref)

    def block(j, c):
        rows = pl.ds(pl.multiple_of(j * tk, tk), tk)
        s = lax.dot_general(qs_ref[...], k_ref[rows, :], (((1,), (1,)), ((), ())), preferred_element_type=F32)
        acc_ref[...] += _dot(jnp.exp2(s).astype(BF16), v_ref[rows, :])
        return c

    lax.fori_loop(0, nk, block, 0, unroll=unroll)
    acc = acc_ref[...]
    o = acc[:, :HEAD_DIM] / acc[:, HEAD_DIM:]
    for gq in range(group):
        o_ref[:, gq * HEAD_DIM:(gq + 1) * HEAD_DIM] = o[gq * tq:(gq + 1) * tq].astype(o_ref.dtype)


SAFE_SCORE_BOUND = 80.0


def flash_attention(qkv, v_aug, score_bound, B, L, n_heads, *, tq=256, tk=512):
    group = n_heads // N_KV_HEADS
    tq = _tile(L, tq)
    tk = _tile(L, tk)
    nk = L // tk
    qw = group * HEAD_DIM
    k_col = n_heads
    rows = group * tq
    common = dict(
        grid=(B, N_KV_HEADS, L // tq),
        in_specs=[pl.BlockSpec((None, tq, qw), lambda b, h, qi: (b, qi, h)),
                  pl.BlockSpec((None, L, HEAD_DIM), lambda b, h, qi: (b, 0, k_col + h)),
                  pl.BlockSpec((None, L, 2 * HEAD_DIM), lambda b, h, qi: (b, 0, h))],
        out_specs=pl.BlockSpec((None, tq, qw), lambda b, h, qi: (b, qi, h)),
        out_shape=jax.ShapeDtypeStruct((B, L, n_heads * HEAD_DIM), BF16),
        compiler_params=_params(("parallel", "parallel", "parallel")),
    )
    general = pl.pallas_call(
        functools.partial(_flash_kernel, group=group, tq=tq, tk=tk, nk=nk),
        scratch_shapes=[pltpu.VMEM((rows, HEAD_DIM), BF16),
                        pltpu.VMEM((rows, tk), F32), pltpu.VMEM((rows, tk), F32),
                        pltpu.VMEM((rows, tk), BF16), pltpu.VMEM((rows, tk), BF16),
                        pltpu.VMEM((rows, LANES), F32), pltpu.VMEM((rows, LANES), F32),
                        pltpu.VMEM((rows, LANES), F32),
                        pltpu.VMEM((rows, 2 * HEAD_DIM), F32)],
        name="flash_attention", **common)
    bounded = pl.pallas_call(
        functools.partial(_flash_bounded_kernel, group=group, tq=tq, tk=tk, nk=nk,
                          unroll=math.gcd(nk, 8)),
        scratch_shapes=[pltpu.VMEM((rows, HEAD_DIM), BF16),
                        pltpu.VMEM((rows, 2 * HEAD_DIM), F32)],
        name="flash_attention_bounded", **common)
    return lax.cond(score_bound <= SAFE_SCORE_BOUND, bounded, general, qkv, qkv, v_aug)


def _pack_bf16_pairs(x):
    half = x.shape[1] // 2
    bits = lax.bitcast_convert_type(x.astype(BF16).astype(F32), jnp.uint32)
    return (bits[:, :half] >> 16) | (bits[:, half:] & jnp.uint32(0xFFFF0000))


def _unpack_bf16_pairs(p):
    lo = lax.bitcast_convert_type(p << 16, F32).astype(BF16)
    hi = lax.bitcast_convert_type(p & jnp.uint32(0xFFFF0000), F32).astype(BF16)
    return lo, hi


def _router_kernel(xa_ref, xb_ref, g_ref, rw_ref, h_ref, info_ref, cnt_ref, *, n_a_tiles):
    i = pl.program_id(0)

    def route(x_ref):
        hf = _norm_rows(x_ref[...], g_ref[...])
        h_ref[...] = _pack_bf16_pairs(hf)
        logits = jnp.dot(hf, rw_ref[...], precision=lax.Precision.HIGHEST, preferred_element_type=F32)
        shape = logits.shape
        lane = lax.broadcasted_iota(jnp.int32, shape, 1)
        lanef = lane.astype(F32)
        neg = -jnp.inf
        big = float(LANES)
        is_g = lane < N_GROUPS
        gl = jnp.where(is_g, logits, neg)
        gmax = jnp.max(gl, axis=-1, keepdims=True)
        gidx = jnp.min(jnp.where(gl == gmax, lanef, big), axis=-1, keepdims=True)
        p_g = 1.0 / jnp.sum(jnp.where(is_g, jnp.exp(logits - gmax), 0.0), axis=-1, keepdims=True)
        lo = N_GROUPS + gidx * EXPERTS_PER_GROUP
        el = jnp.where(lanef >= lo, jnp.where(lanef < lo + EXPERTS_PER_GROUP, logits, neg), neg)
        v1 = jnp.max(el, axis=-1, keepdims=True)
        i1 = jnp.min(jnp.where(el == v1, lanef, big), axis=-1, keepdims=True)
        el2 = jnp.where(lanef == i1, neg, el)
        v2 = jnp.max(el2, axis=-1, keepdims=True)
        i2 = jnp.min(jnp.where(el2 == v2, lanef, big), axis=-1, keepdims=True)
        t = jnp.exp(v2 - v1)
        w1 = p_g / (1.0 + t)
        w2 = p_g * t / (1.0 + t)
        e1 = i1 - N_GROUPS
        e2 = i2 - N_GROUPS
        oh1 = jnp.where(lanef == e1, 1.0, 0.0)
        oh2 = jnp.where(lanef == e2, 1.0, 0.0)
        oh = oh1 + oh2
        tm = shape[0]
        tri = jnp.where(lax.broadcasted_iota(jnp.int32, (tm, tm), 1) < lax.broadcasted_iota(jnp.int32, (tm, tm), 0),
                        1.0, 0.0).astype(BF16)
        before = _dot(tri, oh.astype(BF16)) + cnt_ref[...]
        r1 = jnp.sum(before * oh1, axis=-1, keepdims=True)
        r2 = jnp.sum(before * oh2, axis=-1, keepdims=True)
        cnt_ref[...] += jnp.sum(oh, axis=0, keepdims=True)
        vals = (e1, e2, w1, w2, r1, r2)
        info = jnp.zeros(shape, F32)
        for idx, val in enumerate(vals):
            info = jnp.where(lane == idx, val, info)
        info_ref[...] = info

    @pl.when(i == 0)
    def _():
        cnt_ref[...] = jnp.zeros_like(cnt_ref)

    @pl.when(i < n_a_tiles)
    def _():
        route(xa_ref)

    @pl.when(i >= n_a_tiles)
    def _():
        route(xb_ref)


def moe_route(xa, xb, g, router_group, router_expert, *, tm=512):
    Ta, D = xa.shape
    Tb = xb.shape[0]
    tm = _tile(math.gcd(Ta, Tb), tm)
    na, nb = Ta // tm, Tb // tm
    n_e = N_GROUPS * EXPERTS_PER_GROUP
    rw = jnp.concatenate([router_group, jnp.transpose(router_expert, (1, 0, 2)).reshape(D, n_e)], axis=1)
    rw = jnp.pad(rw, ((0, 0), (0, LANES - rw.shape[1])))
    kern = functools.partial(_router_kernel, n_a_tiles=na)
    return pl.pallas_call(
        kern,
        grid=(na + nb,),
        in_specs=[pl.BlockSpec((tm, D), lambda i: (jnp.minimum(i, na - 1), 0)),
                  pl.BlockSpec((tm, D), lambda i: (jnp.maximum(i - na, 0), 0)),
                  pl.BlockSpec((1, D), lambda i: (0, 0)),
                  pl.BlockSpec((D, LANES), lambda i: (0, 0))],
        out_specs=[pl.BlockSpec((tm, D // 2), lambda i: (i, 0)),
                   pl.BlockSpec((tm, LANES), lambda i: (i, 0)),
                   pl.BlockSpec((1, LANES), lambda i: (0, 0))],
        out_shape=[jax.ShapeDtypeStruct((Ta + Tb, D // 2), jnp.uint32),
                   jax.ShapeDtypeStruct((Ta + Tb, LANES), F32),
                   jax.ShapeDtypeStruct((1, LANES), F32)],
        compiler_params=_params(("arbitrary",)),
        name="moe_route",
    )(xa, xb, g, rw)


def moe_plan(info, cnt, tm, n_experts):
    T = info.shape[0]
    n_tiles = (2 * T) // tm + n_experts
    counts = cnt[0, :n_experts].astype(jnp.int32)
    padded = ((counts + tm - 1) // tm) * tm
    ends = jnp.cumsum(padded)
    off = ends - padded
    e = info[:, 0:2].astype(jnp.int32)
    rank = info[:, 4:6].astype(jnp.int32)
    pos = (off[e] + rank).reshape(-1)
    tile_start = jnp.arange(n_tiles, dtype=jnp.int32) * tm
    tile_e = jnp.sum((tile_start[:, None] >= ends[None, :]).astype(jnp.int32), axis=1)
    tile_e = jnp.minimum(tile_e, n_experts - 1)
    tile_valid = jnp.clip(counts[tile_e] - (tile_start - off[tile_e]), 0, tm).astype(jnp.int32)
    n_used = (ends[-1:] // tm).astype(jnp.int32)
    return pos, tile_e, tile_valid, n_used


DMA_UNROLL = 8


def _dispatch_kernel(pos_ref, h_ref, zeros_hbm, hs_hbm, sem):
    del zeros_hbm
    tm = h_ref.shape[0]

    def row_copy(r, slot):
        return pltpu.make_async_copy(h_ref.at[pl.ds(r, 1)], hs_hbm.at[pl.ds(pos_ref[0, 0, 2 * r + slot], 1)], sem)

    def start(r, c):
        row_copy(r, 0).start()
        row_copy(r, 1).start()
        return c

    def wait(r, c):
        row_copy(r, 0).wait()
        row_copy(r, 1).wait()
        return c

    lax.fori_loop(0, tm, start, 0, unroll=DMA_UNROLL)
    lax.fori_loop(0, tm, wait, 0, unroll=DMA_UNROLL)


def moe_dispatch(h, pos, n_sorted_rows, *, tm=512):
    T, C = h.shape
    tm = _tile(T, tm)
    return pl.pallas_call(
        _dispatch_kernel,
        grid=(T // tm,),
        in_specs=[pl.BlockSpec((1, 1, 2 * tm), lambda i: (i, 0, 0), memory_space=pltpu.SMEM),
                  pl.BlockSpec((tm, C), lambda i: (i, 0)),
                  pl.BlockSpec(memory_space=pl.ANY)],
        out_specs=pl.BlockSpec(memory_space=pl.ANY),
        out_shape=jax.ShapeDtypeStruct((n_sorted_rows, C), h.dtype),
        scratch_shapes=[pltpu.SemaphoreType.DMA],
        input_output_aliases={2: 0},
        compiler_params=_params(("arbitrary",)),
        name="moe_dispatch",
    )(pos.reshape(T // tm, 1, 2 * tm), h, jnp.zeros((n_sorted_rows, C), h.dtype))


def _expert_kernel(te_ref, tv_ref, nu_ref, x_ref, wg_ref, wu_ref, wd_ref, y_ref):
    del te_ref, nu_ref
    valid = tv_ref[pl.program_id(0)]

    @pl.when(valid == 0)
    def _():
        y_ref[...] = jnp.zeros_like(y_ref)

    @pl.when(valid > 0)
    def _():
        x_lo, x_hi = _unpack_bf16_pairs(x_ref[...])
        half = x_ref.shape[1]
        gate = _dot(x_lo, wg_ref[:half, :]) + _dot(x_hi, wg_ref[half:, :])
        up = _dot(x_lo, wu_ref[:half, :]) + _dot(x_hi, wu_ref[half:, :])
        a = (gate * jax.nn.sigmoid(gate) * up).astype(BF16)
        y_ref[...] = _dot(a, wd_ref[...])


def moe_experts(hs, tile_e, tile_valid, n_used, w_gate, w_up, w_down, tm):
    P = hs.shape[0]
    D = w_gate.shape[1]
    d_e = w_gate.shape[-1]

    def row_map(i, te, tv, nu):
        return (jnp.minimum(i, nu[0] - 1), 0)

    def out_map(i, te, tv, nu):
        return (i, 0)

    def w_map(i, te, tv, nu):
        return (te[i], 0, 0)

    grid_spec = pltpu.PrefetchScalarGridSpec(
        num_scalar_prefetch=3,
        grid=(P // tm,),
        in_specs=[pl.BlockSpec((tm, D // 2), row_map),
                  pl.BlockSpec((None, D, d_e), w_map),
                  pl.BlockSpec((None, D, d_e), w_map),
                  pl.BlockSpec((None, d_e, D), w_map)],
        out_specs=pl.BlockSpec((tm, D), out_map),
    )
    return pl.pallas_call(
        _expert_kernel,
        grid_spec=grid_spec,
        out_shape=jax.ShapeDtypeStruct((P, D), F32),
        compiler_params=_params(("arbitrary",)),
        name="moe_experts",
    )(tile_e, tile_valid, n_used, hs, w_gate, w_up, w_down)


def _combine_kernel(pos_ref, x_ref, info_ref, g_ref, y_hbm, o_ref, ybuf, sem, *, final_norm):
    tm = x_ref.shape[0]

    def row_copy(r, slot):
        return pltpu.make_async_copy(y_hbm.at[pl.ds(pos_ref[0, 0, 2 * r + slot], 1)],
                                     ybuf.at[slot, pl.ds(r, 1)], sem)

    def start(r, c):
        row_copy(r, 0).start()
        row_copy(r, 1).start()
        return c

    def wait(r, c):
        row_copy(r, 0).wait()
        row_copy(r, 1).wait()
        return c

    lax.fori_loop(0, tm, start, 0, unroll=DMA_UNROLL)
    lax.fori_loop(0, tm, wait, 0, unroll=DMA_UNROLL)
    info = info_ref[...]
    y = info[:, 2:3] * ybuf[0] + info[:, 3:4] * ybuf[1]
    x = x_ref[...] + y
    if final_norm:
        x = _norm_rows(x, g_ref[...])
    o_ref[...] = x


def moe_combine(x, y, pos, info, row_off, g_final, final_norm, *, tm=256):
    Tg, D = x.shape
    T = info.shape[0]
    tm = _tile(math.gcd(Tg, row_off) if row_off else Tg, tm)
    ob = row_off // tm
    kern = functools.partial(_combine_kernel, final_norm=final_norm)
    return pl.pallas_call(
        kern,
        grid=(Tg // tm,),
        in_specs=[pl.BlockSpec((1, 1, 2 * tm), lambda i: (i + ob, 0, 0), memory_space=pltpu.SMEM),
                  pl.BlockSpec((tm, D), lambda i: (i, 0)),
                  pl.BlockSpec((tm, LANES), lambda i: (i + ob, 0)),
                  pl.BlockSpec((1, D), lambda i: (0, 0)),
                  pl.BlockSpec(memory_space=pl.ANY)],
        out_specs=pl.BlockSpec((tm, D), lambda i: (i, 0)),
        out_shape=jax.ShapeDtypeStruct((Tg, D), F32),
        scratch_shapes=[pltpu.VMEM((2, tm, D), F32), pltpu.SemaphoreType.DMA],
        compiler_params=_params(("arbitrary",)),
        name="moe_combine",
    )(pos.reshape(T // tm, 1, 2 * tm), x, info, g_final, y)


MOE_TILE = 512


def hier_moe_block(xs, g, router_group, router_expert, w_gate, w_up, w_down, g_final, final_norm):
    xa, xb = xs
    n_experts = w_gate.shape[0]
    h, info, cnt = moe_route(xa, xb, g, router_group, router_expert)
    T = h.shape[0]
    tm = _tile(2 * T, MOE_TILE)
    pos, tile_e, tile_valid, n_used = moe_plan(info, cnt, tm, n_experts)
    hs = moe_dispatch(h, pos, tile_e.shape[0] * tm)
    y = moe_experts(hs, tile_e, tile_valid, n_used, w_gate, w_up, w_down, tm)
    oa = moe_combine(xa, y, pos, info, 0, g_final, final_norm)
    ob = moe_combine(xb, y, pos, info, xa.shape[0], g_final, final_norm)
    return oa, ob


def hyena_block(x, L, g, w_in, conv_w, conv_b, f_w1, f_b1, f_win, f_bin, f_freq, f_w3, skip_b, w_out):
    D = x.shape[1]
    x0, zv = hyena_inproj(x, g, w_in, conv_w, conv_b[None, :], L)
    taps, asum = hyena_filter_taps(L, w_out.shape[0], f_w1, f_b1, f_win, f_bin, f_freq, f_w3)
    y = hyena_long_conv(zv, x0, taps, asum, skip_b[None, :], L)
    return matmul_residual(y, w_out, x)


def attention_block(x, B, L, g, w_qkv, q_gain, k_gain, w_o):
    D = x.shape[1]
    n_heads = w_o.shape[0] // HEAD_DIM
    qkv, v_aug = qkv_project(x, g, w_qkv, q_gain, k_gain, L)
    score_bound = 1.02 * HEAD_DIM * SCORE_SCALE * jnp.max(jnp.abs(q_gain)) * jnp.max(jnp.abs(k_gain))
    o = flash_attention(qkv.reshape(B, L, qkv.shape[1]), v_aug.reshape(B, L, v_aug.shape[1]), score_bound,
                        B, L, n_heads)
    return matmul_residual(o.reshape(B * L, n_heads * HEAD_DIM), w_o, x)


def kernel(x_prompt, x_sample, norm_mix, norm_ffn, norm_final, hy_w_in, hy_conv_w, hy_conv_b, hy_f_w1, hy_f_b1, hy_f_win, hy_f_bin, hy_f_freq, hy_f_w3, hy_skip_b, hy_w_out, at_w_qkv, at_q_gain, at_k_gain, at_w_o, moe_router_group, moe_router_expert, moe_w_gate, moe_w_up, moe_w_down):
    D = x_prompt.shape[-1]
    depth = norm_mix.shape[0]
    shapes = [x_prompt.shape, x_sample.shape]
    for s in shapes:
        assert s[0] == 2, "the long convolution packs exactly two sequences per group"
    xs = [x_prompt.reshape(-1, D), x_sample.reshape(-1, D)]
    hy_w_in_b = hy_w_in.astype(BF16)
    hy_w_out_b = hy_w_out.astype(BF16)
    at_w_qkv_b = at_w_qkv.astype(BF16)
    at_w_o_b = at_w_o.astype(BF16)
    w_gate_b = moe_w_gate.astype(BF16)
    w_up_b = moe_w_up.astype(BF16)
    w_down_b = moe_w_down.astype(BF16)
    g_final = norm_final[None, :]
    for i in range(depth):
        j = i // N_MIXERS
        g = norm_mix[i][None, :]
        if i % N_MIXERS == 0:
            xs = [hyena_block(x, s[1], g, hy_w_in_b[j], hy_conv_w[j], hy_conv_b[j], hy_f_w1[j], hy_f_b1[j],
                              hy_f_win[j], hy_f_bin[j], hy_f_freq[j], hy_f_w3[j], hy_skip_b[j], hy_w_out_b[j])
                  for x, s in zip(xs, shapes)]
        else:
            xs = [attention_block(x, s[0], s[1], g, at_w_qkv_b[j], at_q_gain[j], at_k_gain[j], at_w_o_b[j])
                  for x, s in zip(xs, shapes)]
        xs = hier_moe_block(xs, norm_ffn[i][None, :], moe_router_group[i], moe_router_expert[i],
                            w_gate_b[i], w_up_b[i], w_down_b[i], g_final, i == depth - 1)
    return (xs[0].reshape(shapes[0]), xs[1].reshape(shapes[1]))
```

```python
import functools
import math

import jax
import jax.numpy as jnp
import numpy as np
from jax import lax
from jax.experimental import pallas as pl
from jax.experimental.pallas import tpu as pltpu

RMS_EPS = 1e-6
GRID_W = 64
EMB_DIM = 33
DECAY_TARGET = 1e-2
FAST_DECAY_PCT = 0.3
SLOW_DECAY_PCT = 1.5
MIN_DECAY = math.log(DECAY_TARGET) / FAST_DECAY_PCT
MAX_DECAY = math.log(DECAY_TARGET) / SLOW_DECAY_PCT
HEAD_DIM = 128
SCORE_SCALE = HEAD_DIM ** -0.5 * math.log2(math.e)
N_KV_HEADS = 4
ROPE_THETA = 10000.0
N_GROUPS = 4
EXPERTS_PER_GROUP = 8
N_MIXERS = 2

LANES = 128
BF16_SUBLANES = 16
VMEM_LIMIT_BYTES = 52 * 1024 * 1024
DFT_N2 = 128

BF16 = jnp.bfloat16
F32 = jnp.float32


def _params(sem):
    return pltpu.CompilerParams(dimension_semantics=sem, vmem_limit_bytes=VMEM_LIMIT_BYTES)


def _dot(a, b):
    return jnp.dot(a, b, preferred_element_type=F32)


def _norm_rows(x, g):
    ms = jnp.mean(x * x, axis=-1, keepdims=True)
    return x * lax.rsqrt(ms + RMS_EPS) * g


def _tile(n, pref):
    return pref if n % pref == 0 else n


def _matmul_res_kernel(a_ref, w_ref, r_ref, o_ref):
    o_ref[...] = r_ref[...] + _dot(a_ref[...].astype(BF16), w_ref[...])


def matmul_residual(a, w, res, *, tm=1024, tn=512):
    T, K = a.shape
    N = w.shape[1]
    tm = _tile(T, tm)
    tn = _tile(N, tn)
    return pl.pallas_call(
        _matmul_res_kernel,
        grid=(T // tm, N // tn),
        in_specs=[pl.BlockSpec((tm, K), lambda i, j: (i, 0)),
                  pl.BlockSpec((K, tn), lambda i, j: (0, j)),
                  pl.BlockSpec((tm, tn), lambda i, j: (i, j))],
        out_specs=pl.BlockSpec((tm, tn), lambda i, j: (i, j)),
        out_shape=jax.ShapeDtypeStruct((T, N), F32),
        compiler_params=_params(("parallel", "arbitrary")),
        name="matmul_residual",
    )(a, w, res)


HALO = BF16_SUBLANES


def _hy_inproj_kernel(x_ref, xp_ref, xn_ref, g_ref, w0_ref, w1_ref, w2_ref,
                      cw0_ref, cw1_ref, cw2_ref, cb0_ref, cb1_ref, cb2_ref,
                      x0_out, zv_out, hn_ref, *, tm, tiles_per_seq):
    i = pl.program_id(0)
    j = pl.program_id(1)

    @pl.when(j == 0)
    def _():
        g = g_ref[...]
        hn_ref[HALO:HALO + tm, :] = _norm_rows(x_ref[...], g).astype(BF16)
        first = (i % tiles_per_seq) == 0
        last = (i % tiles_per_seq) == tiles_per_seq - 1
        hp = _norm_rows(xp_ref[...], g)
        hn_ref[0:HALO, :] = jnp.where(first, 0.0, hp).astype(BF16)
        hx = _norm_rows(xn_ref[...], g)
        hn_ref[HALO + tm:, :] = jnp.where(last, 0.0, hx).astype(BF16)

    h = hn_ref[...]
    rows = tm + 2 * HALO

    def conv(w_ref, cw_ref, cb_ref):
        u = _dot(h, w_ref[...])
        um = pltpu.roll(u, 1, 0)[HALO:HALO + tm]
        up = pltpu.roll(u, rows - 1, 0)[HALO:HALO + tm]
        uc = u[HALO:HALO + tm]
        cw = cw_ref[...]
        return um * cw[0:1] + uc * cw[1:2] + up * cw[2:3] + cb_ref[...]

    x0_out[...] = conv(w0_ref, cw0_ref, cb0_ref)
    x1 = conv(w1_ref, cw1_ref, cb1_ref)
    v = conv(w2_ref, cw2_ref, cb2_ref)
    zv_out[...] = v * x1


def hyena_inproj(x, g, w_in, conv_w, conv_b, seq_len, *, tm=512, tn=512):
    T, D = x.shape
    W = w_in.shape[1] // 3
    tm = _tile(seq_len, tm)
    tn = _tile(W, tn)
    nj = W // tn
    hb = tm // HALO
    n_halo_blocks = T // HALO
    kern = functools.partial(_hy_inproj_kernel, tm=tm, tiles_per_seq=seq_len // tm)

    def wspec(c):
        return pl.BlockSpec((D, tn), lambda i, j, c=c: (0, j + c * nj))

    def cwspec(c):
        return pl.BlockSpec((3, tn), lambda i, j, c=c: (0, j + c * nj))

    def cbspec(c):
        return pl.BlockSpec((1, tn), lambda i, j, c=c: (0, j + c * nj))

    return pl.pallas_call(
        kern,
        grid=(T // tm, nj),
        in_specs=[pl.BlockSpec((tm, D), lambda i, j: (i, 0)),
                  pl.BlockSpec((HALO, D), lambda i, j: (jnp.maximum(i * hb - 1, 0), 0)),
                  pl.BlockSpec((HALO, D), lambda i, j: (jnp.minimum((i + 1) * hb, n_halo_blocks - 1), 0)),
                  pl.BlockSpec((1, D), lambda i, j: (0, 0)),
                  wspec(0), wspec(1), wspec(2),
                  cwspec(0), cwspec(1), cwspec(2),
                  cbspec(0), cbspec(1), cbspec(2)],
        out_specs=[pl.BlockSpec((tm, tn), lambda i, j: (i, j)),
                   pl.BlockSpec((tm, tn), lambda i, j: (i, j))],
        out_shape=[jax.ShapeDtypeStruct((T, W), F32), jax.ShapeDtypeStruct((T, W), F32)],
        scratch_shapes=[pltpu.VMEM((tm + 2 * HALO, D), BF16)],
        compiler_params=_params(("parallel", "arbitrary")),
        name="hyena_inproj",
    )(x, x, x, g, w_in, w_in, w_in, conv_w, conv_w, conv_w, conv_b, conv_b, conv_b)


def _filter_mlp_kernel(z_ref, w1_ref, b1_ref, win_ref, bin_ref, fr_ref, h_ref):
    hp = lax.Precision.HIGHEST
    fr = fr_ref[...]
    h = jnp.sin(fr * (jnp.dot(z_ref[...], w1_ref[...], precision=hp, preferred_element_type=F32) + b1_ref[...]))
    for l in range(win_ref.shape[0]):
        h = jnp.sin(fr * (jnp.dot(h, win_ref[l], precision=hp, preferred_element_type=F32) + bin_ref[l]))
    h_ref[...] = h


def _filter_expand_kernel(h_ref, t_ref, w3_ref, dl_ref, k_ref, s_ref, *, tl, n_fwd_tiles):
    j = pl.program_id(0)
    i = pl.program_id(1)
    taps = jnp.dot(h_ref[...], w3_ref[...], precision=lax.Precision.HIGHEST, preferred_element_type=F32)
    taps = taps * jnp.exp(-t_ref[...] * dl_ref[...])
    row = i * tl + lax.broadcasted_iota(jnp.int32, taps.shape, 0)
    taps = jnp.where(jnp.logical_and(j >= n_fwd_tiles, row == 0), 0.0, taps)
    k_ref[...] = taps.astype(k_ref.dtype)

    @pl.when(i == 0)
    def _():
        s_ref[...] = jnp.zeros_like(s_ref)

    s_ref[...] += jnp.sum(jnp.abs(taps), axis=0, keepdims=True)


def hyena_filter_taps(L, D, f_w1, f_b1, f_win, f_bin, f_freq, f_w3):
    H = f_w1.shape[1]
    t = jnp.linspace(0.0, 1.0, L, dtype=F32)[:, None]
    bands = (EMB_DIM - 1) // 2
    w = 2.0 * math.pi * jnp.arange(L, dtype=F32)[:, None] / L
    fb = jnp.linspace(1e-4, bands - 1, bands, dtype=F32)[None, :]
    z = jnp.concatenate([t, jnp.cos(fb * w), -jnp.sin(fb * w)], axis=-1)

    def lag_major_to_dft_order(a):
        return a.reshape(L // DFT_N2, DFT_N2, a.shape[1]).transpose(1, 0, 2).reshape(L, a.shape[1])

    z = lag_major_to_dft_order(z)
    t = lag_major_to_dft_order(t)
    zp = jnp.pad(z, ((0, 0), (0, LANES - EMB_DIM)))
    w1p = jnp.pad(f_w1, ((0, LANES - EMB_DIM), (0, 0)))
    n_in = f_win.shape[0]
    tl = _tile(L, 512)
    h = pl.pallas_call(
        _filter_mlp_kernel,
        grid=(L // tl,),
        in_specs=[pl.BlockSpec((tl, LANES), lambda i: (i, 0)),
                  pl.BlockSpec((LANES, H), lambda i: (0, 0)),
                  pl.BlockSpec((1, H), lambda i: (0, 0)),
                  pl.BlockSpec((n_in, H, H), lambda i: (0, 0, 0)),
                  pl.BlockSpec((n_in, 1, H), lambda i: (0, 0, 0)),
                  pl.BlockSpec((1, H), lambda i: (0, 0))],
        out_specs=pl.BlockSpec((tl, H), lambda i: (i, 0)),
        out_shape=jax.ShapeDtypeStruct((L, H), F32),
        compiler_params=_params(("parallel",)),
        name="filter_mlp",
    )(zp, w1p, f_b1[None, :], f_win, f_bin[:, None, :], f_freq[None, :])

    deltas = jnp.abs(jnp.linspace(MIN_DECAY, MAX_DECAY, D, dtype=F32))[None, :]
    deltas2 = jnp.concatenate([deltas, deltas], axis=1)
    tn = _tile(D, 1024)
    n_fwd = D // tn
    kern = functools.partial(_filter_expand_kernel, tl=tl, n_fwd_tiles=n_fwd)
    taps, asum = pl.pallas_call(
        kern,
        grid=(2 * n_fwd, L // tl),
        in_specs=[pl.BlockSpec((tl, H), lambda j, i: (i, 0)),
                  pl.BlockSpec((tl, 1), lambda j, i: (i, 0)),
                  pl.BlockSpec((H, tn), lambda j, i: (0, j)),
                  pl.BlockSpec((1, tn), lambda j, i: (0, j))],
        out_specs=[pl.BlockSpec((tl, tn), lambda j, i: (i, j)),
                   pl.BlockSpec((1, tn), lambda j, i: (0, j))],
        out_shape=[jax.ShapeDtypeStruct((L, 2 * D), F32), jax.ShapeDtypeStruct((1, 2 * D), F32)],
        compiler_params=_params(("parallel", "arbitrary")),
        name="filter_expand",
    )(h, t, f_w3, deltas2)
    return taps, asum


def dft_tables(N1, N2):
    N = N1 * N2
    half = N1 // 2
    k1 = jnp.arange(N1, dtype=jnp.int32)[None, :, None]
    n = (jnp.arange(half, dtype=jnp.int32)[None, None, :] * N2
         + jnp.arange(N2, dtype=jnp.int32)[:, None, None])
    ang = ((k1 * n) % N).astype(F32) * (2.0 * math.pi / N)
    c, s = jnp.cos(ang), jnp.sin(ang)
    fwd = jnp.concatenate([jnp.concatenate([c, s], axis=2),
                           jnp.concatenate([-s, c], axis=2)], axis=1)
    inv = jnp.swapaxes(fwd, 1, 2)
    fwd_real = jnp.concatenate([c, -s], axis=1)
    k2 = jnp.arange(N2, dtype=jnp.int32)
    ang2 = ((k2[:, None] * k2[None, :]) % N2).astype(F32) * (2.0 * math.pi / N2)
    c2, s2 = jnp.cos(ang2), jnp.sin(ang2)
    f2 = jnp.concatenate([jnp.concatenate([c2, s2], axis=1),
                          jnp.concatenate([-s2, c2], axis=1)], axis=0)
    g2 = f2.T
    return fwd.astype(BF16), inv.astype(BF16), fwd_real.astype(BF16), f2.astype(BF16), g2.astype(BF16)


def _pack_complex(re, im):
    rb = lax.bitcast_convert_type(re.astype(BF16).astype(F32), jnp.uint32)
    ib = lax.bitcast_convert_type(im.astype(BF16).astype(F32), jnp.uint32)
    return (rb >> 16) | (ib & jnp.uint32(0xFFFF0000))


def _unpack_complex(p):
    re = lax.bitcast_convert_type(p << 16, F32)
    im = lax.bitcast_convert_type(p & jnp.uint32(0xFFFF0000), F32)
    return jnp.concatenate([re, im], axis=0).astype(BF16)


def _double_buffered(in_copies, out_copies, compute):
    j = pl.program_id(0)
    n = pl.num_programs(0)
    slot = j % 2

    @pl.when(j == 0)
    def _():
        for cp in in_copies(0, 0):
            cp.start()

    @pl.when(j + 1 < n)
    def _():
        for cp in in_copies(j + 1, 1 - slot):
            cp.start()

    for cp in in_copies(j, slot):
        cp.wait()

    @pl.when(j >= 2)
    def _():
        for cp in out_copies(j - 2, slot):
            cp.wait()

    compute(slot)
    for cp in out_copies(j, slot):
        cp.start()

    @pl.when(j == n - 1)
    def _():
        for cp in out_copies(j, slot):
            cp.wait()

        @pl.when(n >= 2)
        def _():
            for cp in out_copies(j - 1, 1 - slot):
                cp.wait()


def _fft_s1_kernel(z_hbm, m_ref, a_hbm, zbuf, obuf, sem_in, sem_out):
    half = zbuf.shape[1] // 2
    n1 = obuf.shape[1]

    def in_copies(step, slot):
        return [pltpu.make_async_copy(z_hbm.at[b, :, step, :], zbuf.at[slot, pl.ds(b * half, half)], sem_in.at[slot])
                for b in range(2)]

    def out_copies(step, slot):
        return [pltpu.make_async_copy(obuf.at[slot], a_hbm.at[:, step, :], sem_out.at[slot])]

    def compute(slot):
        a = _dot(m_ref[...], zbuf[slot].astype(BF16))
        obuf[slot] = _pack_complex(a[:n1], a[n1:])

    _double_buffered(in_copies, out_copies, compute)


def _fft_s1_real_kernel(k_ref, m_ref, a_hbm, obuf, sem_out):
    n1 = obuf.shape[1]

    def out_copies(step, slot):
        return [pltpu.make_async_copy(obuf.at[slot], a_hbm.at[:, step, :], sem_out.at[slot])]

    def compute(slot):
        a = _dot(m_ref[...], k_ref[...].astype(BF16))
        obuf[slot] = _pack_complex(a[:n1], a[n1:])

    _double_buffered(lambda step, slot: [], out_copies, compute)


def _fft_mid_kernel(a_ref, af_ref, ab_ref, s_ref, f2_ref, g2_ref, c_ref):
    n2 = a_ref.shape[0]
    f2 = f2_ref[...]
    xf = _dot(f2, _unpack_complex(af_ref[...]))
    xb = _dot(f2, _unpack_complex(ab_ref[...]))
    s = s_ref[...]
    kr = (xf[:n2] + xb[:n2]) * s
    ki = (xf[n2:] - xb[n2:]) * s
    x = _dot(f2, _unpack_complex(a_ref[...]))
    xr, xi = x[:n2], x[n2:]
    y = jnp.concatenate([xr * kr - xi * ki, xr * ki + xi * kr], axis=0).astype(BF16)
    c = _dot(g2_ref[...], y)
    c_ref[...] = _pack_complex(c[:n2], c[n2:])


def _fft_out_kernel(c_hbm, m_ref, zv_hbm, x0_hbm, sk_ref, y_hbm, cbuf, zbuf, xbuf, ybuf, sem_in, sem_out):
    half = zbuf.shape[1] // 2

    def in_copies(step, slot):
        cps = [pltpu.make_async_copy(c_hbm.at[:, step, :], cbuf.at[slot], sem_in.at[slot])]
        for b in range(2):
            rows = pl.ds(b * half, half)
            cps.append(pltpu.make_async_copy(zv_hbm.at[b, :, step, :], zbuf.at[slot, rows], sem_in.at[slot]))
            cps.append(pltpu.make_async_copy(x0_hbm.at[b, :, step, :], xbuf.at[slot, rows], sem_in.at[slot]))
        return cps

    def out_copies(step, slot):
        return [pltpu.make_async_copy(ybuf.at[slot, pl.ds(b * half, half)], y_hbm.at[b, :, step, :], sem_out.at[slot])
                for b in range(2)]

    def compute(slot):
        y = _dot(m_ref[...], _unpack_complex(cbuf[slot]))
        ybuf[slot] = (y + zbuf[slot] * sk_ref[...]) * xbuf[slot]

    _double_buffered(in_copies, out_copies, compute)


def hyena_long_conv(zv, x0, taps, asum, skip_b, L):
    D = zv.shape[1]
    N = 2 * L
    N2 = DFT_N2
    N1 = N // N2
    half = N1 // 2
    fwd, inv, fwd_real, f2, g2 = dft_tables(N1, N2)
    td2 = _tile(D, 2048)
    nd2 = D // td2
    zv4 = zv.reshape(2, half, N2, D)
    x04 = x0.reshape(2, half, N2, D)
    any_spec = pl.BlockSpec(memory_space=pl.ANY)

    af = pl.pallas_call(
        _fft_s1_real_kernel,
        grid=(N2,),
        in_specs=[pl.BlockSpec((None, half, 2 * D), lambda j: (j, 0, 0)),
                  pl.BlockSpec((None, 2 * N1, half), lambda j: (j, 0, 0))],
        out_specs=any_spec,
        out_shape=jax.ShapeDtypeStruct((N1, N2, 2 * D), jnp.uint32),
        scratch_shapes=[pltpu.VMEM((2, N1, 2 * D), jnp.uint32), pltpu.SemaphoreType.DMA((2,))],
        compiler_params=_params(("arbitrary",)),
        name="filter_dft_stage1",
    )(taps.reshape(N2, half, 2 * D), fwd_real)
    scale = 1.0 / ((asum[:, :D] + asum[:, D:]) * N)

    a = pl.pallas_call(
        _fft_s1_kernel,
        grid=(N2,),
        in_specs=[any_spec,
                  pl.BlockSpec((None, 2 * N1, N1), lambda j: (j, 0, 0))],
        out_specs=any_spec,
        out_shape=jax.ShapeDtypeStruct((N1, N2, D), jnp.uint32),
        scratch_shapes=[pltpu.VMEM((2, N1, D), F32), pltpu.VMEM((2, N1, D), jnp.uint32),
                        pltpu.SemaphoreType.DMA((2,)), pltpu.SemaphoreType.DMA((2,))],
        compiler_params=_params(("arbitrary",)),
        name="conv_dft_stage1",
    )(zv4, fwd)

    c = pl.pallas_call(
        _fft_mid_kernel,
        grid=(N1, nd2),
        in_specs=[pl.BlockSpec((None, N2, td2), lambda k, j: (k, 0, j)),
                  pl.BlockSpec((None, N2, td2), lambda k, j: (k, 0, j)),
                  pl.BlockSpec((None, N2, td2), lambda k, j: (k, 0, j + nd2)),
                  pl.BlockSpec((1, td2), lambda k, j: (0, j)),
                  pl.BlockSpec((2 * N2, 2 * N2), lambda k, j: (0, 0)),
                  pl.BlockSpec((2 * N2, 2 * N2), lambda k, j: (0, 0))],
        out_specs=pl.BlockSpec((None, N2, td2), lambda k, j: (k, 0, j)),
        out_shape=jax.ShapeDtypeStruct((N1, N2, D), jnp.uint32),
        compiler_params=_params(("parallel", "parallel")),
        name="conv_dft_stage2",
    )(a, af, af, scale, f2, g2)

    y = pl.pallas_call(
        _fft_out_kernel,
        grid=(N2,),
        in_specs=[any_spec,
                  pl.BlockSpec((None, N1, 2 * N1), lambda j: (j, 0, 0)),
                  any_spec, any_spec,
                  pl.BlockSpec((1, D), lambda j: (0, 0))],
        out_specs=any_spec,
        out_shape=jax.ShapeDtypeStruct((2, half, N2, D), F32),
        scratch_shapes=[pltpu.VMEM((2, N1, D), jnp.uint32), pltpu.VMEM((2, N1, D), F32),
                        pltpu.VMEM((2, N1, D), F32), pltpu.VMEM((2, N1, D), F32),
                        pltpu.SemaphoreType.DMA((2,)), pltpu.SemaphoreType.DMA((2,))],
        compiler_params=_params(("arbitrary",)),
        name="conv_dft_inverse",
    )(c, inv, zv4, x04, skip_b)
    return y.reshape(2 * L, D)


def _qkv_kernel(x_ref, g_ref, w_ref, gain_ref, cos_ref, sin_ref, o_ref, va_ref, hn_ref, *, n_rope_tiles):
    j = pl.program_id(1)

    @pl.when(j == 0)
    def _():
        hn_ref[...] = _norm_rows(x_ref[...], g_ref[...]).astype(BF16)

    u = _dot(hn_ref[...], w_ref[...])
    tn = u.shape[1]

    @pl.when(j < n_rope_tiles)
    def _():
        gain = gain_ref[...]
        cos = cos_ref[...]
        sin = sin_ref[...]
        even = (lax.broadcasted_iota(jnp.int32, cos.shape, 1) % 2) == 0
        for hh in range(tn // HEAD_DIM):
            xh = u[:, hh * HEAD_DIM:(hh + 1) * HEAD_DIM]
            y = _norm_rows(xh, gain)
            sw = jnp.where(even, pltpu.roll(y, HEAD_DIM - 1, 1), pltpu.roll(y, 1, 1))
            o_ref[:, hh * HEAD_DIM:(hh + 1) * HEAD_DIM] = (y * cos + sw * sin).astype(o_ref.dtype)

    @pl.when(j >= n_rope_tiles)
    def _():
        o_ref[...] = u.astype(o_ref.dtype)
        ones = jnp.ones((u.shape[0], HEAD_DIM), va_ref.dtype)
        for hh in range(tn // HEAD_DIM):
            va_ref[:, 2 * hh * HEAD_DIM:(2 * hh + 1) * HEAD_DIM] = u[:, hh * HEAD_DIM:(hh + 1) * HEAD_DIM].astype(va_ref.dtype)
            va_ref[:, (2 * hh + 1) * HEAD_DIM:(2 * hh + 2) * HEAD_DIM] = ones


def rope_tables(L):
    rows = L // GRID_W
    r, c = jnp.meshgrid(jnp.arange(rows, dtype=F32), jnp.arange(GRID_W, dtype=F32), indexing="ij")
    r = r.reshape(L)
    c = c.reshape(L)
    axis_dim = HEAD_DIM // 2
    inv = 1.0 / (ROPE_THETA ** (jnp.arange(0, axis_dim, 2, dtype=F32) / axis_dim))
    ang = jnp.concatenate([r[:, None] * inv[None], c[:, None] * inv[None]], axis=-1)
    cos = jnp.repeat(jnp.cos(ang), 2, axis=1)
    sign = jnp.tile(jnp.array([-1.0, 1.0], F32), HEAD_DIM // 2)[None, :]
    sin = jnp.repeat(jnp.sin(ang), 2, axis=1) * sign
    return cos, sin


def qkv_project(x, g, w_qkv, q_gain, k_gain, seq_len, *, tm=512):
    T, D = x.shape
    n_out = w_qkv.shape[1]
    nk = N_KV_HEADS * HEAD_DIM
    nq = n_out - 2 * nk
    tn = nk
    tm = _tile(seq_len, tm)
    n_rope = (nq + nk) // tn
    gains = jnp.concatenate([jnp.tile((q_gain * SCORE_SCALE)[None, None, :], (nq // tn, 1, 1)),
                             k_gain[None, None, :],
                             jnp.ones((1, 1, HEAD_DIM), F32)], axis=0)
    cos, sin = rope_tables(seq_len)
    spt = seq_len // tm
    kern = functools.partial(_qkv_kernel, n_rope_tiles=n_rope)
    return pl.pallas_call(
        kern,
        grid=(T // tm, n_out // tn),
        in_specs=[pl.BlockSpec((tm, D), lambda i, j: (i, 0)),
                  pl.BlockSpec((1, D), lambda i, j: (0, 0)),
                  pl.BlockSpec((D, tn), lambda i, j: (0, j)),
                  pl.BlockSpec((None, 1, HEAD_DIM), lambda i, j: (j, 0, 0)),
                  pl.BlockSpec((tm, HEAD_DIM), lambda i, j: (i % spt, 0)),
                  pl.BlockSpec((tm, HEAD_DIM), lambda i, j: (i % spt, 0))],
        out_specs=[pl.BlockSpec((tm, tn), lambda i, j: (i, j)),
                   pl.BlockSpec((tm, 2 * nk), lambda i, j: (i, 0))],
        out_shape=[jax.ShapeDtypeStruct((T, n_out), BF16), jax.ShapeDtypeStruct((T, 2 * nk), BF16)],
        scratch_shapes=[pltpu.VMEM((tm, D), BF16)],
        compiler_params=_params(("parallel", "arbitrary")),
        name="qkv_project",
    )(x, g, w_qkv, gains, cos, sin)


def _flash_kernel(q_ref, k_ref, v_ref, o_ref, qs_ref, s0_ref, s1_ref, p0_ref, p1_ref, a0_ref, a1_ref,
                  m_ref, acc_ref, *, group, tq, tk, nk):
    s_refs = (s0_ref, s1_ref)
    p_refs = (p0_ref, p1_ref)
    a_refs = (a0_ref, a1_ref)
    for gq in range(group):
        qs_ref[gq * tq:(gq + 1) * tq, :] = q_ref[:, gq * HEAD_DIM:(gq + 1) * HEAD_DIM]
    m_ref[...] = jnp.full_like(m_ref, -jnp.inf)
    acc_ref[...] = jnp.zeros_like(acc_ref)

    def rows_of(j):
        if isinstance(j, int):
            return pl.ds(j * tk, tk)
        return pl.ds(pl.multiple_of(j * tk, tk), tk)

    def scores(j, slot):
        kb = k_ref[rows_of(j), :]
        s_refs[slot][...] = lax.dot_general(qs_ref[...], kb, (((1,), (1,)), ((), ())),
                                            preferred_element_type=F32)

    def softmax(slot):
        s = s_refs[slot][...]
        m_prev = m_ref[...]
        m_new = jnp.maximum(m_prev, jnp.max(s, axis=-1, keepdims=True))
        a_refs[slot][...] = jnp.exp2(m_prev - m_new)
        p_refs[slot][...] = jnp.exp2(s - jnp.tile(m_new, (1, tk // LANES))).astype(BF16)
        m_ref[...] = m_new

    def accumulate(j, slot):
        pv = _dot(p_refs[slot][...], v_ref[rows_of(j), :])
        acc_ref[...] = acc_ref[...] * jnp.tile(a_refs[slot][...], (1, 2)) + pv

    def step(j, slot, first, last):
        if not last:
            scores(j + 1, 1 - slot)
        softmax(slot)
        if not first:
            accumulate(j - 1, 1 - slot)

    scores(0, 0)
    if nk >= 4 and nk % 2 == 0:
        step(0, 0, True, False)

        def pair(t, c):
            j = 2 * t + 1
            step(j, 1, False, False)
            step(j + 1, 0, False, False)
            return c

        lax.fori_loop(0, (nk - 2) // 2, pair, 0)
        step(nk - 1, 1, False, True)
    else:
        for j in range(nk):
            step(j, j % 2, j == 0, j == nk - 1)
    accumulate(nk - 1, (nk - 1) % 2)

    acc = acc_ref[...]
    o = acc[:, :HEAD_DIM] / acc[:, HEAD_DIM:]
    for gq in range(group):
        o_ref[:, gq * HEAD_DIM:(gq + 1) * HEAD_DIM] = o[gq * tq:(gq + 1) * tq].astype(o_ref.dtype)


def _flash_bounded_kernel(q_ref, k_ref, v_ref, o_ref, qs_ref, acc_ref, *, group, tq, tk, nk, unroll):
    for gq in range(group):
        qs_ref[gq * tq:(gq + 1) * tq, :] = q_ref[:, gq * HEAD_DIM:(gq + 1) * HEAD_DIM]
    acc_ref[...] = jnp.zeros_like(acc_ref)

    def block(j, c):
        rows = pl.ds(pl.multiple_of(j * tk, tk), tk)
        s = lax.dot_general(qs_ref[...], k_ref[rows, :], (((1,), (1,)), ((), ())), preferred_element_type=F32)
        acc_ref[...] += _dot(jnp.exp2(s).astype(BF16), v_ref[rows, :])
        return c

    lax.fori_loop(0, nk, block, 0, unroll=unroll)
    acc = acc_ref[...]
    o = acc[:, :HEAD_DIM] / acc[:, HEAD_DIM:]
    for gq in range(group):
        o_ref[:, gq * HEAD_DIM:(gq + 1) * HEAD_DIM] = o[gq * tq:(gq + 1) * tq].astype(o_ref.dtype)


SAFE_SCORE_BOUND = 80.0


def flash_attention(qkv, v_aug, score_bound, B, L, n_heads, *, tq=256, tk=512):
    group = n_heads // N_KV_HEADS
    tq = _tile(L, tq)
    tk = _tile(L, tk)
    nk = L // tk
    qw = group * HEAD_DIM
    k_col = n_heads
    rows = group * tq
    common = dict(
        grid=(B, N_KV_HEADS, L // tq),
        in_specs=[pl.BlockSpec((None, tq, qw), lambda b, h, qi: (b, qi, h)),
                  pl.BlockSpec((None, L, HEAD_DIM), lambda b, h, qi: (b, 0, k_col + h)),
                  pl.BlockSpec((None, L, 2 * HEAD_DIM), lambda b, h, qi: (b, 0, h))],
        out_specs=pl.BlockSpec((None, tq, qw), lambda b, h, qi: (b, qi, h)),
        out_shape=jax.ShapeDtypeStruct((B, L, n_heads * HEAD_DIM), BF16),
        compiler_params=_params(("parallel", "parallel", "parallel")),
    )
    general = pl.pallas_call(
        functools.partial(_flash_kernel, group=group, tq=tq, tk=tk, nk=nk),
        scratch_shapes=[pltpu.VMEM((rows, HEAD_DIM), BF16),
                        pltpu.VMEM((rows, tk), F32), pltpu.VMEM((rows, tk), F32),
                        pltpu.VMEM((rows, tk), BF16), pltpu.VMEM((rows, tk), BF16),
                        pltpu.VMEM((rows, LANES), F32), pltpu.VMEM((rows, LANES), F32),
                        pltpu.VMEM((rows, LANES), F32),
                        pltpu.VMEM((rows, 2 * HEAD_DIM), F32)],
        name="flash_attention", **common)
    bounded = pl.pallas_call(
        functools.partial(_flash_bounded_kernel, group=group, tq=tq, tk=tk, nk=nk,
                          unroll=math.gcd(nk, 8)),
        scratch_shapes=[pltpu.VMEM((rows, HEAD_DIM), BF16),
                        pltpu.VMEM((rows, 2 * HEAD_DIM), F32)],
        name="flash_attention_bounded", **common)
    return lax.cond(score_bound <= SAFE_SCORE_BOUND, bounded, general, qkv, qkv, v_aug)


def _pack_bf16_pairs(x):
    half = x.shape[1] // 2
    bits = lax.bitcast_convert_type(x.astype(BF16).astype(F32), jnp.uint32)
    return (bits[:, :half] >> 16) | (bits[:, half:] & jnp.uint32(0xFFFF0000))


def _unpack_bf16_pairs(p):
    lo = lax.bitcast_convert_type(p << 16, F32).astype(BF16)
    hi = lax.bitcast_convert_type(p & jnp.uint32(0xFFFF0000), F32).astype(BF16)
    return lo, hi


def _router_kernel(xa_ref, xb_ref, g_ref, rw_ref, h_ref, info_ref, cnt_ref, *, n_a_tiles):
    i = pl.program_id(0)

    def route(x_ref):
        hf = _norm_rows(x_ref[...], g_ref[...])
        h_ref[...] = _pack_bf16_pairs(hf)
        h_hi = hf.astype(BF16)
        h_lo = (hf - h_hi.astype(F32)).astype(BF16)
        rw = rw_ref[...]
        hw = _dot(h_hi, rw)
        logits = hw[:, :LANES] + hw[:, LANES:] + _dot(h_lo, rw[:, :LANES])
        shape = logits.shape
        lane = lax.broadcasted_iota(jnp.int32, shape, 1)
        lanef = lane.astype(F32)
        neg = -jnp.inf
        big = float(LANES)
        is_g = lane < N_GROUPS
        gl = jnp.where(is_g, logits, neg)
        gmax = jnp.max(gl, axis=-1, keepdims=True)
        gidx = jnp.min(jnp.where(gl == gmax, lanef, big), axis=-1, keepdims=True)
        p_g = 1.0 / jnp.sum(jnp.where(is_g, jnp.exp(logits - gmax), 0.0), axis=-1, keepdims=True)
        lo = N_GROUPS + gidx * EXPERTS_PER_GROUP
        el = jnp.where(lanef >= lo, jnp.where(lanef < lo + EXPERTS_PER_GROUP, logits, neg), neg)
        v1 = jnp.max(el, axis=-1, keepdims=True)
        i1 = jnp.min(jnp.where(el == v1, lanef, big), axis=-1, keepdims=True)
        el2 = jnp.where(lanef == i1, neg, el)
        v2 = jnp.max(el2, axis=-1, keepdims=True)
        i2 = jnp.min(jnp.where(el2 == v2, lanef, big), axis=-1, keepdims=True)
        t = jnp.exp(v2 - v1)
        w1 = p_g / (1.0 + t)
        w2 = p_g * t / (1.0 + t)
        e1 = i1 - N_GROUPS
        e2 = i2 - N_GROUPS
        oh1 = jnp.where(lanef == e1, 1.0, 0.0)
        oh2 = jnp.where(lanef == e2, 1.0, 0.0)
        oh = oh1 + oh2
        tm = shape[0]
        tri = jnp.where(lax.broadcasted_iota(jnp.int32, (tm, tm), 1) < lax.broadcasted_iota(jnp.int32, (tm, tm), 0),
                        1.0, 0.0).astype(BF16)
        before = _dot(tri, oh.astype(BF16)) + cnt_ref[...]
        r1 = jnp.sum(before * oh1, axis=-1, keepdims=True)
        r2 = jnp.sum(before * oh2, axis=-1, keepdims=True)
        cnt_ref[...] += jnp.sum(oh, axis=0, keepdims=True)
        vals = (e1, e2, w1, w2, r1, r2)
        info = jnp.zeros(shape, F32)
        for idx, val in enumerate(vals):
            info = jnp.where(lane == idx, val, info)
        info_ref[...] = info

    @pl.when(i == 0)
    def _():
        cnt_ref[...] = jnp.zeros_like(cnt_ref)

    @pl.when(i < n_a_tiles)
    def _():
        route(xa_ref)

    @pl.when(i >= n_a_tiles)
    def _():
        route(xb_ref)


def moe_route(xa, xb, g, router_group, router_expert, *, tm=512):
    Ta, D = xa.shape
    Tb = xb.shape[0]
    tm = _tile(math.gcd(Ta, Tb), tm)
    na, nb = Ta // tm, Tb // tm
    n_e = N_GROUPS * EXPERTS_PER_GROUP
    rw = jnp.concatenate([router_group, jnp.transpose(router_expert, (1, 0, 2)).reshape(D, n_e)], axis=1)
    rw = jnp.pad(rw, ((0, 0), (0, LANES - rw.shape[1])))
    rw_hi = rw.astype(BF16)
    rw = jnp.concatenate([rw_hi, (rw - rw_hi.astype(F32)).astype(BF16)], axis=1)
    kern = functools.partial(_router_kernel, n_a_tiles=na)
    return pl.pallas_call(
        kern,
        grid=(na + nb,),
        in_specs=[pl.BlockSpec((tm, D), lambda i: (jnp.minimum(i, na - 1), 0)),
                  pl.BlockSpec((tm, D), lambda i: (jnp.maximum(i - na, 0), 0)),
                  pl.BlockSpec((1, D), lambda i: (0, 0)),
                  pl.BlockSpec((D, 2 * LANES), lambda i: (0, 0))],
        out_specs=[pl.BlockSpec((tm, D // 2), lambda i: (i, 0)),
                   pl.BlockSpec((tm, LANES), lambda i: (i, 0)),
                   pl.BlockSpec((1, LANES), lambda i: (0, 0))],
        out_shape=[jax.ShapeDtypeStruct((Ta + Tb, D // 2), jnp.uint32),
                   jax.ShapeDtypeStruct((Ta + Tb, LANES), F32),
                   jax.ShapeDtypeStruct((1, LANES), F32)],
        compiler_params=_params(("arbitrary",)),
        name="moe_route",
    )(xa, xb, g, rw)


def moe_plan(info, cnt, tm, n_experts):
    T = info.shape[0]
    n_tiles = (2 * T) // tm + n_experts
    counts = cnt[0, :n_experts].astype(jnp.int32)
    padded = ((counts + tm - 1) // tm) * tm
    ends = jnp.cumsum(padded)
    off = ends - padded
    e = info[:, 0:2].astype(jnp.int32)
    rank = info[:, 4:6].astype(jnp.int32)
    pos = (off[e] + rank).reshape(-1)
    tile_start = jnp.arange(n_tiles, dtype=jnp.int32) * tm
    tile_e = jnp.sum((tile_start[:, None] >= ends[None, :]).astype(jnp.int32), axis=1)
    tile_e = jnp.minimum(tile_e, n_experts - 1)
    tile_valid = jnp.clip(counts[tile_e] - (tile_start - off[tile_e]), 0, tm).astype(jnp.int32)
    n_used = (ends[-1:] // tm).astype(jnp.int32)
    return pos, tile_e, tile_valid, n_used


DMA_UNROLL = 8


def _dispatch_kernel(pos_ref, h_ref, zeros_hbm, hs_hbm, sem):
    del zeros_hbm
    tm = h_ref.shape[0]

    def row_copy(r, slot):
        return pltpu.make_async_copy(h_ref.at[pl.ds(r, 1)], hs_hbm.at[pl.ds(pos_ref[0, 0, 2 * r + slot], 1)], sem)

    def start(r, c):
        row_copy(r, 0).start()
        row_copy(r, 1).start()
        return c

    def wait(r, c):
        row_copy(r, 0).wait()
        row_copy(r, 1).wait()
        return c

    lax.fori_loop(0, tm, start, 0, unroll=DMA_UNROLL)
    lax.fori_loop(0, tm, wait, 0, unroll=DMA_UNROLL)


def moe_dispatch(h, pos, n_sorted_rows, *, tm=512):
    T, C = h.shape
    tm = _tile(T, tm)
    return pl.pallas_call(
        _dispatch_kernel,
        grid=(T // tm,),
        in_specs=[pl.BlockSpec((1, 1, 2 * tm), lambda i: (i, 0, 0), memory_space=pltpu.SMEM),
                  pl.BlockSpec((tm, C), lambda i: (i, 0)),
                  pl.BlockSpec(memory_space=pl.ANY)],
        out_specs=pl.BlockSpec(memory_space=pl.ANY),
        out_shape=jax.ShapeDtypeStruct((n_sorted_rows, C), h.dtype),
        scratch_shapes=[pltpu.SemaphoreType.DMA],
        input_output_aliases={2: 0},
        compiler_params=_params(("arbitrary",)),
        name="moe_dispatch",
    )(pos.reshape(T // tm, 1, 2 * tm), h, jnp.zeros((n_sorted_rows, C), h.dtype))


def _expert_kernel(te_ref, tv_ref, nu_ref, x_ref, wg_ref, wu_ref, wd_ref, y_ref):
    del te_ref, nu_ref
    valid = tv_ref[pl.program_id(0)]

    @pl.when(valid == 0)
    def _():
        y_ref[...] = jnp.zeros_like(y_ref)

    @pl.when(valid > 0)
    def _():
        x_lo, x_hi = _unpack_bf16_pairs(x_ref[...])
        half = x_ref.shape[1]
        gate = (_dot(x_lo, wg_ref[:half, :].astype(BF16)) + _dot(x_hi, wg_ref[half:, :].astype(BF16)))
        up = (_dot(x_lo, wu_ref[:half, :].astype(BF16)) + _dot(x_hi, wu_ref[half:, :].astype(BF16)))
        a = (gate * jax.nn.sigmoid(gate) * up).astype(BF16)
        y_ref[...] = _dot(a, wd_ref[...].astype(BF16))


def moe_experts(hs, tile_e, tile_valid, n_used, w_gate, w_up, w_down, layer, tm):
    P = hs.shape[0]
    D = w_gate.shape[2]
    d_e = w_gate.shape[-1]

    def row_map(i, te, tv, nu):
        return (jnp.minimum(i, nu[0] - 1), 0)

    def out_map(i, te, tv, nu):
        return (i, 0)

    def w_map(i, te, tv, nu):
        return (layer, te[i], 0, 0)

    grid_spec = pltpu.PrefetchScalarGridSpec(
        num_scalar_prefetch=3,
        grid=(P // tm,),
        in_specs=[pl.BlockSpec((tm, D // 2), row_map),
                  pl.BlockSpec((None, None, D, d_e), w_map),
                  pl.BlockSpec((None, None, D, d_e), w_map),
                  pl.BlockSpec((None, None, d_e, D), w_map)],
        out_specs=pl.BlockSpec((tm, D), out_map),
    )
    return pl.pallas_call(
        _expert_kernel,
        grid_spec=grid_spec,
        out_shape=jax.ShapeDtypeStruct((P, D), F32),
        compiler_params=_params(("arbitrary",)),
        name="moe_experts",
    )(tile_e, tile_valid, n_used, hs, w_gate, w_up, w_down)


def _combine_kernel(pos_ref, posn_ref, x_ref, info_ref, g_ref, y_hbm, o_ref, ybuf, sem, *, final_norm):
    tm = x_ref.shape[0]
    i = pl.program_id(0)
    buf = i % 2

    def row_copy(p_ref, b, r, slot):
        return pltpu.make_async_copy(y_hbm.at[pl.ds(p_ref[0, 0, 2 * r + slot], 1)],
                                     ybuf.at[b, slot, pl.ds(r, 1)], sem.at[b])

    def start_tile(p_ref, b):
        def body(r, c):
            row_copy(p_ref, b, r, 0).start()
            row_copy(p_ref, b, r, 1).start()
            return c
        lax.fori_loop(0, tm, body, 0, unroll=DMA_UNROLL)

    @pl.when(i == 0)
    def _():
        start_tile(pos_ref, 0)

    @pl.when(i + 1 < pl.num_programs(0))
    def _():
        start_tile(posn_ref, 1 - buf)

    def wait(r, c):
        row_copy(pos_ref, buf, r, 0).wait()
        row_copy(pos_ref, buf, r, 1).wait()
        return c

    lax.fori_loop(0, tm, wait, 0, unroll=DMA_UNROLL)
    info = info_ref[...]
    y = info[:, 2:3] * ybuf[buf, 0] + info[:, 3:4] * ybuf[buf, 1]
    x = x_ref[...] + y
    if final_norm:
        x = _norm_rows(x, g_ref[...])
    o_ref[...] = x


def moe_combine(x, y, pos, info, row_off, g_final, final_norm, *, tm=256):
    Tg, D = x.shape
    T = info.shape[0]
    tm = _tile(math.gcd(Tg, row_off) if row_off else Tg, tm)
    ob = row_off // tm
    kern = functools.partial(_combine_kernel, final_norm=final_norm)
    n = Tg // tm
    pos3 = pos.reshape(T // tm, 1, 2 * tm)
    return pl.pallas_call(
        kern,
        grid=(n,),
        in_specs=[pl.BlockSpec((1, 1, 2 * tm), lambda i: (i + ob, 0, 0), memory_space=pltpu.SMEM),
                  pl.BlockSpec((1, 1, 2 * tm), lambda i: (jnp.minimum(i + 1, n - 1) + ob, 0, 0),
                               memory_space=pltpu.SMEM),
                  pl.BlockSpec((tm, D), lambda i: (i, 0)),
                  pl.BlockSpec((tm, LANES), lambda i: (i + ob, 0)),
                  pl.BlockSpec((1, D), lambda i: (0, 0)),
                  pl.BlockSpec(memory_space=pl.ANY)],
        out_specs=pl.BlockSpec((tm, D), lambda i: (i, 0)),
        out_shape=jax.ShapeDtypeStruct((Tg, D), F32),
        scratch_shapes=[pltpu.VMEM((2, 2, tm, D), F32), pltpu.SemaphoreType.DMA((2,))],
        compiler_params=_params(("arbitrary",)),
        name="moe_combine",
    )(pos3, pos3, x, info, g_final, y)


MOE_TILE = 512


def hier_moe_block(xs, g, router_group, router_expert, w_gate, w_up, w_down, layer, g_final, final_norm):
    xa, xb = xs
    n_experts = w_gate.shape[1]
    h, info, cnt = moe_route(xa, xb, g, router_group, router_expert)
    T = h.shape[0]
    tm = _tile(2 * T, MOE_TILE)
    pos, tile_e, tile_valid, n_used = moe_plan(info, cnt, tm, n_experts)
    hs = moe_dispatch(h, pos, tile_e.shape[0] * tm)
    y = moe_experts(hs, tile_e, tile_valid, n_used, w_gate, w_up, w_down, layer, tm)
    oa = moe_combine(xa, y, pos, info, 0, g_final, final_norm)
    ob = moe_combine(xb, y, pos, info, xa.shape[0], g_final, final_norm)
    return oa, ob


def hyena_block(x, L, g, w_in, conv_w, conv_b, f_w1, f_b1, f_win, f_bin, f_freq, f_w3, skip_b, w_out):
    D = x.shape[1]
    x0, zv = hyena_inproj(x, g, w_in, conv_w, conv_b[None, :], L)
    taps, asum = hyena_filter_taps(L, w_out.shape[0], f_w1, f_b1, f_win, f_bin, f_freq, f_w3)
    y = hyena_long_conv(zv, x0, taps, asum, skip_b[None, :], L)
    return matmul_residual(y, w_out, x)


def attention_block(x, B, L, g, w_qkv, q_gain, k_gain, w_o):
    D = x.shape[1]
    n_heads = w_o.shape[0] // HEAD_DIM
    qkv, v_aug = qkv_project(x, g, w_qkv, q_gain, k_gain, L)
    score_bound = 1.02 * HEAD_DIM * SCORE_SCALE * jnp.max(jnp.abs(q_gain)) * jnp.max(jnp.abs(k_gain))
    o = flash_attention(qkv.reshape(B, L, qkv.shape[1]), v_aug.reshape(B, L, v_aug.shape[1]), score_bound,
                        B, L, n_heads)
    return matmul_residual(o.reshape(B * L, n_heads * HEAD_DIM), w_o, x)


def kernel(x_prompt, x_sample, norm_mix, norm_ffn, norm_final, hy_w_in, hy_conv_w, hy_conv_b, hy_f_w1, hy_f_b1, hy_f_win, hy_f_bin, hy_f_freq, hy_f_w3, hy_skip_b, hy_w_out, at_w_qkv, at_q_gain, at_k_gain, at_w_o, moe_router_group, moe_router_expert, moe_w_gate, moe_w_up, moe_w_down):
    D = x_prompt.shape[-1]
    depth = norm_mix.shape[0]
    shapes = [x_prompt.shape, x_sample.shape]
    for s in shapes:
        assert s[0] == 2, "the long convolution packs exactly two sequences per group"
    xs = [x_prompt.reshape(-1, D), x_sample.reshape(-1, D)]
    hy_w_in_b = hy_w_in.astype(BF16)
    hy_w_out_b = hy_w_out.astype(BF16)
    at_w_qkv_b = at_w_qkv.astype(BF16)
    at_w_o_b = at_w_o.astype(BF16)
    g_final = norm_final[None, :]
    for i in range(depth):
        j = i // N_MIXERS
        g = norm_mix[i][None, :]
        if i % N_MIXERS == 0:
            xs = [hyena_block(x, s[1], g, hy_w_in_b[j], hy_conv_w[j], hy_conv_b[j], hy_f_w1[j], hy_f_b1[j],
                              hy_f_win[j], hy_f_bin[j], hy_f_freq[j], hy_f_w3[j], hy_skip_b[j], hy_w_out_b[j])
                  for x, s in zip(xs, shapes)]
        else:
            xs = [attention_block(x, s[0], s[1], g, at_w_qkv_b[j], at_q_gain[j], at_k_gain[j], at_w_o_b[j])
                  for x, s in zip(xs, shapes)]
        xs = hier_moe_block(xs, norm_ffn[i][None, :], moe_router_group[i], moe_router_expert[i],
                            moe_w_gate, moe_w_up, moe_w_down, i, g_final, i == depth - 1)
    return (xs[0].reshape(shapes[0]), xs[1].reshape(shapes[1]))
```

```python
import functools
import math

import jax
import jax.numpy as jnp
import numpy as np
from jax import lax
from jax.experimental import pallas as pl
from jax.experimental.pallas import tpu as pltpu

RMS_EPS = 1e-6
GRID_W = 64
EMB_DIM = 33
DECAY_TARGET = 1e-2
FAST_DECAY_PCT = 0.3
SLOW_DECAY_PCT = 1.5
MIN_DECAY = math.log(DECAY_TARGET) / FAST_DECAY_PCT
MAX_DECAY = math.log(DECAY_TARGET) / SLOW_DECAY_PCT
HEAD_DIM = 128
SCORE_SCALE = HEAD_DIM ** -0.5 * math.log2(math.e)
N_KV_HEADS = 4
ROPE_THETA = 10000.0
N_GROUPS = 4
EXPERTS_PER_GROUP = 8
N_MIXERS = 2

LANES = 128
BF16_SUBLANES = 16
VMEM_LIMIT_BYTES = 52 * 1024 * 1024
DFT_N2 = 128

BF16 = jnp.bfloat16
F32 = jnp.float32


def _params(sem):
    return pltpu.CompilerParams(dimension_semantics=sem, vmem_limit_bytes=VMEM_LIMIT_BYTES)


def _dot(a, b):
    return jnp.dot(a, b, preferred_element_type=F32)


def _norm_rows(x, g):
    ms = jnp.mean(x * x, axis=-1, keepdims=True)
    return x * lax.rsqrt(ms + RMS_EPS) * g


def _tile(n, pref):
    return pref if n % pref == 0 else n


def _matmul_res_kernel(a_ref, w_ref, r_ref, o_ref):
    o_ref[...] = r_ref[...] + _dot(a_ref[...].astype(BF16), w_ref[...])


def matmul_residual(a, w, res, *, tm=1024, tn=512):
    T, K = a.shape
    N = w.shape[1]
    tm = _tile(T, tm)
    tn = _tile(N, tn)
    return pl.pallas_call(
        _matmul_res_kernel,
        grid=(T // tm, N // tn),
        in_specs=[pl.BlockSpec((tm, K), lambda i, j: (i, 0)),
                  pl.BlockSpec((K, tn), lambda i, j: (0, j)),
                  pl.BlockSpec((tm, tn), lambda i, j: (i, j))],
        out_specs=pl.BlockSpec((tm, tn), lambda i, j: (i, j)),
        out_shape=jax.ShapeDtypeStruct((T, N), F32),
        compiler_params=_params(("parallel", "arbitrary")),
        name="matmul_residual",
    )(a, w, res)


HALO = BF16_SUBLANES


def _hy_inproj_kernel(x_ref, xp_ref, xn_ref, g_ref, w0_ref, w1_ref, w2_ref,
                      cw0_ref, cw1_ref, cw2_ref, cb0_ref, cb1_ref, cb2_ref,
                      x0_out, zv_out, hn_ref, *, tm, tiles_per_seq):
    i = pl.program_id(0)
    j = pl.program_id(1)

    @pl.when(j == 0)
    def _():
        g = g_ref[...]
        hn_ref[HALO:HALO + tm, :] = _norm_rows(x_ref[...], g).astype(BF16)
        first = (i % tiles_per_seq) == 0
        last = (i % tiles_per_seq) == tiles_per_seq - 1
        hp = _norm_rows(xp_ref[...], g)
        hn_ref[0:HALO, :] = jnp.where(first, 0.0, hp).astype(BF16)
        hx = _norm_rows(xn_ref[...], g)
        hn_ref[HALO + tm:, :] = jnp.where(last, 0.0, hx).astype(BF16)

    h = hn_ref[...]
    rows = tm + 2 * HALO

    def conv(w_ref, cw_ref, cb_ref):
        u = _dot(h, w_ref[...])
        um = pltpu.roll(u, 1, 0)[HALO:HALO + tm]
        up = pltpu.roll(u, rows - 1, 0)[HALO:HALO + tm]
        uc = u[HALO:HALO + tm]
        cw = cw_ref[...]
        return um * cw[0:1] + uc * cw[1:2] + up * cw[2:3] + cb_ref[...]

    x0_out[...] = conv(w0_ref, cw0_ref, cb0_ref)
    x1 = conv(w1_ref, cw1_ref, cb1_ref)
    v = conv(w2_ref, cw2_ref, cb2_ref)
    zv_out[...] = v * x1


def hyena_inproj(x, g, w_in, conv_w, conv_b, seq_len, *, tm=512, tn=512):
    T, D = x.shape
    W = w_in.shape[1] // 3
    tm = _tile(seq_len, tm)
    tn = _tile(W, tn)
    nj = W // tn
    hb = tm // HALO
    n_halo_blocks = T // HALO
    kern = functools.partial(_hy_inproj_kernel, tm=tm, tiles_per_seq=seq_len // tm)

    def wspec(c):
        return pl.BlockSpec((D, tn), lambda i, j, c=c: (0, j + c * nj))

    def cwspec(c):
        return pl.BlockSpec((3, tn), lambda i, j, c=c: (0, j + c * nj))

    def cbspec(c):
        return pl.BlockSpec((1, tn), lambda i, j, c=c: (0, j + c * nj))

    return pl.pallas_call(
        kern,
        grid=(T // tm, nj),
        in_specs=[pl.BlockSpec((tm, D), lambda i, j: (i, 0)),
                  pl.BlockSpec((HALO, D), lambda i, j: (jnp.maximum(i * hb - 1, 0), 0)),
                  pl.BlockSpec((HALO, D), lambda i, j: (jnp.minimum((i + 1) * hb, n_halo_blocks - 1), 0)),
                  pl.BlockSpec((1, D), lambda i, j: (0, 0)),
                  wspec(0), wspec(1), wspec(2),
                  cwspec(0), cwspec(1), cwspec(2),
                  cbspec(0), cbspec(1), cbspec(2)],
        out_specs=[pl.BlockSpec((tm, tn), lambda i, j: (i, j)),
                   pl.BlockSpec((tm, tn), lambda i, j: (i, j))],
        out_shape=[jax.ShapeDtypeStruct((T, W), F32), jax.ShapeDtypeStruct((T, W), F32)],
        scratch_shapes=[pltpu.VMEM((tm + 2 * HALO, D), BF16)],
        compiler_params=_params(("parallel", "arbitrary")),
        name="hyena_inproj",
    )(x, x, x, g, w_in, w_in, w_in, conv_w, conv_w, conv_w, conv_b, conv_b, conv_b)


def _filter_mlp_kernel(z_ref, w1_ref, b1_ref, win_ref, bin_ref, fr_ref, h_ref):
    hp = lax.Precision.HIGHEST
    fr = fr_ref[...]
    h = jnp.sin(fr * (jnp.dot(z_ref[...], w1_ref[...], precision=hp, preferred_element_type=F32) + b1_ref[...]))
    for l in range(win_ref.shape[0]):
        h = jnp.sin(fr * (jnp.dot(h, win_ref[l], precision=hp, preferred_element_type=F32) + bin_ref[l]))
    h_ref[...] = h


def _filter_expand_kernel(h_ref, t_ref, w3_ref, dl_ref, k_ref, s_ref, *, tl, n_fwd_tiles):
    j = pl.program_id(0)
    i = pl.program_id(1)
    taps = jnp.dot(h_ref[...], w3_ref[...], precision=lax.Precision.HIGHEST, preferred_element_type=F32)
    taps = taps * jnp.exp(-t_ref[...] * dl_ref[...])
    row = i * tl + lax.broadcasted_iota(jnp.int32, taps.shape, 0)
    taps = jnp.where(jnp.logical_and(j >= n_fwd_tiles, row == 0), 0.0, taps)
    k_ref[...] = taps.astype(k_ref.dtype)

    @pl.when(i == 0)
    def _():
        s_ref[...] = jnp.zeros_like(s_ref)

    s_ref[...] += jnp.sum(jnp.abs(taps), axis=0, keepdims=True)


def hyena_filter_taps(L, D, f_w1, f_b1, f_win, f_bin, f_freq, f_w3):
    H = f_w1.shape[1]
    t = jnp.linspace(0.0, 1.0, L, dtype=F32)[:, None]
    bands = (EMB_DIM - 1) // 2
    w = 2.0 * math.pi * jnp.arange(L, dtype=F32)[:, None] / L
    fb = jnp.linspace(1e-4, bands - 1, bands, dtype=F32)[None, :]
    z = jnp.concatenate([t, jnp.cos(fb * w), -jnp.sin(fb * w)], axis=-1)

    def lag_major_to_dft_order(a):
        return a.reshape(L // DFT_N2, DFT_N2, a.shape[1]).transpose(1, 0, 2).reshape(L, a.shape[1])

    z = lag_major_to_dft_order(z)
    t = lag_major_to_dft_order(t)
    zp = jnp.pad(z, ((0, 0), (0, LANES - EMB_DIM)))
    w1p = jnp.pad(f_w1, ((0, LANES - EMB_DIM), (0, 0)))
    n_in = f_win.shape[0]
    tl = _tile(L, 512)
    h = pl.pallas_call(
        _filter_mlp_kernel,
        grid=(L // tl,),
        in_specs=[pl.BlockSpec((tl, LANES), lambda i: (i, 0)),
                  pl.BlockSpec((LANES, H), lambda i: (0, 0)),
                  pl.BlockSpec((1, H), lambda i: (0, 0)),
                  pl.BlockSpec((n_in, H, H), lambda i: (0, 0, 0)),
                  pl.BlockSpec((n_in, 1, H), lambda i: (0, 0, 0)),
                  pl.BlockSpec((1, H), lambda i: (0, 0))],
        out_specs=pl.BlockSpec((tl, H), lambda i: (i, 0)),
        out_shape=jax.ShapeDtypeStruct((L, H), F32),
        compiler_params=_params(("parallel",)),
        name="filter_mlp",
    )(zp, w1p, f_b1[None, :], f_win, f_bin[:, None, :], f_freq[None, :])

    deltas = jnp.abs(jnp.linspace(MIN_DECAY, MAX_DECAY, D, dtype=F32))[None, :]
    deltas2 = jnp.concatenate([deltas, deltas], axis=1)
    tn = _tile(D, 1024)
    n_fwd = D // tn
    kern = functools.partial(_filter_expand_kernel, tl=tl, n_fwd_tiles=n_fwd)
    taps, asum = pl.pallas_call(
        kern,
        grid=(2 * n_fwd, L // tl),
        in_specs=[pl.BlockSpec((tl, H), lambda j, i: (i, 0)),
                  pl.BlockSpec((tl, 1), lambda j, i: (i, 0)),
                  pl.BlockSpec((H, tn), lambda j, i: (0, j)),
                  pl.BlockSpec((1, tn), lambda j, i: (0, j))],
        out_specs=[pl.BlockSpec((tl, tn), lambda j, i: (i, j)),
                   pl.BlockSpec((1, tn), lambda j, i: (0, j))],
        out_shape=[jax.ShapeDtypeStruct((L, 2 * D), F32), jax.ShapeDtypeStruct((1, 2 * D), F32)],
        compiler_params=_params(("parallel", "arbitrary")),
        name="filter_expand",
    )(h, t, f_w3, deltas2)
    return taps, asum


def dft_tables(N1, N2):
    N = N1 * N2
    half = N1 // 2
    k1 = jnp.arange(N1, dtype=jnp.int32)[None, :, None]
    n = (jnp.arange(half, dtype=jnp.int32)[None, None, :] * N2
         + jnp.arange(N2, dtype=jnp.int32)[:, None, None])
    ang = ((k1 * n) % N).astype(F32) * (2.0 * math.pi / N)
    c, s = jnp.cos(ang), jnp.sin(ang)
    fwd = jnp.concatenate([jnp.concatenate([c, s], axis=2),
                           jnp.concatenate([-s, c], axis=2)], axis=1)
    inv = jnp.swapaxes(fwd, 1, 2)
    fwd_real = jnp.concatenate([c, -s], axis=1)
    k2 = jnp.arange(N2, dtype=jnp.int32)
    ang2 = ((k2[:, None] * k2[None, :]) % N2).astype(F32) * (2.0 * math.pi / N2)
    c2, s2 = jnp.cos(ang2), jnp.sin(ang2)
    f2 = jnp.concatenate([jnp.concatenate([c2, s2], axis=1),
                          jnp.concatenate([-s2, c2], axis=1)], axis=0)
    g2 = f2.T
    return fwd.astype(BF16), inv.astype(BF16), fwd_real.astype(BF16), f2.astype(BF16), g2.astype(BF16)


def _pack_complex(re, im):
    rb = lax.bitcast_convert_type(re.astype(BF16).astype(F32), jnp.uint32)
    ib = lax.bitcast_convert_type(im.astype(BF16).astype(F32), jnp.uint32)
    return (rb >> 16) | (ib & jnp.uint32(0xFFFF0000))


def _unpack_complex(p):
    re = lax.bitcast_convert_type(p << 16, F32)
    im = lax.bitcast_convert_type(p & jnp.uint32(0xFFFF0000), F32)
    return jnp.concatenate([re, im], axis=0).astype(BF16)


def _double_buffered(in_copies, out_copies, compute):
    j = pl.program_id(0)
    n = pl.num_programs(0)
    slot = j % 2

    @pl.when(j == 0)
    def _():
        for cp in in_copies(0, 0):
            cp.start()

    @pl.when(j + 1 < n)
    def _():
        for cp in in_copies(j + 1, 1 - slot):
            cp.start()

    for cp in in_copies(j, slot):
        cp.wait()

    @pl.when(j >= 2)
    def _():
        for cp in out_copies(j - 2, slot):
            cp.wait()

    compute(slot)
    for cp in out_copies(j, slot):
        cp.start()

    @pl.when(j == n - 1)
    def _():
        for cp in out_copies(j, slot):
            cp.wait()

        @pl.when(n >= 2)
        def _():
            for cp in out_copies(j - 1, 1 - slot):
                cp.wait()


def _fft_s1_kernel(z_hbm, m_ref, a_hbm, zbuf, obuf, sem_in, sem_out):
    half = zbuf.shape[1] // 2
    n1 = obuf.shape[1]

    def in_copies(step, slot):
        return [pltpu.make_async_copy(z_hbm.at[b, :, step, :], zbuf.at[slot, pl.ds(b * half, half)], sem_in.at[slot])
                for b in range(2)]

    def out_copies(step, slot):
        return [pltpu.make_async_copy(obuf.at[slot], a_hbm.at[:, step, :], sem_out.at[slot])]

    def compute(slot):
        a = _dot(m_ref[...], zbuf[slot].astype(BF16))
        obuf[slot] = _pack_complex(a[:n1], a[n1:])

    _double_buffered(in_copies, out_copies, compute)


def _fft_s1_real_kernel(k_ref, m_ref, a_hbm, obuf, sem_out):
    n1 = obuf.shape[1]

    def out_copies(step, slot):
        return [pltpu.make_async_copy(obuf.at[slot], a_hbm.at[:, step, :], sem_out.at[slot])]

    def compute(slot):
        a = _dot(m_ref[...], k_ref[...].astype(BF16))
        obuf[slot] = _pack_complex(a[:n1], a[n1:])

    _double_buffered(lambda step, slot: [], out_copies, compute)


def _fft_mid_kernel(a_ref, af_ref, ab_ref, s_ref, f2_ref, g2_ref, c_ref):
    n2 = a_ref.shape[0]
    f2 = f2_ref[...]
    xf = _dot(f2, _unpack_complex(af_ref[...]))
    xb = _dot(f2, _unpack_complex(ab_ref[...]))
    s = s_ref[...]
    kr = (xf[:n2] + xb[:n2]) * s
    ki = (xf[n2:] - xb[n2:]) * s
    x = _dot(f2, _unpack_complex(a_ref[...]))
    xr, xi = x[:n2], x[n2:]
    y = jnp.concatenate([xr * kr - xi * ki, xr * ki + xi * kr], axis=0).astype(BF16)
    c = _dot(g2_ref[...], y)
    c_ref[...] = _pack_complex(c[:n2], c[n2:])


def _fft_out_kernel(c_hbm, m_ref, zv_hbm, x0_hbm, sk_ref, y_hbm, cbuf, zbuf, xbuf, ybuf, sem_in, sem_out):
    half = zbuf.shape[1] // 2

    def in_copies(step, slot):
        cps = [pltpu.make_async_copy(c_hbm.at[:, step, :], cbuf.at[slot], sem_in.at[slot])]
        for b in range(2):
            rows = pl.ds(b * half, half)
            cps.append(pltpu.make_async_copy(zv_hbm.at[b, :, step, :], zbuf.at[slot, rows], sem_in.at[slot]))
            cps.append(pltpu.make_async_copy(x0_hbm.at[b, :, step, :], xbuf.at[slot, rows], sem_in.at[slot]))
        return cps

    def out_copies(step, slot):
        return [pltpu.make_async_copy(ybuf.at[slot, pl.ds(b * half, half)], y_hbm.at[b, :, step, :], sem_out.at[slot])
                for b in range(2)]

    def compute(slot):
        y = _dot(m_ref[...], _unpack_complex(cbuf[slot]))
        ybuf[slot] = (y + zbuf[slot] * sk_ref[...]) * xbuf[slot]

    _double_buffered(in_copies, out_copies, compute)


def hyena_long_conv(zv, x0, taps, asum, skip_b, L):
    D = zv.shape[1]
    N = 2 * L
    N2 = DFT_N2
    N1 = N // N2
    half = N1 // 2
    fwd, inv, fwd_real, f2, g2 = dft_tables(N1, N2)
    td2 = _tile(D, 2048)
    nd2 = D // td2
    zv4 = zv.reshape(2, half, N2, D)
    x04 = x0.reshape(2, half, N2, D)
    any_spec = pl.BlockSpec(memory_space=pl.ANY)

    af = pl.pallas_call(
        _fft_s1_real_kernel,
        grid=(N2,),
        in_specs=[pl.BlockSpec((None, half, 2 * D), lambda j: (j, 0, 0)),
                  pl.BlockSpec((None, 2 * N1, half), lambda j: (j, 0, 0))],
        out_specs=any_spec,
        out_shape=jax.ShapeDtypeStruct((N1, N2, 2 * D), jnp.uint32),
        scratch_shapes=[pltpu.VMEM((2, N1, 2 * D), jnp.uint32), pltpu.SemaphoreType.DMA((2,))],
        compiler_params=_params(("arbitrary",)),
        name="filter_dft_stage1",
    )(taps.reshape(N2, half, 2 * D), fwd_real)
    scale = 1.0 / ((asum[:, :D] + asum[:, D:]) * N)

    a = pl.pallas_call(
        _fft_s1_kernel,
        grid=(N2,),
        in_specs=[any_spec,
                  pl.BlockSpec((None, 2 * N1, N1), lambda j: (j, 0, 0))],
        out_specs=any_spec,
        out_shape=jax.ShapeDtypeStruct((N1, N2, D), jnp.uint32),
        scratch_shapes=[pltpu.VMEM((2, N1, D), F32), pltpu.VMEM((2, N1, D), jnp.uint32),
                        pltpu.SemaphoreType.DMA((2,)), pltpu.SemaphoreType.DMA((2,))],
        compiler_params=_params(("arbitrary",)),
        name="conv_dft_stage1",
    )(zv4, fwd)

    c = pl.pallas_call(
        _fft_mid_kernel,
        grid=(N1, nd2),
        in_specs=[pl.BlockSpec((None, N2, td2), lambda k, j: (k, 0, j)),
                  pl.BlockSpec((None, N2, td2), lambda k, j: (k, 0, j)),
                  pl.BlockSpec((None, N2, td2), lambda k, j: (k, 0, j + nd2)),
                  pl.BlockSpec((1, td2), lambda k, j: (0, j)),
                  pl.BlockSpec((2 * N2, 2 * N2), lambda k, j: (0, 0)),
                  pl.BlockSpec((2 * N2, 2 * N2), lambda k, j: (0, 0))],
        out_specs=pl.BlockSpec((None, N2, td2), lambda k, j: (k, 0, j)),
        out_shape=jax.ShapeDtypeStruct((N1, N2, D), jnp.uint32),
        compiler_params=_params(("parallel", "parallel")),
        name="conv_dft_stage2",
    )(a, af, af, scale, f2, g2)

    y = pl.pallas_call(
        _fft_out_kernel,
        grid=(N2,),
        in_specs=[any_spec,
                  pl.BlockSpec((None, N1, 2 * N1), lambda j: (j, 0, 0)),
                  any_spec, any_spec,
                  pl.BlockSpec((1, D), lambda j: (0, 0))],
        out_specs=any_spec,
        out_shape=jax.ShapeDtypeStruct((2, half, N2, D), F32),
        scratch_shapes=[pltpu.VMEM((2, N1, D), jnp.uint32), pltpu.VMEM((2, N1, D), F32),
                        pltpu.VMEM((2, N1, D), F32), pltpu.VMEM((2, N1, D), F32),
                        pltpu.SemaphoreType.DMA((2,)), pltpu.SemaphoreType.DMA((2,))],
        compiler_params=_params(("arbitrary",)),
        name="conv_dft_inverse",
    )(c, inv, zv4, x04, skip_b)
    return y.reshape(2 * L, D)


VT_ROWS = HEAD_DIM + BF16_SUBLANES


def _qkv_kernel(x_ref, g_ref, w_ref, gain_ref, cos_ref, sin_ref, o_ref, va_ref, vt_ref, hn_ref, *, n_rope_tiles):
    j = pl.program_id(1)

    @pl.when(j == 0)
    def _():
        hn_ref[...] = _norm_rows(x_ref[...], g_ref[...]).astype(BF16)

    u = _dot(hn_ref[...], w_ref[...])
    tn = u.shape[1]

    @pl.when(j < n_rope_tiles)
    def _():
        gain = gain_ref[...]
        cos = cos_ref[...]
        sin = sin_ref[...]
        even = (lax.broadcasted_iota(jnp.int32, cos.shape, 1) % 2) == 0
        for hh in range(tn // HEAD_DIM):
            xh = u[:, hh * HEAD_DIM:(hh + 1) * HEAD_DIM]
            y = _norm_rows(xh, gain)
            sw = jnp.where(even, pltpu.roll(y, HEAD_DIM - 1, 1), pltpu.roll(y, 1, 1))
            o_ref[:, hh * HEAD_DIM:(hh + 1) * HEAD_DIM] = (y * cos + sw * sin).astype(o_ref.dtype)

    @pl.when(j >= n_rope_tiles)
    def _():
        o_ref[...] = u.astype(o_ref.dtype)
        ones = jnp.ones((u.shape[0], HEAD_DIM), va_ref.dtype)
        ones_t = jnp.ones((VT_ROWS - HEAD_DIM, u.shape[0]), vt_ref.dtype)
        for hh in range(tn // HEAD_DIM):
            vh = u[:, hh * HEAD_DIM:(hh + 1) * HEAD_DIM]
            va_ref[:, 2 * hh * HEAD_DIM:(2 * hh + 1) * HEAD_DIM] = vh.astype(va_ref.dtype)
            va_ref[:, (2 * hh + 1) * HEAD_DIM:(2 * hh + 2) * HEAD_DIM] = ones
            vt_ref[hh, :HEAD_DIM, :] = vh.T.astype(vt_ref.dtype)
            vt_ref[hh, HEAD_DIM:, :] = ones_t


def rope_tables(L):
    rows = L // GRID_W
    r, c = jnp.meshgrid(jnp.arange(rows, dtype=F32), jnp.arange(GRID_W, dtype=F32), indexing="ij")
    r = r.reshape(L)
    c = c.reshape(L)
    axis_dim = HEAD_DIM // 2
    inv = 1.0 / (ROPE_THETA ** (jnp.arange(0, axis_dim, 2, dtype=F32) / axis_dim))
    ang = jnp.concatenate([r[:, None] * inv[None], c[:, None] * inv[None]], axis=-1)
    cos = jnp.repeat(jnp.cos(ang), 2, axis=1)
    sign = jnp.tile(jnp.array([-1.0, 1.0], F32), HEAD_DIM // 2)[None, :]
    sin = jnp.repeat(jnp.sin(ang), 2, axis=1) * sign
    return cos, sin


def qkv_project(x, g, w_qkv, q_gain, k_gain, seq_len, *, tm=512):
    T, D = x.shape
    n_out = w_qkv.shape[1]
    nk = N_KV_HEADS * HEAD_DIM
    nq = n_out - 2 * nk
    tn = nk
    tm = _tile(seq_len, tm)
    n_rope = (nq + nk) // tn
    gains = jnp.concatenate([jnp.tile((q_gain * SCORE_SCALE)[None, None, :], (nq // tn, 1, 1)),
                             k_gain[None, None, :],
                             jnp.ones((1, 1, HEAD_DIM), F32)], axis=0)
    cos, sin = rope_tables(seq_len)
    spt = seq_len // tm
    kern = functools.partial(_qkv_kernel, n_rope_tiles=n_rope)
    return pl.pallas_call(
        kern,
        grid=(T // tm, n_out // tn),
        in_specs=[pl.BlockSpec((tm, D), lambda i, j: (i, 0)),
                  pl.BlockSpec((1, D), lambda i, j: (0, 0)),
                  pl.BlockSpec((D, tn), lambda i, j: (0, j)),
                  pl.BlockSpec((None, 1, HEAD_DIM), lambda i, j: (j, 0, 0)),
                  pl.BlockSpec((tm, HEAD_DIM), lambda i, j: (i % spt, 0)),
                  pl.BlockSpec((tm, HEAD_DIM), lambda i, j: (i % spt, 0))],
        out_specs=[pl.BlockSpec((tm, tn), lambda i, j: (i, j)),
                   pl.BlockSpec((tm, 2 * nk), lambda i, j: (i, 0)),
                   pl.BlockSpec((None, N_KV_HEADS, VT_ROWS, tm), lambda i, j: (i // spt, 0, 0, i % spt))],
        out_shape=[jax.ShapeDtypeStruct((T, n_out), BF16), jax.ShapeDtypeStruct((T, 2 * nk), BF16),
                   jax.ShapeDtypeStruct((T // seq_len, N_KV_HEADS, VT_ROWS, seq_len), BF16)],
        scratch_shapes=[pltpu.VMEM((tm, D), BF16)],
        compiler_params=_params(("parallel", "arbitrary")),
        name="qkv_project",
    )(x, g, w_qkv, gains, cos, sin)


def _flash_kernel(q_ref, k_ref, v_ref, o_ref, qs_ref, s0_ref, s1_ref, p0_ref, p1_ref, a0_ref, a1_ref,
                  m_ref, acc_ref, *, group, tq, tk, nk):
    s_refs = (s0_ref, s1_ref)
    p_refs = (p0_ref, p1_ref)
    a_refs = (a0_ref, a1_ref)
    for gq in range(group):
        qs_ref[gq * tq:(gq + 1) * tq, :] = q_ref[:, gq * HEAD_DIM:(gq + 1) * HEAD_DIM]
    m_ref[...] = jnp.full_like(m_ref, -jnp.inf)
    acc_ref[...] = jnp.zeros_like(acc_ref)

    def rows_of(j):
        if isinstance(j, int):
            return pl.ds(j * tk, tk)
        return pl.ds(pl.multiple_of(j * tk, tk), tk)

    def scores(j, slot):
        kb = k_ref[rows_of(j), :]
        s_refs[slot][...] = lax.dot_general(qs_ref[...], kb, (((1,), (1,)), ((), ())),
                                            preferred_element_type=F32)

    def softmax(slot):
        s = s_refs[slot][...]
        m_prev = m_ref[...]
        m_new = jnp.maximum(m_prev, jnp.max(s, axis=-1, keepdims=True))
        a_refs[slot][...] = jnp.exp2(m_prev - m_new)
        p_refs[slot][...] = jnp.exp2(s - jnp.tile(m_new, (1, tk // LANES))).astype(BF16)
        m_ref[...] = m_new

    def accumulate(j, slot):
        pv = _dot(p_refs[slot][...], v_ref[rows_of(j), :])
        acc_ref[...] = acc_ref[...] * jnp.tile(a_refs[slot][...], (1, 2)) + pv

    def step(j, slot, first, last):
        if not last:
            scores(j + 1, 1 - slot)
        softmax(slot)
        if not first:
            accumulate(j - 1, 1 - slot)

    scores(0, 0)
    if nk >= 4 and nk % 2 == 0:
        step(0, 0, True, False)

        def pair(t, c):
            j = 2 * t + 1
            step(j, 1, False, False)
            step(j + 1, 0, False, False)
            return c

        lax.fori_loop(0, (nk - 2) // 2, pair, 0)
        step(nk - 1, 1, False, True)
    else:
        for j in range(nk):
            step(j, j % 2, j == 0, j == nk - 1)
    accumulate(nk - 1, (nk - 1) % 2)

    acc = acc_ref[...]
    o = acc[:, :HEAD_DIM] / acc[:, HEAD_DIM:]
    for gq in range(group):
        o_ref[:, gq * HEAD_DIM:(gq + 1) * HEAD_DIM] = o[gq * tq:(gq + 1) * tq].astype(o_ref.dtype)


def _flash_bounded_kernel(q_ref, k_ref, vt_ref, o_ref, qt_ref, acc_ref, *, group, tq, tk, nk, unroll):
    for gq in range(group):
        qt_ref[:, gq * tq:(gq + 1) * tq] = q_ref[:, gq * HEAD_DIM:(gq + 1) * HEAD_DIM].astype(F32).T.astype(BF16)
    acc_ref[...] = jnp.zeros_like(acc_ref)

    def block(j, c):
        rows = pl.ds(pl.multiple_of(j * tk, tk), tk)
        s_t = _dot(k_ref[rows, :], qt_ref[...])
        acc_ref[...] += _dot(vt_ref[:, rows], jnp.exp2(s_t).astype(BF16))
        return c

    lax.fori_loop(0, nk, block, 0, unroll=unroll)
    acc = acc_ref[...]
    o_t = acc[:HEAD_DIM] / acc[HEAD_DIM:HEAD_DIM + 1]
    for gq in range(group):
        o_ref[:, gq * HEAD_DIM:(gq + 1) * HEAD_DIM] = o_t[:, gq * tq:(gq + 1) * tq].T.astype(o_ref.dtype)


SAFE_SCORE_BOUND = 80.0


def flash_attention(qkv, v_aug, v_t, score_bound, B, L, n_heads, *, tq=256, tk=512):
    group = n_heads // N_KV_HEADS
    tq = _tile(L, tq)
    tk = _tile(L, tk)
    nk = L // tk
    qw = group * HEAD_DIM
    k_col = n_heads
    rows = group * tq
    q_spec = pl.BlockSpec((None, tq, qw), lambda b, h, qi: (b, qi, h))
    k_spec = pl.BlockSpec((None, L, HEAD_DIM), lambda b, h, qi: (b, 0, k_col + h))
    common = dict(
        grid=(B, N_KV_HEADS, L // tq),
        out_specs=pl.BlockSpec((None, tq, qw), lambda b, h, qi: (b, qi, h)),
        out_shape=jax.ShapeDtypeStruct((B, L, n_heads * HEAD_DIM), BF16),
        compiler_params=_params(("parallel", "parallel", "parallel")),
    )
    general = pl.pallas_call(
        functools.partial(_flash_kernel, group=group, tq=tq, tk=tk, nk=nk),
        in_specs=[q_spec, k_spec, pl.BlockSpec((None, L, 2 * HEAD_DIM), lambda b, h, qi: (b, 0, h))],
        scratch_shapes=[pltpu.VMEM((rows, HEAD_DIM), BF16),
                        pltpu.VMEM((rows, tk), F32), pltpu.VMEM((rows, tk), F32),
                        pltpu.VMEM((rows, tk), BF16), pltpu.VMEM((rows, tk), BF16),
                        pltpu.VMEM((rows, LANES), F32), pltpu.VMEM((rows, LANES), F32),
                        pltpu.VMEM((rows, LANES), F32),
                        pltpu.VMEM((rows, 2 * HEAD_DIM), F32)],
        name="flash_attention", **common)
    bounded = pl.pallas_call(
        functools.partial(_flash_bounded_kernel, group=group, tq=tq, tk=tk, nk=nk,
                          unroll=math.gcd(nk, 8)),
        in_specs=[q_spec, k_spec, pl.BlockSpec((None, None, VT_ROWS, L), lambda b, h, qi: (b, h, 0, 0))],
        scratch_shapes=[pltpu.VMEM((HEAD_DIM, rows), BF16),
                        pltpu.VMEM((VT_ROWS, rows), F32)],
        name="flash_attention_bounded", **common)
    return lax.cond(score_bound <= SAFE_SCORE_BOUND,
                    lambda ops: bounded(ops[0], ops[0], ops[2]),
                    lambda ops: general(ops[0], ops[0], ops[1]),
                    (qkv, v_aug, v_t))


def _pack_bf16_pairs(x):
    half = x.shape[1] // 2
    bits = lax.bitcast_convert_type(x.astype(BF16).astype(F32), jnp.uint32)
    return (bits[:, :half] >> 16) | (bits[:, half:] & jnp.uint32(0xFFFF0000))


def _unpack_bf16_pairs(p):
    lo = lax.bitcast_convert_type(p << 16, F32).astype(BF16)
    hi = lax.bitcast_convert_type(p & jnp.uint32(0xFFFF0000), F32).astype(BF16)
    return lo, hi


def _router_kernel(xa_ref, xb_ref, g_ref, rw_ref, h_ref, info_ref, cnt_ref, *, n_a_tiles):
    i = pl.program_id(0)

    def route(x_ref):
        hf = _norm_rows(x_ref[...], g_ref[...])
        h_ref[...] = _pack_bf16_pairs(hf)
        h_hi = hf.astype(BF16)
        h_lo = (hf - h_hi.astype(F32)).astype(BF16)
        rw = rw_ref[...]
        hw = _dot(h_hi, rw)
        logits = hw[:, :LANES] + hw[:, LANES:] + _dot(h_lo, rw[:, :LANES])
        shape = logits.shape
        lane = lax.broadcasted_iota(jnp.int32, shape, 1)
        lanef = lane.astype(F32)
        neg = -jnp.inf
        big = float(LANES)
        is_g = lane < N_GROUPS
        gl = jnp.where(is_g, logits, neg)
        gmax = jnp.max(gl, axis=-1, keepdims=True)
        gidx = jnp.min(jnp.where(gl == gmax, lanef, big), axis=-1, keepdims=True)
        p_g = 1.0 / jnp.sum(jnp.where(is_g, jnp.exp(logits - gmax), 0.0), axis=-1, keepdims=True)
        lo = N_GROUPS + gidx * EXPERTS_PER_GROUP
        el = jnp.where(lanef >= lo, jnp.where(lanef < lo + EXPERTS_PER_GROUP, logits, neg), neg)
        v1 = jnp.max(el, axis=-1, keepdims=True)
        i1 = jnp.min(jnp.where(el == v1, lanef, big), axis=-1, keepdims=True)
        el2 = jnp.where(lanef == i1, neg, el)
        v2 = jnp.max(el2, axis=-1, keepdims=True)
        i2 = jnp.min(jnp.where(el2 == v2, lanef, big), axis=-1, keepdims=True)
        t = jnp.exp(v2 - v1)
        w1 = p_g / (1.0 + t)
        w2 = p_g * t / (1.0 + t)
        e1 = i1 - N_GROUPS
        e2 = i2 - N_GROUPS
        oh1 = jnp.where(lanef == e1, 1.0, 0.0)
        oh2 = jnp.where(lanef == e2, 1.0, 0.0)
        oh = oh1 + oh2
        tm = shape[0]
        tri = jnp.where(lax.broadcasted_iota(jnp.int32, (tm, tm), 1) < lax.broadcasted_iota(jnp.int32, (tm, tm), 0),
                        1.0, 0.0).astype(BF16)
        before = _dot(tri, oh.astype(BF16)) + cnt_ref[...]
        r1 = jnp.sum(before * oh1, axis=-1, keepdims=True)
        r2 = jnp.sum(before * oh2, axis=-1, keepdims=True)
        cnt_ref[...] += jnp.sum(oh, axis=0, keepdims=True)
        vals = (e1, e2, w1, w2, r1, r2)
        info = jnp.zeros(shape, F32)
        for idx, val in enumerate(vals):
            info = jnp.where(lane == idx, val, info)
        info_ref[...] = info

    @pl.when(i == 0)
    def _():
        cnt_ref[...] = jnp.zeros_like(cnt_ref)

    @pl.when(i < n_a_tiles)
    def _():
        route(xa_ref)

    @pl.when(i >= n_a_tiles)
    def _():
        route(xb_ref)


def moe_route(xa, xb, g, router_group, router_expert, *, tm=512):
    Ta, D = xa.shape
    Tb = xb.shape[0]
    tm = _tile(math.gcd(Ta, Tb), tm)
    na, nb = Ta // tm, Tb // tm
    n_e = N_GROUPS * EXPERTS_PER_GROUP
    rw = jnp.concatenate([router_group, jnp.transpose(router_expert, (1, 0, 2)).reshape(D, n_e)], axis=1)
    rw = jnp.pad(rw, ((0, 0), (0, LANES - rw.shape[1])))
    rw_hi = rw.astype(BF16)
    rw = jnp.concatenate([rw_hi, (rw - rw_hi.astype(F32)).astype(BF16)], axis=1)
    kern = functools.partial(_router_kernel, n_a_tiles=na)
    return pl.pallas_call(
        kern,
        grid=(na + nb,),
        in_specs=[pl.BlockSpec((tm, D), lambda i: (jnp.minimum(i, na - 1), 0)),
                  pl.BlockSpec((tm, D), lambda i: (jnp.maximum(i - na, 0), 0)),
                  pl.BlockSpec((1, D), lambda i: (0, 0)),
                  pl.BlockSpec((D, 2 * LANES), lambda i: (0, 0))],
        out_specs=[pl.BlockSpec((tm, D // 2), lambda i: (i, 0)),
                   pl.BlockSpec((tm, LANES), lambda i: (i, 0)),
                   pl.BlockSpec((1, LANES), lambda i: (0, 0))],
        out_shape=[jax.ShapeDtypeStruct((Ta + Tb, D // 2), jnp.uint32),
                   jax.ShapeDtypeStruct((Ta + Tb, LANES), F32),
                   jax.ShapeDtypeStruct((1, LANES), F32)],
        compiler_params=_params(("arbitrary",)),
        name="moe_route",
    )(xa, xb, g, rw)


def moe_plan(info, cnt, tm, n_experts):
    T = info.shape[0]
    n_tiles = (2 * T) // tm + n_experts
    counts = cnt[0, :n_experts].astype(jnp.int32)
    padded = ((counts + tm - 1) // tm) * tm
    ends = jnp.cumsum(padded)
    off = ends - padded
    e = info[:, 0:2].astype(jnp.int32)
    rank = info[:, 4:6].astype(jnp.int32)
    pos = (off[e] + rank).reshape(-1)
    tile_start = jnp.arange(n_tiles, dtype=jnp.int32) * tm
    tile_e = jnp.sum((tile_start[:, None] >= ends[None, :]).astype(jnp.int32), axis=1)
    tile_e = jnp.minimum(tile_e, n_experts - 1)
    tile_valid = jnp.clip(counts[tile_e] - (tile_start - off[tile_e]), 0, tm).astype(jnp.int32)
    n_used = (ends[-1:] // tm).astype(jnp.int32)
    return pos, tile_e, tile_valid, n_used


DMA_UNROLL = 8


def _dispatch_kernel(pos_ref, h_ref, zeros_hbm, hs_hbm, sem):
    del zeros_hbm
    tm = h_ref.shape[0]

    def row_copy(r, slot):
        return pltpu.make_async_copy(h_ref.at[pl.ds(r, 1)], hs_hbm.at[pl.ds(pos_ref[0, 0, 2 * r + slot], 1)], sem)

    def start(r, c):
        row_copy(r, 0).start()
        row_copy(r, 1).start()
        return c

    def wait(r, c):
        row_copy(r, 0).wait()
        row_copy(r, 1).wait()
        return c

    lax.fori_loop(0, tm, start, 0, unroll=DMA_UNROLL)
    lax.fori_loop(0, tm, wait, 0, unroll=DMA_UNROLL)


def moe_dispatch(h, pos, n_sorted_rows, *, tm=512):
    T, C = h.shape
    tm = _tile(T, tm)
    return pl.pallas_call(
        _dispatch_kernel,
        grid=(T // tm,),
        in_specs=[pl.BlockSpec((1, 1, 2 * tm), lambda i: (i, 0, 0), memory_space=pltpu.SMEM),
                  pl.BlockSpec((tm, C), lambda i: (i, 0)),
                  pl.BlockSpec(memory_space=pl.ANY)],
        out_specs=pl.BlockSpec(memory_space=pl.ANY),
        out_shape=jax.ShapeDtypeStruct((n_sorted_rows, C), h.dtype),
        scratch_shapes=[pltpu.SemaphoreType.DMA],
        input_output_aliases={2: 0},
        compiler_params=_params(("arbitrary",)),
        name="moe_dispatch",
    )(pos.reshape(T // tm, 1, 2 * tm), h, jnp.zeros((n_sorted_rows, C), h.dtype))


def _expert_kernel(te_ref, tv_ref, nu_ref, x_ref, wg_ref, wu_ref, wd_ref, y_ref):
    del te_ref, nu_ref
    valid = tv_ref[pl.program_id(0)]

    @pl.when(valid == 0)
    def _():
        y_ref[...] = jnp.zeros_like(y_ref)

    @pl.when(valid > 0)
    def _():
        x_lo, x_hi = _unpack_bf16_pairs(x_ref[...])
        half = x_ref.shape[1]
        gate = (_dot(x_lo, wg_ref[:half, :].astype(BF16)) + _dot(x_hi, wg_ref[half:, :].astype(BF16)))
        up = (_dot(x_lo, wu_ref[:half, :].astype(BF16)) + _dot(x_hi, wu_ref[half:, :].astype(BF16)))
        a = (gate * jax.nn.sigmoid(gate) * up).astype(BF16)
        y_ref[...] = _dot(a, wd_ref[...].astype(BF16))


def moe_experts(hs, tile_e, tile_valid, n_used, w_gate, w_up, w_down, layer, tm):
    P = hs.shape[0]
    D = w_gate.shape[2]
    d_e = w_gate.shape[-1]

    def row_map(i, te, tv, nu):
        return (jnp.minimum(i, nu[0] - 1), 0)

    def out_map(i, te, tv, nu):
        return (i, 0)

    def w_map(i, te, tv, nu):
        return (layer, te[i], 0, 0)

    grid_spec = pltpu.PrefetchScalarGridSpec(
        num_scalar_prefetch=3,
        grid=(P // tm,),
        in_specs=[pl.BlockSpec((tm, D // 2), row_map),
                  pl.BlockSpec((None, None, D, d_e), w_map),
                  pl.BlockSpec((None, None, D, d_e), w_map),
                  pl.BlockSpec((None, None, d_e, D), w_map)],
        out_specs=pl.BlockSpec((tm, D), out_map),
    )
    return pl.pallas_call(
        _expert_kernel,
        grid_spec=grid_spec,
        out_shape=jax.ShapeDtypeStruct((P, D), F32),
        compiler_params=_params(("arbitrary",)),
        name="moe_experts",
    )(tile_e, tile_valid, n_used, hs, w_gate, w_up, w_down)


def _combine_kernel(pos_ref, posn_ref, x_ref, info_ref, g_ref, y_hbm, o_ref, ybuf, sem, *, final_norm):
    tm = x_ref.shape[0]
    i = pl.program_id(0)
    buf = i % 2

    def row_copy(p_ref, b, r, slot):
        return pltpu.make_async_copy(y_hbm.at[pl.ds(p_ref[0, 0, 2 * r + slot], 1)],
                                     ybuf.at[b, slot, pl.ds(r, 1)], sem.at[b])

    def start_tile(p_ref, b):
        def body(r, c):
            row_copy(p_ref, b, r, 0).start()
            row_copy(p_ref, b, r, 1).start()
            return c
        lax.fori_loop(0, tm, body, 0, unroll=DMA_UNROLL)

    @pl.when(i == 0)
    def _():
        start_tile(pos_ref, 0)

    @pl.when(i + 1 < pl.num_programs(0))
    def _():
        start_tile(posn_ref, 1 - buf)

    def wait(r, c):
        row_copy(pos_ref, buf, r, 0).wait()
        row_copy(pos_ref, buf, r, 1).wait()
        return c

    lax.fori_loop(0, tm, wait, 0, unroll=DMA_UNROLL)
    info = info_ref[...]
    y = info[:, 2:3] * ybuf[buf, 0] + info[:, 3:4] * ybuf[buf, 1]
    x = x_ref[...] + y
    if final_norm:
        x = _norm_rows(x, g_ref[...])
    o_ref[...] = x


def moe_combine(x, y, pos, info, row_off, g_final, final_norm, *, tm=256):
    Tg, D = x.shape
    T = info.shape[0]
    tm = _tile(math.gcd(Tg, row_off) if row_off else Tg, tm)
    ob = row_off // tm
    kern = functools.partial(_combine_kernel, final_norm=final_norm)
    n = Tg // tm
    pos3 = pos.reshape(T // tm, 1, 2 * tm)
    return pl.pallas_call(
        kern,
        grid=(n,),
        in_specs=[pl.BlockSpec((1, 1, 2 * tm), lambda i: (i + ob, 0, 0), memory_space=pltpu.SMEM),
                  pl.BlockSpec((1, 1, 2 * tm), lambda i: (jnp.minimum(i + 1, n - 1) + ob, 0, 0),
                               memory_space=pltpu.SMEM),
                  pl.BlockSpec((tm, D), lambda i: (i, 0)),
                  pl.BlockSpec((tm, LANES), lambda i: (i + ob, 0)),
                  pl.BlockSpec((1, D), lambda i: (0, 0)),
                  pl.BlockSpec(memory_space=pl.ANY)],
        out_specs=pl.BlockSpec((tm, D), lambda i: (i, 0)),
        out_shape=jax.ShapeDtypeStruct((Tg, D), F32),
        scratch_shapes=[pltpu.VMEM((2, 2, tm, D), F32), pltpu.SemaphoreType.DMA((2,))],
        compiler_params=_params(("arbitrary",)),
        name="moe_combine",
    )(pos3, pos3, x, info, g_final, y)


MOE_TILE = 512


def hier_moe_block(xs, g, router_group, router_expert, w_gate, w_up, w_down, layer, g_final, final_norm):
    xa, xb = xs
    n_experts = w_gate.shape[1]
    h, info, cnt = moe_route(xa, xb, g, router_group, router_expert)
    T = h.shape[0]
    tm = _tile(2 * T, MOE_TILE)
    pos, tile_e, tile_valid, n_used = moe_plan(info, cnt, tm, n_experts)
    hs = moe_dispatch(h, pos, tile_e.shape[0] * tm)
    y = moe_experts(hs, tile_e, tile_valid, n_used, w_gate, w_up, w_down, layer, tm)
    oa = moe_combine(xa, y, pos, info, 0, g_final, final_norm)
    ob = moe_combine(xb, y, pos, info, xa.shape[0], g_final, final_norm)
    return oa, ob


def hyena_block(x, L, g, w_in, conv_w, conv_b, f_w1, f_b1, f_win, f_bin, f_freq, f_w3, skip_b, w_out):
    D = x.shape[1]
    x0, zv = hyena_inproj(x, g, w_in, conv_w, conv_b[None, :], L)
    taps, asum = hyena_filter_taps(L, w_out.shape[0], f_w1, f_b1, f_win, f_bin, f_freq, f_w3)
    y = hyena_long_conv(zv, x0, taps, asum, skip_b[None, :], L)
    return matmul_residual(y, w_out, x)


def attention_block(x, B, L, g, w_qkv, q_gain, k_gain, w_o):
    D = x.shape[1]
    n_heads = w_o.shape[0] // HEAD_DIM
    qkv, v_aug, v_t = qkv_project(x, g, w_qkv, q_gain, k_gain, L)
    score_bound = 1.02 * HEAD_DIM * SCORE_SCALE * jnp.max(jnp.abs(q_gain)) * jnp.max(jnp.abs(k_gain))
    o = flash_attention(qkv.reshape(B, L, qkv.shape[1]), v_aug.reshape(B, L, v_aug.shape[1]), v_t, score_bound,
                        B, L, n_heads)
    return matmul_residual(o.reshape(B * L, n_heads * HEAD_DIM), w_o, x)


def kernel(x_prompt, x_sample, norm_mix, norm_ffn, norm_final, hy_w_in, hy_conv_w, hy_conv_b, hy_f_w1, hy_f_b1, hy_f_win, hy_f_bin, hy_f_freq, hy_f_w3, hy_skip_b, hy_w_out, at_w_qkv, at_q_gain, at_k_gain, at_w_o, moe_router_group, moe_router_expert, moe_w_gate, moe_w_up, moe_w_down):
    D = x_prompt.shape[-1]
    depth = norm_mix.shape[0]
    shapes = [x_prompt.shape, x_sample.shape]
    for s in shapes:
        assert s[0] == 2, "the long convolution packs exactly two sequences per group"
    xs = [x_prompt.reshape(-1, D), x_sample.reshape(-1, D)]
    hy_w_in_b = hy_w_in.astype(BF16)
    hy_w_out_b = hy_w_out.astype(BF16)
    at_w_qkv_b = at_w_qkv.astype(BF16)
    at_w_o_b = at_w_o.astype(BF16)
    g_final = norm_final[None, :]
    for i in range(depth):
        j = i // N_MIXERS
        g = norm_mix[i][None, :]
        if i % N_MIXERS == 0:
            xs = [hyena_block(x, s[1], g, hy_w_in_b[j], hy_conv_w[j], hy_conv_b[j], hy_f_w1[j], hy_f_b1[j],
                              hy_f_win[j], hy_f_bin[j], hy_f_freq[j], hy_f_w3[j], hy_skip_b[j], hy_w_out_b[j])
                  for x, s in zip(xs, shapes)]
        else:
            xs = [attention_block(x, s[0], s[1], g, at_w_qkv_b[j], at_q_gain[j], at_k_gain[j], at_w_o_b[j])
                  for x, s in zip(xs, shapes)]
        xs = hier_moe_block(xs, norm_ffn[i][None, :], moe_router_group[i], moe_router_expert[i],
                            moe_w_gate, moe_w_up, moe_w_down, i, g_final, i == depth - 1)
    return (xs[0].reshape(shapes[0]), xs[1].reshape(shapes[1]))
```

```python
import functools
import math

import jax
import jax.numpy as jnp
import numpy as np
from jax import lax
from jax.experimental import pallas as pl
from jax.experimental.pallas import tpu as pltpu

RMS_EPS = 1e-6
GRID_W = 64
EMB_DIM = 33
DECAY_TARGET = 1e-2
FAST_DECAY_PCT = 0.3
SLOW_DECAY_PCT = 1.5
MIN_DECAY = math.log(DECAY_TARGET) / FAST_DECAY_PCT
MAX_DECAY = math.log(DECAY_TARGET) / SLOW_DECAY_PCT
HEAD_DIM = 128
SCORE_SCALE = HEAD_DIM ** -0.5 * math.log2(math.e)
N_KV_HEADS = 4
ROPE_THETA = 10000.0
N_GROUPS = 4
EXPERTS_PER_GROUP = 8
N_MIXERS = 2

LANES = 128
BF16_SUBLANES = 16
VMEM_LIMIT_BYTES = 52 * 1024 * 1024
DFT_N2 = 128

BF16 = jnp.bfloat16
F32 = jnp.float32


def _params(sem):
    return pltpu.CompilerParams(dimension_semantics=sem, vmem_limit_bytes=VMEM_LIMIT_BYTES)


def _dot(a, b):
    return jnp.dot(a, b, preferred_element_type=F32)


def _dot_3pass(a, b):
    a_hi = a.astype(BF16)
    a_lo = (a - a_hi.astype(F32)).astype(BF16)
    b_hi = b.astype(BF16)
    b_lo = (b - b_hi.astype(F32)).astype(BF16)
    return _dot(a_hi, b_hi) + _dot(a_hi, b_lo) + _dot(a_lo, b_hi)


def _norm_rows(x, g):
    ms = jnp.mean(x * x, axis=-1, keepdims=True)
    return x * lax.rsqrt(ms + RMS_EPS) * g


def _tile(n, pref):
    return pref if n % pref == 0 else n


def _matmul_res_kernel(a_ref, w_ref, r_ref, o_ref):
    o_ref[...] = r_ref[...] + _dot(a_ref[...].astype(BF16), w_ref[...])


def matmul_residual(a, w, res, *, tm=1024, tn=512):
    T, K = a.shape
    N = w.shape[1]
    tm = _tile(T, tm)
    tn = _tile(N, tn)
    return pl.pallas_call(
        _matmul_res_kernel,
        grid=(T // tm, N // tn),
        in_specs=[pl.BlockSpec((tm, K), lambda i, j: (i, 0)),
                  pl.BlockSpec((K, tn), lambda i, j: (0, j)),
                  pl.BlockSpec((tm, tn), lambda i, j: (i, j))],
        out_specs=pl.BlockSpec((tm, tn), lambda i, j: (i, j)),
        out_shape=jax.ShapeDtypeStruct((T, N), F32),
        compiler_params=_params(("parallel", "arbitrary")),
        name="matmul_residual",
    )(a, w, res)


HALO = BF16_SUBLANES


def _hy_inproj_kernel(x_ref, xp_ref, xn_ref, g_ref, w0_ref, w1_ref, w2_ref,
                      cw0_ref, cw1_ref, cw2_ref, cb0_ref, cb1_ref, cb2_ref,
                      x0_out, zv_out, hn_ref, *, tm, tiles_per_seq):
    i = pl.program_id(0)
    j = pl.program_id(1)

    @pl.when(j == 0)
    def _():
        g = g_ref[...]
        hn_ref[HALO:HALO + tm, :] = _norm_rows(x_ref[...], g).astype(BF16)
        first = (i % tiles_per_seq) == 0
        last = (i % tiles_per_seq) == tiles_per_seq - 1
        hp = _norm_rows(xp_ref[...], g)
        hn_ref[0:HALO, :] = jnp.where(first, 0.0, hp).astype(BF16)
        hx = _norm_rows(xn_ref[...], g)
        hn_ref[HALO + tm:, :] = jnp.where(last, 0.0, hx).astype(BF16)

    h = hn_ref[...]
    rows = tm + 2 * HALO

    def conv(w_ref, cw_ref, cb_ref):
        u = _dot(h, w_ref[...])
        um = pltpu.roll(u, 1, 0)[HALO:HALO + tm]
        up = pltpu.roll(u, rows - 1, 0)[HALO:HALO + tm]
        uc = u[HALO:HALO + tm]
        cw = cw_ref[...]
        return um * cw[0:1] + uc * cw[1:2] + up * cw[2:3] + cb_ref[...]

    x0_out[...] = conv(w0_ref, cw0_ref, cb0_ref)
    x1 = conv(w1_ref, cw1_ref, cb1_ref)
    v = conv(w2_ref, cw2_ref, cb2_ref)
    zv_out[...] = v * x1


def hyena_inproj(x, g, w_in, conv_w, conv_b, seq_len, *, tm=512, tn=512):
    T, D = x.shape
    W = w_in.shape[1] // 3
    tm = _tile(seq_len, tm)
    tn = _tile(W, tn)
    nj = W // tn
    hb = tm // HALO
    n_halo_blocks = T // HALO
    kern = functools.partial(_hy_inproj_kernel, tm=tm, tiles_per_seq=seq_len // tm)

    def wspec(c):
        return pl.BlockSpec((D, tn), lambda i, j, c=c: (0, j + c * nj))

    def cwspec(c):
        return pl.BlockSpec((3, tn), lambda i, j, c=c: (0, j + c * nj))

    def cbspec(c):
        return pl.BlockSpec((1, tn), lambda i, j, c=c: (0, j + c * nj))

    return pl.pallas_call(
        kern,
        grid=(T // tm, nj),
        in_specs=[pl.BlockSpec((tm, D), lambda i, j: (i, 0)),
                  pl.BlockSpec((HALO, D), lambda i, j: (jnp.maximum(i * hb - 1, 0), 0)),
                  pl.BlockSpec((HALO, D), lambda i, j: (jnp.minimum((i + 1) * hb, n_halo_blocks - 1), 0)),
                  pl.BlockSpec((1, D), lambda i, j: (0, 0)),
                  wspec(0), wspec(1), wspec(2),
                  cwspec(0), cwspec(1), cwspec(2),
                  cbspec(0), cbspec(1), cbspec(2)],
        out_specs=[pl.BlockSpec((tm, tn), lambda i, j: (i, j)),
                   pl.BlockSpec((tm, tn), lambda i, j: (i, j))],
        out_shape=[jax.ShapeDtypeStruct((T, W), F32), jax.ShapeDtypeStruct((T, W), F32)],
        scratch_shapes=[pltpu.VMEM((tm + 2 * HALO, D), BF16)],
        compiler_params=_params(("parallel", "arbitrary")),
        name="hyena_inproj",
    )(x, x, x, g, w_in, w_in, w_in, conv_w, conv_w, conv_w, conv_b, conv_b, conv_b)


def _filter_mlp_kernel(z_ref, w1_ref, b1_ref, win_ref, bin_ref, fr_ref, h_ref):
    hp = lax.Precision.HIGHEST
    fr = fr_ref[...]
    h = jnp.sin(fr * (jnp.dot(z_ref[...], w1_ref[...], precision=hp, preferred_element_type=F32) + b1_ref[...]))
    for l in range(win_ref.shape[0]):
        h = jnp.sin(fr * (jnp.dot(h, win_ref[l], precision=hp, preferred_element_type=F32) + bin_ref[l]))
    h_ref[...] = h


def _filter_expand_kernel(h_ref, t_ref, w3_ref, dl_ref, k_ref, s_ref, *, tl, n_fwd_tiles):
    j = pl.program_id(0)
    i = pl.program_id(1)
    taps = _dot_3pass(h_ref[...], w3_ref[...])
    taps = taps * jnp.exp(-t_ref[...] * dl_ref[...])
    row = i * tl + lax.broadcasted_iota(jnp.int32, taps.shape, 0)
    taps = jnp.where(jnp.logical_and(j >= n_fwd_tiles, row == 0), 0.0, taps)
    k_ref[...] = taps.astype(k_ref.dtype)

    @pl.when(i == 0)
    def _():
        s_ref[...] = jnp.zeros_like(s_ref)

    s_ref[...] += jnp.sum(jnp.abs(taps), axis=0, keepdims=True)


def hyena_filter_taps(L, D, f_w1, f_b1, f_win, f_bin, f_freq, f_w3):
    H = f_w1.shape[1]
    t = jnp.linspace(0.0, 1.0, L, dtype=F32)[:, None]
    bands = (EMB_DIM - 1) // 2
    w = 2.0 * math.pi * jnp.arange(L, dtype=F32)[:, None] / L
    fb = jnp.linspace(1e-4, bands - 1, bands, dtype=F32)[None, :]
    z = jnp.concatenate([t, jnp.cos(fb * w), -jnp.sin(fb * w)], axis=-1)

    def lag_major_to_dft_order(a):
        return a.reshape(L // DFT_N2, DFT_N2, a.shape[1]).transpose(1, 0, 2).reshape(L, a.shape[1])

    z = lag_major_to_dft_order(z)
    t = lag_major_to_dft_order(t)
    zp = jnp.pad(z, ((0, 0), (0, LANES - EMB_DIM)))
    w1p = jnp.pad(f_w1, ((0, LANES - EMB_DIM), (0, 0)))
    n_in = f_win.shape[0]
    tl = _tile(L, 512)
    h = pl.pallas_call(
        _filter_mlp_kernel,
        grid=(L // tl,),
        in_specs=[pl.BlockSpec((tl, LANES), lambda i: (i, 0)),
                  pl.BlockSpec((LANES, H), lambda i: (0, 0)),
                  pl.BlockSpec((1, H), lambda i: (0, 0)),
                  pl.BlockSpec((n_in, H, H), lambda i: (0, 0, 0)),
                  pl.BlockSpec((n_in, 1, H), lambda i: (0, 0, 0)),
                  pl.BlockSpec((1, H), lambda i: (0, 0))],
        out_specs=pl.BlockSpec((tl, H), lambda i: (i, 0)),
        out_shape=jax.ShapeDtypeStruct((L, H), F32),
        compiler_params=_params(("parallel",)),
        name="filter_mlp",
    )(zp, w1p, f_b1[None, :], f_win, f_bin[:, None, :], f_freq[None, :])

    deltas = jnp.abs(jnp.linspace(MIN_DECAY, MAX_DECAY, D, dtype=F32))[None, :]
    deltas2 = jnp.concatenate([deltas, deltas], axis=1)
    tn = _tile(D, 1024)
    n_fwd = D // tn
    kern = functools.partial(_filter_expand_kernel, tl=tl, n_fwd_tiles=n_fwd)
    taps, asum = pl.pallas_call(
        kern,
        grid=(2 * n_fwd, L // tl),
        in_specs=[pl.BlockSpec((tl, H), lambda j, i: (i, 0)),
                  pl.BlockSpec((tl, 1), lambda j, i: (i, 0)),
                  pl.BlockSpec((H, tn), lambda j, i: (0, j)),
                  pl.BlockSpec((1, tn), lambda j, i: (0, j))],
        out_specs=[pl.BlockSpec((tl, tn), lambda j, i: (i, j)),
                   pl.BlockSpec((1, tn), lambda j, i: (0, j))],
        out_shape=[jax.ShapeDtypeStruct((L, 2 * D), F32), jax.ShapeDtypeStruct((1, 2 * D), F32)],
        compiler_params=_params(("parallel", "arbitrary")),
        name="filter_expand",
    )(h, t, f_w3, deltas2)
    return taps, asum


def dft_tables(N1, N2):
    N = N1 * N2
    half = N1 // 2
    k1 = jnp.arange(N1, dtype=jnp.int32)[None, :, None]
    n = (jnp.arange(half, dtype=jnp.int32)[None, None, :] * N2
         + jnp.arange(N2, dtype=jnp.int32)[:, None, None])
    ang = ((k1 * n) % N).astype(F32) * (2.0 * math.pi / N)
    c, s = jnp.cos(ang), jnp.sin(ang)
    fwd = jnp.concatenate([jnp.concatenate([c, s], axis=2),
                           jnp.concatenate([-s, c], axis=2)], axis=1)
    inv = jnp.swapaxes(fwd, 1, 2)
    fwd_real = jnp.concatenate([c, -s], axis=1)
    k2 = jnp.arange(N2, dtype=jnp.int32)
    ang2 = ((k2[:, None] * k2[None, :]) % N2).astype(F32) * (2.0 * math.pi / N2)
    c2, s2 = jnp.cos(ang2), jnp.sin(ang2)
    f2 = jnp.concatenate([jnp.concatenate([c2, s2], axis=1),
                          jnp.concatenate([-s2, c2], axis=1)], axis=0)
    g2 = f2.T
    return fwd.astype(BF16), inv.astype(BF16), fwd_real.astype(BF16), f2.astype(BF16), g2.astype(BF16)


def _pack_complex(re, im):
    rb = lax.bitcast_convert_type(re.astype(BF16).astype(F32), jnp.uint32)
    ib = lax.bitcast_convert_type(im.astype(BF16).astype(F32), jnp.uint32)
    return (rb >> 16) | (ib & jnp.uint32(0xFFFF0000))


def _unpack_complex(p):
    re = lax.bitcast_convert_type(p << 16, F32)
    im = lax.bitcast_convert_type(p & jnp.uint32(0xFFFF0000), F32)
    return jnp.concatenate([re, im], axis=0).astype(BF16)


def _double_buffered(in_copies, out_copies, compute):
    j = pl.program_id(0)
    n = pl.num_programs(0)
    slot = j % 2

    @pl.when(j == 0)
    def _():
        for cp in in_copies(0, 0):
            cp.start()

    @pl.when(j + 1 < n)
    def _():
        for cp in in_copies(j + 1, 1 - slot):
            cp.start()

    for cp in in_copies(j, slot):
        cp.wait()

    @pl.when(j >= 2)
    def _():
        for cp in out_copies(j - 2, slot):
            cp.wait()

    compute(slot)
    for cp in out_copies(j, slot):
        cp.start()

    @pl.when(j == n - 1)
    def _():
        for cp in out_copies(j, slot):
            cp.wait()

        @pl.when(n >= 2)
        def _():
            for cp in out_copies(j - 1, 1 - slot):
                cp.wait()


def _fft_s1_kernel(z_hbm, m_ref, a_hbm, zbuf, obuf, sem_in, sem_out):
    half = zbuf.shape[1] // 2
    n1 = obuf.shape[1]

    def in_copies(step, slot):
        return [pltpu.make_async_copy(z_hbm.at[b, :, step, :], zbuf.at[slot, pl.ds(b * half, half)], sem_in.at[slot])
                for b in range(2)]

    def out_copies(step, slot):
        return [pltpu.make_async_copy(obuf.at[slot], a_hbm.at[:, step, :], sem_out.at[slot])]

    def compute(slot):
        a = _dot(m_ref[...], zbuf[slot].astype(BF16))
        obuf[slot] = _pack_complex(a[:n1], a[n1:])

    _double_buffered(in_copies, out_copies, compute)


def _fft_s1_real_kernel(k_ref, m_ref, a_hbm, obuf, sem_out):
    n1 = obuf.shape[1]

    def out_copies(step, slot):
        return [pltpu.make_async_copy(obuf.at[slot], a_hbm.at[:, step, :], sem_out.at[slot])]

    def compute(slot):
        a = _dot(m_ref[...], k_ref[...].astype(BF16))
        obuf[slot] = _pack_complex(a[:n1], a[n1:])

    _double_buffered(lambda step, slot: [], out_copies, compute)


def _fft_mid_kernel(a_ref, af_ref, ab_ref, s_ref, f2_ref, g2_ref, c_ref):
    n2 = a_ref.shape[0]
    f2 = f2_ref[...]
    xf = _dot(f2, _unpack_complex(af_ref[...]))
    xb = _dot(f2, _unpack_complex(ab_ref[...]))
    s = s_ref[...]
    kr = (xf[:n2] + xb[:n2]) * s
    ki = (xf[n2:] - xb[n2:]) * s
    x = _dot(f2, _unpack_complex(a_ref[...]))
    xr, xi = x[:n2], x[n2:]
    y = jnp.concatenate([xr * kr - xi * ki, xr * ki + xi * kr], axis=0).astype(BF16)
    c = _dot(g2_ref[...], y)
    c_ref[...] = _pack_complex(c[:n2], c[n2:])


def _fft_out_kernel(c_hbm, m_ref, zv_hbm, x0_hbm, sk_ref, y_hbm, cbuf, zbuf, xbuf, ybuf, sem_in, sem_out):
    half = zbuf.shape[1] // 2

    def in_copies(step, slot):
        cps = [pltpu.make_async_copy(c_hbm.at[:, step, :], cbuf.at[slot], sem_in.at[slot])]
        for b in range(2):
            rows = pl.ds(b * half, half)
            cps.append(pltpu.make_async_copy(zv_hbm.at[b, :, step, :], zbuf.at[slot, rows], sem_in.at[slot]))
            cps.append(pltpu.make_async_copy(x0_hbm.at[b, :, step, :], xbuf.at[slot, rows], sem_in.at[slot]))
        return cps

    def out_copies(step, slot):
        return [pltpu.make_async_copy(ybuf.at[slot, pl.ds(b * half, half)], y_hbm.at[b, :, step, :], sem_out.at[slot])
                for b in range(2)]

    def compute(slot):
        y = _dot(m_ref[...], _unpack_complex(cbuf[slot]))
        ybuf[slot] = (y + zbuf[slot] * sk_ref[...]) * xbuf[slot]

    _double_buffered(in_copies, out_copies, compute)


def hyena_long_conv(zv, x0, taps, asum, skip_b, L):
    D = zv.shape[1]
    N = 2 * L
    N2 = DFT_N2
    N1 = N // N2
    half = N1 // 2
    fwd, inv, fwd_real, f2, g2 = dft_tables(N1, N2)
    td2 = _tile(D, 2048)
    nd2 = D // td2
    zv4 = zv.reshape(2, half, N2, D)
    x04 = x0.reshape(2, half, N2, D)
    any_spec = pl.BlockSpec(memory_space=pl.ANY)

    af = pl.pallas_call(
        _fft_s1_real_kernel,
        grid=(N2,),
        in_specs=[pl.BlockSpec((None, half, 2 * D), lambda j: (j, 0, 0)),
                  pl.BlockSpec((None, 2 * N1, half), lambda j: (j, 0, 0))],
        out_specs=any_spec,
        out_shape=jax.ShapeDtypeStruct((N1, N2, 2 * D), jnp.uint32),
        scratch_shapes=[pltpu.VMEM((2, N1, 2 * D), jnp.uint32), pltpu.SemaphoreType.DMA((2,))],
        compiler_params=_params(("arbitrary",)),
        name="filter_dft_stage1",
    )(taps.reshape(N2, half, 2 * D), fwd_real)
    scale = 1.0 / ((asum[:, :D] + asum[:, D:]) * N)

    a = pl.pallas_call(
        _fft_s1_kernel,
        grid=(N2,),
        in_specs=[any_spec,
                  pl.BlockSpec((None, 2 * N1, N1), lambda j: (j, 0, 0))],
        out_specs=any_spec,
        out_shape=jax.ShapeDtypeStruct((N1, N2, D), jnp.uint32),
        scratch_shapes=[pltpu.VMEM((2, N1, D), F32), pltpu.VMEM((2, N1, D), jnp.uint32),
                        pltpu.SemaphoreType.DMA((2,)), pltpu.SemaphoreType.DMA((2,))],
        compiler_params=_params(("arbitrary",)),
        name="conv_dft_stage1",
    )(zv4, fwd)

    c = pl.pallas_call(
        _fft_mid_kernel,
        grid=(N1, nd2),
        in_specs=[pl.BlockSpec((None, N2, td2), lambda k, j: (k, 0, j)),
                  pl.BlockSpec((None, N2, td2), lambda k, j: (k, 0, j)),
                  pl.BlockSpec((None, N2, td2), lambda k, j: (k, 0, j + nd2)),
                  pl.BlockSpec((1, td2), lambda k, j: (0, j)),
                  pl.BlockSpec((2 * N2, 2 * N2), lambda k, j: (0, 0)),
                  pl.BlockSpec((2 * N2, 2 * N2), lambda k, j: (0, 0))],
        out_specs=pl.BlockSpec((None, N2, td2), lambda k, j: (k, 0, j)),
        out_shape=jax.ShapeDtypeStruct((N1, N2, D), jnp.uint32),
        compiler_params=_params(("parallel", "parallel")),
        name="conv_dft_stage2",
    )(a, af, af, scale, f2, g2)

    y = pl.pallas_call(
        _fft_out_kernel,
        grid=(N2,),
        in_specs=[any_spec,
                  pl.BlockSpec((None, N1, 2 * N1), lambda j: (j, 0, 0)),
                  any_spec, any_spec,
                  pl.BlockSpec((1, D), lambda j: (0, 0))],
        out_specs=any_spec,
        out_shape=jax.ShapeDtypeStruct((2, half, N2, D), F32),
        scratch_shapes=[pltpu.VMEM((2, N1, D), jnp.uint32), pltpu.VMEM((2, N1, D), F32),
                        pltpu.VMEM((2, N1, D), F32), pltpu.VMEM((2, N1, D), F32),
                        pltpu.SemaphoreType.DMA((2,)), pltpu.SemaphoreType.DMA((2,))],
        compiler_params=_params(("arbitrary",)),
        name="conv_dft_inverse",
    )(c, inv, zv4, x04, skip_b)
    return y.reshape(2 * L, D)


VT_ROWS = HEAD_DIM + BF16_SUBLANES


def _qkv_kernel(x_ref, g_ref, w_ref, gain_ref, cos_ref, sin_ref, o_ref, va_ref, vt_ref, hn_ref, *, n_rope_tiles):
    j = pl.program_id(1)

    @pl.when(j == 0)
    def _():
        hn_ref[...] = _norm_rows(x_ref[...], g_ref[...]).astype(BF16)

    u = _dot(hn_ref[...], w_ref[...])
    tn = u.shape[1]

    @pl.when(j < n_rope_tiles)
    def _():
        gain = gain_ref[...]
        cos = cos_ref[...]
        sin = sin_ref[...]
        even = (lax.broadcasted_iota(jnp.int32, cos.shape, 1) % 2) == 0
        for hh in range(tn // HEAD_DIM):
            xh = u[:, hh * HEAD_DIM:(hh + 1) * HEAD_DIM]
            y = _norm_rows(xh, gain)
            sw = jnp.where(even, pltpu.roll(y, HEAD_DIM - 1, 1), pltpu.roll(y, 1, 1))
            o_ref[:, hh * HEAD_DIM:(hh + 1) * HEAD_DIM] = (y * cos + sw * sin).astype(o_ref.dtype)

    @pl.when(j >= n_rope_tiles)
    def _():
        o_ref[...] = u.astype(o_ref.dtype)
        ones = jnp.ones((u.shape[0], HEAD_DIM), va_ref.dtype)
        ones_t = jnp.ones((VT_ROWS - HEAD_DIM, u.shape[0]), vt_ref.dtype)
        for hh in range(tn // HEAD_DIM):
            vh = u[:, hh * HEAD_DIM:(hh + 1) * HEAD_DIM]
            va_ref[:, 2 * hh * HEAD_DIM:(2 * hh + 1) * HEAD_DIM] = vh.astype(va_ref.dtype)
            va_ref[:, (2 * hh + 1) * HEAD_DIM:(2 * hh + 2) * HEAD_DIM] = ones
            vt_ref[hh, :HEAD_DIM, :] = vh.T.astype(vt_ref.dtype)
            vt_ref[hh, HEAD_DIM:, :] = ones_t


def rope_tables(L):
    rows = L // GRID_W
    r, c = jnp.meshgrid(jnp.arange(rows, dtype=F32), jnp.arange(GRID_W, dtype=F32), indexing="ij")
    r = r.reshape(L)
    c = c.reshape(L)
    axis_dim = HEAD_DIM // 2
    inv = 1.0 / (ROPE_THETA ** (jnp.arange(0, axis_dim, 2, dtype=F32) / axis_dim))
    ang = jnp.concatenate([r[:, None] * inv[None], c[:, None] * inv[None]], axis=-1)
    cos = jnp.repeat(jnp.cos(ang), 2, axis=1)
    sign = jnp.tile(jnp.array([-1.0, 1.0], F32), HEAD_DIM // 2)[None, :]
    sin = jnp.repeat(jnp.sin(ang), 2, axis=1) * sign
    return cos, sin


def qkv_project(x, g, w_qkv, q_gain, k_gain, seq_len, *, tm=512):
    T, D = x.shape
    n_out = w_qkv.shape[1]
    nk = N_KV_HEADS * HEAD_DIM
    nq = n_out - 2 * nk
    tn = nk
    tm = _tile(seq_len, tm)
    n_rope = (nq + nk) // tn
    gains = jnp.concatenate([jnp.tile((q_gain * SCORE_SCALE)[None, None, :], (nq // tn, 1, 1)),
                             k_gain[None, None, :],
                             jnp.ones((1, 1, HEAD_DIM), F32)], axis=0)
    cos, sin = rope_tables(seq_len)
    spt = seq_len // tm
    kern = functools.partial(_qkv_kernel, n_rope_tiles=n_rope)
    return pl.pallas_call(
        kern,
        grid=(T // tm, n_out // tn),
        in_specs=[pl.BlockSpec((tm, D), lambda i, j: (i, 0)),
                  pl.BlockSpec((1, D), lambda i, j: (0, 0)),
                  pl.BlockSpec((D, tn), lambda i, j: (0, j)),
                  pl.BlockSpec((None, 1, HEAD_DIM), lambda i, j: (j, 0, 0)),
                  pl.BlockSpec((tm, HEAD_DIM), lambda i, j: (i % spt, 0)),
                  pl.BlockSpec((tm, HEAD_DIM), lambda i, j: (i % spt, 0))],
        out_specs=[pl.BlockSpec((tm, tn), lambda i, j: (i, j)),
                   pl.BlockSpec((tm, 2 * nk), lambda i, j: (i, 0)),
                   pl.BlockSpec((None, N_KV_HEADS, VT_ROWS, tm), lambda i, j: (i // spt, 0, 0, i % spt))],
        out_shape=[jax.ShapeDtypeStruct((T, n_out), BF16), jax.ShapeDtypeStruct((T, 2 * nk), BF16),
                   jax.ShapeDtypeStruct((T // seq_len, N_KV_HEADS, VT_ROWS, seq_len), BF16)],
        scratch_shapes=[pltpu.VMEM((tm, D), BF16)],
        compiler_params=_params(("parallel", "arbitrary")),
        name="qkv_project",
    )(x, g, w_qkv, gains, cos, sin)


def _flash_kernel(q_ref, k_ref, v_ref, o_ref, qs_ref, s0_ref, s1_ref, p0_ref, p1_ref, a0_ref, a1_ref,
                  m_ref, acc_ref, *, group, tq, tk, nk):
    s_refs = (s0_ref, s1_ref)
    p_refs = (p0_ref, p1_ref)
    a_refs = (a0_ref, a1_ref)
    for gq in range(group):
        qs_ref[gq * tq:(gq + 1) * tq, :] = q_ref[:, gq * HEAD_DIM:(gq + 1) * HEAD_DIM]
    m_ref[...] = jnp.full_like(m_ref, -jnp.inf)
    acc_ref[...] = jnp.zeros_like(acc_ref)

    def rows_of(j):
        if isinstance(j, int):
            return pl.ds(j * tk, tk)
        return pl.ds(pl.multiple_of(j * tk, tk), tk)

    def scores(j, slot):
        kb = k_ref[rows_of(j), :]
        s_refs[slot][...] = lax.dot_general(qs_ref[...], kb, (((1,), (1,)), ((), ())),
                                            preferred_element_type=F32)

    def softmax(slot):
        s = s_refs[slot][...]
        m_prev = m_ref[...]
        m_new = jnp.maximum(m_prev, jnp.max(s, axis=-1, keepdims=True))
        a_refs[slot][...] = jnp.exp2(m_prev - m_new)
        p_refs[slot][...] = jnp.exp2(s - jnp.tile(m_new, (1, tk // LANES))).astype(BF16)
        m_ref[...] = m_new

    def accumulate(j, slot):
        pv = _dot(p_refs[slot][...], v_ref[rows_of(j), :])
        acc_ref[...] = acc_ref[...] * jnp.tile(a_refs[slot][...], (1, 2)) + pv

    def step(j, slot, first, last):
        if not last:
            scores(j + 1, 1 - slot)
        softmax(slot)
        if not first:
            accumulate(j - 1, 1 - slot)

    scores(0, 0)
    if nk >= 4 and nk % 2 == 0:
        step(0, 0, True, False)

        def pair(t, c):
            j = 2 * t + 1
            step(j, 1, False, False)
            step(j + 1, 0, False, False)
            return c

        lax.fori_loop(0, (nk - 2) // 2, pair, 0)
        step(nk - 1, 1, False, True)
    else:
        for j in range(nk):
            step(j, j % 2, j == 0, j == nk - 1)
    accumulate(nk - 1, (nk - 1) % 2)

    acc = acc_ref[...]
    o = acc[:, :HEAD_DIM] / acc[:, HEAD_DIM:]
    for gq in range(group):
        o_ref[:, gq * HEAD_DIM:(gq + 1) * HEAD_DIM] = o[gq * tq:(gq + 1) * tq].astype(o_ref.dtype)


def _flash_bounded_kernel(q_ref, k_ref, vt_ref, o_ref, qt_ref, acc_ref, *, group, tq, tk, nk, unroll):
    for gq in range(group):
        qt_ref[:, gq * tq:(gq + 1) * tq] = q_ref[:, gq * HEAD_DIM:(gq + 1) * HEAD_DIM].astype(F32).T.astype(BF16)
    acc_ref[...] = jnp.zeros_like(acc_ref)

    def block(j, c):
        rows = pl.ds(pl.multiple_of(j * tk, tk), tk)
        s_t = _dot(k_ref[rows, :], qt_ref[...])
        acc_ref[...] += _dot(vt_ref[:, rows], jnp.exp2(s_t).astype(BF16))
        return c

    lax.fori_loop(0, nk, block, 0, unroll=unroll)
    acc = acc_ref[...]
    o_t = acc[:HEAD_DIM] / acc[HEAD_DIM:HEAD_DIM + 1]
    for gq in range(group):
        o_ref[:, gq * HEAD_DIM:(gq + 1) * HEAD_DIM] = o_t[:, gq * tq:(gq + 1) * tq].T.astype(o_ref.dtype)


SAFE_SCORE_BOUND = 80.0


def flash_attention(qkv, v_aug, v_t, score_bound, B, L, n_heads, *, tq=256, tk=512):
    group = n_heads // N_KV_HEADS
    tq = _tile(L, tq)
    tk = _tile(L, tk)
    nk = L // tk
    qw = group * HEAD_DIM
    k_col = n_heads
    rows = group * tq
    q_spec = pl.BlockSpec((None, tq, qw), lambda b, h, qi: (b, qi, h))
    k_spec = pl.BlockSpec((None, L, HEAD_DIM), lambda b, h, qi: (b, 0, k_col + h))
    common = dict(
        grid=(B, N_KV_HEADS, L // tq),
        out_specs=pl.BlockSpec((None, tq, qw), lambda b, h, qi: (b, qi, h)),
        out_shape=jax.ShapeDtypeStruct((B, L, n_heads * HEAD_DIM), BF16),
        compiler_params=_params(("parallel", "parallel", "parallel")),
    )
    general = pl.pallas_call(
        functools.partial(_flash_kernel, group=group, tq=tq, tk=tk, nk=nk),
        in_specs=[q_spec, k_spec, pl.BlockSpec((None, L, 2 * HEAD_DIM), lambda b, h, qi: (b, 0, h))],
        scratch_shapes=[pltpu.VMEM((rows, HEAD_DIM), BF16),
                        pltpu.VMEM((rows, tk), F32), pltpu.VMEM((rows, tk), F32),
                        pltpu.VMEM((rows, tk), BF16), pltpu.VMEM((rows, tk), BF16),
                        pltpu.VMEM((rows, LANES), F32), pltpu.VMEM((rows, LANES), F32),
                        pltpu.VMEM((rows, LANES), F32),
                        pltpu.VMEM((rows, 2 * HEAD_DIM), F32)],
        name="flash_attention", **common)
    bounded = pl.pallas_call(
        functools.partial(_flash_bounded_kernel, group=group, tq=tq, tk=tk, nk=nk,
                          unroll=math.gcd(nk, 8)),
        in_specs=[q_spec, k_spec, pl.BlockSpec((None, None, VT_ROWS, L), lambda b, h, qi: (b, h, 0, 0))],
        scratch_shapes=[pltpu.VMEM((HEAD_DIM, rows), BF16),
                        pltpu.VMEM((VT_ROWS, rows), F32)],
        name="flash_attention_bounded", **common)
    return lax.cond(score_bound <= SAFE_SCORE_BOUND,
                    lambda ops: bounded(ops[0], ops[0], ops[2]),
                    lambda ops: general(ops[0], ops[0], ops[1]),
                    (qkv, v_aug, v_t))


def _pack_bf16_pairs(x):
    half = x.shape[1] // 2
    bits = lax.bitcast_convert_type(x.astype(BF16).astype(F32), jnp.uint32)
    return (bits[:, :half] >> 16) | (bits[:, half:] & jnp.uint32(0xFFFF0000))


def _unpack_bf16_pairs(p):
    lo = lax.bitcast_convert_type(p << 16, F32).astype(BF16)
    hi = lax.bitcast_convert_type(p & jnp.uint32(0xFFFF0000), F32).astype(BF16)
    return lo, hi


def _router_kernel(xa_ref, xb_ref, g_ref, rw_ref, h_ref, info_ref, cnt_ref, *, n_a_tiles):
    i = pl.program_id(0)

    def route(x_ref):
        hf = _norm_rows(x_ref[...], g_ref[...])
        h_ref[...] = _pack_bf16_pairs(hf)
        h_hi = hf.astype(BF16)
        h_lo = (hf - h_hi.astype(F32)).astype(BF16)
        rw = rw_ref[...]
        hw = _dot(h_hi, rw)
        logits = hw[:, :LANES] + hw[:, LANES:] + _dot(h_lo, rw[:, :LANES])
        shape = logits.shape
        lane = lax.broadcasted_iota(jnp.int32, shape, 1)
        lanef = lane.astype(F32)
        neg = -jnp.inf
        big = float(LANES)
        is_g = lane < N_GROUPS
        gl = jnp.where(is_g, logits, neg)
        gmax = jnp.max(gl, axis=-1, keepdims=True)
        gidx = jnp.min(jnp.where(gl == gmax, lanef, big), axis=-1, keepdims=True)
        p_g = 1.0 / jnp.sum(jnp.where(is_g, jnp.exp(logits - gmax), 0.0), axis=-1, keepdims=True)
        lo = N_GROUPS + gidx * EXPERTS_PER_GROUP
        el = jnp.where(lanef >= lo, jnp.where(lanef < lo + EXPERTS_PER_GROUP, logits, neg), neg)
        v1 = jnp.max(el, axis=-1, keepdims=True)
        i1 = jnp.min(jnp.where(el == v1, lanef, big), axis=-1, keepdims=True)
        el2 = jnp.where(lanef == i1, neg, el)
        v2 = jnp.max(el2, axis=-1, keepdims=True)
        i2 = jnp.min(jnp.where(el2 == v2, lanef, big), axis=-1, keepdims=True)
        t = jnp.exp(v2 - v1)
        w1 = p_g / (1.0 + t)
        w2 = p_g * t / (1.0 + t)
        e1 = i1 - N_GROUPS
        e2 = i2 - N_GROUPS
        oh1 = jnp.where(lanef == e1, 1.0, 0.0)
        oh2 = jnp.where(lanef == e2, 1.0, 0.0)
        oh = oh1 + oh2
        tm = shape[0]
        tri = jnp.where(lax.broadcasted_iota(jnp.int32, (tm, tm), 1) < lax.broadcasted_iota(jnp.int32, (tm, tm), 0),
                        1.0, 0.0).astype(BF16)
        before = _dot(tri, oh.astype(BF16)) + cnt_ref[...]
        r1 = jnp.sum(before * oh1, axis=-1, keepdims=True)
        r2 = jnp.sum(before * oh2, axis=-1, keepdims=True)
        cnt_ref[...] += jnp.sum(oh, axis=0, keepdims=True)
        vals = (e1, e2, w1, w2, r1, r2)
        info = jnp.zeros(shape, F32)
        for idx, val in enumerate(vals):
            info = jnp.where(lane == idx, val, info)
        info_ref[...] = info

    @pl.when(i == 0)
    def _():
        cnt_ref[...] = jnp.zeros_like(cnt_ref)

    @pl.when(i < n_a_tiles)
    def _():
        route(xa_ref)

    @pl.when(i >= n_a_tiles)
    def _():
        route(xb_ref)


def moe_route(xa, xb, g, router_group, router_expert, *, tm=512):
    Ta, D = xa.shape
    Tb = xb.shape[0]
    tm = _tile(math.gcd(Ta, Tb), tm)
    na, nb = Ta // tm, Tb // tm
    n_e = N_GROUPS * EXPERTS_PER_GROUP
    rw = jnp.concatenate([router_group, jnp.transpose(router_expert, (1, 0, 2)).reshape(D, n_e)], axis=1)
    rw = jnp.pad(rw, ((0, 0), (0, LANES - rw.shape[1])))
    rw_hi = rw.astype(BF16)
    rw = jnp.concatenate([rw_hi, (rw - rw_hi.astype(F32)).astype(BF16)], axis=1)
    kern = functools.partial(_router_kernel, n_a_tiles=na)
    return pl.pallas_call(
        kern,
        grid=(na + nb,),
        in_specs=[pl.BlockSpec((tm, D), lambda i: (jnp.minimum(i, na - 1), 0)),
                  pl.BlockSpec((tm, D), lambda i: (jnp.maximum(i - na, 0), 0)),
                  pl.BlockSpec((1, D), lambda i: (0, 0)),
                  pl.BlockSpec((D, 2 * LANES), lambda i: (0, 0))],
        out_specs=[pl.BlockSpec((tm, D // 2), lambda i: (i, 0)),
                   pl.BlockSpec((tm, LANES), lambda i: (i, 0)),
                   pl.BlockSpec((1, LANES), lambda i: (0, 0))],
        out_shape=[jax.ShapeDtypeStruct((Ta + Tb, D // 2), jnp.uint32),
                   jax.ShapeDtypeStruct((Ta + Tb, LANES), F32),
                   jax.ShapeDtypeStruct((1, LANES), F32)],
        compiler_params=_params(("arbitrary",)),
        name="moe_route",
    )(xa, xb, g, rw)


def moe_plan(info, cnt, tm, n_experts):
    T = info.shape[0]
    n_tiles = (2 * T) // tm + n_experts
    counts = cnt[0, :n_experts].astype(jnp.int32)
    padded = ((counts + tm - 1) // tm) * tm
    ends = jnp.cumsum(padded)
    off = ends - padded
    e = info[:, 0:2].astype(jnp.int32)
    rank = info[:, 4:6].astype(jnp.int32)
    pos = (off[e] + rank).reshape(-1)
    tile_start = jnp.arange(n_tiles, dtype=jnp.int32) * tm
    tile_e = jnp.sum((tile_start[:, None] >= ends[None, :]).astype(jnp.int32), axis=1)
    tile_e = jnp.minimum(tile_e, n_experts - 1)
    tile_valid = jnp.clip(counts[tile_e] - (tile_start - off[tile_e]), 0, tm).astype(jnp.int32)
    n_used = (ends[-1:] // tm).astype(jnp.int32)
    last_row = (n_tiles - 1) * tm
    expert_last = jnp.where(padded > 0, ends - tm, last_row)
    tail = jnp.minimum(n_used[0] + jnp.arange(n_experts, dtype=jnp.int32), n_tiles - 1) * tm
    zero_tiles = jnp.concatenate([expert_last, tail]).astype(jnp.int32)[None, :]
    return pos, tile_e, tile_valid, n_used, zero_tiles


DMA_UNROLL = 8


def _dispatch_kernel(pos_ref, ztile_ref, h_ref, hs_hbm, zbuf, sem, zsem):
    tm = h_ref.shape[0]

    @pl.when(pl.program_id(0) == 0)
    def _():
        zbuf[...] = jnp.zeros_like(zbuf)
        zrows = zbuf.shape[0]

        def zero_copy(k):
            return pltpu.make_async_copy(zbuf, hs_hbm.at[pl.ds(pl.multiple_of(ztile_ref[0, k], zrows), zrows)], zsem)

        for k in range(ztile_ref.shape[1]):
            zero_copy(k).start()
        for k in range(ztile_ref.shape[1]):
            zero_copy(k).wait()

    def row_copy(r, slot):
        return pltpu.make_async_copy(h_ref.at[pl.ds(r, 1)], hs_hbm.at[pl.ds(pos_ref[0, 0, 2 * r + slot], 1)], sem)

    def start(r, c):
        row_copy(r, 0).start()
        row_copy(r, 1).start()
        return c

    def wait(r, c):
        row_copy(r, 0).wait()
        row_copy(r, 1).wait()
        return c

    lax.fori_loop(0, tm, start, 0, unroll=DMA_UNROLL)
    lax.fori_loop(0, tm, wait, 0, unroll=DMA_UNROLL)


def moe_dispatch(h, pos, zero_tiles, n_sorted_rows, tile_rows, *, tm=512):
    T, C = h.shape
    tm = _tile(T, tm)
    return pl.pallas_call(
        _dispatch_kernel,
        grid=(T // tm,),
        in_specs=[pl.BlockSpec((1, 1, 2 * tm), lambda i: (i, 0, 0), memory_space=pltpu.SMEM),
                  pl.BlockSpec(zero_tiles.shape, lambda i: (0, 0), memory_space=pltpu.SMEM),
                  pl.BlockSpec((tm, C), lambda i: (i, 0))],
        out_specs=pl.BlockSpec(memory_space=pl.ANY),
        out_shape=jax.ShapeDtypeStruct((n_sorted_rows, C), h.dtype),
        scratch_shapes=[pltpu.VMEM((tile_rows, C), h.dtype), pltpu.SemaphoreType.DMA, pltpu.SemaphoreType.DMA],
        compiler_params=_params(("arbitrary",)),
        name="moe_dispatch",
    )(pos.reshape(T // tm, 1, 2 * tm), zero_tiles, h)


def _expert_kernel(te_ref, tv_ref, nu_ref, x_ref, wg_ref, wu_ref, wd_ref, y_ref):
    del te_ref, nu_ref
    valid = tv_ref[pl.program_id(0)]

    @pl.when(valid == 0)
    def _():
        y_ref[...] = jnp.zeros_like(y_ref)

    @pl.when(valid > 0)
    def _():
        x_lo, x_hi = _unpack_bf16_pairs(x_ref[...])
        half = x_ref.shape[1]
        gate = (_dot(x_lo, wg_ref[:half, :].astype(BF16)) + _dot(x_hi, wg_ref[half:, :].astype(BF16)))
        up = (_dot(x_lo, wu_ref[:half, :].astype(BF16)) + _dot(x_hi, wu_ref[half:, :].astype(BF16)))
        a = (gate * jax.nn.sigmoid(gate) * up).astype(BF16)
        y_ref[...] = _dot(a, wd_ref[...].astype(BF16))


def moe_experts(hs, tile_e, tile_valid, n_used, w_gate, w_up, w_down, layer, tm):
    P = hs.shape[0]
    D = w_gate.shape[2]
    d_e = w_gate.shape[-1]

    def row_map(i, te, tv, nu):
        return (jnp.minimum(i, nu[0] - 1), 0)

    def out_map(i, te, tv, nu):
        return (i, 0)

    def w_map(i, te, tv, nu):
        return (layer, te[i], 0, 0)

    grid_spec = pltpu.PrefetchScalarGridSpec(
        num_scalar_prefetch=3,
        grid=(P // tm,),
        in_specs=[pl.BlockSpec((tm, D // 2), row_map),
                  pl.BlockSpec((None, None, D, d_e), w_map),
                  pl.BlockSpec((None, None, D, d_e), w_map),
                  pl.BlockSpec((None, None, d_e, D), w_map)],
        out_specs=pl.BlockSpec((tm, D), out_map),
    )
    return pl.pallas_call(
        _expert_kernel,
        grid_spec=grid_spec,
        out_shape=jax.ShapeDtypeStruct((P, D), F32),
        compiler_params=_params(("arbitrary",)),
        name="moe_experts",
    )(tile_e, tile_valid, n_used, hs, w_gate, w_up, w_down)


def _combine_kernel(pos_ref, posn_ref, x_ref, info_ref, g_ref, y_hbm, o_ref, ybuf, sem, *, final_norm):
    tm = x_ref.shape[0]
    i = pl.program_id(0)
    buf = i % 2

    def row_copy(p_ref, b, r, slot):
        return pltpu.make_async_copy(y_hbm.at[pl.ds(p_ref[0, 0, 2 * r + slot], 1)],
                                     ybuf.at[b, slot, pl.ds(r, 1)], sem.at[b])

    def start_tile(p_ref, b):
        def body(r, c):
            row_copy(p_ref, b, r, 0).start()
            row_copy(p_ref, b, r, 1).start()
            return c
        lax.fori_loop(0, tm, body, 0, unroll=DMA_UNROLL)

    @pl.when(i == 0)
    def _():
        start_tile(pos_ref, 0)

    @pl.when(i + 1 < pl.num_programs(0))
    def _():
        start_tile(posn_ref, 1 - buf)

    def wait(r, c):
        row_copy(pos_ref, buf, r, 0).wait()
        row_copy(pos_ref, buf, r, 1).wait()
        return c

    lax.fori_loop(0, tm, wait, 0, unroll=DMA_UNROLL)
    info = info_ref[...]
    y = info[:, 2:3] * ybuf[buf, 0] + info[:, 3:4] * ybuf[buf, 1]
    x = x_ref[...] + y
    if final_norm:
        x = _norm_rows(x, g_ref[...])
    o_ref[...] = x


def moe_combine(x, y, pos, info, row_off, g_final, final_norm, *, tm=256):
    Tg, D = x.shape
    T = info.shape[0]
    tm = _tile(math.gcd(Tg, row_off) if row_off else Tg, tm)
    ob = row_off // tm
    kern = functools.partial(_combine_kernel, final_norm=final_norm)
    n = Tg // tm
    pos3 = pos.reshape(T // tm, 1, 2 * tm)
    return pl.pallas_call(
        kern,
        grid=(n,),
        in_specs=[pl.BlockSpec((1, 1, 2 * tm), lambda i: (i + ob, 0, 0), memory_space=pltpu.SMEM),
                  pl.BlockSpec((1, 1, 2 * tm), lambda i: (jnp.minimum(i + 1, n - 1) + ob, 0, 0),
                               memory_space=pltpu.SMEM),
                  pl.BlockSpec((tm, D), lambda i: (i, 0)),
                  pl.BlockSpec((tm, LANES), lambda i: (i + ob, 0)),
                  pl.BlockSpec((1, D), lambda i: (0, 0)),
                  pl.BlockSpec(memory_space=pl.ANY)],
        out_specs=pl.BlockSpec((tm, D), lambda i: (i, 0)),
        out_shape=jax.ShapeDtypeStruct((Tg, D), F32),
        scratch_shapes=[pltpu.VMEM((2, 2, tm, D), F32), pltpu.SemaphoreType.DMA((2,))],
        compiler_params=_params(("arbitrary",)),
        name="moe_combine",
    )(pos3, pos3, x, info, g_final, y)


MOE_TILE = 512


def hier_moe_block(xs, g, router_group, router_expert, w_gate, w_up, w_down, layer, g_final, final_norm):
    xa, xb = xs
    n_experts = w_gate.shape[1]
    h, info, cnt = moe_route(xa, xb, g, router_group, router_expert)
    T = h.shape[0]
    tm = _tile(2 * T, MOE_TILE)
    pos, tile_e, tile_valid, n_used, zero_tiles = moe_plan(info, cnt, tm, n_experts)
    hs = moe_dispatch(h, pos, zero_tiles, tile_e.shape[0] * tm, tm)
    y = moe_experts(hs, tile_e, tile_valid, n_used, w_gate, w_up, w_down, layer, tm)
    oa = moe_combine(xa, y, pos, info, 0, g_final, final_norm)
    ob = moe_combine(xb, y, pos, info, xa.shape[0], g_final, final_norm)
    return oa, ob


def hyena_block(x, L, g, w_in, conv_w, conv_b, f_w1, f_b1, f_win, f_bin, f_freq, f_w3, skip_b, w_out):
    D = x.shape[1]
    x0, zv = hyena_inproj(x, g, w_in, conv_w, conv_b[None, :], L)
    taps, asum = hyena_filter_taps(L, w_out.shape[0], f_w1, f_b1, f_win, f_bin, f_freq, f_w3)
    y = hyena_long_conv(zv, x0, taps, asum, skip_b[None, :], L)
    return matmul_residual(y, w_out, x)


def attention_block(x, B, L, g, w_qkv, q_gain, k_gain, w_o):
    D = x.shape[1]
    n_heads = w_o.shape[0] // HEAD_DIM
    qkv, v_aug, v_t = qkv_project(x, g, w_qkv, q_gain, k_gain, L)
    score_bound = 1.02 * HEAD_DIM * SCORE_SCALE * jnp.max(jnp.abs(q_gain)) * jnp.max(jnp.abs(k_gain))
    o = flash_attention(qkv.reshape(B, L, qkv.shape[1]), v_aug.reshape(B, L, v_aug.shape[1]), v_t, score_bound,
                        B, L, n_heads)
    return matmul_residual(o.reshape(B * L, n_heads * HEAD_DIM), w_o, x)


def kernel(x_prompt, x_sample, norm_mix, norm_ffn, norm_final, hy_w_in, hy_conv_w, hy_conv_b, hy_f_w1, hy_f_b1, hy_f_win, hy_f_bin, hy_f_freq, hy_f_w3, hy_skip_b, hy_w_out, at_w_qkv, at_q_gain, at_k_gain, at_w_o, moe_router_group, moe_router_expert, moe_w_gate, moe_w_up, moe_w_down):
    D = x_prompt.shape[-1]
    depth = norm_mix.shape[0]
    shapes = [x_prompt.shape, x_sample.shape]
    for s in shapes:
        assert s[0] == 2, "the long convolution packs exactly two sequences per group"
    xs = [x_prompt.reshape(-1, D), x_sample.reshape(-1, D)]
    hy_w_in_b = hy_w_in.astype(BF16)
    hy_w_out_b = hy_w_out.astype(BF16)
    at_w_qkv_b = at_w_qkv.astype(BF16)
    at_w_o_b = at_w_o.astype(BF16)
    g_final = norm_final[None, :]
    for i in range(depth):
        j = i // N_MIXERS
        g = norm_mix[i][None, :]
        if i % N_MIXERS == 0:
            xs = [hyena_block(x, s[1], g, hy_w_in_b[j], hy_conv_w[j], hy_conv_b[j], hy_f_w1[j], hy_f_b1[j],
                              hy_f_win[j], hy_f_bin[j], hy_f_freq[j], hy_f_w3[j], hy_skip_b[j], hy_w_out_b[j])
                  for x, s in zip(xs, shapes)]
        else:
            xs = [attention_block(x, s[0], s[1], g, at_w_qkv_b[j], at_q_gain[j], at_k_gain[j], at_w_o_b[j])
                  for x, s in zip(xs, shapes)]
        xs = hier_moe_block(xs, norm_ffn[i][None, :], moe_router_group[i], moe_router_expert[i],
                            moe_w_gate, moe_w_up, moe_w_down, i, g_final, i == depth - 1)
    return (xs[0].reshape(shapes[0]), xs[1].reshape(shapes[1]))
```

```python
import functools
import math

import jax
import jax.numpy as jnp
import numpy as np
from jax import lax
from jax.experimental import pallas as pl
from jax.experimental.pallas import tpu as pltpu

RMS_EPS = 1e-6
GRID_W = 64
EMB_DIM = 33
DECAY_TARGET = 1e-2
FAST_DECAY_PCT = 0.3
SLOW_DECAY_PCT = 1.5
MIN_DECAY = math.log(DECAY_TARGET) / FAST_DECAY_PCT
MAX_DECAY = math.log(DECAY_TARGET) / SLOW_DECAY_PCT
HEAD_DIM = 128
SCORE_SCALE = HEAD_DIM ** -0.5 * math.log2(math.e)
N_KV_HEADS = 4
ROPE_THETA = 10000.0
N_GROUPS = 4
EXPERTS_PER_GROUP = 8
N_MIXERS = 2

LANES = 128
BF16_SUBLANES = 16
VMEM_LIMIT_BYTES = 52 * 1024 * 1024
DFT_N2 = 128
MXU_ROWS = 256

BF16 = jnp.bfloat16
F32 = jnp.float32


def _params(sem):
    return pltpu.CompilerParams(dimension_semantics=sem, vmem_limit_bytes=VMEM_LIMIT_BYTES)


def _dot(a, b):
    return jnp.dot(a, b, preferred_element_type=F32)


def _dot_3pass(a, b):
    a_hi = a.astype(BF16)
    a_lo = (a - a_hi.astype(F32)).astype(BF16)
    b_hi = b.astype(BF16)
    b_lo = (b - b_hi.astype(F32)).astype(BF16)
    return _dot(a_hi, b_hi) + _dot(a_hi, b_lo) + _dot(a_lo, b_hi)


def _norm_rows(x, g):
    ms = jnp.mean(x * x, axis=-1, keepdims=True)
    return x * lax.rsqrt(ms + RMS_EPS) * g


def _tile(n, pref):
    return pref if n % pref == 0 else n


def _matmul_res_kernel(a_ref, w_ref, r_ref, o_ref):
    o_ref[...] = r_ref[...] + _dot(a_ref[...].astype(BF16), w_ref[...])


def matmul_residual(a, w, res, *, tm=1024, tn=512):
    T, K = a.shape
    N = w.shape[1]
    tm = _tile(T, tm)
    tn = _tile(N, tn)
    return pl.pallas_call(
        _matmul_res_kernel,
        grid=(T // tm, N // tn),
        in_specs=[pl.BlockSpec((tm, K), lambda i, j: (i, 0)),
                  pl.BlockSpec((K, tn), lambda i, j: (0, j)),
                  pl.BlockSpec((tm, tn), lambda i, j: (i, j))],
        out_specs=pl.BlockSpec((tm, tn), lambda i, j: (i, j)),
        out_shape=jax.ShapeDtypeStruct((T, N), F32),
        compiler_params=_params(("parallel", "arbitrary")),
        name="matmul_residual",
    )(a, w, res)


HALO = BF16_SUBLANES


def _hy_inproj_kernel(x_ref, xp_ref, xn_ref, g_ref, w0_ref, w1_ref, w2_ref,
                      cw0_ref, cw1_ref, cw2_ref, cb0_ref, cb1_ref, cb2_ref,
                      x0_out, zv_out, hn_ref, *, tm, tiles_per_seq):
    i = pl.program_id(0)
    j = pl.program_id(1)

    @pl.when(j == 0)
    def _():
        g = g_ref[...]
        hn_ref[HALO:HALO + tm, :] = _norm_rows(x_ref[...], g).astype(BF16)
        first = (i % tiles_per_seq) == 0
        last = (i % tiles_per_seq) == tiles_per_seq - 1
        hp = _norm_rows(xp_ref[...], g)
        hn_ref[0:HALO, :] = jnp.where(first, 0.0, hp).astype(BF16)
        hx = _norm_rows(xn_ref[...], g)
        hn_ref[HALO + tm:, :] = jnp.where(last, 0.0, hx).astype(BF16)

    h = hn_ref[...]
    rows = tm + 2 * HALO

    def conv(w_ref, cw_ref, cb_ref):
        u = _dot(h, w_ref[...])
        um = pltpu.roll(u, 1, 0)[HALO:HALO + tm]
        up = pltpu.roll(u, rows - 1, 0)[HALO:HALO + tm]
        uc = u[HALO:HALO + tm]
        cw = cw_ref[...]
        return um * cw[0:1] + uc * cw[1:2] + up * cw[2:3] + cb_ref[...]

    x0_out[...] = conv(w0_ref, cw0_ref, cb0_ref)
    x1 = conv(w1_ref, cw1_ref, cb1_ref)
    v = conv(w2_ref, cw2_ref, cb2_ref)
    zv_out[...] = v * x1


def hyena_inproj(x, g, w_in, conv_w, conv_b, seq_len, *, tm=512, tn=512):
    T, D = x.shape
    W = w_in.shape[1] // 3
    tm = _tile(seq_len, tm)
    tn = _tile(W, tn)
    nj = W // tn
    hb = tm // HALO
    n_halo_blocks = T // HALO
    kern = functools.partial(_hy_inproj_kernel, tm=tm, tiles_per_seq=seq_len // tm)

    def wspec(c):
        return pl.BlockSpec((D, tn), lambda i, j, c=c: (0, j + c * nj))

    def cwspec(c):
        return pl.BlockSpec((3, tn), lambda i, j, c=c: (0, j + c * nj))

    def cbspec(c):
        return pl.BlockSpec((1, tn), lambda i, j, c=c: (0, j + c * nj))

    return pl.pallas_call(
        kern,
        grid=(T // tm, nj),
        in_specs=[pl.BlockSpec((tm, D), lambda i, j: (i, 0)),
                  pl.BlockSpec((HALO, D), lambda i, j: (jnp.maximum(i * hb - 1, 0), 0)),
                  pl.BlockSpec((HALO, D), lambda i, j: (jnp.minimum((i + 1) * hb, n_halo_blocks - 1), 0)),
                  pl.BlockSpec((1, D), lambda i, j: (0, 0)),
                  wspec(0), wspec(1), wspec(2),
                  cwspec(0), cwspec(1), cwspec(2),
                  cbspec(0), cbspec(1), cbspec(2)],
        out_specs=[pl.BlockSpec((tm, tn), lambda i, j: (i, j)),
                   pl.BlockSpec((tm, tn), lambda i, j: (i, j))],
        out_shape=[jax.ShapeDtypeStruct((T, W), F32), jax.ShapeDtypeStruct((T, W), F32)],
        scratch_shapes=[pltpu.VMEM((tm + 2 * HALO, D), BF16)],
        compiler_params=_params(("parallel", "arbitrary")),
        name="hyena_inproj",
    )(x, x, x, g, w_in, w_in, w_in, conv_w, conv_w, conv_w, conv_b, conv_b, conv_b)


def _filter_mlp_kernel(z_ref, w1_ref, b1_ref, win_ref, bin_ref, fr_ref, h_ref):
    hp = lax.Precision.HIGHEST
    fr = fr_ref[...]
    h = jnp.sin(fr * (jnp.dot(z_ref[...], w1_ref[...], precision=hp, preferred_element_type=F32) + b1_ref[...]))
    for l in range(win_ref.shape[0]):
        h = jnp.sin(fr * (jnp.dot(h, win_ref[l], precision=hp, preferred_element_type=F32) + bin_ref[l]))
    h_ref[...] = h


def _filter_expand_kernel(h_ref, t_ref, w3_ref, dl_ref, k_ref, s_ref, *, tl, n_fwd_tiles):
    j = pl.program_id(0)
    i = pl.program_id(1)
    taps = _dot_3pass(h_ref[...], w3_ref[...])
    taps = taps * jnp.exp(-t_ref[...] * dl_ref[...])
    row = i * tl + lax.broadcasted_iota(jnp.int32, taps.shape, 0)
    taps = jnp.where(jnp.logical_and(j >= n_fwd_tiles, row == 0), 0.0, taps)
    k_ref[...] = taps.astype(k_ref.dtype)

    @pl.when(i == 0)
    def _():
        s_ref[...] = jnp.zeros_like(s_ref)

    s_ref[...] += jnp.sum(jnp.abs(taps), axis=0, keepdims=True)


def hyena_filter_taps(L, D, f_w1, f_b1, f_win, f_bin, f_freq, f_w3):
    H = f_w1.shape[1]
    t = jnp.linspace(0.0, 1.0, L, dtype=F32)[:, None]
    bands = (EMB_DIM - 1) // 2
    w = 2.0 * math.pi * jnp.arange(L, dtype=F32)[:, None] / L
    fb = jnp.linspace(1e-4, bands - 1, bands, dtype=F32)[None, :]
    z = jnp.concatenate([t, jnp.cos(fb * w), -jnp.sin(fb * w)], axis=-1)

    def lag_major_to_dft_order(a):
        return a.reshape(L // DFT_N2, DFT_N2, a.shape[1]).transpose(1, 0, 2).reshape(L, a.shape[1])

    z = lag_major_to_dft_order(z)
    t = lag_major_to_dft_order(t)
    zp = jnp.pad(z, ((0, 0), (0, LANES - EMB_DIM)))
    w1p = jnp.pad(f_w1, ((0, LANES - EMB_DIM), (0, 0)))
    n_in = f_win.shape[0]
    tl = _tile(L, 512)
    h = pl.pallas_call(
        _filter_mlp_kernel,
        grid=(L // tl,),
        in_specs=[pl.BlockSpec((tl, LANES), lambda i: (i, 0)),
                  pl.BlockSpec((LANES, H), lambda i: (0, 0)),
                  pl.BlockSpec((1, H), lambda i: (0, 0)),
                  pl.BlockSpec((n_in, H, H), lambda i: (0, 0, 0)),
                  pl.BlockSpec((n_in, 1, H), lambda i: (0, 0, 0)),
                  pl.BlockSpec((1, H), lambda i: (0, 0))],
        out_specs=pl.BlockSpec((tl, H), lambda i: (i, 0)),
        out_shape=jax.ShapeDtypeStruct((L, H), F32),
        compiler_params=_params(("parallel",)),
        name="filter_mlp",
    )(zp, w1p, f_b1[None, :], f_win, f_bin[:, None, :], f_freq[None, :])

    deltas = jnp.abs(jnp.linspace(MIN_DECAY, MAX_DECAY, D, dtype=F32))[None, :]
    deltas2 = jnp.concatenate([deltas, deltas], axis=1)
    tn = _tile(D, 1024)
    n_fwd = D // tn
    kern = functools.partial(_filter_expand_kernel, tl=tl, n_fwd_tiles=n_fwd)
    taps, asum = pl.pallas_call(
        kern,
        grid=(2 * n_fwd, L // tl),
        in_specs=[pl.BlockSpec((tl, H), lambda j, i: (i, 0)),
                  pl.BlockSpec((tl, 1), lambda j, i: (i, 0)),
                  pl.BlockSpec((H, tn), lambda j, i: (0, j)),
                  pl.BlockSpec((1, tn), lambda j, i: (0, j))],
        out_specs=[pl.BlockSpec((tl, tn), lambda j, i: (i, j)),
                   pl.BlockSpec((1, tn), lambda j, i: (0, j))],
        out_shape=[jax.ShapeDtypeStruct((L, 2 * D), F32), jax.ShapeDtypeStruct((1, 2 * D), F32)],
        compiler_params=_params(("parallel", "arbitrary")),
        name="filter_expand",
    )(h, t, f_w3, deltas2)
    return taps, asum


def dft_tables(N1, N2):
    N = N1 * N2
    half = N1 // 2
    k1 = jnp.arange(N1, dtype=jnp.int32)[None, :, None]
    n = (jnp.arange(half, dtype=jnp.int32)[None, None, :] * N2
         + jnp.arange(N2, dtype=jnp.int32)[:, None, None])
    ang = ((k1 * n) % N).astype(F32) * (2.0 * math.pi / N)
    c, s = jnp.cos(ang), jnp.sin(ang)
    fwd = jnp.concatenate([jnp.concatenate([c, s], axis=2),
                           jnp.concatenate([-s, c], axis=2)], axis=1)
    inv = jnp.swapaxes(fwd, 1, 2)
    fwd_real = jnp.concatenate([c, -s], axis=1)
    k2 = jnp.arange(N2, dtype=jnp.int32)
    ang2 = ((k2[:, None] * k2[None, :]) % N2).astype(F32) * (2.0 * math.pi / N2)
    c2, s2 = jnp.cos(ang2), jnp.sin(ang2)
    f2 = jnp.concatenate([jnp.concatenate([c2, s2], axis=1),
                          jnp.concatenate([-s2, c2], axis=1)], axis=0)
    g2 = f2.T
    return fwd.astype(BF16), inv.astype(BF16), fwd_real.astype(BF16), f2.astype(BF16), g2.astype(BF16)


def _pack_complex(re, im):
    rb = lax.bitcast_convert_type(re.astype(BF16).astype(F32), jnp.uint32)
    ib = lax.bitcast_convert_type(im.astype(BF16).astype(F32), jnp.uint32)
    return (rb >> 16) | (ib & jnp.uint32(0xFFFF0000))


def _unpack_complex(p):
    re = lax.bitcast_convert_type(p << 16, F32)
    im = lax.bitcast_convert_type(p & jnp.uint32(0xFFFF0000), F32)
    return jnp.concatenate([re, im], axis=0).astype(BF16)


def _double_buffered(in_copies, out_copies, compute):
    j = pl.program_id(0)
    n = pl.num_programs(0)
    slot = j % 2

    @pl.when(j == 0)
    def _():
        for cp in in_copies(0, 0):
            cp.start()

    @pl.when(j + 1 < n)
    def _():
        for cp in in_copies(j + 1, 1 - slot):
            cp.start()

    for cp in in_copies(j, slot):
        cp.wait()

    @pl.when(j >= 2)
    def _():
        for cp in out_copies(j - 2, slot):
            cp.wait()

    compute(slot)
    for cp in out_copies(j, slot):
        cp.start()

    @pl.when(j == n - 1)
    def _():
        for cp in out_copies(j, slot):
            cp.wait()

        @pl.when(n >= 2)
        def _():
            for cp in out_copies(j - 1, 1 - slot):
                cp.wait()


def _fft_s1_kernel(z_hbm, m_ref, a_hbm, zbuf, obuf, sem_in, sem_out):
    n_g = zbuf.shape[1]
    half = zbuf.shape[2] // 2
    n1 = obuf.shape[2]

    def in_copies(step, slot):
        return [pltpu.make_async_copy(z_hbm.at[b, :, step * n_g + g, :], zbuf.at[slot, g, pl.ds(b * half, half)],
                                      sem_in.at[slot])
                for g in range(n_g) for b in range(2)]

    def out_copies(step, slot):
        return [pltpu.make_async_copy(obuf.at[slot, g], a_hbm.at[:, step * n_g + g, :], sem_out.at[slot])
                for g in range(n_g)]

    def compute(slot):
        for g in range(n_g):
            a = _dot(m_ref[g], zbuf[slot, g].astype(BF16))
            obuf[slot, g] = _pack_complex(a[:n1], a[n1:])

    _double_buffered(in_copies, out_copies, compute)


def _fft_s1_real_kernel(k_ref, m_ref, a_hbm, obuf, sem_out):
    n_g = obuf.shape[1]
    n1 = obuf.shape[2]

    def out_copies(step, slot):
        return [pltpu.make_async_copy(obuf.at[slot, g], a_hbm.at[:, step * n_g + g, :], sem_out.at[slot])
                for g in range(n_g)]

    def compute(slot):
        for g in range(n_g):
            a = _dot(m_ref[g], k_ref[g].astype(BF16))
            obuf[slot, g] = _pack_complex(a[:n1], a[n1:])

    _double_buffered(lambda step, slot: [], out_copies, compute)


def _fft_mid_kernel(a_ref, af_ref, ab_ref, s_ref, f2_ref, g2_ref, c_ref):
    n2 = a_ref.shape[0]
    f2 = f2_ref[...]
    xf = _dot(f2, _unpack_complex(af_ref[...]))
    xb = _dot(f2, _unpack_complex(ab_ref[...]))
    s = s_ref[...]
    kr = (xf[:n2] + xb[:n2]) * s
    ki = (xf[n2:] - xb[n2:]) * s
    x = _dot(f2, _unpack_complex(a_ref[...]))
    xr, xi = x[:n2], x[n2:]
    y = jnp.concatenate([xr * kr - xi * ki, xr * ki + xi * kr], axis=0).astype(BF16)
    c = _dot(g2_ref[...], y)
    c_ref[...] = _pack_complex(c[:n2], c[n2:])


def _fft_out_kernel(c_hbm, m_ref, zv_hbm, x0_hbm, r_hbm, sk_ref, w_ref, o_hbm,
                    cbuf, zbuf, xbuf, rbuf, obuf, sem_in, sem_out):
    n_g = zbuf.shape[1]
    n1 = zbuf.shape[2]
    half = n1 // 2

    def in_copies(step, slot):
        cps = []
        for g in range(n_g):
            n2 = step * n_g + g
            cps.append(pltpu.make_async_copy(c_hbm.at[:, n2, :], cbuf.at[slot, g], sem_in.at[slot]))
            for b in range(2):
                rows = pl.ds(b * half, half)
                for src, dst in ((zv_hbm, zbuf), (x0_hbm, xbuf), (r_hbm, rbuf)):
                    cps.append(pltpu.make_async_copy(src.at[b, :, n2, :], dst.at[slot, g, rows], sem_in.at[slot]))
        return cps

    def out_copies(step, slot):
        return [pltpu.make_async_copy(obuf.at[slot, g, pl.ds(b * half, half)], o_hbm.at[b, :, step * n_g + g, :],
                                      sem_out.at[slot])
                for g in range(n_g) for b in range(2)]

    def compute(slot):
        ys = []
        for g in range(n_g):
            y = _dot(m_ref[g], _unpack_complex(cbuf[slot, g]))
            ys.append(((y + zbuf[slot, g] * sk_ref[...]) * xbuf[slot, g]).astype(BF16))
        proj = _dot(ys[0] if n_g == 1 else jnp.concatenate(ys, axis=0), w_ref[...])
        for g in range(n_g):
            obuf[slot, g] = rbuf[slot, g] + proj[g * n1:(g + 1) * n1]

    _double_buffered(in_copies, out_copies, compute)


def hyena_long_conv(zv, x0, x, w_out, taps, asum, skip_b, L):
    D = zv.shape[1]
    N = 2 * L
    N2 = DFT_N2
    N1 = N // N2
    half = N1 // 2
    n_g = max(1, min(N2, MXU_ROWS // N1))
    n_steps = N2 // n_g
    fwd, inv, fwd_real, f2, g2 = dft_tables(N1, N2)
    td2 = _tile(D, 2048)
    nd2 = D // td2
    zv4 = zv.reshape(2, half, N2, D)
    x04 = x0.reshape(2, half, N2, D)
    x4 = x.reshape(2, half, N2, D)
    any_spec = pl.BlockSpec(memory_space=pl.ANY)

    af = pl.pallas_call(
        _fft_s1_real_kernel,
        grid=(n_steps,),
        in_specs=[pl.BlockSpec((n_g, half, 2 * D), lambda j: (j, 0, 0)),
                  pl.BlockSpec((n_g, 2 * N1, half), lambda j: (j, 0, 0))],
        out_specs=any_spec,
        out_shape=jax.ShapeDtypeStruct((N1, N2, 2 * D), jnp.uint32),
        scratch_shapes=[pltpu.VMEM((2, n_g, N1, 2 * D), jnp.uint32), pltpu.SemaphoreType.DMA((2,))],
        compiler_params=_params(("arbitrary",)),
        name="filter_dft_stage1",
    )(taps.reshape(N2, half, 2 * D), fwd_real)
    scale = 1.0 / ((asum[:, :D] + asum[:, D:]) * N)

    a = pl.pallas_call(
        _fft_s1_kernel,
        grid=(n_steps,),
        in_specs=[any_spec,
                  pl.BlockSpec((n_g, 2 * N1, N1), lambda j: (j, 0, 0))],
        out_specs=any_spec,
        out_shape=jax.ShapeDtypeStruct((N1, N2, D), jnp.uint32),
        scratch_shapes=[pltpu.VMEM((2, n_g, N1, D), F32), pltpu.VMEM((2, n_g, N1, D), jnp.uint32),
                        pltpu.SemaphoreType.DMA((2,)), pltpu.SemaphoreType.DMA((2,))],
        compiler_params=_params(("arbitrary",)),
        name="conv_dft_stage1",
    )(zv4, fwd)

    c = pl.pallas_call(
        _fft_mid_kernel,
        grid=(N1, nd2),
        in_specs=[pl.BlockSpec((None, N2, td2), lambda k, j: (k, 0, j)),
                  pl.BlockSpec((None, N2, td2), lambda k, j: (k, 0, j)),
                  pl.BlockSpec((None, N2, td2), lambda k, j: (k, 0, j + nd2)),
                  pl.BlockSpec((1, td2), lambda k, j: (0, j)),
                  pl.BlockSpec((2 * N2, 2 * N2), lambda k, j: (0, 0)),
                  pl.BlockSpec((2 * N2, 2 * N2), lambda k, j: (0, 0))],
        out_specs=pl.BlockSpec((None, N2, td2), lambda k, j: (k, 0, j)),
        out_shape=jax.ShapeDtypeStruct((N1, N2, D), jnp.uint32),
        compiler_params=_params(("parallel", "parallel")),
        name="conv_dft_stage2",
    )(a, af, af, scale, f2, g2)

    def buf(dtype):
        return pltpu.VMEM((2, n_g, N1, D), dtype)

    o = pl.pallas_call(
        _fft_out_kernel,
        grid=(n_steps,),
        in_specs=[any_spec,
                  pl.BlockSpec((n_g, N1, 2 * N1), lambda j: (j, 0, 0)),
                  any_spec, any_spec, any_spec,
                  pl.BlockSpec((1, D), lambda j: (0, 0)),
                  pl.BlockSpec(w_out.shape, lambda j: (0, 0))],
        out_specs=any_spec,
        out_shape=jax.ShapeDtypeStruct((2, half, N2, D), F32),
        scratch_shapes=[buf(jnp.uint32), buf(F32), buf(F32), buf(F32), buf(F32),
                        pltpu.SemaphoreType.DMA((2,)), pltpu.SemaphoreType.DMA((2,))],
        compiler_params=_params(("arbitrary",)),
        name="conv_dft_inverse_outproj",
    )(c, inv, zv4, x04, x4, skip_b, w_out)
    return o.reshape(2 * L, D)


VT_ROWS = HEAD_DIM + BF16_SUBLANES


def _qkv_kernel(x_ref, g_ref, w_ref, gain_ref, cos_ref, sin_ref, o_ref, va_ref, vt_ref, hn_ref, *, n_rope_tiles):
    j = pl.program_id(1)

    @pl.when(j == 0)
    def _():
        hn_ref[...] = _norm_rows(x_ref[...], g_ref[...]).astype(BF16)

    u = _dot(hn_ref[...], w_ref[...])
    tn = u.shape[1]

    @pl.when(j < n_rope_tiles)
    def _():
        gain = gain_ref[...]
        cos = cos_ref[...]
        sin = sin_ref[...]
        even = (lax.broadcasted_iota(jnp.int32, cos.shape, 1) % 2) == 0
        for hh in range(tn // HEAD_DIM):
            xh = u[:, hh * HEAD_DIM:(hh + 1) * HEAD_DIM]
            y = _norm_rows(xh, gain)
            sw = jnp.where(even, pltpu.roll(y, HEAD_DIM - 1, 1), pltpu.roll(y, 1, 1))
            o_ref[:, hh * HEAD_DIM:(hh + 1) * HEAD_DIM] = (y * cos + sw * sin).astype(o_ref.dtype)

    @pl.when(j >= n_rope_tiles)
    def _():
        o_ref[...] = u.astype(o_ref.dtype)
        ones = jnp.ones((u.shape[0], HEAD_DIM), va_ref.dtype)
        ones_t = jnp.ones((VT_ROWS - HEAD_DIM, u.shape[0]), vt_ref.dtype)
        for hh in range(tn // HEAD_DIM):
            vh = u[:, hh * HEAD_DIM:(hh + 1) * HEAD_DIM]
            va_ref[:, 2 * hh * HEAD_DIM:(2 * hh + 1) * HEAD_DIM] = vh.astype(va_ref.dtype)
            va_ref[:, (2 * hh + 1) * HEAD_DIM:(2 * hh + 2) * HEAD_DIM] = ones
            vt_ref[hh, :HEAD_DIM, :] = vh.T.astype(vt_ref.dtype)
            vt_ref[hh, HEAD_DIM:, :] = ones_t


def rope_tables(L):
    rows = L // GRID_W
    r, c = jnp.meshgrid(jnp.arange(rows, dtype=F32), jnp.arange(GRID_W, dtype=F32), indexing="ij")
    r = r.reshape(L)
    c = c.reshape(L)
    axis_dim = HEAD_DIM // 2
    inv = 1.0 / (ROPE_THETA ** (jnp.arange(0, axis_dim, 2, dtype=F32) / axis_dim))
    ang = jnp.concatenate([r[:, None] * inv[None], c[:, None] * inv[None]], axis=-1)
    cos = jnp.repeat(jnp.cos(ang), 2, axis=1)
    sign = jnp.tile(jnp.array([-1.0, 1.0], F32), HEAD_DIM // 2)[None, :]
    sin = jnp.repeat(jnp.sin(ang), 2, axis=1) * sign
    return cos, sin


def qkv_project(x, g, w_qkv, q_gain, k_gain, seq_len, *, tm=512):
    T, D = x.shape
    n_out = w_qkv.shape[1]
    nk = N_KV_HEADS * HEAD_DIM
    nq = n_out - 2 * nk
    tn = nk
    tm = _tile(seq_len, tm)
    n_rope = (nq + nk) // tn
    gains = jnp.concatenate([jnp.tile((q_gain * SCORE_SCALE)[None, None, :], (nq // tn, 1, 1)),
                             k_gain[None, None, :],
                             jnp.ones((1, 1, HEAD_DIM), F32)], axis=0)
    cos, sin = rope_tables(seq_len)
    spt = seq_len // tm
    kern = functools.partial(_qkv_kernel, n_rope_tiles=n_rope)
    return pl.pallas_call(
        kern,
        grid=(T // tm, n_out // tn),
        in_specs=[pl.BlockSpec((tm, D), lambda i, j: (i, 0)),
                  pl.BlockSpec((1, D), lambda i, j: (0, 0)),
                  pl.BlockSpec((D, tn), lambda i, j: (0, j)),
                  pl.BlockSpec((None, 1, HEAD_DIM), lambda i, j: (j, 0, 0)),
                  pl.BlockSpec((tm, HEAD_DIM), lambda i, j: (i % spt, 0)),
                  pl.BlockSpec((tm, HEAD_DIM), lambda i, j: (i % spt, 0))],
        out_specs=[pl.BlockSpec((tm, tn), lambda i, j: (i, j)),
                   pl.BlockSpec((tm, 2 * nk), lambda i, j: (i, 0)),
                   pl.BlockSpec((None, N_KV_HEADS, VT_ROWS, tm), lambda i, j: (i // spt, 0, 0, i % spt))],
        out_shape=[jax.ShapeDtypeStruct((T, n_out), BF16), jax.ShapeDtypeStruct((T, 2 * nk), BF16),
                   jax.ShapeDtypeStruct((T // seq_len, N_KV_HEADS, VT_ROWS, seq_len), BF16)],
        scratch_shapes=[pltpu.VMEM((tm, D), BF16)],
        compiler_params=_params(("parallel", "arbitrary")),
        name="qkv_project",
    )(x, g, w_qkv, gains, cos, sin)


def _flash_kernel(q_ref, k_ref, v_ref, o_ref, qs_ref, s0_ref, s1_ref, p0_ref, p1_ref, a0_ref, a1_ref,
                  m_ref, acc_ref, *, group, tq, tk, nk):
    s_refs = (s0_ref, s1_ref)
    p_refs = (p0_ref, p1_ref)
    a_refs = (a0_ref, a1_ref)
    for gq in range(group):
        qs_ref[gq * tq:(gq + 1) * tq, :] = q_ref[:, gq * HEAD_DIM:(gq + 1) * HEAD_DIM]
    m_ref[...] = jnp.full_like(m_ref, -jnp.inf)
    acc_ref[...] = jnp.zeros_like(acc_ref)

    def rows_of(j):
        if isinstance(j, int):
            return pl.ds(j * tk, tk)
        return pl.ds(pl.multiple_of(j * tk, tk), tk)

    def scores(j, slot):
        kb = k_ref[rows_of(j), :]
        s_refs[slot][...] = lax.dot_general(qs_ref[...], kb, (((1,), (1,)), ((), ())),
                                            preferred_element_type=F32)

    def softmax(slot):
        s = s_refs[slot][...]
        m_prev = m_ref[...]
        m_new = jnp.maximum(m_prev, jnp.max(s, axis=-1, keepdims=True))
        a_refs[slot][...] = jnp.exp2(m_prev - m_new)
        p_refs[slot][...] = jnp.exp2(s - jnp.tile(m_new, (1, tk // LANES))).astype(BF16)
        m_ref[...] = m_new

    def accumulate(j, slot):
        pv = _dot(p_refs[slot][...], v_ref[rows_of(j), :])
        acc_ref[...] = acc_ref[...] * jnp.tile(a_refs[slot][...], (1, 2)) + pv

    def step(j, slot, first, last):
        if not last:
            scores(j + 1, 1 - slot)
        softmax(slot)
        if not first:
            accumulate(j - 1, 1 - slot)

    scores(0, 0)
    if nk >= 4 and nk % 2 == 0:
        step(0, 0, True, False)

        def pair(t, c):
            j = 2 * t + 1
            step(j, 1, False, False)
            step(j + 1, 0, False, False)
            return c

        lax.fori_loop(0, (nk - 2) // 2, pair, 0)
        step(nk - 1, 1, False, True)
    else:
        for j in range(nk):
            step(j, j % 2, j == 0, j == nk - 1)
    accumulate(nk - 1, (nk - 1) % 2)

    acc = acc_ref[...]
    o = acc[:, :HEAD_DIM] / acc[:, HEAD_DIM:]
    for gq in range(group):
        o_ref[:, gq * HEAD_DIM:(gq + 1) * HEAD_DIM] = o[gq * tq:(gq + 1) * tq].astype(o_ref.dtype)


def _flash_bounded_kernel(q_ref, k_ref, vt_ref, o_ref, qt_ref, acc_ref, *, group, tq, tk, nk, unroll):
    for gq in range(group):
        qt_ref[:, gq * tq:(gq + 1) * tq] = q_ref[:, gq * HEAD_DIM:(gq + 1) * HEAD_DIM].astype(F32).T.astype(BF16)
    acc_ref[...] = jnp.zeros_like(acc_ref)

    def block(j, c):
        rows = pl.ds(pl.multiple_of(j * tk, tk), tk)
        s_t = _dot(k_ref[rows, :], qt_ref[...])
        acc_ref[...] += _dot(vt_ref[:, rows], jnp.exp2(s_t).astype(BF16))
        return c

    lax.fori_loop(0, nk, block, 0, unroll=unroll)
    acc = acc_ref[...]
    o_t = acc[:HEAD_DIM] / acc[HEAD_DIM:HEAD_DIM + 1]
    for gq in range(group):
        o_ref[:, gq * HEAD_DIM:(gq + 1) * HEAD_DIM] = o_t[:, gq * tq:(gq + 1) * tq].T.astype(o_ref.dtype)


SAFE_SCORE_BOUND = 80.0


def flash_attention(qkv, v_aug, v_t, score_bound, B, L, n_heads, *, tq=256, tk=512):
    group = n_heads // N_KV_HEADS
    tq = _tile(L, tq)
    tk = _tile(L, tk)
    nk = L // tk
    qw = group * HEAD_DIM
    k_col = n_heads
    rows = group * tq
    q_spec = pl.BlockSpec((None, tq, qw), lambda b, h, qi: (b, qi, h))
    k_spec = pl.BlockSpec((None, L, HEAD_DIM), lambda b, h, qi: (b, 0, k_col + h))
    common = dict(
        grid=(B, N_KV_HEADS, L // tq),
        out_specs=pl.BlockSpec((None, tq, qw), lambda b, h, qi: (b, qi, h)),
        out_shape=jax.ShapeDtypeStruct((B, L, n_heads * HEAD_DIM), BF16),
        compiler_params=_params(("parallel", "parallel", "parallel")),
    )
    general = pl.pallas_call(
        functools.partial(_flash_kernel, group=group, tq=tq, tk=tk, nk=nk),
        in_specs=[q_spec, k_spec, pl.BlockSpec((None, L, 2 * HEAD_DIM), lambda b, h, qi: (b, 0, h))],
        scratch_shapes=[pltpu.VMEM((rows, HEAD_DIM), BF16),
                        pltpu.VMEM((rows, tk), F32), pltpu.VMEM((rows, tk), F32),
                        pltpu.VMEM((rows, tk), BF16), pltpu.VMEM((rows, tk), BF16),
                        pltpu.VMEM((rows, LANES), F32), pltpu.VMEM((rows, LANES), F32),
                        pltpu.VMEM((rows, LANES), F32),
                        pltpu.VMEM((rows, 2 * HEAD_DIM), F32)],
        name="flash_attention", **common)
    bounded = pl.pallas_call(
        functools.partial(_flash_bounded_kernel, group=group, tq=tq, tk=tk, nk=nk,
                          unroll=math.gcd(nk, 8)),
        in_specs=[q_spec, k_spec, pl.BlockSpec((None, None, VT_ROWS, L), lambda b, h, qi: (b, h, 0, 0))],
        scratch_shapes=[pltpu.VMEM((HEAD_DIM, rows), BF16),
                        pltpu.VMEM((VT_ROWS, rows), F32)],
        name="flash_attention_bounded", **common)
    return lax.cond(score_bound <= SAFE_SCORE_BOUND,
                    lambda ops: bounded(ops[0], ops[0], ops[2]),
                    lambda ops: general(ops[0], ops[0], ops[1]),
                    (qkv, v_aug, v_t))


def _pack_bf16_pairs(x):
    half = x.shape[1] // 2
    bits = lax.bitcast_convert_type(x.astype(BF16).astype(F32), jnp.uint32)
    return (bits[:, :half] >> 16) | (bits[:, half:] & jnp.uint32(0xFFFF0000))


def _unpack_bf16_pairs(p):
    lo = lax.bitcast_convert_type(p << 16, F32).astype(BF16)
    hi = lax.bitcast_convert_type(p & jnp.uint32(0xFFFF0000), F32).astype(BF16)
    return lo, hi


def _router_kernel(xa_ref, xb_ref, g_ref, rw_ref, h_ref, info_ref, cnt_ref, *, n_a_tiles):
    i = pl.program_id(0)

    def route(x_ref):
        hf = _norm_rows(x_ref[...], g_ref[...])
        h_ref[...] = _pack_bf16_pairs(hf)
        h_hi = hf.astype(BF16)
        h_lo = (hf - h_hi.astype(F32)).astype(BF16)
        rw = rw_ref[...]
        hw = _dot(h_hi, rw)
        logits = hw[:, :LANES] + hw[:, LANES:] + _dot(h_lo, rw[:, :LANES])
        shape = logits.shape
        lane = lax.broadcasted_iota(jnp.int32, shape, 1)
        lanef = lane.astype(F32)
        neg = -jnp.inf
        big = float(LANES)
        is_g = lane < N_GROUPS
        gl = jnp.where(is_g, logits, neg)
        gmax = jnp.max(gl, axis=-1, keepdims=True)
        gidx = jnp.min(jnp.where(gl == gmax, lanef, big), axis=-1, keepdims=True)
        p_g = 1.0 / jnp.sum(jnp.where(is_g, jnp.exp(logits - gmax), 0.0), axis=-1, keepdims=True)
        lo = N_GROUPS + gidx * EXPERTS_PER_GROUP
        el = jnp.where(lanef >= lo, jnp.where(lanef < lo + EXPERTS_PER_GROUP, logits, neg), neg)
        v1 = jnp.max(el, axis=-1, keepdims=True)
        i1 = jnp.min(jnp.where(el == v1, lanef, big), axis=-1, keepdims=True)
        el2 = jnp.where(lanef == i1, neg, el)
        v2 = jnp.max(el2, axis=-1, keepdims=True)
        i2 = jnp.min(jnp.where(el2 == v2, lanef, big), axis=-1, keepdims=True)
        t = jnp.exp(v2 - v1)
        w1 = p_g / (1.0 + t)
        w2 = p_g * t / (1.0 + t)
        e1 = i1 - N_GROUPS
        e2 = i2 - N_GROUPS
        oh1 = jnp.where(lanef == e1, 1.0, 0.0)
        oh2 = jnp.where(lanef == e2, 1.0, 0.0)
        oh = oh1 + oh2
        tm = shape[0]
        tri = jnp.where(lax.broadcasted_iota(jnp.int32, (tm, tm), 1) < lax.broadcasted_iota(jnp.int32, (tm, tm), 0),
                        1.0, 0.0).astype(BF16)
        before = _dot(tri, oh.astype(BF16)) + cnt_ref[...]
        r1 = jnp.sum(before * oh1, axis=-1, keepdims=True)
        r2 = jnp.sum(before * oh2, axis=-1, keepdims=True)
        cnt_ref[...] += jnp.sum(oh, axis=0, keepdims=True)
        vals = (e1, e2, w1, w2, r1, r2)
        info = jnp.zeros(shape, F32)
        for idx, val in enumerate(vals):
            info = jnp.where(lane == idx, val, info)
        info_ref[...] = info

    @pl.when(i == 0)
    def _():
        cnt_ref[...] = jnp.zeros_like(cnt_ref)

    @pl.when(i < n_a_tiles)
    def _():
        route(xa_ref)

    @pl.when(i >= n_a_tiles)
    def _():
        route(xb_ref)


def moe_route(xa, xb, g, router_group, router_expert, *, tm=512):
    Ta, D = xa.shape
    Tb = xb.shape[0]
    tm = _tile(math.gcd(Ta, Tb), tm)
    na, nb = Ta // tm, Tb // tm
    n_e = N_GROUPS * EXPERTS_PER_GROUP
    rw = jnp.concatenate([router_group, jnp.transpose(router_expert, (1, 0, 2)).reshape(D, n_e)], axis=1)
    rw = jnp.pad(rw, ((0, 0), (0, LANES - rw.shape[1])))
    rw_hi = rw.astype(BF16)
    rw = jnp.concatenate([rw_hi, (rw - rw_hi.astype(F32)).astype(BF16)], axis=1)
    kern = functools.partial(_router_kernel, n_a_tiles=na)
    return pl.pallas_call(
        kern,
        grid=(na + nb,),
        in_specs=[pl.BlockSpec((tm, D), lambda i: (jnp.minimum(i, na - 1), 0)),
                  pl.BlockSpec((tm, D), lambda i: (jnp.maximum(i - na, 0), 0)),
                  pl.BlockSpec((1, D), lambda i: (0, 0)),
                  pl.BlockSpec((D, 2 * LANES), lambda i: (0, 0))],
        out_specs=[pl.BlockSpec((tm, D // 2), lambda i: (i, 0)),
                   pl.BlockSpec((tm, LANES), lambda i: (i, 0)),
                   pl.BlockSpec((1, LANES), lambda i: (0, 0))],
        out_shape=[jax.ShapeDtypeStruct((Ta + Tb, D // 2), jnp.uint32),
                   jax.ShapeDtypeStruct((Ta + Tb, LANES), F32),
                   jax.ShapeDtypeStruct((1, LANES), F32)],
        compiler_params=_params(("arbitrary",)),
        name="moe_route",
    )(xa, xb, g, rw)


def moe_plan(info, cnt, tm, n_experts):
    T = info.shape[0]
    n_tiles = (2 * T) // tm + n_experts
    counts = cnt[0, :n_experts].astype(jnp.int32)
    padded = ((counts + tm - 1) // tm) * tm
    ends = jnp.cumsum(padded)
    off = ends - padded
    e = info[:, 0:2].astype(jnp.int32)
    rank = info[:, 4:6].astype(jnp.int32)
    pos = (off[e] + rank).reshape(-1)
    tile_start = jnp.arange(n_tiles, dtype=jnp.int32) * tm
    tile_e = jnp.sum((tile_start[:, None] >= ends[None, :]).astype(jnp.int32), axis=1)
    tile_e = jnp.minimum(tile_e, n_experts - 1)
    tile_valid = jnp.clip(counts[tile_e] - (tile_start - off[tile_e]), 0, tm).astype(jnp.int32)
    n_used = (ends[-1:] // tm).astype(jnp.int32)
    last_row = (n_tiles - 1) * tm
    expert_last = jnp.where(padded > 0, ends - tm, last_row)
    tail = jnp.minimum(n_used[0] + jnp.arange(n_experts, dtype=jnp.int32), n_tiles - 1) * tm
    zero_tiles = jnp.concatenate([expert_last, tail]).astype(jnp.int32)[None, :]
    return pos, tile_e, tile_valid, n_used, zero_tiles


DMA_UNROLL = 8


def _dispatch_kernel(pos_ref, ztile_ref, h_ref, hs_hbm, zbuf, sem, zsem):
    tm = h_ref.shape[0]

    @pl.when(pl.program_id(0) == 0)
    def _():
        zbuf[...] = jnp.zeros_like(zbuf)
        zrows = zbuf.shape[0]

        def zero_copy(k):
            return pltpu.make_async_copy(zbuf, hs_hbm.at[pl.ds(pl.multiple_of(ztile_ref[0, k], zrows), zrows)], zsem)

        for k in range(ztile_ref.shape[1]):
            zero_copy(k).start()
        for k in range(ztile_ref.shape[1]):
            zero_copy(k).wait()

    def row_copy(r, slot):
        return pltpu.make_async_copy(h_ref.at[pl.ds(r, 1)], hs_hbm.at[pl.ds(pos_ref[0, 0, 2 * r + slot], 1)], sem)

    def start(r, c):
        row_copy(r, 0).start()
        row_copy(r, 1).start()
        return c

    def wait(r, c):
        row_copy(r, 0).wait()
        row_copy(r, 1).wait()
        return c

    lax.fori_loop(0, tm, start, 0, unroll=DMA_UNROLL)
    lax.fori_loop(0, tm, wait, 0, unroll=DMA_UNROLL)


def moe_dispatch(h, pos, zero_tiles, n_sorted_rows, tile_rows, *, tm=512):
    T, C = h.shape
    tm = _tile(T, tm)
    return pl.pallas_call(
        _dispatch_kernel,
        grid=(T // tm,),
        in_specs=[pl.BlockSpec((1, 1, 2 * tm), lambda i: (i, 0, 0), memory_space=pltpu.SMEM),
                  pl.BlockSpec(zero_tiles.shape, lambda i: (0, 0), memory_space=pltpu.SMEM),
                  pl.BlockSpec((tm, C), lambda i: (i, 0))],
        out_specs=pl.BlockSpec(memory_space=pl.ANY),
        out_shape=jax.ShapeDtypeStruct((n_sorted_rows, C), h.dtype),
        scratch_shapes=[pltpu.VMEM((tile_rows, C), h.dtype), pltpu.SemaphoreType.DMA, pltpu.SemaphoreType.DMA],
        compiler_params=_params(("arbitrary",)),
        name="moe_dispatch",
    )(pos.reshape(T // tm, 1, 2 * tm), zero_tiles, h)


def _expert_kernel(te_ref, tv_ref, nu_ref, x_ref, wg_ref, wu_ref, wd_ref, y_ref):
    del te_ref, nu_ref
    valid = tv_ref[pl.program_id(0)]

    @pl.when(valid == 0)
    def _():
        y_ref[...] = jnp.zeros_like(y_ref)

    @pl.when(valid > 0)
    def _():
        x_lo, x_hi = _unpack_bf16_pairs(x_ref[...])
        half = x_ref.shape[1]
        gate = (_dot(x_lo, wg_ref[:half, :].astype(BF16)) + _dot(x_hi, wg_ref[half:, :].astype(BF16)))
        up = (_dot(x_lo, wu_ref[:half, :].astype(BF16)) + _dot(x_hi, wu_ref[half:, :].astype(BF16)))
        a = (gate * jax.nn.sigmoid(gate) * up).astype(BF16)
        y_ref[...] = _dot(a, wd_ref[...].astype(BF16))


def moe_experts(hs, tile_e, tile_valid, n_used, w_gate, w_up, w_down, layer, tm):
    P = hs.shape[0]
    D = w_gate.shape[2]
    d_e = w_gate.shape[-1]

    def row_map(i, te, tv, nu):
        return (jnp.minimum(i, nu[0] - 1), 0)

    def out_map(i, te, tv, nu):
        return (i, 0)

    def w_map(i, te, tv, nu):
        return (layer, te[i], 0, 0)

    grid_spec = pltpu.PrefetchScalarGridSpec(
        num_scalar_prefetch=3,
        grid=(P // tm,),
        in_specs=[pl.BlockSpec((tm, D // 2), row_map),
                  pl.BlockSpec((None, None, D, d_e), w_map),
                  pl.BlockSpec((None, None, D, d_e), w_map),
                  pl.BlockSpec((None, None, d_e, D), w_map)],
        out_specs=pl.BlockSpec((tm, D), out_map),
    )
    return pl.pallas_call(
        _expert_kernel,
        grid_spec=grid_spec,
        out_shape=jax.ShapeDtypeStruct((P, D), F32),
        compiler_params=_params(("arbitrary",)),
        name="moe_experts",
    )(tile_e, tile_valid, n_used, hs, w_gate, w_up, w_down)


def _combine_kernel(pos_ref, posn_ref, x_ref, info_ref, g_ref, y_hbm, o_ref, ybuf, sem, *, final_norm):
    tm = x_ref.shape[0]
    i = pl.program_id(0)
    buf = i % 2

    def row_copy(p_ref, b, r, slot):
        return pltpu.make_async_copy(y_hbm.at[pl.ds(p_ref[0, 0, 2 * r + slot], 1)],
                                     ybuf.at[b, slot, pl.ds(r, 1)], sem.at[b])

    def start_tile(p_ref, b):
        def body(r, c):
            row_copy(p_ref, b, r, 0).start()
            row_copy(p_ref, b, r, 1).start()
            return c
        lax.fori_loop(0, tm, body, 0, unroll=DMA_UNROLL)

    @pl.when(i == 0)
    def _():
        start_tile(pos_ref, 0)

    @pl.when(i + 1 < pl.num_programs(0))
    def _():
        start_tile(posn_ref, 1 - buf)

    def wait(r, c):
        row_copy(pos_ref, buf, r, 0).wait()
        row_copy(pos_ref, buf, r, 1).wait()
        return c

    lax.fori_loop(0, tm, wait, 0, unroll=DMA_UNROLL)
    info = info_ref[...]
    y = info[:, 2:3] * ybuf[buf, 0] + info[:, 3:4] * ybuf[buf, 1]
    x = x_ref[...] + y
    if final_norm:
        x = _norm_rows(x, g_ref[...])
    o_ref[...] = x


def moe_combine(x, y, pos, info, row_off, g_final, final_norm, *, tm=256):
    Tg, D = x.shape
    T = info.shape[0]
    tm = _tile(math.gcd(Tg, row_off) if row_off else Tg, tm)
    ob = row_off // tm
    kern = functools.partial(_combine_kernel, final_norm=final_norm)
    n = Tg // tm
    pos3 = pos.reshape(T // tm, 1, 2 * tm)
    return pl.pallas_call(
        kern,
        grid=(n,),
        in_specs=[pl.BlockSpec((1, 1, 2 * tm), lambda i: (i + ob, 0, 0), memory_space=pltpu.SMEM),
                  pl.BlockSpec((1, 1, 2 * tm), lambda i: (jnp.minimum(i + 1, n - 1) + ob, 0, 0),
                               memory_space=pltpu.SMEM),
                  pl.BlockSpec((tm, D), lambda i: (i, 0)),
                  pl.BlockSpec((tm, LANES), lambda i: (i + ob, 0)),
                  pl.BlockSpec((1, D), lambda i: (0, 0)),
                  pl.BlockSpec(memory_space=pl.ANY)],
        out_specs=pl.BlockSpec((tm, D), lambda i: (i, 0)),
        out_shape=jax.ShapeDtypeStruct((Tg, D), F32),
        scratch_shapes=[pltpu.VMEM((2, 2, tm, D), F32), pltpu.SemaphoreType.DMA((2,))],
        compiler_params=_params(("arbitrary",)),
        name="moe_combine",
    )(pos3, pos3, x, info, g_final, y)


MOE_TILE = 512


def hier_moe_block(xs, g, router_group, router_expert, w_gate, w_up, w_down, layer, g_final, final_norm):
    xa, xb = xs
    n_experts = w_gate.shape[1]
    h, info, cnt = moe_route(xa, xb, g, router_group, router_expert)
    T = h.shape[0]
    tm = _tile(2 * T, MOE_TILE)
    pos, tile_e, tile_valid, n_used, zero_tiles = moe_plan(info, cnt, tm, n_experts)
    hs = moe_dispatch(h, pos, zero_tiles, tile_e.shape[0] * tm, tm)
    y = moe_experts(hs, tile_e, tile_valid, n_used, w_gate, w_up, w_down, layer, tm)
    oa = moe_combine(xa, y, pos, info, 0, g_final, final_norm)
    ob = moe_combine(xb, y, pos, info, xa.shape[0], g_final, final_norm)
    return oa, ob


def hyena_block(x, L, g, w_in, conv_w, conv_b, f_w1, f_b1, f_win, f_bin, f_freq, f_w3, skip_b, w_out):
    D = x.shape[1]
    x0, zv = hyena_inproj(x, g, w_in, conv_w, conv_b[None, :], L)
    taps, asum = hyena_filter_taps(L, w_out.shape[0], f_w1, f_b1, f_win, f_bin, f_freq, f_w3)
    return hyena_long_conv(zv, x0, x, w_out, taps, asum, skip_b[None, :], L)


def attention_block(x, B, L, g, w_qkv, q_gain, k_gain, w_o):
    D = x.shape[1]
    n_heads = w_o.shape[0] // HEAD_DIM
    qkv, v_aug, v_t = qkv_project(x, g, w_qkv, q_gain, k_gain, L)
    score_bound = 1.02 * HEAD_DIM * SCORE_SCALE * jnp.max(jnp.abs(q_gain)) * jnp.max(jnp.abs(k_gain))
    o = flash_attention(qkv.reshape(B, L, qkv.shape[1]), v_aug.reshape(B, L, v_aug.shape[1]), v_t, score_bound,
                        B, L, n_heads)
    return matmul_residual(o.reshape(B * L, n_heads * HEAD_DIM), w_o, x)


def kernel(x_prompt, x_sample, norm_mix, norm_ffn, norm_final, hy_w_in, hy_conv_w, hy_conv_b, hy_f_w1, hy_f_b1, hy_f_win, hy_f_bin, hy_f_freq, hy_f_w3, hy_skip_b, hy_w_out, at_w_qkv, at_q_gain, at_k_gain, at_w_o, moe_router_group, moe_router_expert, moe_w_gate, moe_w_up, moe_w_down):
    D = x_prompt.shape[-1]
    depth = norm_mix.shape[0]
    shapes = [x_prompt.shape, x_sample.shape]
    for s in shapes:
        assert s[0] == 2, "the long convolution packs exactly two sequences per group"
    xs = [x_prompt.reshape(-1, D), x_sample.reshape(-1, D)]
    hy_w_in_b = hy_w_in.astype(BF16)
    hy_w_out_b = hy_w_out.astype(BF16)
    at_w_qkv_b = at_w_qkv.astype(BF16)
    at_w_o_b = at_w_o.astype(BF16)
    g_final = norm_final[None, :]
    for i in range(depth):
        j = i // N_MIXERS
        g = norm_mix[i][None, :]
        if i % N_MIXERS == 0:
            xs = [hyena_block(x, s[1], g, hy_w_in_b[j], hy_conv_w[j], hy_conv_b[j], hy_f_w1[j], hy_f_b1[j],
                              hy_f_win[j], hy_f_bin[j], hy_f_freq[j], hy_f_w3[j], hy_skip_b[j], hy_w_out_b[j])
                  for x, s in zip(xs, shapes)]
        else:
            xs = [attention_block(x, s[0], s[1], g, at_w_qkv_b[j], at_q_gain[j], at_k_gain[j], at_w_o_b[j])
                  for x, s in zip(xs, shapes)]
        xs = hier_moe_block(xs, norm_ffn[i][None, :], moe_router_group[i], moe_router_expert[i],
                            moe_w_gate, moe_w_up, moe_w_down, i, g_final, i == depth - 1)
    return (xs[0].reshape(shapes[0]), xs[1].reshape(shapes[1]))
```

```python
import functools
import math

import jax
import jax.numpy as jnp
import numpy as np
from jax import lax
from jax.experimental import pallas as pl
from jax.experimental.pallas import tpu as pltpu

RMS_EPS = 1e-6
GRID_W = 64
EMB_DIM = 33
DECAY_TARGET = 1e-2
FAST_DECAY_PCT = 0.3
SLOW_DECAY_PCT = 1.5
MIN_DECAY = math.log(DECAY_TARGET) / FAST_DECAY_PCT
MAX_DECAY = math.log(DECAY_TARGET) / SLOW_DECAY_PCT
HEAD_DIM = 128
SCORE_SCALE = HEAD_DIM ** -0.5 * math.log2(math.e)
N_KV_HEADS = 4
ROPE_THETA = 10000.0
N_GROUPS = 4
EXPERTS_PER_GROUP = 8
N_MIXERS = 2

LANES = 128
BF16_SUBLANES = 16
VMEM_LIMIT_BYTES = 52 * 1024 * 1024
DFT_N2 = 128
MXU_ROWS = 256

BF16 = jnp.bfloat16
F32 = jnp.float32


def _params(sem):
    return pltpu.CompilerParams(dimension_semantics=sem, vmem_limit_bytes=VMEM_LIMIT_BYTES)


def _dot(a, b):
    return jnp.dot(a, b, preferred_element_type=F32)


def _dot_3pass(a, b):
    a_hi = a.astype(BF16)
    a_lo = (a - a_hi.astype(F32)).astype(BF16)
    b_hi = b.astype(BF16)
    b_lo = (b - b_hi.astype(F32)).astype(BF16)
    return _dot(a_hi, b_hi) + _dot(a_hi, b_lo) + _dot(a_lo, b_hi)


def _norm_rows(x, g):
    ms = jnp.mean(x * x, axis=-1, keepdims=True)
    return x * lax.rsqrt(ms + RMS_EPS) * g


def _tile(n, pref):
    return pref if n % pref == 0 else n


def _matmul_res_kernel(a_ref, w_ref, r_ref, o_ref):
    o_ref[...] = r_ref[...] + _dot(a_ref[...].astype(BF16), w_ref[...])


def matmul_residual(a, w, res, *, tm=1024, tn=512):
    T, K = a.shape
    N = w.shape[1]
    tm = _tile(T, tm)
    tn = _tile(N, tn)
    return pl.pallas_call(
        _matmul_res_kernel,
        grid=(T // tm, N // tn),
        in_specs=[pl.BlockSpec((tm, K), lambda i, j: (i, 0)),
                  pl.BlockSpec((K, tn), lambda i, j: (0, j)),
                  pl.BlockSpec((tm, tn), lambda i, j: (i, j))],
        out_specs=pl.BlockSpec((tm, tn), lambda i, j: (i, j)),
        out_shape=jax.ShapeDtypeStruct((T, N), F32),
        compiler_params=_params(("parallel", "arbitrary")),
        name="matmul_residual",
    )(a, w, res)


HALO = BF16_SUBLANES


def _hy_inproj_kernel(x_ref, xp_ref, xn_ref, g_ref, w0_ref, w1_ref, w2_ref,
                      cw0_ref, cw1_ref, cw2_ref, cb0_ref, cb1_ref, cb2_ref,
                      x0_out, zv_out, hn_ref, *, tm, tiles_per_seq):
    i = pl.program_id(0)
    j = pl.program_id(1)

    @pl.when(j == 0)
    def _():
        g = g_ref[...]
        hn_ref[HALO:HALO + tm, :] = _norm_rows(x_ref[...], g).astype(BF16)
        first = (i % tiles_per_seq) == 0
        last = (i % tiles_per_seq) == tiles_per_seq - 1
        hp = _norm_rows(xp_ref[...], g)
        hn_ref[0:HALO, :] = jnp.where(first, 0.0, hp).astype(BF16)
        hx = _norm_rows(xn_ref[...], g)
        hn_ref[HALO + tm:, :] = jnp.where(last, 0.0, hx).astype(BF16)

    h = hn_ref[...]
    rows = tm + 2 * HALO

    def conv(w_ref, cw_ref, cb_ref):
        u = _dot(h, w_ref[...])
        um = pltpu.roll(u, 1, 0)[HALO:HALO + tm]
        up = pltpu.roll(u, rows - 1, 0)[HALO:HALO + tm]
        uc = u[HALO:HALO + tm]
        cw = cw_ref[...]
        return um * cw[0:1] + uc * cw[1:2] + up * cw[2:3] + cb_ref[...]

    x0_out[...] = conv(w0_ref, cw0_ref, cb0_ref)
    x1 = conv(w1_ref, cw1_ref, cb1_ref)
    v = conv(w2_ref, cw2_ref, cb2_ref)
    zv_out[...] = v * x1


def hyena_inproj(x, g, w_in, conv_w, conv_b, seq_len, *, tm=512, tn=512):
    T, D = x.shape
    W = w_in.shape[1] // 3
    tm = _tile(seq_len, tm)
    tn = _tile(W, tn)
    nj = W // tn
    hb = tm // HALO
    n_halo_blocks = T // HALO
    kern = functools.partial(_hy_inproj_kernel, tm=tm, tiles_per_seq=seq_len // tm)

    def wspec(c):
        return pl.BlockSpec((D, tn), lambda i, j, c=c: (0, j + c * nj))

    def cwspec(c):
        return pl.BlockSpec((3, tn), lambda i, j, c=c: (0, j + c * nj))

    def cbspec(c):
        return pl.BlockSpec((1, tn), lambda i, j, c=c: (0, j + c * nj))

    return pl.pallas_call(
        kern,
        grid=(T // tm, nj),
        in_specs=[pl.BlockSpec((tm, D), lambda i, j: (i, 0)),
                  pl.BlockSpec((HALO, D), lambda i, j: (jnp.maximum(i * hb - 1, 0), 0)),
                  pl.BlockSpec((HALO, D), lambda i, j: (jnp.minimum((i + 1) * hb, n_halo_blocks - 1), 0)),
                  pl.BlockSpec((1, D), lambda i, j: (0, 0)),
                  wspec(0), wspec(1), wspec(2),
                  cwspec(0), cwspec(1), cwspec(2),
                  cbspec(0), cbspec(1), cbspec(2)],
        out_specs=[pl.BlockSpec((tm, tn), lambda i, j: (i, j)),
                   pl.BlockSpec((tm, tn), lambda i, j: (i, j))],
        out_shape=[jax.ShapeDtypeStruct((T, W), F32), jax.ShapeDtypeStruct((T, W), F32)],
        scratch_shapes=[pltpu.VMEM((tm + 2 * HALO, D), BF16)],
        compiler_params=_params(("parallel", "arbitrary")),
        name="hyena_inproj",
    )(x, x, x, g, w_in, w_in, w_in, conv_w, conv_w, conv_w, conv_b, conv_b, conv_b)


def _filter_mlp_kernel(z_ref, w1_ref, b1_ref, win_ref, bin_ref, fr_ref, h_ref):
    hp = lax.Precision.HIGHEST
    fr = fr_ref[...]
    h = jnp.sin(fr * (jnp.dot(z_ref[...], w1_ref[...], precision=hp, preferred_element_type=F32) + b1_ref[...]))
    for l in range(win_ref.shape[0]):
        h = jnp.sin(fr * (jnp.dot(h, win_ref[l], precision=hp, preferred_element_type=F32) + bin_ref[l]))
    h_ref[...] = h


def hyena_filter_features(L, D, f_w1, f_b1, f_win, f_bin, f_freq):
    H = f_w1.shape[1]
    t = jnp.linspace(0.0, 1.0, L, dtype=F32)[:, None]
    bands = (EMB_DIM - 1) // 2
    w = 2.0 * math.pi * jnp.arange(L, dtype=F32)[:, None] / L
    fb = jnp.linspace(1e-4, bands - 1, bands, dtype=F32)[None, :]
    z = jnp.concatenate([t, jnp.cos(fb * w), -jnp.sin(fb * w)], axis=-1)

    def lag_major_to_dft_order(a):
        return a.reshape(L // DFT_N2, DFT_N2, a.shape[1]).transpose(1, 0, 2).reshape(L, a.shape[1])

    z = lag_major_to_dft_order(z)
    t = lag_major_to_dft_order(t)
    zp = jnp.pad(z, ((0, 0), (0, LANES - EMB_DIM)))
    w1p = jnp.pad(f_w1, ((0, LANES - EMB_DIM), (0, 0)))
    n_in = f_win.shape[0]
    tl = _tile(L, 512)
    h = pl.pallas_call(
        _filter_mlp_kernel,
        grid=(L // tl,),
        in_specs=[pl.BlockSpec((tl, LANES), lambda i: (i, 0)),
                  pl.BlockSpec((LANES, H), lambda i: (0, 0)),
                  pl.BlockSpec((1, H), lambda i: (0, 0)),
                  pl.BlockSpec((n_in, H, H), lambda i: (0, 0, 0)),
                  pl.BlockSpec((n_in, 1, H), lambda i: (0, 0, 0)),
                  pl.BlockSpec((1, H), lambda i: (0, 0))],
        out_specs=pl.BlockSpec((tl, H), lambda i: (i, 0)),
        out_shape=jax.ShapeDtypeStruct((L, H), F32),
        compiler_params=_params(("parallel",)),
        name="filter_mlp",
    )(zp, w1p, f_b1[None, :], f_win, f_bin[:, None, :], f_freq[None, :])

    deltas = jnp.abs(jnp.linspace(MIN_DECAY, MAX_DECAY, D, dtype=F32))[None, :]
    return h, t, jnp.concatenate([deltas, deltas], axis=1)


def dft_tables(N1, N2):
    N = N1 * N2
    half = N1 // 2
    k1 = jnp.arange(N1, dtype=jnp.int32)[None, :, None]
    n = (jnp.arange(half, dtype=jnp.int32)[None, None, :] * N2
         + jnp.arange(N2, dtype=jnp.int32)[:, None, None])
    ang = ((k1 * n) % N).astype(F32) * (2.0 * math.pi / N)
    c, s = jnp.cos(ang), jnp.sin(ang)
    fwd = jnp.concatenate([jnp.concatenate([c, s], axis=2),
                           jnp.concatenate([-s, c], axis=2)], axis=1)
    inv = jnp.swapaxes(fwd, 1, 2)
    fwd_real = jnp.concatenate([c, -s], axis=1)
    k2 = jnp.arange(N2, dtype=jnp.int32)
    ang2 = ((k2[:, None] * k2[None, :]) % N2).astype(F32) * (2.0 * math.pi / N2)
    c2, s2 = jnp.cos(ang2), jnp.sin(ang2)
    f2 = jnp.concatenate([jnp.concatenate([c2, s2], axis=1),
                          jnp.concatenate([-s2, c2], axis=1)], axis=0)
    g2 = f2.T
    return fwd.astype(BF16), inv.astype(BF16), fwd_real.astype(BF16), f2.astype(BF16), g2.astype(BF16)


def _pack_complex(re, im):
    rb = lax.bitcast_convert_type(re.astype(BF16).astype(F32), jnp.uint32)
    ib = lax.bitcast_convert_type(im.astype(BF16).astype(F32), jnp.uint32)
    return (rb >> 16) | (ib & jnp.uint32(0xFFFF0000))


def _unpack_complex(p):
    re = lax.bitcast_convert_type(p << 16, F32)
    im = lax.bitcast_convert_type(p & jnp.uint32(0xFFFF0000), F32)
    return jnp.concatenate([re, im], axis=0).astype(BF16)


def _double_buffered(in_copies, out_copies, compute):
    j = pl.program_id(0)
    n = pl.num_programs(0)
    slot = j % 2

    @pl.when(j == 0)
    def _():
        for cp in in_copies(0, 0):
            cp.start()

    @pl.when(j + 1 < n)
    def _():
        for cp in in_copies(j + 1, 1 - slot):
            cp.start()

    for cp in in_copies(j, slot):
        cp.wait()

    @pl.when(j >= 2)
    def _():
        for cp in out_copies(j - 2, slot):
            cp.wait()

    compute(slot)
    for cp in out_copies(j, slot):
        cp.start()

    @pl.when(j == n - 1)
    def _():
        for cp in out_copies(j, slot):
            cp.wait()

        @pl.when(n >= 2)
        def _():
            for cp in out_copies(j - 1, 1 - slot):
                cp.wait()


def _fft_s1_kernel(z_hbm, m_ref, a_hbm, zbuf, obuf, sem_in, sem_out):
    n_g = zbuf.shape[1]
    half = zbuf.shape[2] // 2
    n1 = obuf.shape[2]

    def in_copies(step, slot):
        return [pltpu.make_async_copy(z_hbm.at[b, :, step * n_g + g, :], zbuf.at[slot, g, pl.ds(b * half, half)],
                                      sem_in.at[slot])
                for g in range(n_g) for b in range(2)]

    def out_copies(step, slot):
        return [pltpu.make_async_copy(obuf.at[slot, g], a_hbm.at[:, step * n_g + g, :], sem_out.at[slot])
                for g in range(n_g)]

    def compute(slot):
        for g in range(n_g):
            a = _dot(m_ref[g], zbuf[slot, g].astype(BF16))
            obuf[slot, g] = _pack_complex(a[:n1], a[n1:])

    _double_buffered(in_copies, out_copies, compute)


def _fft_s1_real_kernel(h_ref, t_ref, w3_ref, dl_ref, m_ref, a_hbm, s_ref, obuf, sem_out):
    n_g = obuf.shape[1]
    n1 = obuf.shape[2]
    j = pl.program_id(0)

    @pl.when(j == 0)
    def _():
        s_ref[...] = jnp.zeros_like(s_ref)

    def out_copies(step, slot):
        return [pltpu.make_async_copy(obuf.at[slot, g], a_hbm.at[:, step * n_g + g, :], sem_out.at[slot])
                for g in range(n_g)]

    def compute(slot):
        w3 = w3_ref[...]
        for g in range(n_g):
            taps = _dot_3pass(h_ref[g], w3) * jnp.exp(-t_ref[g] * dl_ref[...])
            if g == 0:
                row = lax.broadcasted_iota(jnp.int32, taps.shape, 0)
                col = lax.broadcasted_iota(jnp.int32, taps.shape, 1)
                drop = jnp.logical_and(jnp.logical_and(j == 0, row == 0), col >= taps.shape[1] // 2)
                taps = jnp.where(drop, 0.0, taps)
            s_ref[...] += jnp.sum(jnp.abs(taps), axis=0, keepdims=True)
            a = _dot(m_ref[g], taps.astype(BF16))
            obuf[slot, g] = _pack_complex(a[:n1], a[n1:])

    _double_buffered(lambda step, slot: [], out_copies, compute)


def _fft_mid_kernel(a_ref, af_ref, ab_ref, s_ref, f2_ref, g2_ref, c_ref):
    n2 = a_ref.shape[0]
    f2 = f2_ref[...]
    xf = _dot(f2, _unpack_complex(af_ref[...]))
    xb = _dot(f2, _unpack_complex(ab_ref[...]))
    s = s_ref[...]
    kr = (xf[:n2] + xb[:n2]) * s
    ki = (xf[n2:] - xb[n2:]) * s
    x = _dot(f2, _unpack_complex(a_ref[...]))
    xr, xi = x[:n2], x[n2:]
    y = jnp.concatenate([xr * kr - xi * ki, xr * ki + xi * kr], axis=0).astype(BF16)
    c = _dot(g2_ref[...], y)
    c_ref[...] = _pack_complex(c[:n2], c[n2:])


def _fft_out_kernel(c_hbm, m_ref, zv_hbm, x0_hbm, r_hbm, sk_ref, w_ref, o_hbm,
                    cbuf, zbuf, xbuf, rbuf, obuf, sem_in, sem_out):
    n_g = zbuf.shape[1]
    n1 = zbuf.shape[2]
    half = n1 // 2

    def in_copies(step, slot):
        cps = []
        for g in range(n_g):
            n2 = step * n_g + g
            cps.append(pltpu.make_async_copy(c_hbm.at[:, n2, :], cbuf.at[slot, g], sem_in.at[slot]))
            for b in range(2):
                rows = pl.ds(b * half, half)
                for src, dst in ((zv_hbm, zbuf), (x0_hbm, xbuf), (r_hbm, rbuf)):
                    cps.append(pltpu.make_async_copy(src.at[b, :, n2, :], dst.at[slot, g, rows], sem_in.at[slot]))
        return cps

    def out_copies(step, slot):
        return [pltpu.make_async_copy(obuf.at[slot, g, pl.ds(b * half, half)], o_hbm.at[b, :, step * n_g + g, :],
                                      sem_out.at[slot])
                for g in range(n_g) for b in range(2)]

    def compute(slot):
        ys = []
        for g in range(n_g):
            y = _dot(m_ref[g], _unpack_complex(cbuf[slot, g]))
            ys.append(((y + zbuf[slot, g] * sk_ref[...]) * xbuf[slot, g]).astype(BF16))
        proj = _dot(ys[0] if n_g == 1 else jnp.concatenate(ys, axis=0), w_ref[...])
        for g in range(n_g):
            obuf[slot, g] = rbuf[slot, g] + proj[g * n1:(g + 1) * n1]

    _double_buffered(in_copies, out_copies, compute)


def hyena_long_conv(zv, x0, x, w_out, filt, f_w3, skip_b, L):
    D = zv.shape[1]
    N = 2 * L
    N2 = DFT_N2
    N1 = N // N2
    half = N1 // 2
    n_g = max(1, min(N2, MXU_ROWS // N1))
    n_steps = N2 // n_g
    fwd, inv, fwd_real, f2, g2 = dft_tables(N1, N2)
    td2 = _tile(D, 2048)
    nd2 = D // td2
    zv4 = zv.reshape(2, half, N2, D)
    x04 = x0.reshape(2, half, N2, D)
    x4 = x.reshape(2, half, N2, D)
    any_spec = pl.BlockSpec(memory_space=pl.ANY)

    h, t, rates = filt
    H = h.shape[1]
    af, asum = pl.pallas_call(
        _fft_s1_real_kernel,
        grid=(n_steps,),
        in_specs=[pl.BlockSpec((n_g, half, H), lambda j: (j, 0, 0)),
                  pl.BlockSpec((n_g, half, 1), lambda j: (j, 0, 0)),
                  pl.BlockSpec((H, 2 * D), lambda j: (0, 0)),
                  pl.BlockSpec((1, 2 * D), lambda j: (0, 0)),
                  pl.BlockSpec((n_g, 2 * N1, half), lambda j: (j, 0, 0))],
        out_specs=[any_spec, pl.BlockSpec((1, 2 * D), lambda j: (0, 0))],
        out_shape=[jax.ShapeDtypeStruct((N1, N2, 2 * D), jnp.uint32), jax.ShapeDtypeStruct((1, 2 * D), F32)],
        scratch_shapes=[pltpu.VMEM((2, n_g, N1, 2 * D), jnp.uint32), pltpu.SemaphoreType.DMA((2,))],
        compiler_params=_params(("arbitrary",)),
        name="filter_dft_stage1",
    )(h.reshape(N2, half, H), t.reshape(N2, half, 1), f_w3, rates, fwd_real)
    scale = 1.0 / ((asum[:, :D] + asum[:, D:]) * N)

    a = pl.pallas_call(
        _fft_s1_kernel,
        grid=(n_steps,),
        in_specs=[any_spec,
                  pl.BlockSpec((n_g, 2 * N1, N1), lambda j: (j, 0, 0))],
        out_specs=any_spec,
        out_shape=jax.ShapeDtypeStruct((N1, N2, D), jnp.uint32),
        scratch_shapes=[pltpu.VMEM((2, n_g, N1, D), F32), pltpu.VMEM((2, n_g, N1, D), jnp.uint32),
                        pltpu.SemaphoreType.DMA((2,)), pltpu.SemaphoreType.DMA((2,))],
        compiler_params=_params(("arbitrary",)),
        name="conv_dft_stage1",
    )(zv4, fwd)

    c = pl.pallas_call(
        _fft_mid_kernel,
        grid=(N1, nd2),
        in_specs=[pl.BlockSpec((None, N2, td2), lambda k, j: (k, 0, j)),
                  pl.BlockSpec((None, N2, td2), lambda k, j: (k, 0, j)),
                  pl.BlockSpec((None, N2, td2), lambda k, j: (k, 0, j + nd2)),
                  pl.BlockSpec((1, td2), lambda k, j: (0, j)),
                  pl.BlockSpec((2 * N2, 2 * N2), lambda k, j: (0, 0)),
                  pl.BlockSpec((2 * N2, 2 * N2), lambda k, j: (0, 0))],
        out_specs=pl.BlockSpec((None, N2, td2), lambda k, j: (k, 0, j)),
        out_shape=jax.ShapeDtypeStruct((N1, N2, D), jnp.uint32),
        compiler_params=_params(("parallel", "parallel")),
        name="conv_dft_stage2",
    )(a, af, af, scale, f2, g2)

    def buf(dtype):
        return pltpu.VMEM((2, n_g, N1, D), dtype)

    o = pl.pallas_call(
        _fft_out_kernel,
        grid=(n_steps,),
        in_specs=[any_spec,
                  pl.BlockSpec((n_g, N1, 2 * N1), lambda j: (j, 0, 0)),
                  any_spec, any_spec, any_spec,
                  pl.BlockSpec((1, D), lambda j: (0, 0)),
                  pl.BlockSpec(w_out.shape, lambda j: (0, 0))],
        out_specs=any_spec,
        out_shape=jax.ShapeDtypeStruct((2, half, N2, D), F32),
        scratch_shapes=[buf(jnp.uint32), buf(F32), buf(F32), buf(F32), buf(F32),
                        pltpu.SemaphoreType.DMA((2,)), pltpu.SemaphoreType.DMA((2,))],
        compiler_params=_params(("arbitrary",)),
        name="conv_dft_inverse_outproj",
    )(c, inv, zv4, x04, x4, skip_b, w_out)
    return o.reshape(2 * L, D)


VT_ROWS = HEAD_DIM + BF16_SUBLANES


def _qkv_kernel(x_ref, g_ref, w_ref, gain_ref, cos_ref, sin_ref, o_ref, va_ref, vt_ref, hn_ref, *, n_rope_tiles):
    j = pl.program_id(1)

    @pl.when(j == 0)
    def _():
        hn_ref[...] = _norm_rows(x_ref[...], g_ref[...]).astype(BF16)

    u = _dot(hn_ref[...], w_ref[...])
    tn = u.shape[1]

    @pl.when(j < n_rope_tiles)
    def _():
        gain = gain_ref[...]
        cos = cos_ref[...]
        sin = sin_ref[...]
        even = (lax.broadcasted_iota(jnp.int32, cos.shape, 1) % 2) == 0
        for hh in range(tn // HEAD_DIM):
            xh = u[:, hh * HEAD_DIM:(hh + 1) * HEAD_DIM]
            y = _norm_rows(xh, gain)
            sw = jnp.where(even, pltpu.roll(y, HEAD_DIM - 1, 1), pltpu.roll(y, 1, 1))
            o_ref[:, hh * HEAD_DIM:(hh + 1) * HEAD_DIM] = (y * cos + sw * sin).astype(o_ref.dtype)

    @pl.when(j >= n_rope_tiles)
    def _():
        o_ref[...] = u.astype(o_ref.dtype)
        ones = jnp.ones((u.shape[0], HEAD_DIM), va_ref.dtype)
        ones_t = jnp.ones((VT_ROWS - HEAD_DIM, u.shape[0]), vt_ref.dtype)
        for hh in range(tn // HEAD_DIM):
            vh = u[:, hh * HEAD_DIM:(hh + 1) * HEAD_DIM]
            va_ref[:, 2 * hh * HEAD_DIM:(2 * hh + 1) * HEAD_DIM] = vh.astype(va_ref.dtype)
            va_ref[:, (2 * hh + 1) * HEAD_DIM:(2 * hh + 2) * HEAD_DIM] = ones
            vt_ref[hh, :HEAD_DIM, :] = vh.T.astype(vt_ref.dtype)
            vt_ref[hh, HEAD_DIM:, :] = ones_t


def rope_tables(L):
    rows = L // GRID_W
    r, c = jnp.meshgrid(jnp.arange(rows, dtype=F32), jnp.arange(GRID_W, dtype=F32), indexing="ij")
    r = r.reshape(L)
    c = c.reshape(L)
    axis_dim = HEAD_DIM // 2
    inv = 1.0 / (ROPE_THETA ** (jnp.arange(0, axis_dim, 2, dtype=F32) / axis_dim))
    ang = jnp.concatenate([r[:, None] * inv[None], c[:, None] * inv[None]], axis=-1)
    cos = jnp.repeat(jnp.cos(ang), 2, axis=1)
    sign = jnp.tile(jnp.array([-1.0, 1.0], F32), HEAD_DIM // 2)[None, :]
    sin = jnp.repeat(jnp.sin(ang), 2, axis=1) * sign
    return cos, sin


def qkv_project(x, g, w_qkv, q_gain, k_gain, seq_len, *, tm=512):
    T, D = x.shape
    n_out = w_qkv.shape[1]
    nk = N_KV_HEADS * HEAD_DIM
    nq = n_out - 2 * nk
    tn = nk
    tm = _tile(seq_len, tm)
    n_rope = (nq + nk) // tn
    gains = jnp.concatenate([jnp.tile((q_gain * SCORE_SCALE)[None, None, :], (nq // tn, 1, 1)),
                             k_gain[None, None, :],
                             jnp.ones((1, 1, HEAD_DIM), F32)], axis=0)
    cos, sin = rope_tables(seq_len)
    spt = seq_len // tm
    kern = functools.partial(_qkv_kernel, n_rope_tiles=n_rope)
    return pl.pallas_call(
        kern,
        grid=(T // tm, n_out // tn),
        in_specs=[pl.BlockSpec((tm, D), lambda i, j: (i, 0)),
                  pl.BlockSpec((1, D), lambda i, j: (0, 0)),
                  pl.BlockSpec((D, tn), lambda i, j: (0, j)),
                  pl.BlockSpec((None, 1, HEAD_DIM), lambda i, j: (j, 0, 0)),
                  pl.BlockSpec((tm, HEAD_DIM), lambda i, j: (i % spt, 0)),
                  pl.BlockSpec((tm, HEAD_DIM), lambda i, j: (i % spt, 0))],
        out_specs=[pl.BlockSpec((tm, tn), lambda i, j: (i, j)),
                   pl.BlockSpec((tm, 2 * nk), lambda i, j: (i, 0)),
                   pl.BlockSpec((None, N_KV_HEADS, VT_ROWS, tm), lambda i, j: (i // spt, 0, 0, i % spt))],
        out_shape=[jax.ShapeDtypeStruct((T, n_out), BF16), jax.ShapeDtypeStruct((T, 2 * nk), BF16),
                   jax.ShapeDtypeStruct((T // seq_len, N_KV_HEADS, VT_ROWS, seq_len), BF16)],
        scratch_shapes=[pltpu.VMEM((tm, D), BF16)],
        compiler_params=_params(("parallel", "arbitrary")),
        name="qkv_project",
    )(x, g, w_qkv, gains, cos, sin)


def _flash_kernel(q_ref, k_ref, v_ref, o_ref, qs_ref, s0_ref, s1_ref, p0_ref, p1_ref, a0_ref, a1_ref,
                  m_ref, acc_ref, *, group, tq, tk, nk):
    s_refs = (s0_ref, s1_ref)
    p_refs = (p0_ref, p1_ref)
    a_refs = (a0_ref, a1_ref)
    for gq in range(group):
        qs_ref[gq * tq:(gq + 1) * tq, :] = q_ref[:, gq * HEAD_DIM:(gq + 1) * HEAD_DIM]
    m_ref[...] = jnp.full_like(m_ref, -jnp.inf)
    acc_ref[...] = jnp.zeros_like(acc_ref)

    def rows_of(j):
        if isinstance(j, int):
            return pl.ds(j * tk, tk)
        return pl.ds(pl.multiple_of(j * tk, tk), tk)

    def scores(j, slot):
        kb = k_ref[rows_of(j), :]
        s_refs[slot][...] = lax.dot_general(qs_ref[...], kb, (((1,), (1,)), ((), ())),
                                            preferred_element_type=F32)

    def softmax(slot):
        s = s_refs[slot][...]
        m_prev = m_ref[...]
        m_new = jnp.maximum(m_prev, jnp.max(s, axis=-1, keepdims=True))
        a_refs[slot][...] = jnp.exp2(m_prev - m_new)
        p_refs[slot][...] = jnp.exp2(s - jnp.tile(m_new, (1, tk // LANES))).astype(BF16)
        m_ref[...] = m_new

    def accumulate(j, slot):
        pv = _dot(p_refs[slot][...], v_ref[rows_of(j), :])
        acc_ref[...] = acc_ref[...] * jnp.tile(a_refs[slot][...], (1, 2)) + pv

    def step(j, slot, first, last):
        if not last:
            scores(j + 1, 1 - slot)
        softmax(slot)
        if not first:
            accumulate(j - 1, 1 - slot)

    scores(0, 0)
    if nk >= 4 and nk % 2 == 0:
        step(0, 0, True, False)

        def pair(t, c):
            j = 2 * t + 1
            step(j, 1, False, False)
            step(j + 1, 0, False, False)
            return c

        lax.fori_loop(0, (nk - 2) // 2, pair, 0)
        step(nk - 1, 1, False, True)
    else:
        for j in range(nk):
            step(j, j % 2, j == 0, j == nk - 1)
    accumulate(nk - 1, (nk - 1) % 2)

    acc = acc_ref[...]
    o = acc[:, :HEAD_DIM] / acc[:, HEAD_DIM:]
    for gq in range(group):
        o_ref[:, gq * HEAD_DIM:(gq + 1) * HEAD_DIM] = o[gq * tq:(gq + 1) * tq].astype(o_ref.dtype)


def _flash_bounded_kernel(q_ref, k_ref, vt_ref, o_ref, qt_ref, acc_ref, *, group, tq, tk, nk, unroll):
    for gq in range(group):
        qt_ref[:, gq * tq:(gq + 1) * tq] = q_ref[:, gq * HEAD_DIM:(gq + 1) * HEAD_DIM].astype(F32).T.astype(BF16)
    acc_ref[...] = jnp.zeros_like(acc_ref)

    def block(j, c):
        rows = pl.ds(pl.multiple_of(j * tk, tk), tk)
        s_t = _dot(k_ref[rows, :], qt_ref[...])
        acc_ref[...] += _dot(vt_ref[:, rows], jnp.exp2(s_t).astype(BF16))
        return c

    lax.fori_loop(0, nk, block, 0, unroll=unroll)
    acc = acc_ref[...]
    o_t = acc[:HEAD_DIM] / acc[HEAD_DIM:HEAD_DIM + 1]
    for gq in range(group):
        o_ref[:, gq * HEAD_DIM:(gq + 1) * HEAD_DIM] = o_t[:, gq * tq:(gq + 1) * tq].T.astype(o_ref.dtype)


SAFE_SCORE_BOUND = 80.0


def flash_attention(qkv, v_aug, v_t, score_bound, B, L, n_heads, *, tq=256, tk=512):
    group = n_heads // N_KV_HEADS
    tq = _tile(L, tq)
    tk = _tile(L, tk)
    nk = L // tk
    qw = group * HEAD_DIM
    k_col = n_heads
    rows = group * tq
    q_spec = pl.BlockSpec((None, tq, qw), lambda b, h, qi: (b, qi, h))
    k_spec = pl.BlockSpec((None, L, HEAD_DIM), lambda b, h, qi: (b, 0, k_col + h))
    common = dict(
        grid=(B, N_KV_HEADS, L // tq),
        out_specs=pl.BlockSpec((None, tq, qw), lambda b, h, qi: (b, qi, h)),
        out_shape=jax.ShapeDtypeStruct((B, L, n_heads * HEAD_DIM), BF16),
        compiler_params=_params(("parallel", "parallel", "parallel")),
    )
    general = pl.pallas_call(
        functools.partial(_flash_kernel, group=group, tq=tq, tk=tk, nk=nk),
        in_specs=[q_spec, k_spec, pl.BlockSpec((None, L, 2 * HEAD_DIM), lambda b, h, qi: (b, 0, h))],
        scratch_shapes=[pltpu.VMEM((rows, HEAD_DIM), BF16),
                        pltpu.VMEM((rows, tk), F32), pltpu.VMEM((rows, tk), F32),
                        pltpu.VMEM((rows, tk), BF16), pltpu.VMEM((rows, tk), BF16),
                        pltpu.VMEM((rows, LANES), F32), pltpu.VMEM((rows, LANES), F32),
                        pltpu.VMEM((rows, LANES), F32),
                        pltpu.VMEM((rows, 2 * HEAD_DIM), F32)],
        name="flash_attention", **common)
    bounded = pl.pallas_call(
        functools.partial(_flash_bounded_kernel, group=group, tq=tq, tk=tk, nk=nk,
                          unroll=math.gcd(nk, 8)),
        in_specs=[q_spec, k_spec, pl.BlockSpec((None, None, VT_ROWS, L), lambda b, h, qi: (b, h, 0, 0))],
        scratch_shapes=[pltpu.VMEM((HEAD_DIM, rows), BF16),
                        pltpu.VMEM((VT_ROWS, rows), F32)],
        name="flash_attention_bounded", **common)
    return lax.cond(score_bound <= SAFE_SCORE_BOUND,
                    lambda ops: bounded(ops[0], ops[0], ops[2]),
                    lambda ops: general(ops[0], ops[0], ops[1]),
                    (qkv, v_aug, v_t))


def _pack_bf16_pairs(x):
    half = x.shape[1] // 2
    bits = lax.bitcast_convert_type(x.astype(BF16).astype(F32), jnp.uint32)
    return (bits[:, :half] >> 16) | (bits[:, half:] & jnp.uint32(0xFFFF0000))


def _unpack_bf16_pairs(p):
    lo = lax.bitcast_convert_type(p << 16, F32).astype(BF16)
    hi = lax.bitcast_convert_type(p & jnp.uint32(0xFFFF0000), F32).astype(BF16)
    return lo, hi


def _router_kernel(xa_ref, xb_ref, g_ref, rw_ref, h_ref, info_ref, cnt_ref, *, n_a_tiles):
    i = pl.program_id(0)

    def route(x_ref):
        hf = _norm_rows(x_ref[...], g_ref[...])
        h_ref[...] = _pack_bf16_pairs(hf)
        h_hi = hf.astype(BF16)
        h_lo = (hf - h_hi.astype(F32)).astype(BF16)
        rw = rw_ref[...]
        hw = _dot(h_hi, rw)
        logits = hw[:, :LANES] + hw[:, LANES:] + _dot(h_lo, rw[:, :LANES])
        shape = logits.shape
        lane = lax.broadcasted_iota(jnp.int32, shape, 1)
        lanef = lane.astype(F32)
        neg = -jnp.inf
        big = float(LANES)
        is_g = lane < N_GROUPS
        gl = jnp.where(is_g, logits, neg)
        gmax = jnp.max(gl, axis=-1, keepdims=True)
        gidx = jnp.min(jnp.where(gl == gmax, lanef, big), axis=-1, keepdims=True)
        p_g = 1.0 / jnp.sum(jnp.where(is_g, jnp.exp(logits - gmax), 0.0), axis=-1, keepdims=True)
        lo = N_GROUPS + gidx * EXPERTS_PER_GROUP
        el = jnp.where(lanef >= lo, jnp.where(lanef < lo + EXPERTS_PER_GROUP, logits, neg), neg)
        v1 = jnp.max(el, axis=-1, keepdims=True)
        i1 = jnp.min(jnp.where(el == v1, lanef, big), axis=-1, keepdims=True)
        el2 = jnp.where(lanef == i1, neg, el)
        v2 = jnp.max(el2, axis=-1, keepdims=True)
        i2 = jnp.min(jnp.where(el2 == v2, lanef, big), axis=-1, keepdims=True)
        t = jnp.exp(v2 - v1)
        w1 = p_g / (1.0 + t)
        w2 = p_g * t / (1.0 + t)
        e1 = i1 - N_GROUPS
        e2 = i2 - N_GROUPS
        oh1 = jnp.where(lanef == e1, 1.0, 0.0)
        oh2 = jnp.where(lanef == e2, 1.0, 0.0)
        oh = oh1 + oh2
        tm = shape[0]
        tri = jnp.where(lax.broadcasted_iota(jnp.int32, (tm, tm), 1) < lax.broadcasted_iota(jnp.int32, (tm, tm), 0),
                        1.0, 0.0).astype(BF16)
        before = _dot(tri, oh.astype(BF16)) + cnt_ref[...]
        r1 = jnp.sum(before * oh1, axis=-1, keepdims=True)
        r2 = jnp.sum(before * oh2, axis=-1, keepdims=True)
        cnt_ref[...] += jnp.sum(oh, axis=0, keepdims=True)
        vals = (e1, e2, w1, w2, r1, r2)
        info = jnp.zeros(shape, F32)
        for idx, val in enumerate(vals):
            info = jnp.where(lane == idx, val, info)
        info_ref[...] = info

    @pl.when(i == 0)
    def _():
        cnt_ref[...] = jnp.zeros_like(cnt_ref)

    @pl.when(i < n_a_tiles)
    def _():
        route(xa_ref)

    @pl.when(i >= n_a_tiles)
    def _():
        route(xb_ref)


def moe_route(xa, xb, g, router_group, router_expert, *, tm=512):
    Ta, D = xa.shape
    Tb = xb.shape[0]
    tm = _tile(math.gcd(Ta, Tb), tm)
    na, nb = Ta // tm, Tb // tm
    n_e = N_GROUPS * EXPERTS_PER_GROUP
    rw = jnp.concatenate([router_group, jnp.transpose(router_expert, (1, 0, 2)).reshape(D, n_e)], axis=1)
    rw = jnp.pad(rw, ((0, 0), (0, LANES - rw.shape[1])))
    rw_hi = rw.astype(BF16)
    rw = jnp.concatenate([rw_hi, (rw - rw_hi.astype(F32)).astype(BF16)], axis=1)
    kern = functools.partial(_router_kernel, n_a_tiles=na)
    return pl.pallas_call(
        kern,
        grid=(na + nb,),
        in_specs=[pl.BlockSpec((tm, D), lambda i: (jnp.minimum(i, na - 1), 0)),
                  pl.BlockSpec((tm, D), lambda i: (jnp.maximum(i - na, 0), 0)),
                  pl.BlockSpec((1, D), lambda i: (0, 0)),
                  pl.BlockSpec((D, 2 * LANES), lambda i: (0, 0))],
        out_specs=[pl.BlockSpec((tm, D // 2), lambda i: (i, 0)),
                   pl.BlockSpec((tm, LANES), lambda i: (i, 0)),
                   pl.BlockSpec((1, LANES), lambda i: (0, 0))],
        out_shape=[jax.ShapeDtypeStruct((Ta + Tb, D // 2), jnp.uint32),
                   jax.ShapeDtypeStruct((Ta + Tb, LANES), F32),
                   jax.ShapeDtypeStruct((1, LANES), F32)],
        compiler_params=_params(("arbitrary",)),
        name="moe_route",
    )(xa, xb, g, rw)


def moe_plan(info, cnt, tm, n_experts):
    T = info.shape[0]
    n_tiles = (2 * T) // tm + n_experts
    counts = cnt[0, :n_experts].astype(jnp.int32)
    padded = ((counts + tm - 1) // tm) * tm
    ends = jnp.cumsum(padded)
    off = ends - padded
    e = info[:, 0:2].astype(jnp.int32)
    rank = info[:, 4:6].astype(jnp.int32)
    pos = (off[e] + rank).reshape(-1)
    tile_start = jnp.arange(n_tiles, dtype=jnp.int32) * tm
    tile_e = jnp.sum((tile_start[:, None] >= ends[None, :]).astype(jnp.int32), axis=1)
    tile_e = jnp.minimum(tile_e, n_experts - 1)
    tile_valid = jnp.clip(counts[tile_e] - (tile_start - off[tile_e]), 0, tm).astype(jnp.int32)
    n_used = (ends[-1:] // tm).astype(jnp.int32)
    expert_last = jnp.where(padded > 0, ends - tm, -1)
    tail_tile = n_used[0] + jnp.arange(n_experts, dtype=jnp.int32)
    tail = jnp.where(tail_tile < n_tiles, tail_tile * tm, -1)
    zero_tiles = jnp.concatenate([expert_last, tail]).astype(jnp.int32)[None, :]
    return pos, tile_e, tile_valid, n_used, zero_tiles


DMA_UNROLL = 8


def _dispatch_kernel(pos_ref, ztile_ref, h_ref, hs_hbm, zbuf, sem, zsem):
    tm = h_ref.shape[0]

    @pl.when(pl.program_id(0) == 0)
    def _():
        zbuf[...] = jnp.zeros_like(zbuf)
        zrows = zbuf.shape[0]

        def zero_copy(k):
            return pltpu.make_async_copy(zbuf, hs_hbm.at[pl.ds(pl.multiple_of(ztile_ref[0, k], zrows), zrows)], zsem)

        for k in range(ztile_ref.shape[1]):
            @pl.when(ztile_ref[0, k] >= 0)
            def _():
                zero_copy(k).start()
        for k in range(ztile_ref.shape[1]):
            @pl.when(ztile_ref[0, k] >= 0)
            def _():
                zero_copy(k).wait()

    def row_copy(r, slot):
        return pltpu.make_async_copy(h_ref.at[pl.ds(r, 1)], hs_hbm.at[pl.ds(pos_ref[0, 0, 2 * r + slot], 1)], sem)

    def start(r, c):
        row_copy(r, 0).start()
        row_copy(r, 1).start()
        return c

    def wait(r, c):
        row_copy(r, 0).wait()
        row_copy(r, 1).wait()
        return c

    lax.fori_loop(0, tm, start, 0, unroll=DMA_UNROLL)
    lax.fori_loop(0, tm, wait, 0, unroll=DMA_UNROLL)


def moe_dispatch(h, pos, zero_tiles, n_sorted_rows, tile_rows, *, tm=512):
    T, C = h.shape
    tm = _tile(T, tm)
    return pl.pallas_call(
        _dispatch_kernel,
        grid=(T // tm,),
        in_specs=[pl.BlockSpec((1, 1, 2 * tm), lambda i: (i, 0, 0), memory_space=pltpu.SMEM),
                  pl.BlockSpec(zero_tiles.shape, lambda i: (0, 0), memory_space=pltpu.SMEM),
                  pl.BlockSpec((tm, C), lambda i: (i, 0))],
        out_specs=pl.BlockSpec(memory_space=pl.ANY),
        out_shape=jax.ShapeDtypeStruct((n_sorted_rows, C), h.dtype),
        scratch_shapes=[pltpu.VMEM((tile_rows, C), h.dtype), pltpu.SemaphoreType.DMA, pltpu.SemaphoreType.DMA],
        compiler_params=_params(("arbitrary",)),
        name="moe_dispatch",
    )(pos.reshape(T // tm, 1, 2 * tm), zero_tiles, h)


def _expert_kernel(te_ref, tv_ref, nu_ref, x_ref, wg_ref, wu_ref, wd_ref, y_ref):
    del te_ref, nu_ref
    valid = tv_ref[pl.program_id(0)]

    @pl.when(valid == 0)
    def _():
        y_ref[...] = jnp.zeros_like(y_ref)

    @pl.when(valid > 0)
    def _():
        x_lo, x_hi = _unpack_bf16_pairs(x_ref[...])
        half = x_ref.shape[1]
        gate = (_dot(x_lo, wg_ref[:half, :].astype(BF16)) + _dot(x_hi, wg_ref[half:, :].astype(BF16)))
        up = (_dot(x_lo, wu_ref[:half, :].astype(BF16)) + _dot(x_hi, wu_ref[half:, :].astype(BF16)))
        a = (gate * jax.nn.sigmoid(gate) * up).astype(BF16)
        y_ref[...] = _dot(a, wd_ref[...].astype(BF16))


def moe_experts(hs, tile_e, tile_valid, n_used, w_gate, w_up, w_down, layer, tm):
    P = hs.shape[0]
    D = w_gate.shape[2]
    d_e = w_gate.shape[-1]

    def row_map(i, te, tv, nu):
        return (jnp.minimum(i, nu[0] - 1), 0)

    def out_map(i, te, tv, nu):
        return (i, 0)

    def w_map(i, te, tv, nu):
        return (layer, te[i], 0, 0)

    grid_spec = pltpu.PrefetchScalarGridSpec(
        num_scalar_prefetch=3,
        grid=(P // tm,),
        in_specs=[pl.BlockSpec((tm, D // 2), row_map),
                  pl.BlockSpec((None, None, D, d_e), w_map),
                  pl.BlockSpec((None, None, D, d_e), w_map),
                  pl.BlockSpec((None, None, d_e, D), w_map)],
        out_specs=pl.BlockSpec((tm, D), out_map),
    )
    return pl.pallas_call(
        _expert_kernel,
        grid_spec=grid_spec,
        out_shape=jax.ShapeDtypeStruct((P, D), F32),
        compiler_params=_params(("arbitrary",)),
        name="moe_experts",
    )(tile_e, tile_valid, n_used, hs, w_gate, w_up, w_down)


def _combine_kernel(pos_ref, posn_ref, x_ref, info_ref, g_ref, y_hbm, o_ref, ybuf, sem, *, final_norm):
    tm = x_ref.shape[0]
    i = pl.program_id(0)
    buf = i % 2

    def row_copy(p_ref, b, r, slot):
        return pltpu.make_async_copy(y_hbm.at[pl.ds(p_ref[0, 0, 2 * r + slot], 1)],
                                     ybuf.at[b, slot, pl.ds(r, 1)], sem.at[b])

    def start_tile(p_ref, b):
        def body(r, c):
            row_copy(p_ref, b, r, 0).start()
            row_copy(p_ref, b, r, 1).start()
            return c
        lax.fori_loop(0, tm, body, 0, unroll=DMA_UNROLL)

    @pl.when(i == 0)
    def _():
        start_tile(pos_ref, 0)

    @pl.when(i + 1 < pl.num_programs(0))
    def _():
        start_tile(posn_ref, 1 - buf)

    def wait(r, c):
        row_copy(pos_ref, buf, r, 0).wait()
        row_copy(pos_ref, buf, r, 1).wait()
        return c

    lax.fori_loop(0, tm, wait, 0, unroll=DMA_UNROLL)
    info = info_ref[...]
    y = info[:, 2:3] * ybuf[buf, 0] + info[:, 3:4] * ybuf[buf, 1]
    x = x_ref[...] + y
    if final_norm:
        x = _norm_rows(x, g_ref[...])
    o_ref[...] = x


def moe_combine(x, y, pos, info, row_off, g_final, final_norm, *, tm=256):
    Tg, D = x.shape
    T = info.shape[0]
    tm = _tile(math.gcd(Tg, row_off) if row_off else Tg, tm)
    ob = row_off // tm
    kern = functools.partial(_combine_kernel, final_norm=final_norm)
    n = Tg // tm
    pos3 = pos.reshape(T // tm, 1, 2 * tm)
    return pl.pallas_call(
        kern,
        grid=(n,),
        in_specs=[pl.BlockSpec((1, 1, 2 * tm), lambda i: (i + ob, 0, 0), memory_space=pltpu.SMEM),
                  pl.BlockSpec((1, 1, 2 * tm), lambda i: (jnp.minimum(i + 1, n - 1) + ob, 0, 0),
                               memory_space=pltpu.SMEM),
                  pl.BlockSpec((tm, D), lambda i: (i, 0)),
                  pl.BlockSpec((tm, LANES), lambda i: (i + ob, 0)),
                  pl.BlockSpec((1, D), lambda i: (0, 0)),
                  pl.BlockSpec(memory_space=pl.ANY)],
        out_specs=pl.BlockSpec((tm, D), lambda i: (i, 0)),
        out_shape=jax.ShapeDtypeStruct((Tg, D), F32),
        scratch_shapes=[pltpu.VMEM((2, 2, tm, D), F32), pltpu.SemaphoreType.DMA((2,))],
        compiler_params=_params(("arbitrary",)),
        name="moe_combine",
    )(pos3, pos3, x, info, g_final, y)


MOE_TILE = 512


def hier_moe_block(xs, g, router_group, router_expert, w_gate, w_up, w_down, layer, g_final, final_norm):
    xa, xb = xs
    n_experts = w_gate.shape[1]
    h, info, cnt = moe_route(xa, xb, g, router_group, router_expert)
    T = h.shape[0]
    tm = _tile(2 * T, MOE_TILE)
    pos, tile_e, tile_valid, n_used, zero_tiles = moe_plan(info, cnt, tm, n_experts)
    hs = moe_dispatch(h, pos, zero_tiles, tile_e.shape[0] * tm, tm)
    y = moe_experts(hs, tile_e, tile_valid, n_used, w_gate, w_up, w_down, layer, tm)
    oa = moe_combine(xa, y, pos, info, 0, g_final, final_norm)
    ob = moe_combine(xb, y, pos, info, xa.shape[0], g_final, final_norm)
    return oa, ob


def hyena_block(x, L, g, w_in, conv_w, conv_b, f_w1, f_b1, f_win, f_bin, f_freq, f_w3, skip_b, w_out):
    D = x.shape[1]
    x0, zv = hyena_inproj(x, g, w_in, conv_w, conv_b[None, :], L)
    filt = hyena_filter_features(L, w_out.shape[0], f_w1, f_b1, f_win, f_bin, f_freq)
    return hyena_long_conv(zv, x0, x, w_out, filt, f_w3, skip_b[None, :], L)


def attention_block(x, B, L, g, w_qkv, q_gain, k_gain, w_o):
    D = x.shape[1]
    n_heads = w_o.shape[0] // HEAD_DIM
    qkv, v_aug, v_t = qkv_project(x, g, w_qkv, q_gain, k_gain, L)
    score_bound = 1.02 * HEAD_DIM * SCORE_SCALE * jnp.max(jnp.abs(q_gain)) * jnp.max(jnp.abs(k_gain))
    o = flash_attention(qkv.reshape(B, L, qkv.shape[1]), v_aug.reshape(B, L, v_aug.shape[1]), v_t, score_bound,
                        B, L, n_heads)
    return matmul_residual(o.reshape(B * L, n_heads * HEAD_DIM), w_o, x)


def kernel(x_prompt, x_sample, norm_mix, norm_ffn, norm_final, hy_w_in, hy_conv_w, hy_conv_b, hy_f_w1, hy_f_b1, hy_f_win, hy_f_bin, hy_f_freq, hy_f_w3, hy_skip_b, hy_w_out, at_w_qkv, at_q_gain, at_k_gain, at_w_o, moe_router_group, moe_router_expert, moe_w_gate, moe_w_up, moe_w_down):
    D = x_prompt.shape[-1]
    depth = norm_mix.shape[0]
    shapes = [x_prompt.shape, x_sample.shape]
    for s in shapes:
        assert s[0] == 2, "the long convolution packs exactly two sequences per group"
    xs = [x_prompt.reshape(-1, D), x_sample.reshape(-1, D)]
    hy_w_in_b = hy_w_in.astype(BF16)
    hy_w_out_b = hy_w_out.astype(BF16)
    at_w_qkv_b = at_w_qkv.astype(BF16)
    at_w_o_b = at_w_o.astype(BF16)
    g_final = norm_final[None, :]
    for i in range(depth):
        j = i // N_MIXERS
        g = norm_mix[i][None, :]
        if i % N_MIXERS == 0:
            xs = [hyena_block(x, s[1], g, hy_w_in_b[j], hy_conv_w[j], hy_conv_b[j], hy_f_w1[j], hy_f_b1[j],
                              hy_f_win[j], hy_f_bin[j], hy_f_freq[j], hy_f_w3[j], hy_skip_b[j], hy_w_out_b[j])
                  for x, s in zip(xs, shapes)]
        else:
            xs = [attention_block(x, s[0], s[1], g, at_w_qkv_b[j], at_q_gain[j], at_k_gain[j], at_w_o_b[j])
                  for x, s in zip(xs, shapes)]
        xs = hier_moe_block(xs, norm_ffn[i][None, :], moe_router_group[i], moe_router_expert[i],
                            moe_w_gate, moe_w_up, moe_w_down, i, g_final, i == depth - 1)
    return (xs[0].reshape(shapes[0]), xs[1].reshape(shapes[1]))
```

```python
import functools
import math

import jax
import jax.numpy as jnp
import numpy as np
from jax import lax
from jax.experimental import pallas as pl
from jax.experimental.pallas import tpu as pltpu

RMS_EPS = 1e-6
GRID_W = 64
EMB_DIM = 33
DECAY_TARGET = 1e-2
FAST_DECAY_PCT = 0.3
SLOW_DECAY_PCT = 1.5
MIN_DECAY = math.log(DECAY_TARGET) / FAST_DECAY_PCT
MAX_DECAY = math.log(DECAY_TARGET) / SLOW_DECAY_PCT
HEAD_DIM = 128
SCORE_SCALE = HEAD_DIM ** -0.5 * math.log2(math.e)
N_KV_HEADS = 4
ROPE_THETA = 10000.0
N_GROUPS = 4
EXPERTS_PER_GROUP = 8
N_MIXERS = 2

LANES = 128
BF16_SUBLANES = 16
VMEM_LIMIT_BYTES = 52 * 1024 * 1024
DFT_N2 = 128
MXU_ROWS = 256

BF16 = jnp.bfloat16
F32 = jnp.float32


def _params(sem):
    return pltpu.CompilerParams(dimension_semantics=sem, vmem_limit_bytes=VMEM_LIMIT_BYTES)


def _dot(a, b):
    return jnp.dot(a, b, preferred_element_type=F32)


def _dot_3pass(a, b):
    a_hi = a.astype(BF16)
    a_lo = (a - a_hi.astype(F32)).astype(BF16)
    b_hi = b.astype(BF16)
    b_lo = (b - b_hi.astype(F32)).astype(BF16)
    return _dot(a_hi, b_hi) + _dot(a_hi, b_lo) + _dot(a_lo, b_hi)


def _norm_rows(x, g):
    ms = jnp.mean(x * x, axis=-1, keepdims=True)
    return x * lax.rsqrt(ms + RMS_EPS) * g


def _tile(n, pref):
    return pref if n % pref == 0 else n


def _matmul_res_kernel(a_ref, w_ref, r_ref, o_ref):
    o_ref[...] = r_ref[...] + _dot(a_ref[...].astype(BF16), w_ref[...])


def matmul_residual(a, w, res, *, tm=1024, tn=512):
    T, K = a.shape
    N = w.shape[1]
    tm = _tile(T, tm)
    tn = _tile(N, tn)
    return pl.pallas_call(
        _matmul_res_kernel,
        grid=(T // tm, N // tn),
        in_specs=[pl.BlockSpec((tm, K), lambda i, j: (i, 0)),
                  pl.BlockSpec((K, tn), lambda i, j: (0, j)),
                  pl.BlockSpec((tm, tn), lambda i, j: (i, j))],
        out_specs=pl.BlockSpec((tm, tn), lambda i, j: (i, j)),
        out_shape=jax.ShapeDtypeStruct((T, N), F32),
        compiler_params=_params(("parallel", "arbitrary")),
        name="matmul_residual",
    )(a, w, res)


HALO = BF16_SUBLANES


def _hy_inproj_kernel(x_ref, xp_ref, xn_ref, g_ref, w0_ref, w1_ref, w2_ref,
                      cw0_ref, cw1_ref, cw2_ref, cb0_ref, cb1_ref, cb2_ref,
                      x0_out, zv_out, hn_ref, *, tm, tiles_per_seq):
    i = pl.program_id(0)
    j = pl.program_id(1)

    @pl.when(j == 0)
    def _():
        g = g_ref[...]
        hn_ref[HALO:HALO + tm, :] = _norm_rows(x_ref[...], g).astype(BF16)
        first = (i % tiles_per_seq) == 0
        last = (i % tiles_per_seq) == tiles_per_seq - 1
        hp = _norm_rows(xp_ref[...], g)
        hn_ref[0:HALO, :] = jnp.where(first, 0.0, hp).astype(BF16)
        hx = _norm_rows(xn_ref[...], g)
        hn_ref[HALO + tm:, :] = jnp.where(last, 0.0, hx).astype(BF16)

    h = hn_ref[...]
    rows = tm + 2 * HALO

    def conv(w_ref, cw_ref, cb_ref):
        u = _dot(h, w_ref[...])
        um = pltpu.roll(u, 1, 0)[HALO:HALO + tm]
        up = pltpu.roll(u, rows - 1, 0)[HALO:HALO + tm]
        uc = u[HALO:HALO + tm]
        cw = cw_ref[...]
        return um * cw[0:1] + uc * cw[1:2] + up * cw[2:3] + cb_ref[...]

    x0_out[...] = conv(w0_ref, cw0_ref, cb0_ref)
    x1 = conv(w1_ref, cw1_ref, cb1_ref)
    v = conv(w2_ref, cw2_ref, cb2_ref)
    zv_out[...] = v * x1


def hyena_inproj(x, g, w_in, conv_w, conv_b, seq_len, *, tm=512, tn=512):
    T, D = x.shape
    W = w_in.shape[1] // 3
    tm = _tile(seq_len, tm)
    tn = _tile(W, tn)
    nj = W // tn
    hb = tm // HALO
    n_halo_blocks = T // HALO
    kern = functools.partial(_hy_inproj_kernel, tm=tm, tiles_per_seq=seq_len // tm)

    def wspec(c):
        return pl.BlockSpec((D, tn), lambda i, j, c=c: (0, j + c * nj))

    def cwspec(c):
        return pl.BlockSpec((3, tn), lambda i, j, c=c: (0, j + c * nj))

    def cbspec(c):
        return pl.BlockSpec((1, tn), lambda i, j, c=c: (0, j + c * nj))

    return pl.pallas_call(
        kern,
        grid=(T // tm, nj),
        in_specs=[pl.BlockSpec((tm, D), lambda i, j: (i, 0)),
                  pl.BlockSpec((HALO, D), lambda i, j: (jnp.maximum(i * hb - 1, 0), 0)),
                  pl.BlockSpec((HALO, D), lambda i, j: (jnp.minimum((i + 1) * hb, n_halo_blocks - 1), 0)),
                  pl.BlockSpec((1, D), lambda i, j: (0, 0)),
                  wspec(0), wspec(1), wspec(2),
                  cwspec(0), cwspec(1), cwspec(2),
                  cbspec(0), cbspec(1), cbspec(2)],
        out_specs=[pl.BlockSpec((tm, tn), lambda i, j: (i, j)),
                   pl.BlockSpec((tm, tn), lambda i, j: (i, j))],
        out_shape=[jax.ShapeDtypeStruct((T, W), F32), jax.ShapeDtypeStruct((T, W), F32)],
        scratch_shapes=[pltpu.VMEM((tm + 2 * HALO, D), BF16)],
        compiler_params=_params(("parallel", "arbitrary")),
        name="hyena_inproj",
    )(x, x, x, g, w_in, w_in, w_in, conv_w, conv_w, conv_w, conv_b, conv_b, conv_b)


def _filter_mlp_kernel(z_ref, w1_ref, b1_ref, win_ref, bin_ref, fr_ref, h_ref):
    hp = lax.Precision.HIGHEST
    fr = fr_ref[...]
    h = jnp.sin(fr * (jnp.dot(z_ref[...], w1_ref[...], precision=hp, preferred_element_type=F32) + b1_ref[...]))
    for l in range(win_ref.shape[0]):
        h = jnp.sin(fr * (jnp.dot(h, win_ref[l], precision=hp, preferred_element_type=F32) + bin_ref[l]))
    h_ref[...] = h


def hyena_filter_features(L, D, f_w1, f_b1, f_win, f_bin, f_freq):
    H = f_w1.shape[1]
    t = jnp.linspace(0.0, 1.0, L, dtype=F32)[:, None]
    bands = (EMB_DIM - 1) // 2
    w = 2.0 * math.pi * jnp.arange(L, dtype=F32)[:, None] / L
    fb = jnp.linspace(1e-4, bands - 1, bands, dtype=F32)[None, :]
    z = jnp.concatenate([t, jnp.cos(fb * w), -jnp.sin(fb * w)], axis=-1)

    def lag_major_to_dft_order(a):
        return a.reshape(L // DFT_N2, DFT_N2, a.shape[1]).transpose(1, 0, 2).reshape(L, a.shape[1])

    z = lag_major_to_dft_order(z)
    t = lag_major_to_dft_order(t)
    zp = jnp.pad(z, ((0, 0), (0, LANES - EMB_DIM)))
    w1p = jnp.pad(f_w1, ((0, LANES - EMB_DIM), (0, 0)))
    n_in = f_win.shape[0]
    tl = _tile(L, 512)
    h = pl.pallas_call(
        _filter_mlp_kernel,
        grid=(L // tl,),
        in_specs=[pl.BlockSpec((tl, LANES), lambda i: (i, 0)),
                  pl.BlockSpec((LANES, H), lambda i: (0, 0)),
                  pl.BlockSpec((1, H), lambda i: (0, 0)),
                  pl.BlockSpec((n_in, H, H), lambda i: (0, 0, 0)),
                  pl.BlockSpec((n_in, 1, H), lambda i: (0, 0, 0)),
                  pl.BlockSpec((1, H), lambda i: (0, 0))],
        out_specs=pl.BlockSpec((tl, H), lambda i: (i, 0)),
        out_shape=jax.ShapeDtypeStruct((L, H), F32),
        compiler_params=_params(("parallel",)),
        name="filter_mlp",
    )(zp, w1p, f_b1[None, :], f_win, f_bin[:, None, :], f_freq[None, :])

    deltas = jnp.abs(jnp.linspace(MIN_DECAY, MAX_DECAY, D, dtype=F32))[None, :]
    return h, t, jnp.concatenate([deltas, deltas], axis=1)


def dft_tables(N1, N2):
    N = N1 * N2
    half = N1 // 2
    k1 = jnp.arange(N1, dtype=jnp.int32)
    ang_a = ((k1[:, None] * jnp.arange(half, dtype=jnp.int32)[None, :]) % N1).astype(F32) * (2.0 * math.pi / N1)
    ang_b = ((k1[None, :] * jnp.arange(N2, dtype=jnp.int32)[:, None]) % N).astype(F32) * (2.0 * math.pi / N)
    ca, sa = jnp.cos(ang_a)[None, :, :], jnp.sin(ang_a)[None, :, :]
    cb, sb = jnp.cos(ang_b)[:, :, None], jnp.sin(ang_b)[:, :, None]
    c = ca * cb - sa * sb
    s = sa * cb + ca * sb
    fwd = jnp.concatenate([jnp.concatenate([c, s], axis=2),
                           jnp.concatenate([-s, c], axis=2)], axis=1)
    inv = jnp.swapaxes(fwd, 1, 2)
    fwd_real = jnp.concatenate([c, -s], axis=1)
    k2 = jnp.arange(N2, dtype=jnp.int32)
    ang2 = ((k2[:, None] * k2[None, :]) % N2).astype(F32) * (2.0 * math.pi / N2)
    c2, s2 = jnp.cos(ang2), jnp.sin(ang2)
    f2 = jnp.concatenate([jnp.concatenate([c2, s2], axis=1),
                          jnp.concatenate([-s2, c2], axis=1)], axis=0)
    g2 = f2.T
    return fwd.astype(BF16), inv.astype(BF16), fwd_real.astype(BF16), f2.astype(BF16), g2.astype(BF16)


def _pack_complex(re, im):
    rb = lax.bitcast_convert_type(re.astype(BF16).astype(F32), jnp.uint32)
    ib = lax.bitcast_convert_type(im.astype(BF16).astype(F32), jnp.uint32)
    return (rb >> 16) | (ib & jnp.uint32(0xFFFF0000))


def _unpack_complex(p):
    re = lax.bitcast_convert_type(p << 16, F32)
    im = lax.bitcast_convert_type(p & jnp.uint32(0xFFFF0000), F32)
    return jnp.concatenate([re, im], axis=0).astype(BF16)


def _double_buffered(in_copies, out_copies, compute):
    j = pl.program_id(0)
    n = pl.num_programs(0)
    slot = j % 2

    @pl.when(j == 0)
    def _():
        for cp in in_copies(0, 0):
            cp.start()

    @pl.when(j + 1 < n)
    def _():
        for cp in in_copies(j + 1, 1 - slot):
            cp.start()

    for cp in in_copies(j, slot):
        cp.wait()

    @pl.when(j >= 2)
    def _():
        for cp in out_copies(j - 2, slot):
            cp.wait()

    compute(slot)
    for cp in out_copies(j, slot):
        cp.start()

    @pl.when(j == n - 1)
    def _():
        for cp in out_copies(j, slot):
            cp.wait()

        @pl.when(n >= 2)
        def _():
            for cp in out_copies(j - 1, 1 - slot):
                cp.wait()


def _fft_s1_kernel(z_hbm, m_ref, a_hbm, zbuf, obuf, sem_in, sem_out):
    n_g = zbuf.shape[1]
    half = zbuf.shape[2] // 2
    n1 = obuf.shape[2]

    def in_copies(step, slot):
        return [pltpu.make_async_copy(z_hbm.at[b, :, step * n_g + g, :], zbuf.at[slot, g, pl.ds(b * half, half)],
                                      sem_in.at[slot])
                for g in range(n_g) for b in range(2)]

    def out_copies(step, slot):
        return [pltpu.make_async_copy(obuf.at[slot, g], a_hbm.at[:, step * n_g + g, :], sem_out.at[slot])
                for g in range(n_g)]

    def compute(slot):
        for g in range(n_g):
            a = _dot(m_ref[g], zbuf[slot, g].astype(BF16))
            obuf[slot, g] = _pack_complex(a[:n1], a[n1:])

    _double_buffered(in_copies, out_copies, compute)


def _fft_s1_real_kernel(h_ref, t_ref, w3_ref, dl_ref, m_ref, a_hbm, s_ref, obuf, sem_out):
    n_g = obuf.shape[1]
    n1 = obuf.shape[2]
    j = pl.program_id(0)

    @pl.when(j == 0)
    def _():
        s_ref[...] = jnp.zeros_like(s_ref)

    def out_copies(step, slot):
        return [pltpu.make_async_copy(obuf.at[slot, g], a_hbm.at[:, step * n_g + g, :], sem_out.at[slot])
                for g in range(n_g)]

    def compute(slot):
        w3 = w3_ref[...]
        for g in range(n_g):
            taps = _dot_3pass(h_ref[g], w3) * jnp.exp(-t_ref[g] * dl_ref[...])
            if g == 0:
                row = lax.broadcasted_iota(jnp.int32, taps.shape, 0)
                col = lax.broadcasted_iota(jnp.int32, taps.shape, 1)
                drop = jnp.logical_and(jnp.logical_and(j == 0, row == 0), col >= taps.shape[1] // 2)
                taps = jnp.where(drop, 0.0, taps)
            s_ref[...] += jnp.sum(jnp.abs(taps), axis=0, keepdims=True)
            a = _dot(m_ref[g], taps.astype(BF16))
            obuf[slot, g] = _pack_complex(a[:n1], a[n1:])

    _double_buffered(lambda step, slot: [], out_copies, compute)


def _fft_mid_kernel(a_ref, af_ref, ab_ref, s_ref, f2_ref, g2_ref, c_ref):
    n2 = a_ref.shape[0]
    f2 = f2_ref[...]
    xf = _dot(f2, _unpack_complex(af_ref[...]))
    xb = _dot(f2, _unpack_complex(ab_ref[...]))
    s = s_ref[...]
    kr = (xf[:n2] + xb[:n2]) * s
    ki = (xf[n2:] - xb[n2:]) * s
    x = _dot(f2, _unpack_complex(a_ref[...]))
    xr, xi = x[:n2], x[n2:]
    y = jnp.concatenate([xr * kr - xi * ki, xr * ki + xi * kr], axis=0).astype(BF16)
    c = _dot(g2_ref[...], y)
    c_ref[...] = _pack_complex(c[:n2], c[n2:])


def _fft_out_kernel(c_hbm, m_ref, zv_hbm, x0_hbm, r_hbm, sk_ref, w_ref, o_hbm,
                    cbuf, zbuf, xbuf, rbuf, obuf, sem_in, sem_out):
    n_g = zbuf.shape[1]
    n1 = zbuf.shape[2]
    half = n1 // 2

    def in_copies(step, slot):
        cps = []
        for g in range(n_g):
            n2 = step * n_g + g
            cps.append(pltpu.make_async_copy(c_hbm.at[:, n2, :], cbuf.at[slot, g], sem_in.at[slot]))
            for b in range(2):
                rows = pl.ds(b * half, half)
                for src, dst in ((zv_hbm, zbuf), (x0_hbm, xbuf), (r_hbm, rbuf)):
                    cps.append(pltpu.make_async_copy(src.at[b, :, n2, :], dst.at[slot, g, rows], sem_in.at[slot]))
        return cps

    def out_copies(step, slot):
        return [pltpu.make_async_copy(obuf.at[slot, g, pl.ds(b * half, half)], o_hbm.at[b, :, step * n_g + g, :],
                                      sem_out.at[slot])
                for g in range(n_g) for b in range(2)]

    def compute(slot):
        ys = []
        for g in range(n_g):
            y = _dot(m_ref[g], _unpack_complex(cbuf[slot, g]))
            ys.append(((y + zbuf[slot, g] * sk_ref[...]) * xbuf[slot, g]).astype(BF16))
        proj = _dot(ys[0] if n_g == 1 else jnp.concatenate(ys, axis=0), w_ref[...])
        for g in range(n_g):
            obuf[slot, g] = rbuf[slot, g] + proj[g * n1:(g + 1) * n1]

    _double_buffered(in_copies, out_copies, compute)


def hyena_long_conv(zv, x0, x, w_out, filt, f_w3, skip_b, L):
    D = zv.shape[1]
    N = 2 * L
    N2 = DFT_N2
    N1 = N // N2
    half = N1 // 2
    n_g = max(1, min(N2, MXU_ROWS // N1))
    n_steps = N2 // n_g
    fwd, inv, fwd_real, f2, g2 = dft_tables(N1, N2)
    td2 = _tile(D, 2048)
    nd2 = D // td2
    zv4 = zv.reshape(2, half, N2, D)
    x04 = x0.reshape(2, half, N2, D)
    x4 = x.reshape(2, half, N2, D)
    any_spec = pl.BlockSpec(memory_space=pl.ANY)

    h, t, rates = filt
    H = h.shape[1]
    af, asum = pl.pallas_call(
        _fft_s1_real_kernel,
        grid=(n_steps,),
        in_specs=[pl.BlockSpec((n_g, half, H), lambda j: (j, 0, 0)),
                  pl.BlockSpec((n_g, half, 1), lambda j: (j, 0, 0)),
                  pl.BlockSpec((H, 2 * D), lambda j: (0, 0)),
                  pl.BlockSpec((1, 2 * D), lambda j: (0, 0)),
                  pl.BlockSpec((n_g, 2 * N1, half), lambda j: (j, 0, 0))],
        out_specs=[any_spec, pl.BlockSpec((1, 2 * D), lambda j: (0, 0))],
        out_shape=[jax.ShapeDtypeStruct((N1, N2, 2 * D), jnp.uint32), jax.ShapeDtypeStruct((1, 2 * D), F32)],
        scratch_shapes=[pltpu.VMEM((2, n_g, N1, 2 * D), jnp.uint32), pltpu.SemaphoreType.DMA((2,))],
        compiler_params=_params(("arbitrary",)),
        name="filter_dft_stage1",
    )(h.reshape(N2, half, H), t.reshape(N2, half, 1), f_w3, rates, fwd_real)
    scale = 1.0 / ((asum[:, :D] + asum[:, D:]) * N)

    a = pl.pallas_call(
        _fft_s1_kernel,
        grid=(n_steps,),
        in_specs=[any_spec,
                  pl.BlockSpec((n_g, 2 * N1, N1), lambda j: (j, 0, 0))],
        out_specs=any_spec,
        out_shape=jax.ShapeDtypeStruct((N1, N2, D), jnp.uint32),
        scratch_shapes=[pltpu.VMEM((2, n_g, N1, D), F32), pltpu.VMEM((2, n_g, N1, D), jnp.uint32),
                        pltpu.SemaphoreType.DMA((2,)), pltpu.SemaphoreType.DMA((2,))],
        compiler_params=_params(("arbitrary",)),
        name="conv_dft_stage1",
    )(zv4, fwd)

    c = pl.pallas_call(
        _fft_mid_kernel,
        grid=(N1, nd2),
        in_specs=[pl.BlockSpec((None, N2, td2), lambda k, j: (k, 0, j)),
                  pl.BlockSpec((None, N2, td2), lambda k, j: (k, 0, j)),
                  pl.BlockSpec((None, N2, td2), lambda k, j: (k, 0, j + nd2)),
                  pl.BlockSpec((1, td2), lambda k, j: (0, j)),
                  pl.BlockSpec((2 * N2, 2 * N2), lambda k, j: (0, 0)),
                  pl.BlockSpec((2 * N2, 2 * N2), lambda k, j: (0, 0))],
        out_specs=pl.BlockSpec((None, N2, td2), lambda k, j: (k, 0, j)),
        out_shape=jax.ShapeDtypeStruct((N1, N2, D), jnp.uint32),
        compiler_params=_params(("parallel", "parallel")),
        name="conv_dft_stage2",
    )(a, af, af, scale, f2, g2)

    def buf(dtype):
        return pltpu.VMEM((2, n_g, N1, D), dtype)

    o = pl.pallas_call(
        _fft_out_kernel,
        grid=(n_steps,),
        in_specs=[any_spec,
                  pl.BlockSpec((n_g, N1, 2 * N1), lambda j: (j, 0, 0)),
                  any_spec, any_spec, any_spec,
                  pl.BlockSpec((1, D), lambda j: (0, 0)),
                  pl.BlockSpec(w_out.shape, lambda j: (0, 0))],
        out_specs=any_spec,
        out_shape=jax.ShapeDtypeStruct((2, half, N2, D), F32),
        scratch_shapes=[buf(jnp.uint32), buf(F32), buf(F32), buf(F32), buf(F32),
                        pltpu.SemaphoreType.DMA((2,)), pltpu.SemaphoreType.DMA((2,))],
        compiler_params=_params(("arbitrary",)),
        name="conv_dft_inverse_outproj",
    )(c, inv, zv4, x04, x4, skip_b, w_out)
    return o.reshape(2 * L, D)


VT_ROWS = HEAD_DIM + BF16_SUBLANES


def _qkv_kernel(x_ref, g_ref, w_ref, gain_ref, cos_ref, sin_ref, o_ref, va_ref, vt_ref, hn_ref, *, n_rope_tiles):
    j = pl.program_id(1)

    @pl.when(j == 0)
    def _():
        hn_ref[...] = _norm_rows(x_ref[...], g_ref[...]).astype(BF16)

    u = _dot(hn_ref[...], w_ref[...])
    tn = u.shape[1]

    @pl.when(j < n_rope_tiles)
    def _():
        gain = gain_ref[...]
        cos = cos_ref[...]
        sin = sin_ref[...]
        ones = jnp.ones((HEAD_DIM, HEAD_DIM), BF16)
        for hh in range(tn // HEAD_DIM):
            xh = u[:, hh * HEAD_DIM:(hh + 1) * HEAD_DIM]
            sq = xh * xh
            sq_hi = sq.astype(BF16)
            sq_lo = (sq - sq_hi.astype(F32)).astype(BF16)
            ms = (_dot(sq_hi, ones) + _dot(sq_lo, ones)) * (1.0 / HEAD_DIM)
            y = xh * lax.rsqrt(ms + RMS_EPS) * gain
            sw = pltpu.roll(y, HEAD_DIM // 2, 1)
            o_ref[:, hh * HEAD_DIM:(hh + 1) * HEAD_DIM] = (y * cos + sw * sin).astype(o_ref.dtype)

    @pl.when(j >= n_rope_tiles)
    def _():
        o_ref[...] = u.astype(o_ref.dtype)
        ones = jnp.ones((u.shape[0], HEAD_DIM), va_ref.dtype)
        ones_t = jnp.ones((VT_ROWS - HEAD_DIM, u.shape[0]), vt_ref.dtype)
        for hh in range(tn // HEAD_DIM):
            vh = u[:, hh * HEAD_DIM:(hh + 1) * HEAD_DIM]
            va_ref[:, 2 * hh * HEAD_DIM:(2 * hh + 1) * HEAD_DIM] = vh.astype(va_ref.dtype)
            va_ref[:, (2 * hh + 1) * HEAD_DIM:(2 * hh + 2) * HEAD_DIM] = ones
            vt_ref[hh, :HEAD_DIM, :] = vh.T.astype(vt_ref.dtype)
            vt_ref[hh, HEAD_DIM:, :] = ones_t


def rope_tables(L):
    rows = L // GRID_W
    r, c = jnp.meshgrid(jnp.arange(rows, dtype=F32), jnp.arange(GRID_W, dtype=F32), indexing="ij")
    r = r.reshape(L)
    c = c.reshape(L)
    axis_dim = HEAD_DIM // 2
    inv = 1.0 / (ROPE_THETA ** (jnp.arange(0, axis_dim, 2, dtype=F32) / axis_dim))
    ang = jnp.concatenate([r[:, None] * inv[None], c[:, None] * inv[None]], axis=-1)
    cos = jnp.concatenate([jnp.cos(ang), jnp.cos(ang)], axis=1)
    sin = jnp.concatenate([-jnp.sin(ang), jnp.sin(ang)], axis=1)
    return cos, sin


def _deinterleave_heads(n_cols):
    d = np.arange(HEAD_DIM)
    within = np.concatenate([d[0::2], d[1::2]])
    return (np.arange(n_cols // HEAD_DIM)[:, None] * HEAD_DIM + within[None, :]).reshape(-1)


def qkv_project(x, g, w_qkv, q_gain, k_gain, seq_len, *, tm=512):
    T, D = x.shape
    n_out = w_qkv.shape[1]
    nk = N_KV_HEADS * HEAD_DIM
    nq = n_out - 2 * nk
    tn = nk
    tm = _tile(seq_len, tm)
    n_rope = (nq + nk) // tn
    perm = _deinterleave_heads(nq + nk)
    w_qkv = jnp.concatenate([w_qkv[:, perm], w_qkv[:, nq + nk:]], axis=1)
    q_gain = q_gain[perm[:HEAD_DIM]]
    k_gain = k_gain[perm[:HEAD_DIM]]
    gains = jnp.concatenate([jnp.tile((q_gain * SCORE_SCALE)[None, None, :], (nq // tn, 1, 1)),
                             k_gain[None, None, :],
                             jnp.ones((1, 1, HEAD_DIM), F32)], axis=0)
    cos, sin = rope_tables(seq_len)
    spt = seq_len // tm
    kern = functools.partial(_qkv_kernel, n_rope_tiles=n_rope)
    return pl.pallas_call(
        kern,
        grid=(T // tm, n_out // tn),
        in_specs=[pl.BlockSpec((tm, D), lambda i, j: (i, 0)),
                  pl.BlockSpec((1, D), lambda i, j: (0, 0)),
                  pl.BlockSpec((D, tn), lambda i, j: (0, j)),
                  pl.BlockSpec((None, 1, HEAD_DIM), lambda i, j: (j, 0, 0)),
                  pl.BlockSpec((tm, HEAD_DIM), lambda i, j: (i % spt, 0)),
                  pl.BlockSpec((tm, HEAD_DIM), lambda i, j: (i % spt, 0))],
        out_specs=[pl.BlockSpec((tm, tn), lambda i, j: (i, j)),
                   pl.BlockSpec((tm, 2 * nk), lambda i, j: (i, 0)),
                   pl.BlockSpec((None, N_KV_HEADS, VT_ROWS, tm), lambda i, j: (i // spt, 0, 0, i % spt))],
        out_shape=[jax.ShapeDtypeStruct((T, n_out), BF16), jax.ShapeDtypeStruct((T, 2 * nk), BF16),
                   jax.ShapeDtypeStruct((T // seq_len, N_KV_HEADS, VT_ROWS, seq_len), BF16)],
        scratch_shapes=[pltpu.VMEM((tm, D), BF16)],
        compiler_params=_params(("parallel", "arbitrary")),
        name="qkv_project",
    )(x, g, w_qkv, gains, cos, sin)


def _flash_kernel(q_ref, k_ref, v_ref, o_ref, qs_ref, s0_ref, s1_ref, p0_ref, p1_ref, a0_ref, a1_ref,
                  m_ref, acc_ref, *, group, tq, tk, nk):
    s_refs = (s0_ref, s1_ref)
    p_refs = (p0_ref, p1_ref)
    a_refs = (a0_ref, a1_ref)
    for gq in range(group):
        qs_ref[gq * tq:(gq + 1) * tq, :] = q_ref[:, gq * HEAD_DIM:(gq + 1) * HEAD_DIM]
    m_ref[...] = jnp.full_like(m_ref, -jnp.inf)
    acc_ref[...] = jnp.zeros_like(acc_ref)

    def rows_of(j):
        if isinstance(j, int):
            return pl.ds(j * tk, tk)
        return pl.ds(pl.multiple_of(j * tk, tk), tk)

    def scores(j, slot):
        kb = k_ref[rows_of(j), :]
        s_refs[slot][...] = lax.dot_general(qs_ref[...], kb, (((1,), (1,)), ((), ())),
                                            preferred_element_type=F32)

    def softmax(slot):
        s = s_refs[slot][...]
        m_prev = m_ref[...]
        m_new = jnp.maximum(m_prev, jnp.max(s, axis=-1, keepdims=True))
        a_refs[slot][...] = jnp.exp2(m_prev - m_new)
        p_refs[slot][...] = jnp.exp2(s - jnp.tile(m_new, (1, tk // LANES))).astype(BF16)
        m_ref[...] = m_new

    def accumulate(j, slot):
        pv = _dot(p_refs[slot][...], v_ref[rows_of(j), :])
        acc_ref[...] = acc_ref[...] * jnp.tile(a_refs[slot][...], (1, 2)) + pv

    def step(j, slot, first, last):
        if not last:
            scores(j + 1, 1 - slot)
        softmax(slot)
        if not first:
            accumulate(j - 1, 1 - slot)

    scores(0, 0)
    if nk >= 4 and nk % 2 == 0:
        step(0, 0, True, False)

        def pair(t, c):
            j = 2 * t + 1
            step(j, 1, False, False)
            step(j + 1, 0, False, False)
            return c

        lax.fori_loop(0, (nk - 2) // 2, pair, 0)
        step(nk - 1, 1, False, True)
    else:
        for j in range(nk):
            step(j, j % 2, j == 0, j == nk - 1)
    accumulate(nk - 1, (nk - 1) % 2)

    acc = acc_ref[...]
    o = acc[:, :HEAD_DIM] / acc[:, HEAD_DIM:]
    for gq in range(group):
        o_ref[:, gq * HEAD_DIM:(gq + 1) * HEAD_DIM] = o[gq * tq:(gq + 1) * tq].astype(o_ref.dtype)


def _flash_bounded_kernel(q_ref, k_ref, vt_ref, o_ref, qt_ref, acc_ref, *, group, tq, tk, nk, unroll):
    for gq in range(group):
        qt_ref[:, gq * tq:(gq + 1) * tq] = q_ref[:, gq * HEAD_DIM:(gq + 1) * HEAD_DIM].astype(F32).T.astype(BF16)
    acc_ref[...] = jnp.zeros_like(acc_ref)

    def block(j, c):
        rows = pl.ds(pl.multiple_of(j * tk, tk), tk)
        s_t = _dot(k_ref[rows, :], qt_ref[...])
        acc_ref[...] += _dot(vt_ref[:, rows], jnp.exp2(s_t).astype(BF16))
        return c

    lax.fori_loop(0, nk, block, 0, unroll=unroll)
    acc = acc_ref[...]
    o_t = acc[:HEAD_DIM] / acc[HEAD_DIM:HEAD_DIM + 1]
    for gq in range(group):
        o_ref[:, gq * HEAD_DIM:(gq + 1) * HEAD_DIM] = o_t[:, gq * tq:(gq + 1) * tq].T.astype(o_ref.dtype)


SAFE_SCORE_BOUND = 80.0


def flash_attention(qkv, v_aug, v_t, score_bound, B, L, n_heads, *, tq=256, tk=512):
    group = n_heads // N_KV_HEADS
    tq = _tile(L, tq)
    tk = _tile(L, tk)
    nk = L // tk
    qw = group * HEAD_DIM
    k_col = n_heads
    rows = group * tq
    q_spec = pl.BlockSpec((None, tq, qw), lambda b, h, qi: (b, qi, h))
    k_spec = pl.BlockSpec((None, L, HEAD_DIM), lambda b, h, qi: (b, 0, k_col + h))
    common = dict(
        grid=(B, N_KV_HEADS, L // tq),
        out_specs=pl.BlockSpec((None, tq, qw), lambda b, h, qi: (b, qi, h)),
        out_shape=jax.ShapeDtypeStruct((B, L, n_heads * HEAD_DIM), BF16),
        compiler_params=_params(("parallel", "parallel", "parallel")),
    )
    general = pl.pallas_call(
        functools.partial(_flash_kernel, group=group, tq=tq, tk=tk, nk=nk),
        in_specs=[q_spec, k_spec, pl.BlockSpec((None, L, 2 * HEAD_DIM), lambda b, h, qi: (b, 0, h))],
        scratch_shapes=[pltpu.VMEM((rows, HEAD_DIM), BF16),
                        pltpu.VMEM((rows, tk), F32), pltpu.VMEM((rows, tk), F32),
                        pltpu.VMEM((rows, tk), BF16), pltpu.VMEM((rows, tk), BF16),
                        pltpu.VMEM((rows, LANES), F32), pltpu.VMEM((rows, LANES), F32),
                        pltpu.VMEM((rows, LANES), F32),
                        pltpu.VMEM((rows, 2 * HEAD_DIM), F32)],
        name="flash_attention", **common)
    bounded = pl.pallas_call(
        functools.partial(_flash_bounded_kernel, group=group, tq=tq, tk=tk, nk=nk,
                          unroll=math.gcd(nk, 8)),
        in_specs=[q_spec, k_spec, pl.BlockSpec((None, None, VT_ROWS, L), lambda b, h, qi: (b, h, 0, 0))],
        scratch_shapes=[pltpu.VMEM((HEAD_DIM, rows), BF16),
                        pltpu.VMEM((VT_ROWS, rows), F32)],
        name="flash_attention_bounded", **common)
    return lax.cond(score_bound <= SAFE_SCORE_BOUND,
                    lambda ops: bounded(ops[0], ops[0], ops[2]),
                    lambda ops: general(ops[0], ops[0], ops[1]),
                    (qkv, v_aug, v_t))


def _pack_bf16_pairs(x):
    half = x.shape[1] // 2
    bits = lax.bitcast_convert_type(x.astype(BF16).astype(F32), jnp.uint32)
    return (bits[:, :half] >> 16) | (bits[:, half:] & jnp.uint32(0xFFFF0000))


def _unpack_bf16_pairs(p):
    lo = lax.bitcast_convert_type(p << 16, F32).astype(BF16)
    hi = lax.bitcast_convert_type(p & jnp.uint32(0xFFFF0000), F32).astype(BF16)
    return lo, hi


def _router_kernel(xa_ref, xb_ref, g_ref, rw_ref, h_ref, info_ref, cnt_ref, *, n_a_tiles):
    i = pl.program_id(0)

    def route(x_ref):
        hf = _norm_rows(x_ref[...], g_ref[...])
        h_ref[...] = _pack_bf16_pairs(hf)
        h_hi = hf.astype(BF16)
        h_lo = (hf - h_hi.astype(F32)).astype(BF16)
        rw = rw_ref[...]
        hw = _dot(h_hi, rw)
        logits = hw[:, :LANES] + hw[:, LANES:] + _dot(h_lo, rw[:, :LANES])
        shape = logits.shape
        lane = lax.broadcasted_iota(jnp.int32, shape, 1)
        lanef = lane.astype(F32)
        neg = -jnp.inf
        big = float(LANES)
        is_g = lane < N_GROUPS
        gl = jnp.where(is_g, logits, neg)
        gmax = jnp.max(gl, axis=-1, keepdims=True)
        gidx = jnp.min(jnp.where(gl == gmax, lanef, big), axis=-1, keepdims=True)
        p_g = 1.0 / jnp.sum(jnp.where(is_g, jnp.exp(logits - gmax), 0.0), axis=-1, keepdims=True)
        lo = N_GROUPS + gidx * EXPERTS_PER_GROUP
        el = jnp.where(lanef >= lo, jnp.where(lanef < lo + EXPERTS_PER_GROUP, logits, neg), neg)
        v1 = jnp.max(el, axis=-1, keepdims=True)
        i1 = jnp.min(jnp.where(el == v1, lanef, big), axis=-1, keepdims=True)
        el2 = jnp.where(lanef == i1, neg, el)
        v2 = jnp.max(el2, axis=-1, keepdims=True)
        i2 = jnp.min(jnp.where(el2 == v2, lanef, big), axis=-1, keepdims=True)
        t = jnp.exp(v2 - v1)
        w1 = p_g / (1.0 + t)
        w2 = p_g * t / (1.0 + t)
        e1 = i1 - N_GROUPS
        e2 = i2 - N_GROUPS
        oh1 = jnp.where(lanef == e1, 1.0, 0.0)
        oh2 = jnp.where(lanef == e2, 1.0, 0.0)
        oh = oh1 + oh2
        tm = shape[0]
        tri = jnp.where(lax.broadcasted_iota(jnp.int32, (tm, tm), 1) < lax.broadcasted_iota(jnp.int32, (tm, tm), 0),
                        1.0, 0.0).astype(BF16)
        before = _dot(tri, oh.astype(BF16)) + cnt_ref[...]
        r1 = jnp.sum(before * oh1, axis=-1, keepdims=True)
        r2 = jnp.sum(before * oh2, axis=-1, keepdims=True)
        cnt_ref[...] += jnp.sum(oh, axis=0, keepdims=True)
        vals = (e1, e2, w1, w2, r1, r2)
        info = jnp.zeros(shape, F32)
        for idx, val in enumerate(vals):
            info = jnp.where(lane == idx, val, info)
        info_ref[...] = info

    @pl.when(i == 0)
    def _():
        cnt_ref[...] = jnp.zeros_like(cnt_ref)

    @pl.when(i < n_a_tiles)
    def _():
        route(xa_ref)

    @pl.when(i >= n_a_tiles)
    def _():
        route(xb_ref)


def moe_route(xa, xb, g, router_group, router_expert, *, tm=512):
    Ta, D = xa.shape
    Tb = xb.shape[0]
    tm = _tile(math.gcd(Ta, Tb), tm)
    na, nb = Ta // tm, Tb // tm
    n_e = N_GROUPS * EXPERTS_PER_GROUP
    rw = jnp.concatenate([router_group, jnp.transpose(router_expert, (1, 0, 2)).reshape(D, n_e)], axis=1)
    rw = jnp.pad(rw, ((0, 0), (0, LANES - rw.shape[1])))
    rw_hi = rw.astype(BF16)
    rw = jnp.concatenate([rw_hi, (rw - rw_hi.astype(F32)).astype(BF16)], axis=1)
    kern = functools.partial(_router_kernel, n_a_tiles=na)
    return pl.pallas_call(
        kern,
        grid=(na + nb,),
        in_specs=[pl.BlockSpec((tm, D), lambda i: (jnp.minimum(i, na - 1), 0)),
                  pl.BlockSpec((tm, D), lambda i: (jnp.maximum(i - na, 0), 0)),
                  pl.BlockSpec((1, D), lambda i: (0, 0)),
                  pl.BlockSpec((D, 2 * LANES), lambda i: (0, 0))],
        out_specs=[pl.BlockSpec((tm, D // 2), lambda i: (i, 0)),
                   pl.BlockSpec((tm, LANES), lambda i: (i, 0)),
                   pl.BlockSpec((1, LANES), lambda i: (0, 0))],
        out_shape=[jax.ShapeDtypeStruct((Ta + Tb, D // 2), jnp.uint32),
                   jax.ShapeDtypeStruct((Ta + Tb, LANES), F32),
                   jax.ShapeDtypeStruct((1, LANES), F32)],
        compiler_params=_params(("arbitrary",)),
        name="moe_route",
    )(xa, xb, g, rw)


def moe_plan(info, cnt, tm, n_experts):
    T = info.shape[0]
    n_tiles = (2 * T) // tm + n_experts
    counts = cnt[0, :n_experts].astype(jnp.int32)
    padded = ((counts + tm - 1) // tm) * tm
    ends = jnp.cumsum(padded)
    off = ends - padded
    e = info[:, 0:2].T.astype(jnp.int32)
    rank = info[:, 4:6].T.astype(jnp.int32)
    pos = off[e] + rank
    tile_start = jnp.arange(n_tiles, dtype=jnp.int32) * tm
    tile_e = jnp.sum((tile_start[:, None] >= ends[None, :]).astype(jnp.int32), axis=1)
    tile_e = jnp.minimum(tile_e, n_experts - 1)
    tile_valid = jnp.clip(counts[tile_e] - (tile_start - off[tile_e]), 0, tm).astype(jnp.int32)
    n_used = (ends[-1:] // tm).astype(jnp.int32)
    expert_last = jnp.where(padded > 0, ends - tm, -1)
    tail_tile = n_used[0] + jnp.arange(n_experts, dtype=jnp.int32)
    tail = jnp.where(tail_tile < n_tiles, tail_tile * tm, -1)
    zero_tiles = jnp.concatenate([expert_last, tail]).astype(jnp.int32)[None, :]
    return pos, tile_e, tile_valid, n_used, zero_tiles


def _pos_tiles(pos, tm):
    T = pos.shape[1]
    return pos.reshape(2, T // tm, tm).transpose(1, 0, 2)


DMA_UNROLL = 8


def _dispatch_kernel(pos_ref, ztile_ref, h_ref, hs_hbm, zbuf, sem, zsem):
    tm = h_ref.shape[0]

    @pl.when(pl.program_id(0) == 0)
    def _():
        zbuf[...] = jnp.zeros_like(zbuf)
        zrows = zbuf.shape[0]

        def zero_copy(k):
            return pltpu.make_async_copy(zbuf, hs_hbm.at[pl.ds(pl.multiple_of(ztile_ref[0, k], zrows), zrows)], zsem)

        for k in range(ztile_ref.shape[1]):
            @pl.when(ztile_ref[0, k] >= 0)
            def _():
                zero_copy(k).start()
        for k in range(ztile_ref.shape[1]):
            @pl.when(ztile_ref[0, k] >= 0)
            def _():
                zero_copy(k).wait()

    def row_copy(r, slot):
        return pltpu.make_async_copy(h_ref.at[pl.ds(r, 1)], hs_hbm.at[pl.ds(pos_ref[0, slot, r], 1)], sem)

    def start(r, c):
        row_copy(r, 0).start()
        row_copy(r, 1).start()
        return c

    def wait(r, c):
        row_copy(r, 0).wait()
        row_copy(r, 1).wait()
        return c

    lax.fori_loop(0, tm, start, 0, unroll=DMA_UNROLL)
    lax.fori_loop(0, tm, wait, 0, unroll=DMA_UNROLL)


def moe_dispatch(h, pos, zero_tiles, n_sorted_rows, tile_rows, *, tm=512):
    T, C = h.shape
    tm = _tile(T, tm)
    return pl.pallas_call(
        _dispatch_kernel,
        grid=(T // tm,),
        in_specs=[pl.BlockSpec((1, 2, tm), lambda i: (i, 0, 0), memory_space=pltpu.SMEM),
                  pl.BlockSpec(zero_tiles.shape, lambda i: (0, 0), memory_space=pltpu.SMEM),
                  pl.BlockSpec((tm, C), lambda i: (i, 0))],
        out_specs=pl.BlockSpec(memory_space=pl.ANY),
        out_shape=jax.ShapeDtypeStruct((n_sorted_rows, C), h.dtype),
        scratch_shapes=[pltpu.VMEM((tile_rows, C), h.dtype), pltpu.SemaphoreType.DMA, pltpu.SemaphoreType.DMA],
        compiler_params=_params(("arbitrary",)),
        name="moe_dispatch",
    )(_pos_tiles(pos, tm), zero_tiles, h)


def _expert_kernel(te_ref, tv_ref, nu_ref, x_ref, wg_ref, wu_ref, wd_ref, y_ref):
    del te_ref, nu_ref
    valid = tv_ref[pl.program_id(0)]

    @pl.when(valid == 0)
    def _():
        y_ref[...] = jnp.zeros_like(y_ref)

    @pl.when(valid > 0)
    def _():
        x_lo, x_hi = _unpack_bf16_pairs(x_ref[...])
        half = x_ref.shape[1]
        gate = (_dot(x_lo, wg_ref[:half, :].astype(BF16)) + _dot(x_hi, wg_ref[half:, :].astype(BF16)))
        up = (_dot(x_lo, wu_ref[:half, :].astype(BF16)) + _dot(x_hi, wu_ref[half:, :].astype(BF16)))
        a = (gate * jax.nn.sigmoid(gate) * up).astype(BF16)
        y_ref[...] = _dot(a, wd_ref[...].astype(BF16))


def moe_experts(hs, tile_e, tile_valid, n_used, w_gate, w_up, w_down, layer, tm):
    P = hs.shape[0]
    D = w_gate.shape[2]
    d_e = w_gate.shape[-1]

    def row_map(i, te, tv, nu):
        return (jnp.minimum(i, nu[0] - 1), 0)

    def out_map(i, te, tv, nu):
        return (i, 0)

    def w_map(i, te, tv, nu):
        return (layer, te[i], 0, 0)

    grid_spec = pltpu.PrefetchScalarGridSpec(
        num_scalar_prefetch=3,
        grid=(P // tm,),
        in_specs=[pl.BlockSpec((tm, D // 2), row_map),
                  pl.BlockSpec((None, None, D, d_e), w_map),
                  pl.BlockSpec((None, None, D, d_e), w_map),
                  pl.BlockSpec((None, None, d_e, D), w_map)],
        out_specs=pl.BlockSpec((tm, D), out_map),
    )
    return pl.pallas_call(
        _expert_kernel,
        grid_spec=grid_spec,
        out_shape=jax.ShapeDtypeStruct((P, D), F32),
        compiler_params=_params(("arbitrary",)),
        name="moe_experts",
    )(tile_e, tile_valid, n_used, hs, w_gate, w_up, w_down)


def _combine_kernel(pos_ref, posn_ref, x_ref, info_ref, g_ref, y_hbm, o_ref, ybuf, sem, *, final_norm):
    tm = x_ref.shape[0]
    i = pl.program_id(0)
    buf = i % 2

    def row_copy(p_ref, b, r, slot):
        return pltpu.make_async_copy(y_hbm.at[pl.ds(p_ref[0, slot, r], 1)],
                                     ybuf.at[b, slot, pl.ds(r, 1)], sem.at[b])

    def start_tile(p_ref, b):
        def body(r, c):
            row_copy(p_ref, b, r, 0).start()
            row_copy(p_ref, b, r, 1).start()
            return c
        lax.fori_loop(0, tm, body, 0, unroll=DMA_UNROLL)

    @pl.when(i == 0)
    def _():
        start_tile(pos_ref, 0)

    @pl.when(i + 1 < pl.num_programs(0))
    def _():
        start_tile(posn_ref, 1 - buf)

    def wait(r, c):
        row_copy(pos_ref, buf, r, 0).wait()
        row_copy(pos_ref, buf, r, 1).wait()
        return c

    lax.fori_loop(0, tm, wait, 0, unroll=DMA_UNROLL)
    info = info_ref[...]
    y = info[:, 2:3] * ybuf[buf, 0] + info[:, 3:4] * ybuf[buf, 1]
    x = x_ref[...] + y
    if final_norm:
        x = _norm_rows(x, g_ref[...])
    o_ref[...] = x


def moe_combine(x, y, pos, info, row_off, g_final, final_norm, *, tm=256):
    Tg, D = x.shape
    T = info.shape[0]
    tm = _tile(math.gcd(Tg, row_off) if row_off else Tg, tm)
    ob = row_off // tm
    kern = functools.partial(_combine_kernel, final_norm=final_norm)
    n = Tg // tm
    pos3 = _pos_tiles(pos, tm)
    return pl.pallas_call(
        kern,
        grid=(n,),
        in_specs=[pl.BlockSpec((1, 2, tm), lambda i: (i + ob, 0, 0), memory_space=pltpu.SMEM),
                  pl.BlockSpec((1, 2, tm), lambda i: (jnp.minimum(i + 1, n - 1) + ob, 0, 0),
                               memory_space=pltpu.SMEM),
                  pl.BlockSpec((tm, D), lambda i: (i, 0)),
                  pl.BlockSpec((tm, LANES), lambda i: (i + ob, 0)),
                  pl.BlockSpec((1, D), lambda i: (0, 0)),
                  pl.BlockSpec(memory_space=pl.ANY)],
        out_specs=pl.BlockSpec((tm, D), lambda i: (i, 0)),
        out_shape=jax.ShapeDtypeStruct((Tg, D), F32),
        scratch_shapes=[pltpu.VMEM((2, 2, tm, D), F32), pltpu.SemaphoreType.DMA((2,))],
        compiler_params=_params(("arbitrary",)),
        name="moe_combine",
    )(pos3, pos3, x, info, g_final, y)


MOE_TILE = 512


def hier_moe_block(xs, g, router_group, router_expert, w_gate, w_up, w_down, layer, g_final, final_norm):
    xa, xb = xs
    n_experts = w_gate.shape[1]
    h, info, cnt = moe_route(xa, xb, g, router_group, router_expert)
    T = h.shape[0]
    tm = _tile(2 * T, MOE_TILE)
    pos, tile_e, tile_valid, n_used, zero_tiles = moe_plan(info, cnt, tm, n_experts)
    hs = moe_dispatch(h, pos, zero_tiles, tile_e.shape[0] * tm, tm)
    y = moe_experts(hs, tile_e, tile_valid, n_used, w_gate, w_up, w_down, layer, tm)
    oa = moe_combine(xa, y, pos, info, 0, g_final, final_norm)
    ob = moe_combine(xb, y, pos, info, xa.shape[0], g_final, final_norm)
    return oa, ob


def hyena_block(x, L, g, w_in, conv_w, conv_b, f_w1, f_b1, f_win, f_bin, f_freq, f_w3, skip_b, w_out):
    D = x.shape[1]
    x0, zv = hyena_inproj(x, g, w_in, conv_w, conv_b[None, :], L)
    filt = hyena_filter_features(L, w_out.shape[0], f_w1, f_b1, f_win, f_bin, f_freq)
    return hyena_long_conv(zv, x0, x, w_out, filt, f_w3, skip_b[None, :], L)


def attention_block(x, B, L, g, w_qkv, q_gain, k_gain, w_o):
    D = x.shape[1]
    n_heads = w_o.shape[0] // HEAD_DIM
    qkv, v_aug, v_t = qkv_project(x, g, w_qkv, q_gain, k_gain, L)
    score_bound = 1.02 * HEAD_DIM * SCORE_SCALE * jnp.max(jnp.abs(q_gain)) * jnp.max(jnp.abs(k_gain))
    o = flash_attention(qkv.reshape(B, L, qkv.shape[1]), v_aug.reshape(B, L, v_aug.shape[1]), v_t, score_bound,
                        B, L, n_heads)
    return matmul_residual(o.reshape(B * L, n_heads * HEAD_DIM), w_o, x)


def kernel(x_prompt, x_sample, norm_mix, norm_ffn, norm_final, hy_w_in, hy_conv_w, hy_conv_b, hy_f_w1, hy_f_b1, hy_f_win, hy_f_bin, hy_f_freq, hy_f_w3, hy_skip_b, hy_w_out, at_w_qkv, at_q_gain, at_k_gain, at_w_o, moe_router_group, moe_router_expert, moe_w_gate, moe_w_up, moe_w_down):
    D = x_prompt.shape[-1]
    depth = norm_mix.shape[0]
    shapes = [x_prompt.shape, x_sample.shape]
    for s in shapes:
        assert s[0] == 2, "the long convolution packs exactly two sequences per group"
    xs = [x_prompt.reshape(-1, D), x_sample.reshape(-1, D)]
    hy_w_in_b = hy_w_in.astype(BF16)
    hy_w_out_b = hy_w_out.astype(BF16)
    at_w_qkv_b = at_w_qkv.astype(BF16)
    at_w_o_b = at_w_o.astype(BF16)
    g_final = norm_final[None, :]
    for i in range(depth):
        j = i // N_MIXERS
        g = norm_mix[i][None, :]
        if i % N_MIXERS == 0:
            xs = [hyena_block(x, s[1], g, hy_w_in_b[j], hy_conv_w[j], hy_conv_b[j], hy_f_w1[j], hy_f_b1[j],
                              hy_f_win[j], hy_f_bin[j], hy_f_freq[j], hy_f_w3[j], hy_skip_b[j], hy_w_out_b[j])
                  for x, s in zip(xs, shapes)]
        else:
            xs = [attention_block(x, s[0], s[1], g, at_w_qkv_b[j], at_q_gain[j], at_k_gain[j], at_w_o_b[j])
                  for x, s in zip(xs, shapes)]
        xs = hier_moe_block(xs, norm_ffn[i][None, :], moe_router_group[i], moe_router_expert[i],
                            moe_w_gate, moe_w_up, moe_w_down, i, g_final, i == depth - 1)
    return (xs[0].reshape(shapes[0]), xs[1].reshape(shapes[1]))
```

```python
import functools
import math

import jax
import jax.numpy as jnp
import numpy as np
from jax import lax
from jax.experimental import pallas as pl
from jax.experimental.pallas import tpu as pltpu

RMS_EPS = 1e-6
GRID_W = 64
EMB_DIM = 33
DECAY_TARGET = 1e-2
FAST_DECAY_PCT = 0.3
SLOW_DECAY_PCT = 1.5
MIN_DECAY = math.log(DECAY_TARGET) / FAST_DECAY_PCT
MAX_DECAY = math.log(DECAY_TARGET) / SLOW_DECAY_PCT
HEAD_DIM = 128
SCORE_SCALE = HEAD_DIM ** -0.5 * math.log2(math.e)
N_KV_HEADS = 4
ROPE_THETA = 10000.0
N_GROUPS = 4
EXPERTS_PER_GROUP = 8
N_MIXERS = 2

LANES = 128
BF16_SUBLANES = 16
VMEM_LIMIT_BYTES = 52 * 1024 * 1024
DFT_N2 = 128
MXU_ROWS = 256

BF16 = jnp.bfloat16
F32 = jnp.float32


def _params(sem):
    return pltpu.CompilerParams(dimension_semantics=sem, vmem_limit_bytes=VMEM_LIMIT_BYTES)


def _dot(a, b):
    return jnp.dot(a, b, preferred_element_type=F32)


def _dot_3pass(a, b):
    a_hi = a.astype(BF16)
    a_lo = (a - a_hi.astype(F32)).astype(BF16)
    b_hi = b.astype(BF16)
    b_lo = (b - b_hi.astype(F32)).astype(BF16)
    return _dot(a_hi, b_hi) + _dot(a_hi, b_lo) + _dot(a_lo, b_hi)


def _norm_rows(x, g):
    ms = jnp.mean(x * x, axis=-1, keepdims=True)
    return x * lax.rsqrt(ms + RMS_EPS) * g


def _tile(n, pref):
    return pref if n % pref == 0 else n


def _matmul_res_kernel(a_ref, w_ref, r_ref, o_ref):
    o_ref[...] = r_ref[...] + _dot(a_ref[...].astype(BF16), w_ref[...])


def matmul_residual(a, w, res, *, tm=1024, tn=512):
    T, K = a.shape
    N = w.shape[1]
    tm = _tile(T, tm)
    tn = _tile(N, tn)
    return pl.pallas_call(
        _matmul_res_kernel,
        grid=(T // tm, N // tn),
        in_specs=[pl.BlockSpec((tm, K), lambda i, j: (i, 0)),
                  pl.BlockSpec((K, tn), lambda i, j: (0, j)),
                  pl.BlockSpec((tm, tn), lambda i, j: (i, j))],
        out_specs=pl.BlockSpec((tm, tn), lambda i, j: (i, j)),
        out_shape=jax.ShapeDtypeStruct((T, N), F32),
        compiler_params=_params(("parallel", "arbitrary")),
        name="matmul_residual",
    )(a, w, res)


HALO = BF16_SUBLANES


def _hy_inproj_kernel(x_ref, xp_ref, xn_ref, g_ref, w0_ref, w1_ref, w2_ref,
                      cw0_ref, cw1_ref, cw2_ref, cb0_ref, cb1_ref, cb2_ref,
                      x0_out, zv_out, hn_ref, *, tm, tiles_per_seq):
    i = pl.program_id(0)
    j = pl.program_id(1)

    @pl.when(j == 0)
    def _():
        g = g_ref[...]
        hn_ref[HALO:HALO + tm, :] = _norm_rows(x_ref[...], g).astype(BF16)
        first = (i % tiles_per_seq) == 0
        last = (i % tiles_per_seq) == tiles_per_seq - 1
        hp = _norm_rows(xp_ref[...], g)
        hn_ref[0:HALO, :] = jnp.where(first, 0.0, hp).astype(BF16)
        hx = _norm_rows(xn_ref[...], g)
        hn_ref[HALO + tm:, :] = jnp.where(last, 0.0, hx).astype(BF16)

    h = hn_ref[...]
    rows = tm + 2 * HALO

    def conv(w_ref, cw_ref, cb_ref):
        u = _dot(h, w_ref[...])
        um = pltpu.roll(u, 1, 0)[HALO:HALO + tm]
        up = pltpu.roll(u, rows - 1, 0)[HALO:HALO + tm]
        uc = u[HALO:HALO + tm]
        cw = cw_ref[...]
        return um * cw[0:1] + uc * cw[1:2] + up * cw[2:3] + cb_ref[...]

    x0_out[...] = conv(w0_ref, cw0_ref, cb0_ref)
    x1 = conv(w1_ref, cw1_ref, cb1_ref)
    v = conv(w2_ref, cw2_ref, cb2_ref)
    zv_out[...] = v * x1


def hyena_inproj(x, g, w_in, conv_w, conv_b, seq_len, *, tm=512, tn=512):
    T, D = x.shape
    W = w_in.shape[1] // 3
    tm = _tile(seq_len, tm)
    tn = _tile(W, tn)
    nj = W // tn
    hb = tm // HALO
    n_halo_blocks = T // HALO
    kern = functools.partial(_hy_inproj_kernel, tm=tm, tiles_per_seq=seq_len // tm)

    def wspec(c):
        return pl.BlockSpec((D, tn), lambda i, j, c=c: (0, j + c * nj))

    def cwspec(c):
        return pl.BlockSpec((3, tn), lambda i, j, c=c: (0, j + c * nj))

    def cbspec(c):
        return pl.BlockSpec((1, tn), lambda i, j, c=c: (0, j + c * nj))

    return pl.pallas_call(
        kern,
        grid=(T // tm, nj),
        in_specs=[pl.BlockSpec((tm, D), lambda i, j: (i, 0)),
                  pl.BlockSpec((HALO, D), lambda i, j: (jnp.maximum(i * hb - 1, 0), 0)),
                  pl.BlockSpec((HALO, D), lambda i, j: (jnp.minimum((i + 1) * hb, n_halo_blocks - 1), 0)),
                  pl.BlockSpec((1, D), lambda i, j: (0, 0)),
                  wspec(0), wspec(1), wspec(2),
                  cwspec(0), cwspec(1), cwspec(2),
                  cbspec(0), cbspec(1), cbspec(2)],
        out_specs=[pl.BlockSpec((tm, tn), lambda i, j: (i, j)),
                   pl.BlockSpec((tm, tn), lambda i, j: (i, j))],
        out_shape=[jax.ShapeDtypeStruct((T, W), F32), jax.ShapeDtypeStruct((T, W), F32)],
        scratch_shapes=[pltpu.VMEM((tm + 2 * HALO, D), BF16)],
        compiler_params=_params(("parallel", "arbitrary")),
        name="hyena_inproj",
    )(x, x, x, g, w_in, w_in, w_in, conv_w, conv_w, conv_w, conv_b, conv_b, conv_b)


def _filter_mlp_kernel(z_ref, w1_ref, b1_ref, win_ref, bin_ref, fr_ref, h_ref):
    hp = lax.Precision.HIGHEST
    fr = fr_ref[...]
    h = jnp.sin(fr * (jnp.dot(z_ref[...], w1_ref[...], precision=hp, preferred_element_type=F32) + b1_ref[...]))
    for l in range(win_ref.shape[0]):
        h = jnp.sin(fr * (jnp.dot(h, win_ref[l], precision=hp, preferred_element_type=F32) + bin_ref[l]))
    h_ref[...] = h


def hyena_filter_features(L, D, f_w1, f_b1, f_win, f_bin, f_freq):
    H = f_w1.shape[1]
    t = jnp.linspace(0.0, 1.0, L, dtype=F32)[:, None]
    bands = (EMB_DIM - 1) // 2
    w = 2.0 * math.pi * jnp.arange(L, dtype=F32)[:, None] / L
    fb = jnp.linspace(1e-4, bands - 1, bands, dtype=F32)[None, :]
    z = jnp.concatenate([t, jnp.cos(fb * w), -jnp.sin(fb * w)], axis=-1)

    def lag_major_to_dft_order(a):
        return a.reshape(L // DFT_N2, DFT_N2, a.shape[1]).transpose(1, 0, 2).reshape(L, a.shape[1])

    z = lag_major_to_dft_order(z)
    t = lag_major_to_dft_order(t)
    zp = jnp.pad(z, ((0, 0), (0, LANES - EMB_DIM)))
    w1p = jnp.pad(f_w1, ((0, LANES - EMB_DIM), (0, 0)))
    n_in = f_win.shape[0]
    tl = _tile(L, 512)
    h = pl.pallas_call(
        _filter_mlp_kernel,
        grid=(L // tl,),
        in_specs=[pl.BlockSpec((tl, LANES), lambda i: (i, 0)),
                  pl.BlockSpec((LANES, H), lambda i: (0, 0)),
                  pl.BlockSpec((1, H), lambda i: (0, 0)),
                  pl.BlockSpec((n_in, H, H), lambda i: (0, 0, 0)),
                  pl.BlockSpec((n_in, 1, H), lambda i: (0, 0, 0)),
                  pl.BlockSpec((1, H), lambda i: (0, 0))],
        out_specs=pl.BlockSpec((tl, H), lambda i: (i, 0)),
        out_shape=jax.ShapeDtypeStruct((L, H), F32),
        compiler_params=_params(("parallel",)),
        name="filter_mlp",
    )(zp, w1p, f_b1[None, :], f_win, f_bin[:, None, :], f_freq[None, :])

    deltas = jnp.abs(jnp.linspace(MIN_DECAY, MAX_DECAY, D, dtype=F32))[None, :]
    return h, t, jnp.concatenate([deltas, deltas], axis=1)


def dft_tables(N1, N2):
    N = N1 * N2
    half = N1 // 2
    k1 = jnp.arange(N1, dtype=jnp.int32)
    ang_a = ((k1[:, None] * jnp.arange(half, dtype=jnp.int32)[None, :]) % N1).astype(F32) * (2.0 * math.pi / N1)
    ang_b = ((k1[None, :] * jnp.arange(N2, dtype=jnp.int32)[:, None]) % N).astype(F32) * (2.0 * math.pi / N)
    ca, sa = jnp.cos(ang_a)[None, :, :], jnp.sin(ang_a)[None, :, :]
    cb, sb = jnp.cos(ang_b)[:, :, None], jnp.sin(ang_b)[:, :, None]
    c = ca * cb - sa * sb
    s = sa * cb + ca * sb
    fwd = jnp.concatenate([jnp.concatenate([c, s], axis=2),
                           jnp.concatenate([-s, c], axis=2)], axis=1)
    inv = jnp.swapaxes(fwd, 1, 2)
    fwd_real = jnp.concatenate([c, -s], axis=1)
    k2 = jnp.arange(N2, dtype=jnp.int32)
    ang2 = ((k2[:, None] * k2[None, :]) % N2).astype(F32) * (2.0 * math.pi / N2)
    c2, s2 = jnp.cos(ang2), jnp.sin(ang2)
    f2 = jnp.concatenate([jnp.concatenate([c2, s2], axis=1),
                          jnp.concatenate([-s2, c2], axis=1)], axis=0)
    g2 = f2.T
    return fwd.astype(BF16), inv.astype(BF16), fwd_real.astype(BF16), f2.astype(BF16), g2.astype(BF16)


def _pack_complex(re, im):
    rb = lax.bitcast_convert_type(re.astype(BF16).astype(F32), jnp.uint32)
    ib = lax.bitcast_convert_type(im.astype(BF16).astype(F32), jnp.uint32)
    return (rb >> 16) | (ib & jnp.uint32(0xFFFF0000))


def _unpack_complex(p):
    re = lax.bitcast_convert_type(p << 16, F32)
    im = lax.bitcast_convert_type(p & jnp.uint32(0xFFFF0000), F32)
    return jnp.concatenate([re, im], axis=0).astype(BF16)


def _double_buffered(in_copies, out_copies, compute):
    j = pl.program_id(0)
    n = pl.num_programs(0)
    slot = j % 2

    @pl.when(j == 0)
    def _():
        for cp in in_copies(0, 0):
            cp.start()

    @pl.when(j + 1 < n)
    def _():
        for cp in in_copies(j + 1, 1 - slot):
            cp.start()

    for cp in in_copies(j, slot):
        cp.wait()

    @pl.when(j >= 2)
    def _():
        for cp in out_copies(j - 2, slot):
            cp.wait()

    compute(slot)
    for cp in out_copies(j, slot):
        cp.start()

    @pl.when(j == n - 1)
    def _():
        for cp in out_copies(j, slot):
            cp.wait()

        @pl.when(n >= 2)
        def _():
            for cp in out_copies(j - 1, 1 - slot):
                cp.wait()


def _fft_s1_kernel(z_hbm, m_ref, a_hbm, zbuf, obuf, sem_in, sem_out):
    n_g = zbuf.shape[1]
    half = zbuf.shape[2] // 2
    n1 = obuf.shape[2]

    def in_copies(step, slot):
        return [pltpu.make_async_copy(z_hbm.at[b, :, step * n_g + g, :], zbuf.at[slot, g, pl.ds(b * half, half)],
                                      sem_in.at[slot])
                for g in range(n_g) for b in range(2)]

    def out_copies(step, slot):
        return [pltpu.make_async_copy(obuf.at[slot, g], a_hbm.at[:, step * n_g + g, :], sem_out.at[slot])
                for g in range(n_g)]

    def compute(slot):
        for g in range(n_g):
            a = _dot(m_ref[g], zbuf[slot, g].astype(BF16))
            obuf[slot, g] = _pack_complex(a[:n1], a[n1:])

    _double_buffered(in_copies, out_copies, compute)


def _fft_s1_real_kernel(h_ref, t_ref, w3_ref, dl_ref, m_ref, a_hbm, s_ref, obuf, sem_out):
    n_g = obuf.shape[1]
    n1 = obuf.shape[2]
    j = pl.program_id(0)

    @pl.when(j == 0)
    def _():
        s_ref[...] = jnp.zeros_like(s_ref)

    def out_copies(step, slot):
        return [pltpu.make_async_copy(obuf.at[slot, g], a_hbm.at[:, step * n_g + g, :], sem_out.at[slot])
                for g in range(n_g)]

    def compute(slot):
        w3 = w3_ref[...]
        for g in range(n_g):
            taps = _dot_3pass(h_ref[g], w3) * jnp.exp(-t_ref[g] * dl_ref[...])
            if g == 0:
                row = lax.broadcasted_iota(jnp.int32, taps.shape, 0)
                col = lax.broadcasted_iota(jnp.int32, taps.shape, 1)
                drop = jnp.logical_and(jnp.logical_and(j == 0, row == 0), col >= taps.shape[1] // 2)
                taps = jnp.where(drop, 0.0, taps)
            s_ref[...] += jnp.sum(jnp.abs(taps), axis=0, keepdims=True)
            a = _dot(m_ref[g], taps.astype(BF16))
            obuf[slot, g] = _pack_complex(a[:n1], a[n1:])

    _double_buffered(lambda step, slot: [], out_copies, compute)


def _fft_mid_kernel(a_ref, af_ref, ab_ref, s_ref, f2_ref, g2_ref, c_ref):
    n2 = a_ref.shape[0]
    f2 = f2_ref[...]
    xf = _dot(f2, _unpack_complex(af_ref[...]))
    xb = _dot(f2, _unpack_complex(ab_ref[...]))
    s = s_ref[...]
    kr = (xf[:n2] + xb[:n2]) * s
    ki = (xf[n2:] - xb[n2:]) * s
    x = _dot(f2, _unpack_complex(a_ref[...]))
    xr, xi = x[:n2], x[n2:]
    y = jnp.concatenate([xr * kr - xi * ki, xr * ki + xi * kr], axis=0).astype(BF16)
    c = _dot(g2_ref[...], y)
    c_ref[...] = _pack_complex(c[:n2], c[n2:])


def _fft_out_kernel(c_hbm, m_ref, zv_hbm, x0_hbm, r_hbm, sk_ref, w_ref, o_hbm,
                    cbuf, zbuf, xbuf, rbuf, obuf, sem_in, sem_out):
    n_g = zbuf.shape[1]
    n1 = zbuf.shape[2]
    half = n1 // 2

    def in_copies(step, slot):
        cps = []
        for g in range(n_g):
            n2 = step * n_g + g
            cps.append(pltpu.make_async_copy(c_hbm.at[:, n2, :], cbuf.at[slot, g], sem_in.at[slot]))
            for b in range(2):
                rows = pl.ds(b * half, half)
                for src, dst in ((zv_hbm, zbuf), (x0_hbm, xbuf), (r_hbm, rbuf)):
                    cps.append(pltpu.make_async_copy(src.at[b, :, n2, :], dst.at[slot, g, rows], sem_in.at[slot]))
        return cps

    def out_copies(step, slot):
        return [pltpu.make_async_copy(obuf.at[slot, g, pl.ds(b * half, half)], o_hbm.at[b, :, step * n_g + g, :],
                                      sem_out.at[slot])
                for g in range(n_g) for b in range(2)]

    def compute(slot):
        ys = []
        for g in range(n_g):
            y = _dot(m_ref[g], _unpack_complex(cbuf[slot, g]))
            ys.append(((y + zbuf[slot, g] * sk_ref[...]) * xbuf[slot, g]).astype(BF16))
        proj = _dot(ys[0] if n_g == 1 else jnp.concatenate(ys, axis=0), w_ref[...])
        for g in range(n_g):
            obuf[slot, g] = rbuf[slot, g] + proj[g * n1:(g + 1) * n1]

    _double_buffered(in_copies, out_copies, compute)


def hyena_long_conv(zv, x0, x, w_out, filt, f_w3, skip_b, L):
    D = zv.shape[1]
    N = 2 * L
    N2 = DFT_N2
    N1 = N // N2
    half = N1 // 2
    n_g = max(1, min(N2, MXU_ROWS // N1))
    n_steps = N2 // n_g
    fwd, inv, fwd_real, f2, g2 = dft_tables(N1, N2)
    td2 = _tile(D, 2048)
    nd2 = D // td2
    zv4 = zv.reshape(2, half, N2, D)
    x04 = x0.reshape(2, half, N2, D)
    x4 = x.reshape(2, half, N2, D)
    any_spec = pl.BlockSpec(memory_space=pl.ANY)

    h, t, rates = filt
    H = h.shape[1]
    af, asum = pl.pallas_call(
        _fft_s1_real_kernel,
        grid=(n_steps,),
        in_specs=[pl.BlockSpec((n_g, half, H), lambda j: (j, 0, 0)),
                  pl.BlockSpec((n_g, half, 1), lambda j: (j, 0, 0)),
                  pl.BlockSpec((H, 2 * D), lambda j: (0, 0)),
                  pl.BlockSpec((1, 2 * D), lambda j: (0, 0)),
                  pl.BlockSpec((n_g, 2 * N1, half), lambda j: (j, 0, 0))],
        out_specs=[any_spec, pl.BlockSpec((1, 2 * D), lambda j: (0, 0))],
        out_shape=[jax.ShapeDtypeStruct((N1, N2, 2 * D), jnp.uint32), jax.ShapeDtypeStruct((1, 2 * D), F32)],
        scratch_shapes=[pltpu.VMEM((2, n_g, N1, 2 * D), jnp.uint32), pltpu.SemaphoreType.DMA((2,))],
        compiler_params=_params(("arbitrary",)),
        name="filter_dft_stage1",
    )(h.reshape(N2, half, H), t.reshape(N2, half, 1), f_w3, rates, fwd_real)
    scale = 1.0 / ((asum[:, :D] + asum[:, D:]) * N)

    a = pl.pallas_call(
        _fft_s1_kernel,
        grid=(n_steps,),
        in_specs=[any_spec,
                  pl.BlockSpec((n_g, 2 * N1, N1), lambda j: (j, 0, 0))],
        out_specs=any_spec,
        out_shape=jax.ShapeDtypeStruct((N1, N2, D), jnp.uint32),
        scratch_shapes=[pltpu.VMEM((2, n_g, N1, D), F32), pltpu.VMEM((2, n_g, N1, D), jnp.uint32),
                        pltpu.SemaphoreType.DMA((2,)), pltpu.SemaphoreType.DMA((2,))],
        compiler_params=_params(("arbitrary",)),
        name="conv_dft_stage1",
    )(zv4, fwd)

    c = pl.pallas_call(
        _fft_mid_kernel,
        grid=(N1, nd2),
        in_specs=[pl.BlockSpec((None, N2, td2), lambda k, j: (k, 0, j)),
                  pl.BlockSpec((None, N2, td2), lambda k, j: (k, 0, j)),
                  pl.BlockSpec((None, N2, td2), lambda k, j: (k, 0, j + nd2)),
                  pl.BlockSpec((1, td2), lambda k, j: (0, j)),
                  pl.BlockSpec((2 * N2, 2 * N2), lambda k, j: (0, 0)),
                  pl.BlockSpec((2 * N2, 2 * N2), lambda k, j: (0, 0))],
        out_specs=pl.BlockSpec((None, N2, td2), lambda k, j: (k, 0, j)),
        out_shape=jax.ShapeDtypeStruct((N1, N2, D), jnp.uint32),
        compiler_params=_params(("parallel", "parallel")),
        name="conv_dft_stage2",
    )(a, af, af, scale, f2, g2)

    def buf(dtype):
        return pltpu.VMEM((2, n_g, N1, D), dtype)

    o = pl.pallas_call(
        _fft_out_kernel,
        grid=(n_steps,),
        in_specs=[any_spec,
                  pl.BlockSpec((n_g, N1, 2 * N1), lambda j: (j, 0, 0)),
                  any_spec, any_spec, any_spec,
                  pl.BlockSpec((1, D), lambda j: (0, 0)),
                  pl.BlockSpec(w_out.shape, lambda j: (0, 0))],
        out_specs=any_spec,
        out_shape=jax.ShapeDtypeStruct((2, half, N2, D), F32),
        scratch_shapes=[buf(jnp.uint32), buf(F32), buf(F32), buf(F32), buf(F32),
                        pltpu.SemaphoreType.DMA((2,)), pltpu.SemaphoreType.DMA((2,))],
        compiler_params=_params(("arbitrary",)),
        name="conv_dft_inverse_outproj",
    )(c, inv, zv4, x04, x4, skip_b, w_out)
    return o.reshape(2 * L, D)


VT_ROWS = HEAD_DIM + BF16_SUBLANES


def _qkv_kernel(x_ref, g_ref, w_ref, gain_ref, cos_ref, sin_ref, o_ref, va_ref, vt_ref, hn_ref, *, n_rope_tiles):
    j = pl.program_id(1)

    @pl.when(j == 0)
    def _():
        hn_ref[...] = _norm_rows(x_ref[...], g_ref[...]).astype(BF16)

    u = _dot(hn_ref[...], w_ref[...])
    tn = u.shape[1]

    @pl.when(j < n_rope_tiles)
    def _():
        gain = gain_ref[...]
        cos = cos_ref[...]
        sin = sin_ref[...]
        ones = jnp.ones((HEAD_DIM, HEAD_DIM), BF16)
        for hh in range(tn // HEAD_DIM):
            xh = u[:, hh * HEAD_DIM:(hh + 1) * HEAD_DIM]
            sq = xh * xh
            sq_hi = sq.astype(BF16)
            sq_lo = (sq - sq_hi.astype(F32)).astype(BF16)
            ms = (_dot(sq_hi, ones) + _dot(sq_lo, ones)) * (1.0 / HEAD_DIM)
            y = xh * lax.rsqrt(ms + RMS_EPS) * gain
            sw = pltpu.roll(y, HEAD_DIM // 2, 1)
            o_ref[:, hh * HEAD_DIM:(hh + 1) * HEAD_DIM] = (y * cos + sw * sin).astype(o_ref.dtype)

    @pl.when(j >= n_rope_tiles)
    def _():
        o_ref[...] = u.astype(o_ref.dtype)
        ones = jnp.ones((u.shape[0], HEAD_DIM), va_ref.dtype)
        ones_t = jnp.ones((VT_ROWS - HEAD_DIM, u.shape[0]), vt_ref.dtype)
        for hh in range(tn // HEAD_DIM):
            vh = u[:, hh * HEAD_DIM:(hh + 1) * HEAD_DIM]
            va_ref[:, 2 * hh * HEAD_DIM:(2 * hh + 1) * HEAD_DIM] = vh.astype(va_ref.dtype)
            va_ref[:, (2 * hh + 1) * HEAD_DIM:(2 * hh + 2) * HEAD_DIM] = ones
            vt_ref[hh, :HEAD_DIM, :] = vh.T.astype(vt_ref.dtype)
            vt_ref[hh, HEAD_DIM:, :] = ones_t


def rope_tables(L):
    rows = L // GRID_W
    r, c = jnp.meshgrid(jnp.arange(rows, dtype=F32), jnp.arange(GRID_W, dtype=F32), indexing="ij")
    r = r.reshape(L)
    c = c.reshape(L)
    axis_dim = HEAD_DIM // 2
    inv = 1.0 / (ROPE_THETA ** (jnp.arange(0, axis_dim, 2, dtype=F32) / axis_dim))
    ang = jnp.concatenate([r[:, None] * inv[None], c[:, None] * inv[None]], axis=-1)
    cos = jnp.concatenate([jnp.cos(ang), jnp.cos(ang)], axis=1)
    sin = jnp.concatenate([-jnp.sin(ang), jnp.sin(ang)], axis=1)
    return cos, sin


def qkv_project(x, g, w_qkv, q_gain, k_gain, seq_len, *, tm=512):
    T, D = x.shape
    n_out = w_qkv.shape[1]
    nk = N_KV_HEADS * HEAD_DIM
    nq = n_out - 2 * nk
    tn = nk
    tm = _tile(seq_len, tm)
    n_rope = (nq + nk) // tn
    def deinterleave(a):
        lead = a.shape[:-1]
        a = a.reshape(lead + (a.shape[-1] // HEAD_DIM, HEAD_DIM // 2, 2))
        return jnp.swapaxes(a, -1, -2).reshape(lead + (-1,))

    w_qkv = jnp.concatenate([deinterleave(w_qkv[:, :nq + nk]), w_qkv[:, nq + nk:]], axis=1)
    q_gain = deinterleave(q_gain)
    k_gain = deinterleave(k_gain)
    gains = jnp.concatenate([jnp.tile((q_gain * SCORE_SCALE)[None, None, :], (nq // tn, 1, 1)),
                             k_gain[None, None, :],
                             jnp.ones((1, 1, HEAD_DIM), F32)], axis=0)
    cos, sin = rope_tables(seq_len)
    spt = seq_len // tm
    kern = functools.partial(_qkv_kernel, n_rope_tiles=n_rope)
    return pl.pallas_call(
        kern,
        grid=(T // tm, n_out // tn),
        in_specs=[pl.BlockSpec((tm, D), lambda i, j: (i, 0)),
                  pl.BlockSpec((1, D), lambda i, j: (0, 0)),
                  pl.BlockSpec((D, tn), lambda i, j: (0, j)),
                  pl.BlockSpec((None, 1, HEAD_DIM), lambda i, j: (j, 0, 0)),
                  pl.BlockSpec((tm, HEAD_DIM), lambda i, j: (i % spt, 0)),
                  pl.BlockSpec((tm, HEAD_DIM), lambda i, j: (i % spt, 0))],
        out_specs=[pl.BlockSpec((tm, tn), lambda i, j: (i, j)),
                   pl.BlockSpec((tm, 2 * nk), lambda i, j: (i, 0)),
                   pl.BlockSpec((None, N_KV_HEADS, VT_ROWS, tm), lambda i, j: (i // spt, 0, 0, i % spt))],
        out_shape=[jax.ShapeDtypeStruct((T, n_out), BF16), jax.ShapeDtypeStruct((T, 2 * nk), BF16),
                   jax.ShapeDtypeStruct((T // seq_len, N_KV_HEADS, VT_ROWS, seq_len), BF16)],
        scratch_shapes=[pltpu.VMEM((tm, D), BF16)],
        compiler_params=_params(("parallel", "arbitrary")),
        name="qkv_project",
    )(x, g, w_qkv, gains, cos, sin)


def _flash_kernel(q_ref, k_ref, v_ref, o_ref, qs_ref, s0_ref, s1_ref, p0_ref, p1_ref, a0_ref, a1_ref,
                  m_ref, acc_ref, *, group, tq, tk, nk):
    s_refs = (s0_ref, s1_ref)
    p_refs = (p0_ref, p1_ref)
    a_refs = (a0_ref, a1_ref)
    for gq in range(group):
        qs_ref[gq * tq:(gq + 1) * tq, :] = q_ref[:, gq * HEAD_DIM:(gq + 1) * HEAD_DIM]
    m_ref[...] = jnp.full_like(m_ref, -jnp.inf)
    acc_ref[...] = jnp.zeros_like(acc_ref)

    def rows_of(j):
        if isinstance(j, int):
            return pl.ds(j * tk, tk)
        return pl.ds(pl.multiple_of(j * tk, tk), tk)

    def scores(j, slot):
        kb = k_ref[rows_of(j), :]
        s_refs[slot][...] = lax.dot_general(qs_ref[...], kb, (((1,), (1,)), ((), ())),
                                            preferred_element_type=F32)

    def softmax(slot):
        s = s_refs[slot][...]
        m_prev = m_ref[...]
        m_new = jnp.maximum(m_prev, jnp.max(s, axis=-1, keepdims=True))
        a_refs[slot][...] = jnp.exp2(m_prev - m_new)
        p_refs[slot][...] = jnp.exp2(s - jnp.tile(m_new, (1, tk // LANES))).astype(BF16)
        m_ref[...] = m_new

    def accumulate(j, slot):
        pv = _dot(p_refs[slot][...], v_ref[rows_of(j), :])
        acc_ref[...] = acc_ref[...] * jnp.tile(a_refs[slot][...], (1, 2)) + pv

    def step(j, slot, first, last):
        if not last:
            scores(j + 1, 1 - slot)
        softmax(slot)
        if not first:
            accumulate(j - 1, 1 - slot)

    scores(0, 0)
    if nk >= 4 and nk % 2 == 0:
        step(0, 0, True, False)

        def pair(t, c):
            j = 2 * t + 1
            step(j, 1, False, False)
            step(j + 1, 0, False, False)
            return c

        lax.fori_loop(0, (nk - 2) // 2, pair, 0)
        step(nk - 1, 1, False, True)
    else:
        for j in range(nk):
            step(j, j % 2, j == 0, j == nk - 1)
    accumulate(nk - 1, (nk - 1) % 2)

    acc = acc_ref[...]
    o = acc[:, :HEAD_DIM] / acc[:, HEAD_DIM:]
    for gq in range(group):
        o_ref[:, gq * HEAD_DIM:(gq + 1) * HEAD_DIM] = o[gq * tq:(gq + 1) * tq].astype(o_ref.dtype)


def _flash_bounded_kernel(q_ref, k_ref, vt_ref, o_ref, qt_ref, acc_ref, *, group, tq, tk, nk, unroll):
    for gq in range(group):
        qt_ref[:, gq * tq:(gq + 1) * tq] = q_ref[:, gq * HEAD_DIM:(gq + 1) * HEAD_DIM].astype(F32).T.astype(BF16)
    acc_ref[...] = jnp.zeros_like(acc_ref)

    def block(j, c):
        rows = pl.ds(pl.multiple_of(j * tk, tk), tk)
        s_t = _dot(k_ref[rows, :], qt_ref[...])
        acc_ref[...] += _dot(vt_ref[:, rows], jnp.exp2(s_t).astype(BF16))
        return c

    lax.fori_loop(0, nk, block, 0, unroll=unroll)
    acc = acc_ref[...]
    o_t = acc[:HEAD_DIM] / acc[HEAD_DIM:HEAD_DIM + 1]
    for gq in range(group):
        o_ref[:, gq * HEAD_DIM:(gq + 1) * HEAD_DIM] = o_t[:, gq * tq:(gq + 1) * tq].T.astype(o_ref.dtype)


SAFE_SCORE_BOUND = 80.0


def flash_attention(qkv, v_aug, v_t, score_bound, B, L, n_heads, *, tq=256, tk=512):
    group = n_heads // N_KV_HEADS
    tq = _tile(L, tq)
    tk = _tile(L, tk)
    nk = L // tk
    qw = group * HEAD_DIM
    k_col = n_heads
    rows = group * tq
    q_spec = pl.BlockSpec((None, tq, qw), lambda b, h, qi: (b, qi, h))
    k_spec = pl.BlockSpec((None, L, HEAD_DIM), lambda b, h, qi: (b, 0, k_col + h))
    common = dict(
        grid=(B, N_KV_HEADS, L // tq),
        out_specs=pl.BlockSpec((None, tq, qw), lambda b, h, qi: (b, qi, h)),
        out_shape=jax.ShapeDtypeStruct((B, L, n_heads * HEAD_DIM), BF16),
        compiler_params=_params(("parallel", "parallel", "parallel")),
    )
    general = pl.pallas_call(
        functools.partial(_flash_kernel, group=group, tq=tq, tk=tk, nk=nk),
        in_specs=[q_spec, k_spec, pl.BlockSpec((None, L, 2 * HEAD_DIM), lambda b, h, qi: (b, 0, h))],
        scratch_shapes=[pltpu.VMEM((rows, HEAD_DIM), BF16),
                        pltpu.VMEM((rows, tk), F32), pltpu.VMEM((rows, tk), F32),
                        pltpu.VMEM((rows, tk), BF16), pltpu.VMEM((rows, tk), BF16),
                        pltpu.VMEM((rows, LANES), F32), pltpu.VMEM((rows, LANES), F32),
                        pltpu.VMEM((rows, LANES), F32),
                        pltpu.VMEM((rows, 2 * HEAD_DIM), F32)],
        name="flash_attention", **common)
    bounded = pl.pallas_call(
        functools.partial(_flash_bounded_kernel, group=group, tq=tq, tk=tk, nk=nk,
                          unroll=math.gcd(nk, 8)),
        in_specs=[q_spec, k_spec, pl.BlockSpec((None, None, VT_ROWS, L), lambda b, h, qi: (b, h, 0, 0))],
        scratch_shapes=[pltpu.VMEM((HEAD_DIM, rows), BF16),
                        pltpu.VMEM((VT_ROWS, rows), F32)],
        name="flash_attention_bounded", **common)
    return lax.cond(score_bound <= SAFE_SCORE_BOUND,
                    lambda ops: bounded(ops[0], ops[0], ops[2]),
                    lambda ops: general(ops[0], ops[0], ops[1]),
                    (qkv, v_aug, v_t))


def _pack_bf16_pairs(x):
    half = x.shape[1] // 2
    bits = lax.bitcast_convert_type(x.astype(BF16).astype(F32), jnp.uint32)
    return (bits[:, :half] >> 16) | (bits[:, half:] & jnp.uint32(0xFFFF0000))


def _unpack_bf16_pairs(p):
    lo = lax.bitcast_convert_type(p << 16, F32).astype(BF16)
    hi = lax.bitcast_convert_type(p & jnp.uint32(0xFFFF0000), F32).astype(BF16)
    return lo, hi


def _router_kernel(xa_ref, xb_ref, g_ref, rw_ref, h_ref, info_ref, cnt_ref, *, n_a_tiles):
    i = pl.program_id(0)

    def route(x_ref):
        hf = _norm_rows(x_ref[...], g_ref[...])
        h_ref[...] = _pack_bf16_pairs(hf)
        h_hi = hf.astype(BF16)
        h_lo = (hf - h_hi.astype(F32)).astype(BF16)
        rw = rw_ref[...]
        hw = _dot(h_hi, rw)
        logits = hw[:, :LANES] + hw[:, LANES:] + _dot(h_lo, rw[:, :LANES])
        shape = logits.shape
        lane = lax.broadcasted_iota(jnp.int32, shape, 1)
        lanef = lane.astype(F32)
        neg = -jnp.inf
        big = float(LANES)
        is_g = lane < N_GROUPS
        gl = jnp.where(is_g, logits, neg)
        gmax = jnp.max(gl, axis=-1, keepdims=True)
        gidx = jnp.min(jnp.where(gl == gmax, lanef, big), axis=-1, keepdims=True)
        p_g = 1.0 / jnp.sum(jnp.where(is_g, jnp.exp(logits - gmax), 0.0), axis=-1, keepdims=True)
        lo = N_GROUPS + gidx * EXPERTS_PER_GROUP
        el = jnp.where(lanef >= lo, jnp.where(lanef < lo + EXPERTS_PER_GROUP, logits, neg), neg)
        v1 = jnp.max(el, axis=-1, keepdims=True)
        i1 = jnp.min(jnp.where(el == v1, lanef, big), axis=-1, keepdims=True)
        el2 = jnp.where(lanef == i1, neg, el)
        v2 = jnp.max(el2, axis=-1, keepdims=True)
        i2 = jnp.min(jnp.where(el2 == v2, lanef, big), axis=-1, keepdims=True)
        t = jnp.exp(v2 - v1)
        w1 = p_g / (1.0 + t)
        w2 = p_g * t / (1.0 + t)
        e1 = i1 - N_GROUPS
        e2 = i2 - N_GROUPS
        oh1 = jnp.where(lanef == e1, 1.0, 0.0)
        oh2 = jnp.where(lanef == e2, 1.0, 0.0)
        oh = oh1 + oh2
        tm = shape[0]
        tri = jnp.where(lax.broadcasted_iota(jnp.int32, (tm, tm), 1) < lax.broadcasted_iota(jnp.int32, (tm, tm), 0),
                        1.0, 0.0).astype(BF16)
        before = _dot(tri, oh.astype(BF16)) + cnt_ref[...]
        r1 = jnp.sum(before * oh1, axis=-1, keepdims=True)
        r2 = jnp.sum(before * oh2, axis=-1, keepdims=True)
        cnt_ref[...] += jnp.sum(oh, axis=0, keepdims=True)
        vals = (e1, e2, w1, w2, r1, r2)
        info = jnp.zeros(shape, F32)
        for idx, val in enumerate(vals):
            info = jnp.where(lane == idx, val, info)
        info_ref[...] = info

    @pl.when(i == 0)
    def _():
        cnt_ref[...] = jnp.zeros_like(cnt_ref)

    @pl.when(i < n_a_tiles)
    def _():
        route(xa_ref)

    @pl.when(i >= n_a_tiles)
    def _():
        route(xb_ref)


def moe_route(xa, xb, g, router_group, router_expert, *, tm=512):
    Ta, D = xa.shape
    Tb = xb.shape[0]
    tm = _tile(math.gcd(Ta, Tb), tm)
    na, nb = Ta // tm, Tb // tm
    n_e = N_GROUPS * EXPERTS_PER_GROUP
    rw = jnp.concatenate([router_group, jnp.transpose(router_expert, (1, 0, 2)).reshape(D, n_e)], axis=1)
    rw = jnp.pad(rw, ((0, 0), (0, LANES - rw.shape[1])))
    rw_hi = rw.astype(BF16)
    rw = jnp.concatenate([rw_hi, (rw - rw_hi.astype(F32)).astype(BF16)], axis=1)
    kern = functools.partial(_router_kernel, n_a_tiles=na)
    return pl.pallas_call(
        kern,
        grid=(na + nb,),
        in_specs=[pl.BlockSpec((tm, D), lambda i: (jnp.minimum(i, na - 1), 0)),
                  pl.BlockSpec((tm, D), lambda i: (jnp.maximum(i - na, 0), 0)),
                  pl.BlockSpec((1, D), lambda i: (0, 0)),
                  pl.BlockSpec((D, 2 * LANES), lambda i: (0, 0))],
        out_specs=[pl.BlockSpec((tm, D // 2), lambda i: (i, 0)),
                   pl.BlockSpec((tm, LANES), lambda i: (i, 0)),
                   pl.BlockSpec((1, LANES), lambda i: (0, 0))],
        out_shape=[jax.ShapeDtypeStruct((Ta + Tb, D // 2), jnp.uint32),
                   jax.ShapeDtypeStruct((Ta + Tb, LANES), F32),
                   jax.ShapeDtypeStruct((1, LANES), F32)],
        compiler_params=_params(("arbitrary",)),
        name="moe_route",
    )(xa, xb, g, rw)


def moe_plan(info, cnt, tm, n_experts):
    T = info.shape[0]
    n_tiles = (2 * T) // tm + n_experts
    counts = cnt[0, :n_experts].astype(jnp.int32)
    padded = ((counts + tm - 1) // tm) * tm
    ends = jnp.cumsum(padded)
    off = ends - padded
    e = info[:, 0:2].T.astype(jnp.int32)
    rank = info[:, 4:6].T.astype(jnp.int32)
    k = jnp.arange(n_experts, dtype=jnp.int32)[:, None, None]
    pos = rank + jnp.sum(jnp.where(e[None] == k, off[:, None, None], 0), axis=0)
    pos = pos.T.reshape(-1)
    tile_start = jnp.arange(n_tiles, dtype=jnp.int32) * tm
    tile_e = jnp.sum((tile_start[:, None] >= ends[None, :]).astype(jnp.int32), axis=1)
    tile_e = jnp.minimum(tile_e, n_experts - 1)
    tile_valid = jnp.clip(counts[tile_e] - (tile_start - off[tile_e]), 0, tm).astype(jnp.int32)
    n_used = (ends[-1:] // tm).astype(jnp.int32)
    expert_last = jnp.where(padded > 0, ends - tm, -1)
    tail_tile = n_used[0] + jnp.arange(n_experts, dtype=jnp.int32)
    tail = jnp.where(tail_tile < n_tiles, tail_tile * tm, -1)
    zero_tiles = jnp.concatenate([expert_last, tail]).astype(jnp.int32)[None, :]
    return pos, tile_e, tile_valid, n_used, zero_tiles


DMA_UNROLL = 8


def _dispatch_kernel(pos_ref, ztile_ref, h_ref, hs_hbm, zbuf, sem, zsem):
    tm = h_ref.shape[0]

    @pl.when(pl.program_id(0) == 0)
    def _():
        zbuf[...] = jnp.zeros_like(zbuf)
        zrows = zbuf.shape[0]

        def zero_copy(k):
            return pltpu.make_async_copy(zbuf, hs_hbm.at[pl.ds(pl.multiple_of(ztile_ref[0, k], zrows), zrows)], zsem)

        for k in range(ztile_ref.shape[1]):
            @pl.when(ztile_ref[0, k] >= 0)
            def _():
                zero_copy(k).start()
        for k in range(ztile_ref.shape[1]):
            @pl.when(ztile_ref[0, k] >= 0)
            def _():
                zero_copy(k).wait()

    def row_copy(r, slot):
        return pltpu.make_async_copy(h_ref.at[pl.ds(r, 1)], hs_hbm.at[pl.ds(pos_ref[0, 0, 2 * r + slot], 1)], sem)

    def start(r, c):
        row_copy(r, 0).start()
        row_copy(r, 1).start()
        return c

    def wait(r, c):
        row_copy(r, 0).wait()
        row_copy(r, 1).wait()
        return c

    lax.fori_loop(0, tm, start, 0, unroll=DMA_UNROLL)
    lax.fori_loop(0, tm, wait, 0, unroll=DMA_UNROLL)


def moe_dispatch(h, pos, zero_tiles, n_sorted_rows, tile_rows, *, tm=512):
    T, C = h.shape
    tm = _tile(T, tm)
    return pl.pallas_call(
        _dispatch_kernel,
        grid=(T // tm,),
        in_specs=[pl.BlockSpec((1, 1, 2 * tm), lambda i: (i, 0, 0), memory_space=pltpu.SMEM),
                  pl.BlockSpec(zero_tiles.shape, lambda i: (0, 0), memory_space=pltpu.SMEM),
                  pl.BlockSpec((tm, C), lambda i: (i, 0))],
        out_specs=pl.BlockSpec(memory_space=pl.ANY),
        out_shape=jax.ShapeDtypeStruct((n_sorted_rows, C), h.dtype),
        scratch_shapes=[pltpu.VMEM((tile_rows, C), h.dtype), pltpu.SemaphoreType.DMA, pltpu.SemaphoreType.DMA],
        compiler_params=_params(("arbitrary",)),
        name="moe_dispatch",
    )(pos.reshape(T // tm, 1, 2 * tm), zero_tiles, h)


def _expert_kernel(te_ref, tv_ref, nu_ref, x_ref, wg_ref, wu_ref, wd_ref, y_ref):
    del te_ref, nu_ref
    valid = tv_ref[pl.program_id(0)]

    @pl.when(valid == 0)
    def _():
        y_ref[...] = jnp.zeros_like(y_ref)

    @pl.when(valid > 0)
    def _():
        x_lo, x_hi = _unpack_bf16_pairs(x_ref[...])
        half = x_ref.shape[1]
        gate = (_dot(x_lo, wg_ref[:half, :].astype(BF16)) + _dot(x_hi, wg_ref[half:, :].astype(BF16)))
        up = (_dot(x_lo, wu_ref[:half, :].astype(BF16)) + _dot(x_hi, wu_ref[half:, :].astype(BF16)))
        a = (gate * jax.nn.sigmoid(gate) * up).astype(BF16)
        y_ref[...] = _dot(a, wd_ref[...].astype(BF16))


def moe_experts(hs, tile_e, tile_valid, n_used, w_gate, w_up, w_down, layer, tm):
    P = hs.shape[0]
    D = w_gate.shape[2]
    d_e = w_gate.shape[-1]

    def row_map(i, te, tv, nu):
        return (jnp.minimum(i, nu[0] - 1), 0)

    def out_map(i, te, tv, nu):
        return (i, 0)

    def w_map(i, te, tv, nu):
        return (layer, te[i], 0, 0)

    grid_spec = pltpu.PrefetchScalarGridSpec(
        num_scalar_prefetch=3,
        grid=(P // tm,),
        in_specs=[pl.BlockSpec((tm, D // 2), row_map),
                  pl.BlockSpec((None, None, D, d_e), w_map),
                  pl.BlockSpec((None, None, D, d_e), w_map),
                  pl.BlockSpec((None, None, d_e, D), w_map)],
        out_specs=pl.BlockSpec((tm, D), out_map),
    )
    return pl.pallas_call(
        _expert_kernel,
        grid_spec=grid_spec,
        out_shape=jax.ShapeDtypeStruct((P, D), F32),
        compiler_params=_params(("arbitrary",)),
        name="moe_experts",
    )(tile_e, tile_valid, n_used, hs, w_gate, w_up, w_down)


def _combine_kernel(pos_ref, posn_ref, x_ref, info_ref, g_ref, y_hbm, o_ref, ybuf, sem, *, final_norm):
    tm = x_ref.shape[0]
    i = pl.program_id(0)
    buf = i % 2

    def row_copy(p_ref, b, r, slot):
        return pltpu.make_async_copy(y_hbm.at[pl.ds(p_ref[0, 0, 2 * r + slot], 1)],
                                     ybuf.at[b, slot, pl.ds(r, 1)], sem.at[b])

    def start_tile(p_ref, b):
        def body(r, c):
            row_copy(p_ref, b, r, 0).start()
            row_copy(p_ref, b, r, 1).start()
            return c
        lax.fori_loop(0, tm, body, 0, unroll=DMA_UNROLL)

    @pl.when(i == 0)
    def _():
        start_tile(pos_ref, 0)

    @pl.when(i + 1 < pl.num_programs(0))
    def _():
        start_tile(posn_ref, 1 - buf)

    def wait(r, c):
        row_copy(pos_ref, buf, r, 0).wait()
        row_copy(pos_ref, buf, r, 1).wait()
        return c

    lax.fori_loop(0, tm, wait, 0, unroll=DMA_UNROLL)
    info = info_ref[...]
    y = info[:, 2:3] * ybuf[buf, 0] + info[:, 3:4] * ybuf[buf, 1]
    x = x_ref[...] + y
    if final_norm:
        x = _norm_rows(x, g_ref[...])
    o_ref[...] = x


def moe_combine(x, y, pos, info, row_off, g_final, final_norm, *, tm=256):
    Tg, D = x.shape
    T = info.shape[0]
    tm = _tile(math.gcd(Tg, row_off) if row_off else Tg, tm)
    ob = row_off // tm
    kern = functools.partial(_combine_kernel, final_norm=final_norm)
    n = Tg // tm
    pos3 = pos.reshape(T // tm, 1, 2 * tm)
    return pl.pallas_call(
        kern,
        grid=(n,),
        in_specs=[pl.BlockSpec((1, 1, 2 * tm), lambda i: (i + ob, 0, 0), memory_space=pltpu.SMEM),
                  pl.BlockSpec((1, 1, 2 * tm), lambda i: (jnp.minimum(i + 1, n - 1) + ob, 0, 0),
                               memory_space=pltpu.SMEM),
                  pl.BlockSpec((tm, D), lambda i: (i, 0)),
                  pl.BlockSpec((tm, LANES), lambda i: (i + ob, 0)),
                  pl.BlockSpec((1, D), lambda i: (0, 0)),
                  pl.BlockSpec(memory_space=pl.ANY)],
        out_specs=pl.BlockSpec((tm, D), lambda i: (i, 0)),
        out_shape=jax.ShapeDtypeStruct((Tg, D), F32),
        scratch_shapes=[pltpu.VMEM((2, 2, tm, D), F32), pltpu.SemaphoreType.DMA((2,))],
        compiler_params=_params(("arbitrary",)),
        name="moe_combine",
    )(pos3, pos3, x, info, g_final, y)


MOE_TILE = 512


def hier_moe_block(xs, g, router_group, router_expert, w_gate, w_up, w_down, layer, g_final, final_norm):
    xa, xb = xs
    n_experts = w_gate.shape[1]
    h, info, cnt = moe_route(xa, xb, g, router_group, router_expert)
    T = h.shape[0]
    tm = _tile(2 * T, MOE_TILE)
    pos, tile_e, tile_valid, n_used, zero_tiles = moe_plan(info, cnt, tm, n_experts)
    hs = moe_dispatch(h, pos, zero_tiles, tile_e.shape[0] * tm, tm)
    y = moe_experts(hs, tile_e, tile_valid, n_used, w_gate, w_up, w_down, layer, tm)
    oa = moe_combine(xa, y, pos, info, 0, g_final, final_norm)
    ob = moe_combine(xb, y, pos, info, xa.shape[0], g_final, final_norm)
    return oa, ob


def hyena_block(x, L, g, w_in, conv_w, conv_b, f_w1, f_b1, f_win, f_bin, f_freq, f_w3, skip_b, w_out):
    D = x.shape[1]
    x0, zv = hyena_inproj(x, g, w_in, conv_w, conv_b[None, :], L)
    filt = hyena_filter_features(L, w_out.shape[0], f_w1, f_b1, f_win, f_bin, f_freq)
    return hyena_long_conv(zv, x0, x, w_out, filt, f_w3, skip_b[None, :], L)


def attention_block(x, B, L, g, w_qkv, q_gain, k_gain, w_o):
    D = x.shape[1]
    n_heads = w_o.shape[0] // HEAD_DIM
    qkv, v_aug, v_t = qkv_project(x, g, w_qkv, q_gain, k_gain, L)
    score_bound = 1.02 * HEAD_DIM * SCORE_SCALE * jnp.max(jnp.abs(q_gain)) * jnp.max(jnp.abs(k_gain))
    o = flash_attention(qkv.reshape(B, L, qkv.shape[1]), v_aug.reshape(B, L, v_aug.shape[1]), v_t, score_bound,
                        B, L, n_heads)
    return matmul_residual(o.reshape(B * L, n_heads * HEAD_DIM), w_o, x)


def kernel(x_prompt, x_sample, norm_mix, norm_ffn, norm_final, hy_w_in, hy_conv_w, hy_conv_b, hy_f_w1, hy_f_b1, hy_f_win, hy_f_bin, hy_f_freq, hy_f_w3, hy_skip_b, hy_w_out, at_w_qkv, at_q_gain, at_k_gain, at_w_o, moe_router_group, moe_router_expert, moe_w_gate, moe_w_up, moe_w_down):
    D = x_prompt.shape[-1]
    depth = norm_mix.shape[0]
    shapes = [x_prompt.shape, x_sample.shape]
    for s in shapes:
        assert s[0] == 2, "the long convolution packs exactly two sequences per group"
    xs = [x_prompt.reshape(-1, D), x_sample.reshape(-1, D)]
    hy_w_in_b = hy_w_in.astype(BF16)
    hy_w_out_b = hy_w_out.astype(BF16)
    at_w_qkv_b = at_w_qkv.astype(BF16)
    at_w_o_b = at_w_o.astype(BF16)
    g_final = norm_final[None, :]
    for i in range(depth):
        j = i // N_MIXERS
        g = norm_mix[i][None, :]
        if i % N_MIXERS == 0:
            xs = [hyena_block(x, s[1], g, hy_w_in_b[j], hy_conv_w[j], hy_conv_b[j], hy_f_w1[j], hy_f_b1[j],
                              hy_f_win[j], hy_f_bin[j], hy_f_freq[j], hy_f_w3[j], hy_skip_b[j], hy_w_out_b[j])
                  for x, s in zip(xs, shapes)]
        else:
            xs = [attention_block(x, s[0], s[1], g, at_w_qkv_b[j], at_q_gain[j], at_k_gain[j], at_w_o_b[j])
                  for x, s in zip(xs, shapes)]
        xs = hier_moe_block(xs, norm_ffn[i][None, :], moe_router_group[i], moe_router_expert[i],
                            moe_w_gate, moe_w_up, moe_w_down, i, g_final, i == depth - 1)
    return (xs[0].reshape(shapes[0]), xs[1].reshape(shapes[1]))
```

```python
import functools
import math

import jax
import jax.numpy as jnp
import numpy as np
from jax import lax
from jax.experimental import pallas as pl
from jax.experimental.pallas import tpu as pltpu

RMS_EPS = 1e-6
GRID_W = 64
EMB_DIM = 33
DECAY_TARGET = 1e-2
FAST_DECAY_PCT = 0.3
SLOW_DECAY_PCT = 1.5
MIN_DECAY = math.log(DECAY_TARGET) / FAST_DECAY_PCT
MAX_DECAY = math.log(DECAY_TARGET) / SLOW_DECAY_PCT
HEAD_DIM = 128
SCORE_SCALE = HEAD_DIM ** -0.5 * math.log2(math.e)
N_KV_HEADS = 4
ROPE_THETA = 10000.0
N_GROUPS = 4
EXPERTS_PER_GROUP = 8
N_MIXERS = 2

LANES = 128
BF16_SUBLANES = 16
VMEM_LIMIT_BYTES = 52 * 1024 * 1024
DFT_N2 = 128
MXU_ROWS = 256

BF16 = jnp.bfloat16
F32 = jnp.float32


def _params(sem):
    return pltpu.CompilerParams(dimension_semantics=sem, vmem_limit_bytes=VMEM_LIMIT_BYTES)


def _dot(a, b):
    return jnp.dot(a, b, preferred_element_type=F32)


def _dot_3pass(a, b):
    a_hi = a.astype(BF16)
    a_lo = (a - a_hi.astype(F32)).astype(BF16)
    b_hi = b.astype(BF16)
    b_lo = (b - b_hi.astype(F32)).astype(BF16)
    return _dot(a_hi, b_hi) + _dot(a_hi, b_lo) + _dot(a_lo, b_hi)


def _norm_rows(x, g):
    ms = jnp.mean(x * x, axis=-1, keepdims=True)
    return x * lax.rsqrt(ms + RMS_EPS) * g


def _tile(n, pref):
    return pref if n % pref == 0 else n


def _matmul_res_kernel(a_ref, w_ref, r_ref, o_ref):
    o_ref[...] = r_ref[...] + _dot(a_ref[...].astype(BF16), w_ref[...])


def matmul_residual(a, w, res, *, tm=1024, tn=512):
    T, K = a.shape
    N = w.shape[1]
    tm = _tile(T, tm)
    tn = _tile(N, tn)
    return pl.pallas_call(
        _matmul_res_kernel,
        grid=(T // tm, N // tn),
        in_specs=[pl.BlockSpec((tm, K), lambda i, j: (i, 0)),
                  pl.BlockSpec((K, tn), lambda i, j: (0, j)),
                  pl.BlockSpec((tm, tn), lambda i, j: (i, j))],
        out_specs=pl.BlockSpec((tm, tn), lambda i, j: (i, j)),
        out_shape=jax.ShapeDtypeStruct((T, N), F32),
        compiler_params=_params(("parallel", "arbitrary")),
        name="matmul_residual",
    )(a, w, res)


HALO = BF16_SUBLANES


def _hy_inproj_kernel(x_ref, xp_ref, xn_ref, g_ref, w0_ref, w1_ref, w2_ref,
                      cw0_ref, cw1_ref, cw2_ref, cb0_ref, cb1_ref, cb2_ref,
                      x0_out, zv_out, hn_ref, *, tm, tiles_per_seq):
    i = pl.program_id(0)
    j = pl.program_id(1)

    @pl.when(j == 0)
    def _():
        g = g_ref[...]
        hn_ref[HALO:HALO + tm, :] = _norm_rows(x_ref[...], g).astype(BF16)
        first = (i % tiles_per_seq) == 0
        last = (i % tiles_per_seq) == tiles_per_seq - 1
        hp = _norm_rows(xp_ref[...], g)
        hn_ref[0:HALO, :] = jnp.where(first, 0.0, hp).astype(BF16)
        hx = _norm_rows(xn_ref[...], g)
        hn_ref[HALO + tm:, :] = jnp.where(last, 0.0, hx).astype(BF16)

    h = hn_ref[...]
    rows = tm + 2 * HALO

    def conv(w_ref, cw_ref, cb_ref):
        u = _dot(h, w_ref[...])
        um = pltpu.roll(u, 1, 0)[HALO:HALO + tm]
        up = pltpu.roll(u, rows - 1, 0)[HALO:HALO + tm]
        uc = u[HALO:HALO + tm]
        cw = cw_ref[...]
        return um * cw[0:1] + uc * cw[1:2] + up * cw[2:3] + cb_ref[...]

    x0_out[...] = conv(w0_ref, cw0_ref, cb0_ref)
    x1 = conv(w1_ref, cw1_ref, cb1_ref)
    v = conv(w2_ref, cw2_ref, cb2_ref)
    zv_out[...] = v * x1


def hyena_inproj(x, g, w_in, conv_w, conv_b, seq_len, *, tm=512, tn=512):
    T, D = x.shape
    W = w_in.shape[1] // 3
    tm = _tile(seq_len, tm)
    tn = _tile(W, tn)
    nj = W // tn
    hb = tm // HALO
    n_halo_blocks = T // HALO
    kern = functools.partial(_hy_inproj_kernel, tm=tm, tiles_per_seq=seq_len // tm)

    def wspec(c):
        return pl.BlockSpec((D, tn), lambda i, j, c=c: (0, j + c * nj))

    def cwspec(c):
        return pl.BlockSpec((3, tn), lambda i, j, c=c: (0, j + c * nj))

    def cbspec(c):
        return pl.BlockSpec((1, tn), lambda i, j, c=c: (0, j + c * nj))

    return pl.pallas_call(
        kern,
        grid=(T // tm, nj),
        in_specs=[pl.BlockSpec((tm, D), lambda i, j: (i, 0)),
                  pl.BlockSpec((HALO, D), lambda i, j: (jnp.maximum(i * hb - 1, 0), 0)),
                  pl.BlockSpec((HALO, D), lambda i, j: (jnp.minimum((i + 1) * hb, n_halo_blocks - 1), 0)),
                  pl.BlockSpec((1, D), lambda i, j: (0, 0)),
                  wspec(0), wspec(1), wspec(2),
                  cwspec(0), cwspec(1), cwspec(2),
                  cbspec(0), cbspec(1), cbspec(2)],
        out_specs=[pl.BlockSpec((tm, tn), lambda i, j: (i, j)),
                   pl.BlockSpec((tm, tn), lambda i, j: (i, j))],
        out_shape=[jax.ShapeDtypeStruct((T, W), F32), jax.ShapeDtypeStruct((T, W), F32)],
        scratch_shapes=[pltpu.VMEM((tm + 2 * HALO, D), BF16)],
        compiler_params=_params(("parallel", "arbitrary")),
        name="hyena_inproj",
    )(x, x, x, g, w_in, w_in, w_in, conv_w, conv_w, conv_w, conv_b, conv_b, conv_b)


def _filter_mlp_kernel(z_ref, w1_ref, b1_ref, win_ref, bin_ref, fr_ref, h_ref):
    hp = lax.Precision.HIGHEST
    fr = fr_ref[...]
    h = jnp.sin(fr * (jnp.dot(z_ref[...], w1_ref[...], precision=hp, preferred_element_type=F32) + b1_ref[...]))
    for l in range(win_ref.shape[0]):
        h = jnp.sin(fr * (jnp.dot(h, win_ref[l], precision=hp, preferred_element_type=F32) + bin_ref[l]))
    h_ref[...] = h


def hyena_filter_features(L, D, f_w1, f_b1, f_win, f_bin, f_freq):
    H = f_w1.shape[1]
    t = jnp.linspace(0.0, 1.0, L, dtype=F32)[:, None]
    bands = (EMB_DIM - 1) // 2
    w = 2.0 * math.pi * jnp.arange(L, dtype=F32)[:, None] / L
    fb = jnp.linspace(1e-4, bands - 1, bands, dtype=F32)[None, :]
    z = jnp.concatenate([t, jnp.cos(fb * w), -jnp.sin(fb * w)], axis=-1)

    def lag_major_to_dft_order(a):
        return a.reshape(L // DFT_N2, DFT_N2, a.shape[1]).transpose(1, 0, 2).reshape(L, a.shape[1])

    z = lag_major_to_dft_order(z)
    t = lag_major_to_dft_order(t)
    zp = jnp.pad(z, ((0, 0), (0, LANES - EMB_DIM)))
    w1p = jnp.pad(f_w1, ((0, LANES - EMB_DIM), (0, 0)))
    n_in = f_win.shape[0]
    tl = _tile(L, 512)
    h = pl.pallas_call(
        _filter_mlp_kernel,
        grid=(L // tl,),
        in_specs=[pl.BlockSpec((tl, LANES), lambda i: (i, 0)),
                  pl.BlockSpec((LANES, H), lambda i: (0, 0)),
                  pl.BlockSpec((1, H), lambda i: (0, 0)),
                  pl.BlockSpec((n_in, H, H), lambda i: (0, 0, 0)),
                  pl.BlockSpec((n_in, 1, H), lambda i: (0, 0, 0)),
                  pl.BlockSpec((1, H), lambda i: (0, 0))],
        out_specs=pl.BlockSpec((tl, H), lambda i: (i, 0)),
        out_shape=jax.ShapeDtypeStruct((L, H), F32),
        compiler_params=_params(("parallel",)),
        name="filter_mlp",
    )(zp, w1p, f_b1[None, :], f_win, f_bin[:, None, :], f_freq[None, :])

    deltas = jnp.abs(jnp.linspace(MIN_DECAY, MAX_DECAY, D, dtype=F32))[None, :]
    return h, t, jnp.concatenate([deltas, deltas], axis=1)


def dft_tables(N1, N2):
    N = N1 * N2
    half = N1 // 2
    k1 = jnp.arange(N1, dtype=jnp.int32)
    ang_a = ((k1[:, None] * jnp.arange(half, dtype=jnp.int32)[None, :]) % N1).astype(F32) * (2.0 * math.pi / N1)
    ang_b = ((k1[None, :] * jnp.arange(N2, dtype=jnp.int32)[:, None]) % N).astype(F32) * (2.0 * math.pi / N)
    ca, sa = jnp.cos(ang_a)[None, :, :], jnp.sin(ang_a)[None, :, :]
    cb, sb = jnp.cos(ang_b)[:, :, None], jnp.sin(ang_b)[:, :, None]
    c = ca * cb - sa * sb
    s = sa * cb + ca * sb
    fwd = jnp.concatenate([jnp.concatenate([c, s], axis=2),
                           jnp.concatenate([-s, c], axis=2)], axis=1)
    inv = jnp.swapaxes(fwd, 1, 2)
    fwd_real = jnp.concatenate([c, -s], axis=1)
    k2 = jnp.arange(N2, dtype=jnp.int32)
    ang2 = ((k2[:, None] * k2[None, :]) % N2).astype(F32) * (2.0 * math.pi / N2)
    c2, s2 = jnp.cos(ang2), jnp.sin(ang2)
    f2 = jnp.concatenate([jnp.concatenate([c2, s2], axis=1),
                          jnp.concatenate([-s2, c2], axis=1)], axis=0)
    g2 = f2.T
    return fwd.astype(BF16), inv.astype(BF16), fwd_real.astype(BF16), f2.astype(BF16), g2.astype(BF16)


def _pack_complex(re, im):
    rb = lax.bitcast_convert_type(re.astype(BF16).astype(F32), jnp.uint32)
    ib = lax.bitcast_convert_type(im.astype(BF16).astype(F32), jnp.uint32)
    return (rb >> 16) | (ib & jnp.uint32(0xFFFF0000))


def _unpack_complex(p):
    re = lax.bitcast_convert_type(p << 16, F32)
    im = lax.bitcast_convert_type(p & jnp.uint32(0xFFFF0000), F32)
    return jnp.concatenate([re, im], axis=0).astype(BF16)


def _double_buffered(in_copies, out_copies, compute):
    j = pl.program_id(0)
    n = pl.num_programs(0)
    slot = j % 2

    @pl.when(j == 0)
    def _():
        for cp in in_copies(0, 0):
            cp.start()

    @pl.when(j + 1 < n)
    def _():
        for cp in in_copies(j + 1, 1 - slot):
            cp.start()

    for cp in in_copies(j, slot):
        cp.wait()

    @pl.when(j >= 2)
    def _():
        for cp in out_copies(j - 2, slot):
            cp.wait()

    compute(slot)
    for cp in out_copies(j, slot):
        cp.start()

    @pl.when(j == n - 1)
    def _():
        for cp in out_copies(j, slot):
            cp.wait()

        @pl.when(n >= 2)
        def _():
            for cp in out_copies(j - 1, 1 - slot):
                cp.wait()


def _fft_s1_kernel(z_hbm, m_ref, a_hbm, zbuf, obuf, sem_in, sem_out):
    n_g = zbuf.shape[1]
    half = zbuf.shape[2] // 2
    n1 = obuf.shape[2]

    def in_copies(step, slot):
        return [pltpu.make_async_copy(z_hbm.at[b, :, step * n_g + g, :], zbuf.at[slot, g, pl.ds(b * half, half)],
                                      sem_in.at[slot])
                for g in range(n_g) for b in range(2)]

    def out_copies(step, slot):
        return [pltpu.make_async_copy(obuf.at[slot, g], a_hbm.at[:, step * n_g + g, :], sem_out.at[slot])
                for g in range(n_g)]

    def compute(slot):
        for g in range(n_g):
            a = _dot(m_ref[g], zbuf[slot, g].astype(BF16))
            obuf[slot, g] = _pack_complex(a[:n1], a[n1:])

    _double_buffered(in_copies, out_copies, compute)


def _fft_s1_real_kernel(h_ref, t_ref, w3_ref, dl_ref, m_ref, a_hbm, s_ref, obuf, sem_out):
    n_g = obuf.shape[1]
    n1 = obuf.shape[2]
    j = pl.program_id(0)

    @pl.when(j == 0)
    def _():
        s_ref[...] = jnp.zeros_like(s_ref)

    def out_copies(step, slot):
        return [pltpu.make_async_copy(obuf.at[slot, g], a_hbm.at[:, step * n_g + g, :], sem_out.at[slot])
                for g in range(n_g)]

    def compute(slot):
        w3 = w3_ref[...]
        for g in range(n_g):
            taps = _dot_3pass(h_ref[g], w3) * jnp.exp(-t_ref[g] * dl_ref[...])
            if g == 0:
                row = lax.broadcasted_iota(jnp.int32, taps.shape, 0)
                col = lax.broadcasted_iota(jnp.int32, taps.shape, 1)
                drop = jnp.logical_and(jnp.logical_and(j == 0, row == 0), col >= taps.shape[1] // 2)
                taps = jnp.where(drop, 0.0, taps)
            s_ref[...] += jnp.sum(jnp.abs(taps), axis=0, keepdims=True)
            a = _dot(m_ref[g], taps.astype(BF16))
            obuf[slot, g] = _pack_complex(a[:n1], a[n1:])

    _double_buffered(lambda step, slot: [], out_copies, compute)


def _fft_mid_kernel(a_ref, af_ref, ab_ref, s_ref, f2_ref, g2_ref, c_ref):
    n2 = a_ref.shape[0]
    f2 = f2_ref[...]
    xf = _dot(f2, _unpack_complex(af_ref[...]))
    xb = _dot(f2, _unpack_complex(ab_ref[...]))
    s = s_ref[...]
    kr = (xf[:n2] + xb[:n2]) * s
    ki = (xf[n2:] - xb[n2:]) * s
    x = _dot(f2, _unpack_complex(a_ref[...]))
    xr, xi = x[:n2], x[n2:]
    y = jnp.concatenate([xr * kr - xi * ki, xr * ki + xi * kr], axis=0).astype(BF16)
    c = _dot(g2_ref[...], y)
    c_ref[...] = _pack_complex(c[:n2], c[n2:])


def _fft_out_kernel(c_hbm, m_ref, zv_hbm, x0_hbm, r_hbm, sk_ref, w_ref, o_hbm,
                    cbuf, zbuf, xbuf, rbuf, obuf, sem_in, sem_out):
    n_g = zbuf.shape[1]
    n1 = zbuf.shape[2]
    half = n1 // 2

    def in_copies(step, slot):
        cps = []
        for g in range(n_g):
            n2 = step * n_g + g
            cps.append(pltpu.make_async_copy(c_hbm.at[:, n2, :], cbuf.at[slot, g], sem_in.at[slot]))
            for b in range(2):
                rows = pl.ds(b * half, half)
                for src, dst in ((zv_hbm, zbuf), (x0_hbm, xbuf), (r_hbm, rbuf)):
                    cps.append(pltpu.make_async_copy(src.at[b, :, n2, :], dst.at[slot, g, rows], sem_in.at[slot]))
        return cps

    def out_copies(step, slot):
        return [pltpu.make_async_copy(obuf.at[slot, g, pl.ds(b * half, half)], o_hbm.at[b, :, step * n_g + g, :],
                                      sem_out.at[slot])
                for g in range(n_g) for b in range(2)]

    def compute(slot):
        ys = []
        for g in range(n_g):
            y = _dot(m_ref[g], _unpack_complex(cbuf[slot, g]))
            ys.append(((y + zbuf[slot, g] * sk_ref[...]) * xbuf[slot, g]).astype(BF16))
        proj = _dot(ys[0] if n_g == 1 else jnp.concatenate(ys, axis=0), w_ref[...])
        for g in range(n_g):
            obuf[slot, g] = rbuf[slot, g] + proj[g * n1:(g + 1) * n1]

    _double_buffered(in_copies, out_copies, compute)


def hyena_long_conv(zv, x0, x, w_out, filt, f_w3, skip_b, L):
    D = zv.shape[1]
    N = 2 * L
    N2 = DFT_N2
    N1 = N // N2
    half = N1 // 2
    n_g = max(1, min(N2, MXU_ROWS // N1))
    n_steps = N2 // n_g
    fwd, inv, fwd_real, f2, g2 = dft_tables(N1, N2)
    td2 = _tile(D, 2048)
    nd2 = D // td2
    zv4 = zv.reshape(2, half, N2, D)
    x04 = x0.reshape(2, half, N2, D)
    x4 = x.reshape(2, half, N2, D)
    any_spec = pl.BlockSpec(memory_space=pl.ANY)

    h, t, rates = filt
    H = h.shape[1]
    af, asum = pl.pallas_call(
        _fft_s1_real_kernel,
        grid=(n_steps,),
        in_specs=[pl.BlockSpec((n_g, half, H), lambda j: (j, 0, 0)),
                  pl.BlockSpec((n_g, half, 1), lambda j: (j, 0, 0)),
                  pl.BlockSpec((H, 2 * D), lambda j: (0, 0)),
                  pl.BlockSpec((1, 2 * D), lambda j: (0, 0)),
                  pl.BlockSpec((n_g, 2 * N1, half), lambda j: (j, 0, 0))],
        out_specs=[any_spec, pl.BlockSpec((1, 2 * D), lambda j: (0, 0))],
        out_shape=[jax.ShapeDtypeStruct((N1, N2, 2 * D), jnp.uint32), jax.ShapeDtypeStruct((1, 2 * D), F32)],
        scratch_shapes=[pltpu.VMEM((2, n_g, N1, 2 * D), jnp.uint32), pltpu.SemaphoreType.DMA((2,))],
        compiler_params=_params(("arbitrary",)),
        name="filter_dft_stage1",
    )(h.reshape(N2, half, H), t.reshape(N2, half, 1), f_w3, rates, fwd_real)
    scale = 1.0 / ((asum[:, :D] + asum[:, D:]) * N)

    a = pl.pallas_call(
        _fft_s1_kernel,
        grid=(n_steps,),
        in_specs=[any_spec,
                  pl.BlockSpec((n_g, 2 * N1, N1), lambda j: (j, 0, 0))],
        out_specs=any_spec,
        out_shape=jax.ShapeDtypeStruct((N1, N2, D), jnp.uint32),
        scratch_shapes=[pltpu.VMEM((2, n_g, N1, D), F32), pltpu.VMEM((2, n_g, N1, D), jnp.uint32),
                        pltpu.SemaphoreType.DMA((2,)), pltpu.SemaphoreType.DMA((2,))],
        compiler_params=_params(("arbitrary",)),
        name="conv_dft_stage1",
    )(zv4, fwd)

    c = pl.pallas_call(
        _fft_mid_kernel,
        grid=(N1, nd2),
        in_specs=[pl.BlockSpec((None, N2, td2), lambda k, j: (k, 0, j)),
                  pl.BlockSpec((None, N2, td2), lambda k, j: (k, 0, j)),
                  pl.BlockSpec((None, N2, td2), lambda k, j: (k, 0, j + nd2)),
                  pl.BlockSpec((1, td2), lambda k, j: (0, j)),
                  pl.BlockSpec((2 * N2, 2 * N2), lambda k, j: (0, 0)),
                  pl.BlockSpec((2 * N2, 2 * N2), lambda k, j: (0, 0))],
        out_specs=pl.BlockSpec((None, N2, td2), lambda k, j: (k, 0, j)),
        out_shape=jax.ShapeDtypeStruct((N1, N2, D), jnp.uint32),
        compiler_params=_params(("parallel", "parallel")),
        name="conv_dft_stage2",
    )(a, af, af, scale, f2, g2)

    def buf(dtype):
        return pltpu.VMEM((2, n_g, N1, D), dtype)

    o = pl.pallas_call(
        _fft_out_kernel,
        grid=(n_steps,),
        in_specs=[any_spec,
                  pl.BlockSpec((n_g, N1, 2 * N1), lambda j: (j, 0, 0)),
                  any_spec, any_spec, any_spec,
                  pl.BlockSpec((1, D), lambda j: (0, 0)),
                  pl.BlockSpec(w_out.shape, lambda j: (0, 0))],
        out_specs=any_spec,
        out_shape=jax.ShapeDtypeStruct((2, half, N2, D), F32),
        scratch_shapes=[buf(jnp.uint32), buf(F32), buf(F32), buf(F32), buf(F32),
                        pltpu.SemaphoreType.DMA((2,)), pltpu.SemaphoreType.DMA((2,))],
        compiler_params=_params(("arbitrary",)),
        name="conv_dft_inverse_outproj",
    )(c, inv, zv4, x04, x4, skip_b, w_out)
    return o.reshape(2 * L, D)


VT_ROWS = HEAD_DIM + BF16_SUBLANES


def _qkv_kernel(x_ref, g_ref, w_ref, gain_ref, cos_ref, sin_ref, o_ref, va_ref, vt_ref, hn_ref, *, n_rope_tiles):
    j = pl.program_id(1)

    @pl.when(j == 0)
    def _():
        hn_ref[...] = _norm_rows(x_ref[...], g_ref[...]).astype(BF16)

    u = _dot(hn_ref[...], w_ref[...])
    tn = u.shape[1]

    @pl.when(j < n_rope_tiles)
    def _():
        gain = gain_ref[...]
        cos = cos_ref[...]
        sin = sin_ref[...]
        ones = jnp.ones((HEAD_DIM, HEAD_DIM), BF16)
        for hh in range(tn // HEAD_DIM):
            xh = u[:, hh * HEAD_DIM:(hh + 1) * HEAD_DIM]
            sq = xh * xh
            sq_hi = sq.astype(BF16)
            sq_lo = (sq - sq_hi.astype(F32)).astype(BF16)
            ms = (_dot(sq_hi, ones) + _dot(sq_lo, ones)) * (1.0 / HEAD_DIM)
            y = xh * lax.rsqrt(ms + RMS_EPS) * gain
            sw = pltpu.roll(y, HEAD_DIM // 2, 1)
            o_ref[:, hh * HEAD_DIM:(hh + 1) * HEAD_DIM] = (y * cos + sw * sin).astype(o_ref.dtype)

    @pl.when(j >= n_rope_tiles)
    def _():
        o_ref[...] = u.astype(o_ref.dtype)
        ones = jnp.ones((u.shape[0], HEAD_DIM), va_ref.dtype)
        ones_t = jnp.ones((VT_ROWS - HEAD_DIM, u.shape[0]), vt_ref.dtype)
        for hh in range(tn // HEAD_DIM):
            vh = u[:, hh * HEAD_DIM:(hh + 1) * HEAD_DIM]
            va_ref[:, 2 * hh * HEAD_DIM:(2 * hh + 1) * HEAD_DIM] = vh.astype(va_ref.dtype)
            va_ref[:, (2 * hh + 1) * HEAD_DIM:(2 * hh + 2) * HEAD_DIM] = ones
            vt_ref[hh, :HEAD_DIM, :] = vh.T.astype(vt_ref.dtype)
            vt_ref[hh, HEAD_DIM:, :] = ones_t


def rope_tables(L):
    rows = L // GRID_W
    r, c = jnp.meshgrid(jnp.arange(rows, dtype=F32), jnp.arange(GRID_W, dtype=F32), indexing="ij")
    r = r.reshape(L)
    c = c.reshape(L)
    axis_dim = HEAD_DIM // 2
    inv = 1.0 / (ROPE_THETA ** (jnp.arange(0, axis_dim, 2, dtype=F32) / axis_dim))
    ang = jnp.concatenate([r[:, None] * inv[None], c[:, None] * inv[None]], axis=-1)
    cos = jnp.concatenate([jnp.cos(ang), jnp.cos(ang)], axis=1)
    sin = jnp.concatenate([-jnp.sin(ang), jnp.sin(ang)], axis=1)
    return cos, sin


def qkv_project(x, g, w_qkv, q_gain, k_gain, seq_len, *, tm=512):
    T, D = x.shape
    n_out = w_qkv.shape[1]
    nk = N_KV_HEADS * HEAD_DIM
    nq = n_out - 2 * nk
    tn = nk
    tm = _tile(seq_len, tm)
    n_rope = (nq + nk) // tn
    def deinterleave(a):
        lead = a.shape[:-1]
        a = a.reshape(lead + (a.shape[-1] // HEAD_DIM, HEAD_DIM // 2, 2))
        return jnp.swapaxes(a, -1, -2).reshape(lead + (-1,))

    w_qkv = jnp.concatenate([deinterleave(w_qkv[:, :nq + nk]), w_qkv[:, nq + nk:]], axis=1)
    q_gain = deinterleave(q_gain)
    k_gain = deinterleave(k_gain)
    gains = jnp.concatenate([jnp.tile((q_gain * SCORE_SCALE)[None, None, :], (nq // tn, 1, 1)),
                             k_gain[None, None, :],
                             jnp.ones((1, 1, HEAD_DIM), F32)], axis=0)
    cos, sin = rope_tables(seq_len)
    spt = seq_len // tm
    kern = functools.partial(_qkv_kernel, n_rope_tiles=n_rope)
    return pl.pallas_call(
        kern,
        grid=(T // tm, n_out // tn),
        in_specs=[pl.BlockSpec((tm, D), lambda i, j: (i, 0)),
                  pl.BlockSpec((1, D), lambda i, j: (0, 0)),
                  pl.BlockSpec((D, tn), lambda i, j: (0, j)),
                  pl.BlockSpec((None, 1, HEAD_DIM), lambda i, j: (j, 0, 0)),
                  pl.BlockSpec((tm, HEAD_DIM), lambda i, j: (i % spt, 0)),
                  pl.BlockSpec((tm, HEAD_DIM), lambda i, j: (i % spt, 0))],
        out_specs=[pl.BlockSpec((tm, tn), lambda i, j: (i, j)),
                   pl.BlockSpec((tm, 2 * nk), lambda i, j: (i, 0)),
                   pl.BlockSpec((None, N_KV_HEADS, VT_ROWS, tm), lambda i, j: (i // spt, 0, 0, i % spt))],
        out_shape=[jax.ShapeDtypeStruct((T, n_out), BF16), jax.ShapeDtypeStruct((T, 2 * nk), BF16),
                   jax.ShapeDtypeStruct((T // seq_len, N_KV_HEADS, VT_ROWS, seq_len), BF16)],
        scratch_shapes=[pltpu.VMEM((tm, D), BF16)],
        compiler_params=_params(("parallel", "arbitrary")),
        name="qkv_project",
    )(x, g, w_qkv, gains, cos, sin)


def _flash_kernel(q_ref, k_ref, v_ref, o_ref, qs_ref, s0_ref, s1_ref, p0_ref, p1_ref, a0_ref, a1_ref,
                  m_ref, acc_ref, *, group, tq, tk, nk):
    s_refs = (s0_ref, s1_ref)
    p_refs = (p0_ref, p1_ref)
    a_refs = (a0_ref, a1_ref)
    for gq in range(group):
        qs_ref[gq * tq:(gq + 1) * tq, :] = q_ref[:, gq * HEAD_DIM:(gq + 1) * HEAD_DIM]
    m_ref[...] = jnp.full_like(m_ref, -jnp.inf)
    acc_ref[...] = jnp.zeros_like(acc_ref)

    def rows_of(j):
        if isinstance(j, int):
            return pl.ds(j * tk, tk)
        return pl.ds(pl.multiple_of(j * tk, tk), tk)

    def scores(j, slot):
        kb = k_ref[rows_of(j), :]
        s_refs[slot][...] = lax.dot_general(qs_ref[...], kb, (((1,), (1,)), ((), ())),
                                            preferred_element_type=F32)

    def softmax(slot):
        s = s_refs[slot][...]
        m_prev = m_ref[...]
        m_new = jnp.maximum(m_prev, jnp.max(s, axis=-1, keepdims=True))
        a_refs[slot][...] = jnp.exp2(m_prev - m_new)
        p_refs[slot][...] = jnp.exp2(s - jnp.tile(m_new, (1, tk // LANES))).astype(BF16)
        m_ref[...] = m_new

    def accumulate(j, slot):
        pv = _dot(p_refs[slot][...], v_ref[rows_of(j), :])
        acc_ref[...] = acc_ref[...] * jnp.tile(a_refs[slot][...], (1, 2)) + pv

    def step(j, slot, first, last):
        if not last:
            scores(j + 1, 1 - slot)
        softmax(slot)
        if not first:
            accumulate(j - 1, 1 - slot)

    scores(0, 0)
    if nk >= 4 and nk % 2 == 0:
        step(0, 0, True, False)

        def pair(t, c):
            j = 2 * t + 1
            step(j, 1, False, False)
            step(j + 1, 0, False, False)
            return c

        lax.fori_loop(0, (nk - 2) // 2, pair, 0)
        step(nk - 1, 1, False, True)
    else:
        for j in range(nk):
            step(j, j % 2, j == 0, j == nk - 1)
    accumulate(nk - 1, (nk - 1) % 2)

    acc = acc_ref[...]
    o = acc[:, :HEAD_DIM] / acc[:, HEAD_DIM:]
    for gq in range(group):
        o_ref[:, gq * HEAD_DIM:(gq + 1) * HEAD_DIM] = o[gq * tq:(gq + 1) * tq].astype(o_ref.dtype)


def _flash_bounded_kernel(q_ref, k_ref, vt_ref, o_ref, qt_ref, acc_ref, *, group, tq, tk, nk, unroll):
    for gq in range(group):
        qt_ref[:, gq * tq:(gq + 1) * tq] = q_ref[:, gq * HEAD_DIM:(gq + 1) * HEAD_DIM].astype(F32).T.astype(BF16)
    acc_ref[...] = jnp.zeros_like(acc_ref)

    def block(j, c):
        rows = pl.ds(pl.multiple_of(j * tk, tk), tk)
        s_t = _dot(k_ref[rows, :], qt_ref[...])
        acc_ref[...] += _dot(vt_ref[:, rows], jnp.exp2(s_t).astype(BF16))
        return c

    lax.fori_loop(0, nk, block, 0, unroll=unroll)
    acc = acc_ref[...]
    o_t = acc[:HEAD_DIM] / acc[HEAD_DIM:HEAD_DIM + 1]
    for gq in range(group):
        o_ref[:, gq * HEAD_DIM:(gq + 1) * HEAD_DIM] = o_t[:, gq * tq:(gq + 1) * tq].T.astype(o_ref.dtype)


SAFE_SCORE_BOUND = 80.0


def flash_attention(qkv, v_aug, v_t, score_bound, B, L, n_heads, *, tq=256, tk=512):
    group = n_heads // N_KV_HEADS
    tq = _tile(L, tq)
    tk = _tile(L, tk)
    nk = L // tk
    qw = group * HEAD_DIM
    k_col = n_heads
    rows = group * tq
    q_spec = pl.BlockSpec((None, tq, qw), lambda b, h, qi: (b, qi, h))
    k_spec = pl.BlockSpec((None, L, HEAD_DIM), lambda b, h, qi: (b, 0, k_col + h))
    common = dict(
        grid=(B, N_KV_HEADS, L // tq),
        out_specs=pl.BlockSpec((None, tq, qw), lambda b, h, qi: (b, qi, h)),
        out_shape=jax.ShapeDtypeStruct((B, L, n_heads * HEAD_DIM), BF16),
        compiler_params=_params(("parallel", "parallel", "parallel")),
    )
    general = pl.pallas_call(
        functools.partial(_flash_kernel, group=group, tq=tq, tk=tk, nk=nk),
        in_specs=[q_spec, k_spec, pl.BlockSpec((None, L, 2 * HEAD_DIM), lambda b, h, qi: (b, 0, h))],
        scratch_shapes=[pltpu.VMEM((rows, HEAD_DIM), BF16),
                        pltpu.VMEM((rows, tk), F32), pltpu.VMEM((rows, tk), F32),
                        pltpu.VMEM((rows, tk), BF16), pltpu.VMEM((rows, tk), BF16),
                        pltpu.VMEM((rows, LANES), F32), pltpu.VMEM((rows, LANES), F32),
                        pltpu.VMEM((rows, LANES), F32),
                        pltpu.VMEM((rows, 2 * HEAD_DIM), F32)],
        name="flash_attention", **common)
    bounded = pl.pallas_call(
        functools.partial(_flash_bounded_kernel, group=group, tq=tq, tk=tk, nk=nk,
                          unroll=math.gcd(nk, 16)),
        in_specs=[q_spec, k_spec, pl.BlockSpec((None, None, VT_ROWS, L), lambda b, h, qi: (b, h, 0, 0))],
        scratch_shapes=[pltpu.VMEM((HEAD_DIM, rows), BF16),
                        pltpu.VMEM((VT_ROWS, rows), F32)],
        name="flash_attention_bounded", **common)
    return lax.cond(score_bound <= SAFE_SCORE_BOUND,
                    lambda ops: bounded(ops[0], ops[0], ops[2]),
                    lambda ops: general(ops[0], ops[0], ops[1]),
                    (qkv, v_aug, v_t))


def _pack_bf16_pairs(x):
    half = x.shape[1] // 2
    bits = lax.bitcast_convert_type(x.astype(BF16).astype(F32), jnp.uint32)
    return (bits[:, :half] >> 16) | (bits[:, half:] & jnp.uint32(0xFFFF0000))


def _unpack_bf16_pairs(p):
    lo = lax.bitcast_convert_type(p << 16, F32).astype(BF16)
    hi = lax.bitcast_convert_type(p & jnp.uint32(0xFFFF0000), F32).astype(BF16)
    return lo, hi


def _router_kernel(xa_ref, xb_ref, g_ref, rw_ref, h_ref, info_ref, cnt_ref, *, n_a_tiles):
    i = pl.program_id(0)

    def route(x_ref):
        hf = _norm_rows(x_ref[...], g_ref[...])
        h_ref[...] = _pack_bf16_pairs(hf)
        h_hi = hf.astype(BF16)
        h_lo = (hf - h_hi.astype(F32)).astype(BF16)
        rw = rw_ref[...]
        hw = _dot(h_hi, rw)
        logits = hw[:, :LANES] + hw[:, LANES:] + _dot(h_lo, rw[:, :LANES])
        shape = logits.shape
        lane = lax.broadcasted_iota(jnp.int32, shape, 1)
        lanef = lane.astype(F32)
        neg = -jnp.inf
        big = float(LANES)
        is_g = lane < N_GROUPS
        gl = jnp.where(is_g, logits, neg)
        gmax = jnp.max(gl, axis=-1, keepdims=True)
        gidx = jnp.min(jnp.where(gl == gmax, lanef, big), axis=-1, keepdims=True)
        p_g = 1.0 / jnp.sum(jnp.where(is_g, jnp.exp(logits - gmax), 0.0), axis=-1, keepdims=True)
        lo = N_GROUPS + gidx * EXPERTS_PER_GROUP
        el = jnp.where(lanef >= lo, jnp.where(lanef < lo + EXPERTS_PER_GROUP, logits, neg), neg)
        v1 = jnp.max(el, axis=-1, keepdims=True)
        i1 = jnp.min(jnp.where(el == v1, lanef, big), axis=-1, keepdims=True)
        el2 = jnp.where(lanef == i1, neg, el)
        v2 = jnp.max(el2, axis=-1, keepdims=True)
        i2 = jnp.min(jnp.where(el2 == v2, lanef, big), axis=-1, keepdims=True)
        t = jnp.exp(v2 - v1)
        w1 = p_g / (1.0 + t)
        w2 = p_g * t / (1.0 + t)
        e1 = i1 - N_GROUPS
        e2 = i2 - N_GROUPS
        oh1 = jnp.where(lanef == e1, 1.0, 0.0)
        oh2 = jnp.where(lanef == e2, 1.0, 0.0)
        oh = oh1 + oh2
        tm = shape[0]
        tri = jnp.where(lax.broadcasted_iota(jnp.int32, (tm, tm), 1) < lax.broadcasted_iota(jnp.int32, (tm, tm), 0),
                        1.0, 0.0).astype(BF16)
        before = _dot(tri, oh.astype(BF16)) + cnt_ref[...]
        r1 = jnp.sum(before * oh1, axis=-1, keepdims=True)
        r2 = jnp.sum(before * oh2, axis=-1, keepdims=True)
        cnt_ref[...] += jnp.sum(oh, axis=0, keepdims=True)
        vals = (e1, e2, w1, w2, r1, r2)
        info = jnp.zeros(shape, F32)
        for idx, val in enumerate(vals):
            info = jnp.where(lane == idx, val, info)
        info_ref[...] = info

    @pl.when(i == 0)
    def _():
        cnt_ref[...] = jnp.zeros_like(cnt_ref)

    @pl.when(i < n_a_tiles)
    def _():
        route(xa_ref)

    @pl.when(i >= n_a_tiles)
    def _():
        route(xb_ref)


def moe_route(xa, xb, g, router_group, router_expert, *, tm=512):
    Ta, D = xa.shape
    Tb = xb.shape[0]
    tm = _tile(math.gcd(Ta, Tb), tm)
    na, nb = Ta // tm, Tb // tm
    n_e = N_GROUPS * EXPERTS_PER_GROUP
    rw = jnp.concatenate([router_group, jnp.transpose(router_expert, (1, 0, 2)).reshape(D, n_e)], axis=1)
    rw = jnp.pad(rw, ((0, 0), (0, LANES - rw.shape[1])))
    rw_hi = rw.astype(BF16)
    rw = jnp.concatenate([rw_hi, (rw - rw_hi.astype(F32)).astype(BF16)], axis=1)
    kern = functools.partial(_router_kernel, n_a_tiles=na)
    return pl.pallas_call(
        kern,
        grid=(na + nb,),
        in_specs=[pl.BlockSpec((tm, D), lambda i: (jnp.minimum(i, na - 1), 0)),
                  pl.BlockSpec((tm, D), lambda i: (jnp.maximum(i - na, 0), 0)),
                  pl.BlockSpec((1, D), lambda i: (0, 0)),
                  pl.BlockSpec((D, 2 * LANES), lambda i: (0, 0))],
        out_specs=[pl.BlockSpec((tm, D // 2), lambda i: (i, 0)),
                   pl.BlockSpec((tm, LANES), lambda i: (i, 0)),
                   pl.BlockSpec((1, LANES), lambda i: (0, 0))],
        out_shape=[jax.ShapeDtypeStruct((Ta + Tb, D // 2), jnp.uint32),
                   jax.ShapeDtypeStruct((Ta + Tb, LANES), F32),
                   jax.ShapeDtypeStruct((1, LANES), F32)],
        compiler_params=_params(("arbitrary",)),
        name="moe_route",
    )(xa, xb, g, rw)


def moe_plan(info, cnt, tm, n_experts):
    T = info.shape[0]
    n_tiles = (2 * T) // tm + n_experts
    counts = cnt[0, :n_experts].astype(jnp.int32)
    padded = ((counts + tm - 1) // tm) * tm
    ends = jnp.cumsum(padded)
    off = ends - padded
    e = info[:, 0:2].T.astype(jnp.int32)
    rank = info[:, 4:6].T.astype(jnp.int32)
    k = jnp.arange(n_experts, dtype=jnp.int32)[:, None, None]
    pos = rank + jnp.sum(jnp.where(e[None] == k, off[:, None, None], 0), axis=0)
    pos = pos.T.reshape(-1)
    tile_start = jnp.arange(n_tiles, dtype=jnp.int32) * tm
    tile_e = jnp.sum((tile_start[:, None] >= ends[None, :]).astype(jnp.int32), axis=1)
    tile_e = jnp.minimum(tile_e, n_experts - 1)
    tile_valid = jnp.clip(counts[tile_e] - (tile_start - off[tile_e]), 0, tm).astype(jnp.int32)
    n_used = (ends[-1:] // tm).astype(jnp.int32)
    expert_last = jnp.where(padded > 0, ends - tm, -1)
    tail_tile = n_used[0] + jnp.arange(n_experts, dtype=jnp.int32)
    tail = jnp.where(tail_tile < n_tiles, tail_tile * tm, -1)
    zero_tiles = jnp.concatenate([expert_last, tail]).astype(jnp.int32)[None, :]
    return pos, tile_e, tile_valid, n_used, zero_tiles


DMA_UNROLL = 8


def _dispatch_kernel(pos_ref, ztile_ref, h_ref, hs_hbm, zbuf, sem, zsem):
    tm = h_ref.shape[0]

    @pl.when(pl.program_id(0) == 0)
    def _():
        zbuf[...] = jnp.zeros_like(zbuf)
        zrows = zbuf.shape[0]

        def zero_copy(k):
            return pltpu.make_async_copy(zbuf, hs_hbm.at[pl.ds(pl.multiple_of(ztile_ref[0, k], zrows), zrows)], zsem)

        for k in range(ztile_ref.shape[1]):
            @pl.when(ztile_ref[0, k] >= 0)
            def _():
                zero_copy(k).start()
        for k in range(ztile_ref.shape[1]):
            @pl.when(ztile_ref[0, k] >= 0)
            def _():
                zero_copy(k).wait()

    def row_copy(r, slot):
        return pltpu.make_async_copy(h_ref.at[pl.ds(r, 1)], hs_hbm.at[pl.ds(pos_ref[0, 0, 2 * r + slot], 1)], sem)

    def start(r, c):
        row_copy(r, 0).start()
        row_copy(r, 1).start()
        return c

    def wait(r, c):
        row_copy(r, 0).wait()
        row_copy(r, 1).wait()
        return c

    lax.fori_loop(0, tm, start, 0, unroll=DMA_UNROLL)
    lax.fori_loop(0, tm, wait, 0, unroll=DMA_UNROLL)


def moe_dispatch(h, pos, zero_tiles, n_sorted_rows, tile_rows, *, tm=512):
    T, C = h.shape
    tm = _tile(T, tm)
    return pl.pallas_call(
        _dispatch_kernel,
        grid=(T // tm,),
        in_specs=[pl.BlockSpec((1, 1, 2 * tm), lambda i: (i, 0, 0), memory_space=pltpu.SMEM),
                  pl.BlockSpec(zero_tiles.shape, lambda i: (0, 0), memory_space=pltpu.SMEM),
                  pl.BlockSpec((tm, C), lambda i: (i, 0))],
        out_specs=pl.BlockSpec(memory_space=pl.ANY),
        out_shape=jax.ShapeDtypeStruct((n_sorted_rows, C), h.dtype),
        scratch_shapes=[pltpu.VMEM((tile_rows, C), h.dtype), pltpu.SemaphoreType.DMA, pltpu.SemaphoreType.DMA],
        compiler_params=_params(("arbitrary",)),
        name="moe_dispatch",
    )(pos.reshape(T // tm, 1, 2 * tm), zero_tiles, h)


def _expert_kernel(te_ref, tv_ref, nu_ref, x_ref, wg_ref, wu_ref, wd_ref, y_ref):
    del te_ref, nu_ref
    valid = tv_ref[pl.program_id(0)]

    @pl.when(valid == 0)
    def _():
        y_ref[...] = jnp.zeros_like(y_ref)

    @pl.when(valid > 0)
    def _():
        x_lo, x_hi = _unpack_bf16_pairs(x_ref[...])
        half = x_ref.shape[1]
        gate = (_dot(x_lo, wg_ref[:half, :].astype(BF16)) + _dot(x_hi, wg_ref[half:, :].astype(BF16)))
        up = (_dot(x_lo, wu_ref[:half, :].astype(BF16)) + _dot(x_hi, wu_ref[half:, :].astype(BF16)))
        a = (gate * jax.nn.sigmoid(gate) * up).astype(BF16)
        y_ref[...] = _dot(a, wd_ref[...].astype(BF16))


def moe_experts(hs, tile_e, tile_valid, n_used, w_gate, w_up, w_down, layer, tm):
    P = hs.shape[0]
    D = w_gate.shape[2]
    d_e = w_gate.shape[-1]

    def row_map(i, te, tv, nu):
        return (jnp.minimum(i, nu[0] - 1), 0)

    def out_map(i, te, tv, nu):
        return (i, 0)

    def w_map(i, te, tv, nu):
        return (layer, te[i], 0, 0)

    grid_spec = pltpu.PrefetchScalarGridSpec(
        num_scalar_prefetch=3,
        grid=(P // tm,),
        in_specs=[pl.BlockSpec((tm, D // 2), row_map),
                  pl.BlockSpec((None, None, D, d_e), w_map),
                  pl.BlockSpec((None, None, D, d_e), w_map),
                  pl.BlockSpec((None, None, d_e, D), w_map)],
        out_specs=pl.BlockSpec((tm, D), out_map),
    )
    return pl.pallas_call(
        _expert_kernel,
        grid_spec=grid_spec,
        out_shape=jax.ShapeDtypeStruct((P, D), F32),
        compiler_params=_params(("arbitrary",)),
        name="moe_experts",
    )(tile_e, tile_valid, n_used, hs, w_gate, w_up, w_down)


def _combine_kernel(pos_ref, posn_ref, x_ref, info_ref, g_ref, y_hbm, o_ref, ybuf, sem, *, final_norm):
    tm = x_ref.shape[0]
    i = pl.program_id(0)
    buf = i % 2

    def row_copy(p_ref, b, r, slot):
        return pltpu.make_async_copy(y_hbm.at[pl.ds(p_ref[0, 0, 2 * r + slot], 1)],
                                     ybuf.at[b, slot, pl.ds(r, 1)], sem.at[b])

    def start_tile(p_ref, b):
        def body(r, c):
            row_copy(p_ref, b, r, 0).start()
            row_copy(p_ref, b, r, 1).start()
            return c
        lax.fori_loop(0, tm, body, 0, unroll=DMA_UNROLL)

    @pl.when(i == 0)
    def _():
        start_tile(pos_ref, 0)

    @pl.when(i + 1 < pl.num_programs(0))
    def _():
        start_tile(posn_ref, 1 - buf)

    def wait(r, c):
        row_copy(pos_ref, buf, r, 0).wait()
        row_copy(pos_ref, buf, r, 1).wait()
        return c

    lax.fori_loop(0, tm, wait, 0, unroll=DMA_UNROLL)
    info = info_ref[...]
    y = info[:, 2:3] * ybuf[buf, 0] + info[:, 3:4] * ybuf[buf, 1]
    x = x_ref[...] + y
    if final_norm:
        x = _norm_rows(x, g_ref[...])
    o_ref[...] = x


def moe_combine(x, y, pos, info, row_off, g_final, final_norm, *, tm=256):
    Tg, D = x.shape
    T = info.shape[0]
    tm = _tile(math.gcd(Tg, row_off) if row_off else Tg, tm)
    ob = row_off // tm
    kern = functools.partial(_combine_kernel, final_norm=final_norm)
    n = Tg // tm
    pos3 = pos.reshape(T // tm, 1, 2 * tm)
    return pl.pallas_call(
        kern,
        grid=(n,),
        in_specs=[pl.BlockSpec((1, 1, 2 * tm), lambda i: (i + ob, 0, 0), memory_space=pltpu.SMEM),
                  pl.BlockSpec((1, 1, 2 * tm), lambda i: (jnp.minimum(i + 1, n - 1) + ob, 0, 0),
                               memory_space=pltpu.SMEM),
                  pl.BlockSpec((tm, D), lambda i: (i, 0)),
                  pl.BlockSpec((tm, LANES), lambda i: (i + ob, 0)),
                  pl.BlockSpec((1, D), lambda i: (0, 0)),
                  pl.BlockSpec(memory_space=pl.ANY)],
        out_specs=pl.BlockSpec((tm, D), lambda i: (i, 0)),
        out_shape=jax.ShapeDtypeStruct((Tg, D), F32),
        scratch_shapes=[pltpu.VMEM((2, 2, tm, D), F32), pltpu.SemaphoreType.DMA((2,))],
        compiler_params=_params(("arbitrary",)),
        name="moe_combine",
    )(pos3, pos3, x, info, g_final, y)


MOE_TILE = 512


def hier_moe_block(xs, g, router_group, router_expert, w_gate, w_up, w_down, layer, g_final, final_norm):
    xa, xb = xs
    n_experts = w_gate.shape[1]
    h, info, cnt = moe_route(xa, xb, g, router_group, router_expert)
    T = h.shape[0]
    tm = _tile(2 * T, MOE_TILE)
    pos, tile_e, tile_valid, n_used, zero_tiles = moe_plan(info, cnt, tm, n_experts)
    hs = moe_dispatch(h, pos, zero_tiles, tile_e.shape[0] * tm, tm)
    y = moe_experts(hs, tile_e, tile_valid, n_used, w_gate, w_up, w_down, layer, tm)
    oa = moe_combine(xa, y, pos, info, 0, g_final, final_norm)
    ob = moe_combine(xb, y, pos, info, xa.shape[0], g_final, final_norm)
    return oa, ob


def hyena_block(x, L, g, w_in, conv_w, conv_b, f_w1, f_b1, f_win, f_bin, f_freq, f_w3, skip_b, w_out):
    D = x.shape[1]
    x0, zv = hyena_inproj(x, g, w_in, conv_w, conv_b[None, :], L)
    filt = hyena_filter_features(L, w_out.shape[0], f_w1, f_b1, f_win, f_bin, f_freq)
    return hyena_long_conv(zv, x0, x, w_out, filt, f_w3, skip_b[None, :], L)


def attention_block(x, B, L, g, w_qkv, q_gain, k_gain, w_o):
    D = x.shape[1]
    n_heads = w_o.shape[0] // HEAD_DIM
    qkv, v_aug, v_t = qkv_project(x, g, w_qkv, q_gain, k_gain, L)
    score_bound = 1.02 * HEAD_DIM * SCORE_SCALE * jnp.max(jnp.abs(q_gain)) * jnp.max(jnp.abs(k_gain))
    o = flash_attention(qkv.reshape(B, L, qkv.shape[1]), v_aug.reshape(B, L, v_aug.shape[1]), v_t, score_bound,
                        B, L, n_heads)
    return matmul_residual(o.reshape(B * L, n_heads * HEAD_DIM), w_o, x)


def kernel(x_prompt, x_sample, norm_mix, norm_ffn, norm_final, hy_w_in, hy_conv_w, hy_conv_b, hy_f_w1, hy_f_b1, hy_f_win, hy_f_bin, hy_f_freq, hy_f_w3, hy_skip_b, hy_w_out, at_w_qkv, at_q_gain, at_k_gain, at_w_o, moe_router_group, moe_router_expert, moe_w_gate, moe_w_up, moe_w_down):
    D = x_prompt.shape[-1]
    depth = norm_mix.shape[0]
    shapes = [x_prompt.shape, x_sample.shape]
    for s in shapes:
        assert s[0] == 2, "the long convolution packs exactly two sequences per group"
    xs = [x_prompt.reshape(-1, D), x_sample.reshape(-1, D)]
    hy_w_in_b = hy_w_in.astype(BF16)
    hy_w_out_b = hy_w_out.astype(BF16)
    at_w_qkv_b = at_w_qkv.astype(BF16)
    at_w_o_b = at_w_o.astype(BF16)
    g_final = norm_final[None, :]
    for i in range(depth):
        j = i // N_MIXERS
        g = norm_mix[i][None, :]
        if i % N_MIXERS == 0:
            xs = [hyena_block(x, s[1], g, hy_w_in_b[j], hy_conv_w[j], hy_conv_b[j], hy_f_w1[j], hy_f_b1[j],
                              hy_f_win[j], hy_f_bin[j], hy_f_freq[j], hy_f_w3[j], hy_skip_b[j], hy_w_out_b[j])
                  for x, s in zip(xs, shapes)]
        else:
            xs = [attention_block(x, s[0], s[1], g, at_w_qkv_b[j], at_q_gain[j], at_k_gain[j], at_w_o_b[j])
                  for x, s in zip(xs, shapes)]
        xs = hier_moe_block(xs, norm_ffn[i][None, :], moe_router_group[i], moe_router_expert[i],
                            moe_w_gate, moe_w_up, moe_w_down, i, g_final, i == depth - 1)
    return (xs[0].reshape(shapes[0]), xs[1].reshape(shapes[1]))
```

```python
import functools
import math

import jax
import jax.numpy as jnp
import numpy as np
from jax import lax
from jax.experimental import pallas as pl
from jax.experimental.pallas import tpu as pltpu

RMS_EPS = 1e-6
GRID_W = 64
EMB_DIM = 33
DECAY_TARGET = 1e-2
FAST_DECAY_PCT = 0.3
SLOW_DECAY_PCT = 1.5
MIN_DECAY = math.log(DECAY_TARGET) / FAST_DECAY_PCT
MAX_DECAY = math.log(DECAY_TARGET) / SLOW_DECAY_PCT
HEAD_DIM = 128
SCORE_SCALE = HEAD_DIM ** -0.5 * math.log2(math.e)
N_KV_HEADS = 4
ROPE_THETA = 10000.0
N_GROUPS = 4
EXPERTS_PER_GROUP = 8
N_MIXERS = 2

LANES = 128
BF16_SUBLANES = 16
VMEM_LIMIT_BYTES = 52 * 1024 * 1024
DFT_N2 = 128
MXU_ROWS = 256

BF16 = jnp.bfloat16
F32 = jnp.float32


def _params(sem):
    return pltpu.CompilerParams(dimension_semantics=sem, vmem_limit_bytes=VMEM_LIMIT_BYTES)


def _dot(a, b):
    return jnp.dot(a, b, preferred_element_type=F32)


def _dot_3pass(a, b):
    a_hi = a.astype(BF16)
    a_lo = (a - a_hi.astype(F32)).astype(BF16)
    b_hi = b.astype(BF16)
    b_lo = (b - b_hi.astype(F32)).astype(BF16)
    return _dot(a_hi, b_hi) + _dot(a_hi, b_lo) + _dot(a_lo, b_hi)


def _norm_rows(x, g):
    ms = jnp.mean(x * x, axis=-1, keepdims=True)
    return x * lax.rsqrt(ms + RMS_EPS) * g


def _tile(n, pref):
    return pref if n % pref == 0 else n


def _matmul_res_kernel(a_ref, w_ref, r_ref, o_ref):
    o_ref[...] = r_ref[...] + _dot(a_ref[...].astype(BF16), w_ref[...])


def matmul_residual(a, w, res, *, tm=1024, tn=512):
    T, K = a.shape
    N = w.shape[1]
    tm = _tile(T, tm)
    tn = _tile(N, tn)
    return pl.pallas_call(
        _matmul_res_kernel,
        grid=(T // tm, N // tn),
        in_specs=[pl.BlockSpec((tm, K), lambda i, j: (i, 0)),
                  pl.BlockSpec((K, tn), lambda i, j: (0, j)),
                  pl.BlockSpec((tm, tn), lambda i, j: (i, j))],
        out_specs=pl.BlockSpec((tm, tn), lambda i, j: (i, j)),
        out_shape=jax.ShapeDtypeStruct((T, N), F32),
        compiler_params=_params(("parallel", "arbitrary")),
        name="matmul_residual",
    )(a, w, res)


HALO = BF16_SUBLANES


def _hy_inproj_kernel(x_ref, xp_ref, xn_ref, g_ref, w0_ref, w1_ref, w2_ref,
                      cw0_ref, cw1_ref, cw2_ref, cb0_ref, cb1_ref, cb2_ref,
                      x0_out, zv_out, hn_ref, *, tm, tiles_per_seq):
    i = pl.program_id(0)
    j = pl.program_id(1)

    @pl.when(j == 0)
    def _():
        g = g_ref[...]
        hn_ref[HALO:HALO + tm, :] = _norm_rows(x_ref[...], g).astype(BF16)
        first = (i % tiles_per_seq) == 0
        last = (i % tiles_per_seq) == tiles_per_seq - 1
        hp = _norm_rows(xp_ref[...], g)
        hn_ref[0:HALO, :] = jnp.where(first, 0.0, hp).astype(BF16)
        hx = _norm_rows(xn_ref[...], g)
        hn_ref[HALO + tm:, :] = jnp.where(last, 0.0, hx).astype(BF16)

    h = hn_ref[...]
    rows = tm + 2 * HALO

    def conv(w_ref, cw_ref, cb_ref):
        u = _dot(h, w_ref[...])
        um = pltpu.roll(u, 1, 0)[HALO:HALO + tm]
        up = pltpu.roll(u, rows - 1, 0)[HALO:HALO + tm]
        uc = u[HALO:HALO + tm]
        cw = cw_ref[...]
        return um * cw[0:1] + uc * cw[1:2] + up * cw[2:3] + cb_ref[...]

    x0_out[...] = conv(w0_ref, cw0_ref, cb0_ref)
    x1 = conv(w1_ref, cw1_ref, cb1_ref)
    v = conv(w2_ref, cw2_ref, cb2_ref)
    zv_out[...] = v * x1


def hyena_inproj(x, g, w_in, conv_w, conv_b, seq_len, *, tm=512, tn=512):
    T, D = x.shape
    W = w_in.shape[1] // 3
    tm = _tile(seq_len, tm)
    tn = _tile(W, tn)
    nj = W // tn
    hb = tm // HALO
    n_halo_blocks = T // HALO
    kern = functools.partial(_hy_inproj_kernel, tm=tm, tiles_per_seq=seq_len // tm)

    def wspec(c):
        return pl.BlockSpec((D, tn), lambda i, j, c=c: (0, j + c * nj))

    def cwspec(c):
        return pl.BlockSpec((3, tn), lambda i, j, c=c: (0, j + c * nj))

    def cbspec(c):
        return pl.BlockSpec((1, tn), lambda i, j, c=c: (0, j + c * nj))

    return pl.pallas_call(
        kern,
        grid=(T // tm, nj),
        in_specs=[pl.BlockSpec((tm, D), lambda i, j: (i, 0)),
                  pl.BlockSpec((HALO, D), lambda i, j: (jnp.maximum(i * hb - 1, 0), 0)),
                  pl.BlockSpec((HALO, D), lambda i, j: (jnp.minimum((i + 1) * hb, n_halo_blocks - 1), 0)),
                  pl.BlockSpec((1, D), lambda i, j: (0, 0)),
                  wspec(0), wspec(1), wspec(2),
                  cwspec(0), cwspec(1), cwspec(2),
                  cbspec(0), cbspec(1), cbspec(2)],
        out_specs=[pl.BlockSpec((tm, tn), lambda i, j: (i, j)),
                   pl.BlockSpec((tm, tn), lambda i, j: (i, j))],
        out_shape=[jax.ShapeDtypeStruct((T, W), F32), jax.ShapeDtypeStruct((T, W), F32)],
        scratch_shapes=[pltpu.VMEM((tm + 2 * HALO, D), BF16)],
        compiler_params=_params(("parallel", "arbitrary")),
        name="hyena_inproj",
    )(x, x, x, g, w_in, w_in, w_in, conv_w, conv_w, conv_w, conv_b, conv_b, conv_b)


def _filter_mlp_kernel(z_ref, w1_ref, b1_ref, win_ref, bin_ref, fr_ref, h_ref):
    hp = lax.Precision.HIGHEST
    fr = fr_ref[...]
    h = jnp.sin(fr * (jnp.dot(z_ref[...], w1_ref[...], precision=hp, preferred_element_type=F32) + b1_ref[...]))
    for l in range(win_ref.shape[0]):
        h = jnp.sin(fr * (jnp.dot(h, win_ref[l], precision=hp, preferred_element_type=F32) + bin_ref[l]))
    h_ref[...] = h


def hyena_filter_features(L, D, f_w1, f_b1, f_win, f_bin, f_freq):
    H = f_w1.shape[1]
    t = jnp.linspace(0.0, 1.0, L, dtype=F32)[:, None]
    bands = (EMB_DIM - 1) // 2
    w = 2.0 * math.pi * jnp.arange(L, dtype=F32)[:, None] / L
    fb = jnp.linspace(1e-4, bands - 1, bands, dtype=F32)[None, :]
    z = jnp.concatenate([t, jnp.cos(fb * w), -jnp.sin(fb * w)], axis=-1)

    def lag_major_to_dft_order(a):
        return a.reshape(L // DFT_N2, DFT_N2, a.shape[1]).transpose(1, 0, 2).reshape(L, a.shape[1])

    z = lag_major_to_dft_order(z)
    t = lag_major_to_dft_order(t)
    zp = jnp.pad(z, ((0, 0), (0, LANES - EMB_DIM)))
    w1p = jnp.pad(f_w1, ((0, LANES - EMB_DIM), (0, 0)))
    n_in = f_win.shape[0]
    tl = _tile(L, 512)
    h = pl.pallas_call(
        _filter_mlp_kernel,
        grid=(L // tl,),
        in_specs=[pl.BlockSpec((tl, LANES), lambda i: (i, 0)),
                  pl.BlockSpec((LANES, H), lambda i: (0, 0)),
                  pl.BlockSpec((1, H), lambda i: (0, 0)),
                  pl.BlockSpec((n_in, H, H), lambda i: (0, 0, 0)),
                  pl.BlockSpec((n_in, 1, H), lambda i: (0, 0, 0)),
                  pl.BlockSpec((1, H), lambda i: (0, 0))],
        out_specs=pl.BlockSpec((tl, H), lambda i: (i, 0)),
        out_shape=jax.ShapeDtypeStruct((L, H), F32),
        compiler_params=_params(("parallel",)),
        name="filter_mlp",
    )(zp, w1p, f_b1[None, :], f_win, f_bin[:, None, :], f_freq[None, :])

    deltas = jnp.abs(jnp.linspace(MIN_DECAY, MAX_DECAY, D, dtype=F32))[None, :]
    return h, t, jnp.concatenate([deltas, deltas], axis=1)


def dft_tables(N1, N2):
    N = N1 * N2
    half = N1 // 2
    k1 = jnp.arange(N1, dtype=jnp.int32)
    ang_a = ((k1[:, None] * jnp.arange(half, dtype=jnp.int32)[None, :]) % N1).astype(F32) * (2.0 * math.pi / N1)
    ang_b = ((k1[None, :] * jnp.arange(N2, dtype=jnp.int32)[:, None]) % N).astype(F32) * (2.0 * math.pi / N)
    ca, sa = jnp.cos(ang_a)[None, :, :], jnp.sin(ang_a)[None, :, :]
    cb, sb = jnp.cos(ang_b)[:, :, None], jnp.sin(ang_b)[:, :, None]
    c = ca * cb - sa * sb
    s = sa * cb + ca * sb
    fwd = jnp.concatenate([jnp.concatenate([c, s], axis=2),
                           jnp.concatenate([-s, c], axis=2)], axis=1)
    inv = jnp.swapaxes(fwd, 1, 2)
    fwd_real = jnp.concatenate([c, -s], axis=1)
    k2 = jnp.arange(N2, dtype=jnp.int32)
    ang2 = ((k2[:, None] * k2[None, :]) % N2).astype(F32) * (2.0 * math.pi / N2)
    c2, s2 = jnp.cos(ang2), jnp.sin(ang2)
    f2 = jnp.concatenate([jnp.concatenate([c2, s2], axis=1),
                          jnp.concatenate([-s2, c2], axis=1)], axis=0)
    g2 = f2.T
    return fwd.astype(BF16), inv.astype(BF16), fwd_real.astype(BF16), f2.astype(BF16), g2.astype(BF16)


def _pack_complex(re, im):
    rb = lax.bitcast_convert_type(re.astype(BF16).astype(F32), jnp.uint32)
    ib = lax.bitcast_convert_type(im.astype(BF16).astype(F32), jnp.uint32)
    return (rb >> 16) | (ib & jnp.uint32(0xFFFF0000))


def _unpack_complex(p):
    re = lax.bitcast_convert_type(p << 16, F32)
    im = lax.bitcast_convert_type(p & jnp.uint32(0xFFFF0000), F32)
    return jnp.concatenate([re, im], axis=0).astype(BF16)


def _double_buffered(in_copies, out_copies, compute):
    j = pl.program_id(0)
    n = pl.num_programs(0)
    slot = j % 2

    @pl.when(j == 0)
    def _():
        for cp in in_copies(0, 0):
            cp.start()

    @pl.when(j + 1 < n)
    def _():
        for cp in in_copies(j + 1, 1 - slot):
            cp.start()

    for cp in in_copies(j, slot):
        cp.wait()

    @pl.when(j >= 2)
    def _():
        for cp in out_copies(j - 2, slot):
            cp.wait()

    compute(slot)
    for cp in out_copies(j, slot):
        cp.start()

    @pl.when(j == n - 1)
    def _():
        for cp in out_copies(j, slot):
            cp.wait()

        @pl.when(n >= 2)
        def _():
            for cp in out_copies(j - 1, 1 - slot):
                cp.wait()


def _fft_s1_kernel(z_hbm, m_ref, a_hbm, zbuf, obuf, sem_in, sem_out):
    n_g = zbuf.shape[1]
    half = zbuf.shape[2] // 2
    n1 = obuf.shape[2]

    def in_copies(step, slot):
        return [pltpu.make_async_copy(z_hbm.at[b, :, step * n_g + g, :], zbuf.at[slot, g, pl.ds(b * half, half)],
                                      sem_in.at[slot])
                for g in range(n_g) for b in range(2)]

    def out_copies(step, slot):
        return [pltpu.make_async_copy(obuf.at[slot, g], a_hbm.at[:, step * n_g + g, :], sem_out.at[slot])
                for g in range(n_g)]

    def compute(slot):
        for g in range(n_g):
            a = _dot(m_ref[g], zbuf[slot, g].astype(BF16))
            obuf[slot, g] = _pack_complex(a[:n1], a[n1:])

    _double_buffered(in_copies, out_copies, compute)


def _fft_s1_real_kernel(h_ref, t_ref, w3_ref, dl_ref, m_ref, a_hbm, s_ref, obuf, sem_out):
    n_g = obuf.shape[1]
    n1 = obuf.shape[2]
    j = pl.program_id(0)

    @pl.when(j == 0)
    def _():
        s_ref[...] = jnp.zeros_like(s_ref)

    def out_copies(step, slot):
        return [pltpu.make_async_copy(obuf.at[slot, g], a_hbm.at[:, step * n_g + g, :], sem_out.at[slot])
                for g in range(n_g)]

    def compute(slot):
        w3 = w3_ref[...]
        for g in range(n_g):
            taps = _dot_3pass(h_ref[g], w3) * jnp.exp(-t_ref[g] * dl_ref[...])
            if g == 0:
                row = lax.broadcasted_iota(jnp.int32, taps.shape, 0)
                col = lax.broadcasted_iota(jnp.int32, taps.shape, 1)
                drop = jnp.logical_and(jnp.logical_and(j == 0, row == 0), col >= taps.shape[1] // 2)
                taps = jnp.where(drop, 0.0, taps)
            s_ref[...] += jnp.sum(jnp.abs(taps), axis=0, keepdims=True)
            a = _dot(m_ref[g], taps.astype(BF16))
            obuf[slot, g] = _pack_complex(a[:n1], a[n1:])

    _double_buffered(lambda step, slot: [], out_copies, compute)


def _fft_mid_kernel(a_ref, af_ref, ab_ref, s_ref, f2_ref, g2_ref, c_ref):
    n2 = a_ref.shape[0]
    f2 = f2_ref[...]
    xf = _dot(f2, _unpack_complex(af_ref[...]))
    xb = _dot(f2, _unpack_complex(ab_ref[...]))
    s = s_ref[...]
    kr = (xf[:n2] + xb[:n2]) * s
    ki = (xf[n2:] - xb[n2:]) * s
    x = _dot(f2, _unpack_complex(a_ref[...]))
    xr, xi = x[:n2], x[n2:]
    y = jnp.concatenate([xr * kr - xi * ki, xr * ki + xi * kr], axis=0).astype(BF16)
    c = _dot(g2_ref[...], y)
    c_ref[...] = _pack_complex(c[:n2], c[n2:])


def _fft_out_kernel(c_hbm, m_ref, zv_hbm, x0_hbm, r_hbm, sk_ref, w_ref, o_hbm,
                    cbuf, zbuf, xbuf, rbuf, obuf, sem_in, sem_out):
    n_g = zbuf.shape[1]
    n1 = zbuf.shape[2]
    half = n1 // 2

    def in_copies(step, slot):
        cps = []
        for g in range(n_g):
            n2 = step * n_g + g
            cps.append(pltpu.make_async_copy(c_hbm.at[:, n2, :], cbuf.at[slot, g], sem_in.at[slot]))
            for b in range(2):
                rows = pl.ds(b * half, half)
                for src, dst in ((zv_hbm, zbuf), (x0_hbm, xbuf), (r_hbm, rbuf)):
                    cps.append(pltpu.make_async_copy(src.at[b, :, n2, :], dst.at[slot, g, rows], sem_in.at[slot]))
        return cps

    def out_copies(step, slot):
        return [pltpu.make_async_copy(obuf.at[slot, g, pl.ds(b * half, half)], o_hbm.at[b, :, step * n_g + g, :],
                                      sem_out.at[slot])
                for g in range(n_g) for b in range(2)]

    def compute(slot):
        ys = []
        for g in range(n_g):
            y = _dot(m_ref[g], _unpack_complex(cbuf[slot, g]))
            ys.append(((y + zbuf[slot, g] * sk_ref[...]) * xbuf[slot, g]).astype(BF16))
        proj = _dot(ys[0] if n_g == 1 else jnp.concatenate(ys, axis=0), w_ref[...])
        for g in range(n_g):
            obuf[slot, g] = rbuf[slot, g] + proj[g * n1:(g + 1) * n1]

    _double_buffered(in_copies, out_copies, compute)


def hyena_long_conv(zv, x0, x, w_out, filt, f_w3, skip_b, L):
    D = zv.shape[1]
    N = 2 * L
    N2 = DFT_N2
    N1 = N // N2
    half = N1 // 2
    n_g = max(1, min(N2, MXU_ROWS // N1))
    n_steps = N2 // n_g
    fwd, inv, fwd_real, f2, g2 = dft_tables(N1, N2)
    td2 = _tile(D, 2048)
    nd2 = D // td2
    zv4 = zv.reshape(2, half, N2, D)
    x04 = x0.reshape(2, half, N2, D)
    x4 = x.reshape(2, half, N2, D)
    any_spec = pl.BlockSpec(memory_space=pl.ANY)

    h, t, rates = filt
    H = h.shape[1]
    af, asum = pl.pallas_call(
        _fft_s1_real_kernel,
        grid=(n_steps,),
        in_specs=[pl.BlockSpec((n_g, half, H), lambda j: (j, 0, 0)),
                  pl.BlockSpec((n_g, half, 1), lambda j: (j, 0, 0)),
                  pl.BlockSpec((H, 2 * D), lambda j: (0, 0)),
                  pl.BlockSpec((1, 2 * D), lambda j: (0, 0)),
                  pl.BlockSpec((n_g, 2 * N1, half), lambda j: (j, 0, 0))],
        out_specs=[any_spec, pl.BlockSpec((1, 2 * D), lambda j: (0, 0))],
        out_shape=[jax.ShapeDtypeStruct((N1, N2, 2 * D), jnp.uint32), jax.ShapeDtypeStruct((1, 2 * D), F32)],
        scratch_shapes=[pltpu.VMEM((2, n_g, N1, 2 * D), jnp.uint32), pltpu.SemaphoreType.DMA((2,))],
        compiler_params=_params(("arbitrary",)),
        name="filter_dft_stage1",
    )(h.reshape(N2, half, H), t.reshape(N2, half, 1), f_w3, rates, fwd_real)
    scale = 1.0 / ((asum[:, :D] + asum[:, D:]) * N)

    a = pl.pallas_call(
        _fft_s1_kernel,
        grid=(n_steps,),
        in_specs=[any_spec,
                  pl.BlockSpec((n_g, 2 * N1, N1), lambda j: (j, 0, 0))],
        out_specs=any_spec,
        out_shape=jax.ShapeDtypeStruct((N1, N2, D), jnp.uint32),
        scratch_shapes=[pltpu.VMEM((2, n_g, N1, D), F32), pltpu.VMEM((2, n_g, N1, D), jnp.uint32),
                        pltpu.SemaphoreType.DMA((2,)), pltpu.SemaphoreType.DMA((2,))],
        compiler_params=_params(("arbitrary",)),
        name="conv_dft_stage1",
    )(zv4, fwd)

    c = pl.pallas_call(
        _fft_mid_kernel,
        grid=(N1, nd2),
        in_specs=[pl.BlockSpec((None, N2, td2), lambda k, j: (k, 0, j)),
                  pl.BlockSpec((None, N2, td2), lambda k, j: (k, 0, j)),
                  pl.BlockSpec((None, N2, td2), lambda k, j: (k, 0, j + nd2)),
                  pl.BlockSpec((1, td2), lambda k, j: (0, j)),
                  pl.BlockSpec((2 * N2, 2 * N2), lambda k, j: (0, 0)),
                  pl.BlockSpec((2 * N2, 2 * N2), lambda k, j: (0, 0))],
        out_specs=pl.BlockSpec((None, N2, td2), lambda k, j: (k, 0, j)),
        out_shape=jax.ShapeDtypeStruct((N1, N2, D), jnp.uint32),
        compiler_params=_params(("parallel", "parallel")),
        name="conv_dft_stage2",
    )(a, af, af, scale, f2, g2)

    def buf(dtype):
        return pltpu.VMEM((2, n_g, N1, D), dtype)

    o = pl.pallas_call(
        _fft_out_kernel,
        grid=(n_steps,),
        in_specs=[any_spec,
                  pl.BlockSpec((n_g, N1, 2 * N1), lambda j: (j, 0, 0)),
                  any_spec, any_spec, any_spec,
                  pl.BlockSpec((1, D), lambda j: (0, 0)),
                  pl.BlockSpec(w_out.shape, lambda j: (0, 0))],
        out_specs=any_spec,
        out_shape=jax.ShapeDtypeStruct((2, half, N2, D), F32),
        scratch_shapes=[buf(jnp.uint32), buf(F32), buf(F32), buf(F32), buf(F32),
                        pltpu.SemaphoreType.DMA((2,)), pltpu.SemaphoreType.DMA((2,))],
        compiler_params=_params(("arbitrary",)),
        name="conv_dft_inverse_outproj",
    )(c, inv, zv4, x04, x4, skip_b, w_out)
    return o.reshape(2 * L, D)


VT_ROWS = HEAD_DIM + BF16_SUBLANES


def _qkv_kernel(x_ref, g_ref, w_ref, gain_ref, cos_ref, sin_ref, o_ref, va_ref, vt_ref, hn_ref, *, n_rope_tiles):
    j = pl.program_id(1)

    @pl.when(j == 0)
    def _():
        hn_ref[...] = _norm_rows(x_ref[...], g_ref[...]).astype(BF16)

    u = _dot(hn_ref[...], w_ref[...])
    tn = u.shape[1]

    @pl.when(j < n_rope_tiles)
    def _():
        gain = gain_ref[...]
        cos = cos_ref[...]
        sin = sin_ref[...]
        ones = jnp.ones((HEAD_DIM, HEAD_DIM), BF16)
        for hh in range(tn // HEAD_DIM):
            xh = u[:, hh * HEAD_DIM:(hh + 1) * HEAD_DIM]
            sq = xh * xh
            sq_hi = sq.astype(BF16)
            sq_lo = (sq - sq_hi.astype(F32)).astype(BF16)
            ms = (_dot(sq_hi, ones) + _dot(sq_lo, ones)) * (1.0 / HEAD_DIM)
            y = xh * lax.rsqrt(ms + RMS_EPS) * gain
            sw = pltpu.roll(y, HEAD_DIM // 2, 1)
            o_ref[:, hh * HEAD_DIM:(hh + 1) * HEAD_DIM] = (y * cos + sw * sin).astype(o_ref.dtype)

    @pl.when(j >= n_rope_tiles)
    def _():
        o_ref[...] = u.astype(o_ref.dtype)
        ones = jnp.ones((u.shape[0], HEAD_DIM), va_ref.dtype)
        ones_t = jnp.ones((VT_ROWS - HEAD_DIM, u.shape[0]), vt_ref.dtype)
        for hh in range(tn // HEAD_DIM):
            vh = u[:, hh * HEAD_DIM:(hh + 1) * HEAD_DIM]
            va_ref[:, 2 * hh * HEAD_DIM:(2 * hh + 1) * HEAD_DIM] = vh.astype(va_ref.dtype)
            va_ref[:, (2 * hh + 1) * HEAD_DIM:(2 * hh + 2) * HEAD_DIM] = ones
            vt_ref[hh, :HEAD_DIM, :] = vh.T.astype(vt_ref.dtype)
            vt_ref[hh, HEAD_DIM:, :] = ones_t


def rope_tables(L):
    rows = L // GRID_W
    r, c = jnp.meshgrid(jnp.arange(rows, dtype=F32), jnp.arange(GRID_W, dtype=F32), indexing="ij")
    r = r.reshape(L)
    c = c.reshape(L)
    axis_dim = HEAD_DIM // 2
    inv = 1.0 / (ROPE_THETA ** (jnp.arange(0, axis_dim, 2, dtype=F32) / axis_dim))
    ang = jnp.concatenate([r[:, None] * inv[None], c[:, None] * inv[None]], axis=-1)
    cos = jnp.concatenate([jnp.cos(ang), jnp.cos(ang)], axis=1)
    sin = jnp.concatenate([-jnp.sin(ang), jnp.sin(ang)], axis=1)
    return cos, sin


def qkv_project(x, g, w_qkv, q_gain, k_gain, seq_len, *, tm=512):
    T, D = x.shape
    n_out = w_qkv.shape[1]
    nk = N_KV_HEADS * HEAD_DIM
    nq = n_out - 2 * nk
    tn = nk
    tm = _tile(seq_len, tm)
    n_rope = (nq + nk) // tn
    def deinterleave(a):
        lead = a.shape[:-1]
        a = a.reshape(lead + (a.shape[-1] // HEAD_DIM, HEAD_DIM // 2, 2))
        return jnp.swapaxes(a, -1, -2).reshape(lead + (-1,))

    w_qkv = jnp.concatenate([deinterleave(w_qkv[:, :nq + nk]), w_qkv[:, nq + nk:]], axis=1)
    q_gain = deinterleave(q_gain)
    k_gain = deinterleave(k_gain)
    gains = jnp.concatenate([jnp.tile((q_gain * SCORE_SCALE)[None, None, :], (nq // tn, 1, 1)),
                             k_gain[None, None, :],
                             jnp.ones((1, 1, HEAD_DIM), F32)], axis=0)
    cos, sin = rope_tables(seq_len)
    spt = seq_len // tm
    kern = functools.partial(_qkv_kernel, n_rope_tiles=n_rope)
    return pl.pallas_call(
        kern,
        grid=(T // tm, n_out // tn),
        in_specs=[pl.BlockSpec((tm, D), lambda i, j: (i, 0)),
                  pl.BlockSpec((1, D), lambda i, j: (0, 0)),
                  pl.BlockSpec((D, tn), lambda i, j: (0, j)),
                  pl.BlockSpec((None, 1, HEAD_DIM), lambda i, j: (j, 0, 0)),
                  pl.BlockSpec((tm, HEAD_DIM), lambda i, j: (i % spt, 0)),
                  pl.BlockSpec((tm, HEAD_DIM), lambda i, j: (i % spt, 0))],
        out_specs=[pl.BlockSpec((tm, tn), lambda i, j: (i, j)),
                   pl.BlockSpec((tm, 2 * nk), lambda i, j: (i, 0)),
                   pl.BlockSpec((None, N_KV_HEADS, VT_ROWS, tm), lambda i, j: (i // spt, 0, 0, i % spt))],
        out_shape=[jax.ShapeDtypeStruct((T, n_out), BF16), jax.ShapeDtypeStruct((T, 2 * nk), BF16),
                   jax.ShapeDtypeStruct((T // seq_len, N_KV_HEADS, VT_ROWS, seq_len), BF16)],
        scratch_shapes=[pltpu.VMEM((tm, D), BF16)],
        compiler_params=_params(("parallel", "arbitrary")),
        name="qkv_project",
    )(x, g, w_qkv, gains, cos, sin)


def _flash_kernel(q_ref, k_ref, v_ref, o_ref, qs_ref, s0_ref, s1_ref, p0_ref, p1_ref, a0_ref, a1_ref,
                  m_ref, acc_ref, *, group, tq, tk, nk):
    s_refs = (s0_ref, s1_ref)
    p_refs = (p0_ref, p1_ref)
    a_refs = (a0_ref, a1_ref)
    for gq in range(group):
        qs_ref[gq * tq:(gq + 1) * tq, :] = q_ref[:, gq * HEAD_DIM:(gq + 1) * HEAD_DIM]
    m_ref[...] = jnp.full_like(m_ref, -jnp.inf)
    acc_ref[...] = jnp.zeros_like(acc_ref)

    def rows_of(j):
        if isinstance(j, int):
            return pl.ds(j * tk, tk)
        return pl.ds(pl.multiple_of(j * tk, tk), tk)

    def scores(j, slot):
        kb = k_ref[rows_of(j), :]
        s_refs[slot][...] = lax.dot_general(qs_ref[...], kb, (((1,), (1,)), ((), ())),
                                            preferred_element_type=F32)

    def softmax(slot):
        s = s_refs[slot][...]
        m_prev = m_ref[...]
        m_new = jnp.maximum(m_prev, jnp.max(s, axis=-1, keepdims=True))
        a_refs[slot][...] = jnp.exp2(m_prev - m_new)
        p_refs[slot][...] = jnp.exp2(s - jnp.tile(m_new, (1, tk // LANES))).astype(BF16)
        m_ref[...] = m_new

    def accumulate(j, slot):
        pv = _dot(p_refs[slot][...], v_ref[rows_of(j), :])
        acc_ref[...] = acc_ref[...] * jnp.tile(a_refs[slot][...], (1, 2)) + pv

    def step(j, slot, first, last):
        if not last:
            scores(j + 1, 1 - slot)
        softmax(slot)
        if not first:
            accumulate(j - 1, 1 - slot)

    scores(0, 0)
    if nk >= 4 and nk % 2 == 0:
        step(0, 0, True, False)

        def pair(t, c):
            j = 2 * t + 1
            step(j, 1, False, False)
            step(j + 1, 0, False, False)
            return c

        lax.fori_loop(0, (nk - 2) // 2, pair, 0)
        step(nk - 1, 1, False, True)
    else:
        for j in range(nk):
            step(j, j % 2, j == 0, j == nk - 1)
    accumulate(nk - 1, (nk - 1) % 2)

    acc = acc_ref[...]
    o = acc[:, :HEAD_DIM] / acc[:, HEAD_DIM:]
    for gq in range(group):
        o_ref[:, gq * HEAD_DIM:(gq + 1) * HEAD_DIM] = o[gq * tq:(gq + 1) * tq].astype(o_ref.dtype)


def _flash_bounded_kernel(q_ref, k_ref, vt_ref, o_ref, qt_ref, acc_ref, *, group, tq, tk, nk, unroll):
    for gq in range(group):
        qt_ref[:, gq * tq:(gq + 1) * tq] = q_ref[:, gq * HEAD_DIM:(gq + 1) * HEAD_DIM].astype(F32).T.astype(BF16)
    acc_ref[...] = jnp.zeros_like(acc_ref)

    def block(j, c):
        rows = pl.ds(pl.multiple_of(j * tk, tk), tk)
        s_t = _dot(k_ref[rows, :], qt_ref[...])
        acc_ref[...] += _dot(vt_ref[:, rows], jnp.exp2(s_t).astype(BF16))
        return c

    lax.fori_loop(0, nk, block, 0, unroll=unroll)
    acc = acc_ref[...]
    o_t = acc[:HEAD_DIM] / acc[HEAD_DIM:HEAD_DIM + 1]
    for gq in range(group):
        o_ref[:, gq * HEAD_DIM:(gq + 1) * HEAD_DIM] = o_t[:, gq * tq:(gq + 1) * tq].T.astype(o_ref.dtype)


SAFE_SCORE_BOUND = 80.0


def flash_attention(qkv, v_aug, v_t, score_bound, B, L, n_heads, *, tq=256, tk=512):
    group = n_heads // N_KV_HEADS
    tq = _tile(L, tq)
    tk = _tile(L, tk)
    nk = L // tk
    qw = group * HEAD_DIM
    k_col = n_heads
    rows = group * tq
    q_spec = pl.BlockSpec((None, tq, qw), lambda b, h, qi: (b, qi, h))
    k_spec = pl.BlockSpec((None, L, HEAD_DIM), lambda b, h, qi: (b, 0, k_col + h))
    common = dict(
        grid=(B, N_KV_HEADS, L // tq),
        out_specs=pl.BlockSpec((None, tq, qw), lambda b, h, qi: (b, qi, h)),
        out_shape=jax.ShapeDtypeStruct((B, L, n_heads * HEAD_DIM), BF16),
        compiler_params=_params(("parallel", "parallel", "parallel")),
    )
    general = pl.pallas_call(
        functools.partial(_flash_kernel, group=group, tq=tq, tk=tk, nk=nk),
        in_specs=[q_spec, k_spec, pl.BlockSpec((None, L, 2 * HEAD_DIM), lambda b, h, qi: (b, 0, h))],
        scratch_shapes=[pltpu.VMEM((rows, HEAD_DIM), BF16),
                        pltpu.VMEM((rows, tk), F32), pltpu.VMEM((rows, tk), F32),
                        pltpu.VMEM((rows, tk), BF16), pltpu.VMEM((rows, tk), BF16),
                        pltpu.VMEM((rows, LANES), F32), pltpu.VMEM((rows, LANES), F32),
                        pltpu.VMEM((rows, LANES), F32),
                        pltpu.VMEM((rows, 2 * HEAD_DIM), F32)],
        name="flash_attention", **common)
    bounded = pl.pallas_call(
        functools.partial(_flash_bounded_kernel, group=group, tq=tq, tk=tk, nk=nk,
                          unroll=math.gcd(nk, 32)),
        in_specs=[q_spec, k_spec, pl.BlockSpec((None, None, VT_ROWS, L), lambda b, h, qi: (b, h, 0, 0))],
        scratch_shapes=[pltpu.VMEM((HEAD_DIM, rows), BF16),
                        pltpu.VMEM((VT_ROWS, rows), F32)],
        name="flash_attention_bounded", **common)
    return lax.cond(score_bound <= SAFE_SCORE_BOUND,
                    lambda ops: bounded(ops[0], ops[0], ops[2]),
                    lambda ops: general(ops[0], ops[0], ops[1]),
                    (qkv, v_aug, v_t))


def _pack_bf16_pairs(x):
    half = x.shape[1] // 2
    bits = lax.bitcast_convert_type(x.astype(BF16).astype(F32), jnp.uint32)
    return (bits[:, :half] >> 16) | (bits[:, half:] & jnp.uint32(0xFFFF0000))


def _unpack_bf16_pairs(p):
    lo = lax.bitcast_convert_type(p << 16, F32).astype(BF16)
    hi = lax.bitcast_convert_type(p & jnp.uint32(0xFFFF0000), F32).astype(BF16)
    return lo, hi


def _router_kernel(xa_ref, xb_ref, g_ref, rw_ref, h_ref, info_ref, cnt_ref, *, n_a_tiles):
    i = pl.program_id(0)

    def route(x_ref):
        hf = _norm_rows(x_ref[...], g_ref[...])
        h_ref[...] = _pack_bf16_pairs(hf)
        h_hi = hf.astype(BF16)
        h_lo = (hf - h_hi.astype(F32)).astype(BF16)
        rw = rw_ref[...]
        hw = _dot(h_hi, rw)
        logits = hw[:, :LANES] + hw[:, LANES:] + _dot(h_lo, rw[:, :LANES])
        shape = logits.shape
        lane = lax.broadcasted_iota(jnp.int32, shape, 1)
        lanef = lane.astype(F32)
        neg = -jnp.inf
        big = float(LANES)
        is_g = lane < N_GROUPS
        gl = jnp.where(is_g, logits, neg)
        gmax = jnp.max(gl, axis=-1, keepdims=True)
        gidx = jnp.min(jnp.where(gl == gmax, lanef, big), axis=-1, keepdims=True)
        p_g = 1.0 / jnp.sum(jnp.where(is_g, jnp.exp(logits - gmax), 0.0), axis=-1, keepdims=True)
        lo = N_GROUPS + gidx * EXPERTS_PER_GROUP
        el = jnp.where(lanef >= lo, jnp.where(lanef < lo + EXPERTS_PER_GROUP, logits, neg), neg)
        v1 = jnp.max(el, axis=-1, keepdims=True)
        i1 = jnp.min(jnp.where(el == v1, lanef, big), axis=-1, keepdims=True)
        el2 = jnp.where(lanef == i1, neg, el)
        v2 = jnp.max(el2, axis=-1, keepdims=True)
        i2 = jnp.min(jnp.where(el2 == v2, lanef, big), axis=-1, keepdims=True)
        t = jnp.exp(v2 - v1)
        w1 = p_g / (1.0 + t)
        w2 = p_g * t / (1.0 + t)
        e1 = i1 - N_GROUPS
        e2 = i2 - N_GROUPS
        oh1 = jnp.where(lanef == e1, 1.0, 0.0)
        oh2 = jnp.where(lanef == e2, 1.0, 0.0)
        oh = oh1 + oh2
        tm = shape[0]
        tri = jnp.where(lax.broadcasted_iota(jnp.int32, (tm, tm), 1) < lax.broadcasted_iota(jnp.int32, (tm, tm), 0),
                        1.0, 0.0).astype(BF16)
        before = _dot(tri, oh.astype(BF16)) + cnt_ref[...]
        r1 = jnp.sum(before * oh1, axis=-1, keepdims=True)
        r2 = jnp.sum(before * oh2, axis=-1, keepdims=True)
        cnt_ref[...] += jnp.sum(oh, axis=0, keepdims=True)
        vals = (e1, e2, w1, w2, r1, r2)
        info = jnp.zeros(shape, F32)
        for idx, val in enumerate(vals):
            info = jnp.where(lane == idx, val, info)
        info_ref[...] = info

    @pl.when(i == 0)
    def _():
        cnt_ref[...] = jnp.zeros_like(cnt_ref)

    @pl.when(i < n_a_tiles)
    def _():
        route(xa_ref)

    @pl.when(i >= n_a_tiles)
    def _():
        route(xb_ref)


def moe_route(xa, xb, g, router_group, router_expert, *, tm=512):
    Ta, D = xa.shape
    Tb = xb.shape[0]
    tm = _tile(math.gcd(Ta, Tb), tm)
    na, nb = Ta // tm, Tb // tm
    n_e = N_GROUPS * EXPERTS_PER_GROUP
    rw = jnp.concatenate([router_group, jnp.transpose(router_expert, (1, 0, 2)).reshape(D, n_e)], axis=1)
    rw = jnp.pad(rw, ((0, 0), (0, LANES - rw.shape[1])))
    rw_hi = rw.astype(BF16)
    rw = jnp.concatenate([rw_hi, (rw - rw_hi.astype(F32)).astype(BF16)], axis=1)
    kern = functools.partial(_router_kernel, n_a_tiles=na)
    return pl.pallas_call(
        kern,
        grid=(na + nb,),
        in_specs=[pl.BlockSpec((tm, D), lambda i: (jnp.minimum(i, na - 1), 0)),
                  pl.BlockSpec((tm, D), lambda i: (jnp.maximum(i - na, 0), 0)),
                  pl.BlockSpec((1, D), lambda i: (0, 0)),
                  pl.BlockSpec((D, 2 * LANES), lambda i: (0, 0))],
        out_specs=[pl.BlockSpec((tm, D // 2), lambda i: (i, 0)),
                   pl.BlockSpec((tm, LANES), lambda i: (i, 0)),
                   pl.BlockSpec((1, LANES), lambda i: (0, 0))],
        out_shape=[jax.ShapeDtypeStruct((Ta + Tb, D // 2), jnp.uint32),
                   jax.ShapeDtypeStruct((Ta + Tb, LANES), F32),
                   jax.ShapeDtypeStruct((1, LANES), F32)],
        compiler_params=_params(("arbitrary",)),
        name="moe_route",
    )(xa, xb, g, rw)


def moe_plan(info, cnt, tm, n_experts):
    T = info.shape[0]
    n_tiles = (2 * T) // tm + n_experts
    counts = cnt[0, :n_experts].astype(jnp.int32)
    padded = ((counts + tm - 1) // tm) * tm
    ends = jnp.cumsum(padded)
    off = ends - padded
    e = info[:, 0:2].T.astype(jnp.int32)
    rank = info[:, 4:6].T.astype(jnp.int32)
    k = jnp.arange(n_experts, dtype=jnp.int32)[:, None, None]
    pos = rank + jnp.sum(jnp.where(e[None] == k, off[:, None, None], 0), axis=0)
    pos = pos.T.reshape(-1)
    tile_start = jnp.arange(n_tiles, dtype=jnp.int32) * tm
    tile_e = jnp.sum((tile_start[:, None] >= ends[None, :]).astype(jnp.int32), axis=1)
    tile_e = jnp.minimum(tile_e, n_experts - 1)
    tile_valid = jnp.clip(counts[tile_e] - (tile_start - off[tile_e]), 0, tm).astype(jnp.int32)
    n_used = (ends[-1:] // tm).astype(jnp.int32)
    expert_last = jnp.where(padded > 0, ends - tm, -1)
    tail_tile = n_used[0] + jnp.arange(n_experts, dtype=jnp.int32)
    tail = jnp.where(tail_tile < n_tiles, tail_tile * tm, -1)
    zero_tiles = jnp.concatenate([expert_last, tail]).astype(jnp.int32)[None, :]
    return pos, tile_e, tile_valid, n_used, zero_tiles


DMA_UNROLL = 8


def _dispatch_kernel(pos_ref, ztile_ref, h_ref, hs_hbm, zbuf, sem, zsem):
    tm = h_ref.shape[0]

    @pl.when(pl.program_id(0) == 0)
    def _():
        zbuf[...] = jnp.zeros_like(zbuf)
        zrows = zbuf.shape[0]

        def zero_copy(k):
            return pltpu.make_async_copy(zbuf, hs_hbm.at[pl.ds(pl.multiple_of(ztile_ref[0, k], zrows), zrows)], zsem)

        for k in range(ztile_ref.shape[1]):
            @pl.when(ztile_ref[0, k] >= 0)
            def _():
                zero_copy(k).start()
        for k in range(ztile_ref.shape[1]):
            @pl.when(ztile_ref[0, k] >= 0)
            def _():
                zero_copy(k).wait()

    def row_copy(r, slot):
        return pltpu.make_async_copy(h_ref.at[pl.ds(r, 1)], hs_hbm.at[pl.ds(pos_ref[0, 0, 2 * r + slot], 1)], sem)

    def start(r, c):
        row_copy(r, 0).start()
        row_copy(r, 1).start()
        return c

    def wait(r, c):
        row_copy(r, 0).wait()
        row_copy(r, 1).wait()
        return c

    lax.fori_loop(0, tm, start, 0, unroll=DMA_UNROLL)
    lax.fori_loop(0, tm, wait, 0, unroll=DMA_UNROLL)


def moe_dispatch(h, pos, zero_tiles, n_sorted_rows, tile_rows, *, tm=512):
    T, C = h.shape
    tm = _tile(T, tm)
    return pl.pallas_call(
        _dispatch_kernel,
        grid=(T // tm,),
        in_specs=[pl.BlockSpec((1, 1, 2 * tm), lambda i: (i, 0, 0), memory_space=pltpu.SMEM),
                  pl.BlockSpec(zero_tiles.shape, lambda i: (0, 0), memory_space=pltpu.SMEM),
                  pl.BlockSpec((tm, C), lambda i: (i, 0))],
        out_specs=pl.BlockSpec(memory_space=pl.ANY),
        out_shape=jax.ShapeDtypeStruct((n_sorted_rows, C), h.dtype),
        scratch_shapes=[pltpu.VMEM((tile_rows, C), h.dtype), pltpu.SemaphoreType.DMA, pltpu.SemaphoreType.DMA],
        compiler_params=_params(("arbitrary",)),
        name="moe_dispatch",
    )(pos.reshape(T // tm, 1, 2 * tm), zero_tiles, h)


def _expert_kernel(te_ref, tv_ref, nu_ref, x_ref, wg_ref, wu_ref, wd_ref, y_ref):
    del te_ref, nu_ref
    valid = tv_ref[pl.program_id(0)]

    @pl.when(valid == 0)
    def _():
        y_ref[...] = jnp.zeros_like(y_ref)

    @pl.when(valid > 0)
    def _():
        x_lo, x_hi = _unpack_bf16_pairs(x_ref[...])
        half = x_ref.shape[1]
        gate = (_dot(x_lo, wg_ref[:half, :].astype(BF16)) + _dot(x_hi, wg_ref[half:, :].astype(BF16)))
        up = (_dot(x_lo, wu_ref[:half, :].astype(BF16)) + _dot(x_hi, wu_ref[half:, :].astype(BF16)))
        a = (gate * jax.nn.sigmoid(gate) * up).astype(BF16)
        y_ref[...] = _dot(a, wd_ref[...].astype(BF16))


def moe_experts(hs, tile_e, tile_valid, n_used, w_gate, w_up, w_down, layer, tm):
    P = hs.shape[0]
    D = w_gate.shape[2]
    d_e = w_gate.shape[-1]

    def row_map(i, te, tv, nu):
        return (jnp.minimum(i, nu[0] - 1), 0)

    def out_map(i, te, tv, nu):
        return (i, 0)

    def w_map(i, te, tv, nu):
        return (layer, te[i], 0, 0)

    grid_spec = pltpu.PrefetchScalarGridSpec(
        num_scalar_prefetch=3,
        grid=(P // tm,),
        in_specs=[pl.BlockSpec((tm, D // 2), row_map),
                  pl.BlockSpec((None, None, D, d_e), w_map),
                  pl.BlockSpec((None, None, D, d_e), w_map),
                  pl.BlockSpec((None, None, d_e, D), w_map)],
        out_specs=pl.BlockSpec((tm, D), out_map),
    )
    return pl.pallas_call(
        _expert_kernel,
        grid_spec=grid_spec,
        out_shape=jax.ShapeDtypeStruct((P, D), F32),
        compiler_params=_params(("arbitrary",)),
        name="moe_experts",
    )(tile_e, tile_valid, n_used, hs, w_gate, w_up, w_down)


def _combine_kernel(pos_ref, posn_ref, x_ref, info_ref, g_ref, y_hbm, o_ref, ybuf, sem, *, final_norm):
    tm = x_ref.shape[0]
    i = pl.program_id(0)
    buf = i % 2

    def row_copy(p_ref, b, r, slot):
        return pltpu.make_async_copy(y_hbm.at[pl.ds(p_ref[0, 0, 2 * r + slot], 1)],
                                     ybuf.at[b, slot, pl.ds(r, 1)], sem.at[b])

    def start_tile(p_ref, b):
        def body(r, c):
            row_copy(p_ref, b, r, 0).start()
            row_copy(p_ref, b, r, 1).start()
            return c
        lax.fori_loop(0, tm, body, 0, unroll=DMA_UNROLL)

    @pl.when(i == 0)
    def _():
        start_tile(pos_ref, 0)

    @pl.when(i + 1 < pl.num_programs(0))
    def _():
        start_tile(posn_ref, 1 - buf)

    def wait(r, c):
        row_copy(pos_ref, buf, r, 0).wait()
        row_copy(pos_ref, buf, r, 1).wait()
        return c

    lax.fori_loop(0, tm, wait, 0, unroll=DMA_UNROLL)
    info = info_ref[...]
    y = info[:, 2:3] * ybuf[buf, 0] + info[:, 3:4] * ybuf[buf, 1]
    x = x_ref[...] + y
    if final_norm:
        x = _norm_rows(x, g_ref[...])
    o_ref[...] = x


def moe_combine(x, y, pos, info, row_off, g_final, final_norm, *, tm=256):
    Tg, D = x.shape
    T = info.shape[0]
    tm = _tile(math.gcd(Tg, row_off) if row_off else Tg, tm)
    ob = row_off // tm
    kern = functools.partial(_combine_kernel, final_norm=final_norm)
    n = Tg // tm
    pos3 = pos.reshape(T // tm, 1, 2 * tm)
    return pl.pallas_call(
        kern,
        grid=(n,),
        in_specs=[pl.BlockSpec((1, 1, 2 * tm), lambda i: (i + ob, 0, 0), memory_space=pltpu.SMEM),
                  pl.BlockSpec((1, 1, 2 * tm), lambda i: (jnp.minimum(i + 1, n - 1) + ob, 0, 0),
                               memory_space=pltpu.SMEM),
                  pl.BlockSpec((tm, D), lambda i: (i, 0)),
                  pl.BlockSpec((tm, LANES), lambda i: (i + ob, 0)),
                  pl.BlockSpec((1, D), lambda i: (0, 0)),
                  pl.BlockSpec(memory_space=pl.ANY)],
        out_specs=pl.BlockSpec((tm, D), lambda i: (i, 0)),
        out_shape=jax.ShapeDtypeStruct((Tg, D), F32),
        scratch_shapes=[pltpu.VMEM((2, 2, tm, D), F32), pltpu.SemaphoreType.DMA((2,))],
        compiler_params=_params(("arbitrary",)),
        name="moe_combine",
    )(pos3, pos3, x, info, g_final, y)


MOE_TILE = 512


def hier_moe_block(xs, g, router_group, router_expert, w_gate, w_up, w_down, layer, g_final, final_norm):
    xa, xb = xs
    n_experts = w_gate.shape[1]
    h, info, cnt = moe_route(xa, xb, g, router_group, router_expert)
    T = h.shape[0]
    tm = _tile(2 * T, MOE_TILE)
    pos, tile_e, tile_valid, n_used, zero_tiles = moe_plan(info, cnt, tm, n_experts)
    hs = moe_dispatch(h, pos, zero_tiles, tile_e.shape[0] * tm, tm)
    y = moe_experts(hs, tile_e, tile_valid, n_used, w_gate, w_up, w_down, layer, tm)
    oa = moe_combine(xa, y, pos, info, 0, g_final, final_norm)
    ob = moe_combine(xb, y, pos, info, xa.shape[0], g_final, final_norm)
    return oa, ob


def hyena_block(x, L, g, w_in, conv_w, conv_b, f_w1, f_b1, f_win, f_bin, f_freq, f_w3, skip_b, w_out):
    D = x.shape[1]
    x0, zv = hyena_inproj(x, g, w_in, conv_w, conv_b[None, :], L)
    filt = hyena_filter_features(L, w_out.shape[0], f_w1, f_b1, f_win, f_bin, f_freq)
    return hyena_long_conv(zv, x0, x, w_out, filt, f_w3, skip_b[None, :], L)


def attention_block(x, B, L, g, w_qkv, q_gain, k_gain, w_o):
    D = x.shape[1]
    n_heads = w_o.shape[0] // HEAD_DIM
    qkv, v_aug, v_t = qkv_project(x, g, w_qkv, q_gain, k_gain, L)
    score_bound = 1.02 * HEAD_DIM * SCORE_SCALE * jnp.max(jnp.abs(q_gain)) * jnp.max(jnp.abs(k_gain))
    o = flash_attention(qkv.reshape(B, L, qkv.shape[1]), v_aug.reshape(B, L, v_aug.shape[1]), v_t, score_bound,
                        B, L, n_heads)
    return matmul_residual(o.reshape(B * L, n_heads * HEAD_DIM), w_o, x)


def kernel(x_prompt, x_sample, norm_mix, norm_ffn, norm_final, hy_w_in, hy_conv_w, hy_conv_b, hy_f_w1, hy_f_b1, hy_f_win, hy_f_bin, hy_f_freq, hy_f_w3, hy_skip_b, hy_w_out, at_w_qkv, at_q_gain, at_k_gain, at_w_o, moe_router_group, moe_router_expert, moe_w_gate, moe_w_up, moe_w_down):
    D = x_prompt.shape[-1]
    depth = norm_mix.shape[0]
    shapes = [x_prompt.shape, x_sample.shape]
    for s in shapes:
        assert s[0] == 2, "the long convolution packs exactly two sequences per group"
    xs = [x_prompt.reshape(-1, D), x_sample.reshape(-1, D)]
    hy_w_in_b = hy_w_in.astype(BF16)
    hy_w_out_b = hy_w_out.astype(BF16)
    at_w_qkv_b = at_w_qkv.astype(BF16)
    at_w_o_b = at_w_o.astype(BF16)
    g_final = norm_final[None, :]
    for i in range(depth):
        j = i // N_MIXERS
        g = norm_mix[i][None, :]
        if i % N_MIXERS == 0:
            xs = [hyena_block(x, s[1], g, hy_w_in_b[j], hy_conv_w[j], hy_conv_b[j], hy_f_w1[j], hy_f_b1[j],
                              hy_f_win[j], hy_f_bin[j], hy_f_freq[j], hy_f_w3[j], hy_skip_b[j], hy_w_out_b[j])
                  for x, s in zip(xs, shapes)]
        else:
            xs = [attention_block(x, s[0], s[1], g, at_w_qkv_b[j], at_q_gain[j], at_k_gain[j], at_w_o_b[j])
                  for x, s in zip(xs, shapes)]
        xs = hier_moe_block(xs, norm_ffn[i][None, :], moe_router_group[i], moe_router_expert[i],
                            moe_w_gate, moe_w_up, moe_w_down, i, g_final, i == depth - 1)
    return (xs[0].reshape(shapes[0]), xs[1].reshape(shapes[1]))
```

```python
import functools
import math

import jax
import jax.numpy as jnp
import numpy as np
from jax import lax
from jax.experimental import pallas as pl
from jax.experimental.pallas import tpu as pltpu

RMS_EPS = 1e-6
GRID_W = 64
EMB_DIM = 33
DECAY_TARGET = 1e-2
FAST_DECAY_PCT = 0.3
SLOW_DECAY_PCT = 1.5
MIN_DECAY = math.log(DECAY_TARGET) / FAST_DECAY_PCT
MAX_DECAY = math.log(DECAY_TARGET) / SLOW_DECAY_PCT
HEAD_DIM = 128
SCORE_SCALE = HEAD_DIM ** -0.5 * math.log2(math.e)
N_KV_HEADS = 4
ROPE_THETA = 10000.0
N_GROUPS = 4
EXPERTS_PER_GROUP = 8
N_MIXERS = 2

LANES = 128
BF16_SUBLANES = 16
VMEM_LIMIT_BYTES = 52 * 1024 * 1024
DFT_N2 = 128
MXU_ROWS = 256

BF16 = jnp.bfloat16
F32 = jnp.float32


def _params(sem):
    return pltpu.CompilerParams(dimension_semantics=sem, vmem_limit_bytes=VMEM_LIMIT_BYTES)


def _dot(a, b):
    return jnp.dot(a, b, preferred_element_type=F32)


def _dot_3pass(a, b):
    a_hi = a.astype(BF16)
    a_lo = (a - a_hi.astype(F32)).astype(BF16)
    b_hi = b.astype(BF16)
    b_lo = (b - b_hi.astype(F32)).astype(BF16)
    return _dot(a_hi, b_hi) + _dot(a_hi, b_lo) + _dot(a_lo, b_hi)


def _norm_rows(x, g):
    ms = jnp.mean(x * x, axis=-1, keepdims=True)
    return x * lax.rsqrt(ms + RMS_EPS) * g


def _tile(n, pref):
    return pref if n % pref == 0 else n


def _matmul_res_kernel(a_ref, w_ref, r_ref, o_ref):
    o_ref[...] = r_ref[...] + _dot(a_ref[...].astype(BF16), w_ref[...])


def matmul_residual(a, w, res, *, tm=1024, tn=512):
    T, K = a.shape
    N = w.shape[1]
    tm = _tile(T, tm)
    tn = _tile(N, tn)
    return pl.pallas_call(
        _matmul_res_kernel,
        grid=(T // tm, N // tn),
        in_specs=[pl.BlockSpec((tm, K), lambda i, j: (i, 0)),
                  pl.BlockSpec((K, tn), lambda i, j: (0, j)),
                  pl.BlockSpec((tm, tn), lambda i, j: (i, j))],
        out_specs=pl.BlockSpec((tm, tn), lambda i, j: (i, j)),
        out_shape=jax.ShapeDtypeStruct((T, N), F32),
        compiler_params=_params(("parallel", "arbitrary")),
        name="matmul_residual",
    )(a, w, res)


HALO = BF16_SUBLANES


def _hy_inproj_kernel(x_ref, xp_ref, xn_ref, g_ref, w0_ref, w1_ref, w2_ref,
                      cw0_ref, cw1_ref, cw2_ref, cb0_ref, cb1_ref, cb2_ref,
                      x0_out, zv_out, hn_ref, *, tm, tiles_per_seq):
    i = pl.program_id(0)
    j = pl.program_id(1)

    @pl.when(j == 0)
    def _():
        g = g_ref[...]
        hn_ref[HALO:HALO + tm, :] = _norm_rows(x_ref[...], g).astype(BF16)
        first = (i % tiles_per_seq) == 0
        last = (i % tiles_per_seq) == tiles_per_seq - 1
        hp = _norm_rows(xp_ref[...], g)
        hn_ref[0:HALO, :] = jnp.where(first, 0.0, hp).astype(BF16)
        hx = _norm_rows(xn_ref[...], g)
        hn_ref[HALO + tm:, :] = jnp.where(last, 0.0, hx).astype(BF16)

    h = hn_ref[...]
    rows = tm + 2 * HALO

    def conv(w_ref, cw_ref, cb_ref):
        u = _dot(h, w_ref[...])
        um = pltpu.roll(u, 1, 0)[HALO:HALO + tm]
        up = pltpu.roll(u, rows - 1, 0)[HALO:HALO + tm]
        uc = u[HALO:HALO + tm]
        cw = cw_ref[...]
        return um * cw[0:1] + uc * cw[1:2] + up * cw[2:3] + cb_ref[...]

    x0_out[...] = conv(w0_ref, cw0_ref, cb0_ref)
    x1 = conv(w1_ref, cw1_ref, cb1_ref)
    v = conv(w2_ref, cw2_ref, cb2_ref)
    zv_out[...] = v * x1


def hyena_inproj(x, g, w_in, conv_w, conv_b, seq_len, *, tm=512, tn=512):
    T, D = x.shape
    W = w_in.shape[1] // 3
    tm = _tile(seq_len, tm)
    tn = _tile(W, tn)
    nj = W // tn
    hb = tm // HALO
    n_halo_blocks = T // HALO
    kern = functools.partial(_hy_inproj_kernel, tm=tm, tiles_per_seq=seq_len // tm)

    def wspec(c):
        return pl.BlockSpec((D, tn), lambda i, j, c=c: (0, j + c * nj))

    def cwspec(c):
        return pl.BlockSpec((3, tn), lambda i, j, c=c: (0, j + c * nj))

    def cbspec(c):
        return pl.BlockSpec((1, tn), lambda i, j, c=c: (0, j + c * nj))

    return pl.pallas_call(
        kern,
        grid=(T // tm, nj),
        in_specs=[pl.BlockSpec((tm, D), lambda i, j: (i, 0)),
                  pl.BlockSpec((HALO, D), lambda i, j: (jnp.maximum(i * hb - 1, 0), 0)),
                  pl.BlockSpec((HALO, D), lambda i, j: (jnp.minimum((i + 1) * hb, n_halo_blocks - 1), 0)),
                  pl.BlockSpec((1, D), lambda i, j: (0, 0)),
                  wspec(0), wspec(1), wspec(2),
                  cwspec(0), cwspec(1), cwspec(2),
                  cbspec(0), cbspec(1), cbspec(2)],
        out_specs=[pl.BlockSpec((tm, tn), lambda i, j: (i, j)),
                   pl.BlockSpec((tm, tn), lambda i, j: (i, j))],
        out_shape=[jax.ShapeDtypeStruct((T, W), F32), jax.ShapeDtypeStruct((T, W), F32)],
        scratch_shapes=[pltpu.VMEM((tm + 2 * HALO, D), BF16)],
        compiler_params=_params(("parallel", "arbitrary")),
        name="hyena_inproj",
    )(x, x, x, g, w_in, w_in, w_in, conv_w, conv_w, conv_w, conv_b, conv_b, conv_b)


def _filter_mlp_kernel(z_ref, w1_ref, b1_ref, win_ref, bin_ref, fr_ref, h_ref):
    hp = lax.Precision.HIGHEST
    fr = fr_ref[...]
    h = jnp.sin(fr * (jnp.dot(z_ref[...], w1_ref[...], precision=hp, preferred_element_type=F32) + b1_ref[...]))
    for l in range(win_ref.shape[0]):
        h = jnp.sin(fr * (jnp.dot(h, win_ref[l], precision=hp, preferred_element_type=F32) + bin_ref[l]))
    h_ref[...] = h


def hyena_filter_features(L, D, f_w1, f_b1, f_win, f_bin, f_freq):
    H = f_w1.shape[1]
    t = jnp.linspace(0.0, 1.0, L, dtype=F32)[:, None]
    bands = (EMB_DIM - 1) // 2
    w = 2.0 * math.pi * jnp.arange(L, dtype=F32)[:, None] / L
    fb = jnp.linspace(1e-4, bands - 1, bands, dtype=F32)[None, :]
    z = jnp.concatenate([t, jnp.cos(fb * w), -jnp.sin(fb * w)], axis=-1)

    def lag_major_to_dft_order(a):
        return a.reshape(L // DFT_N2, DFT_N2, a.shape[1]).transpose(1, 0, 2).reshape(L, a.shape[1])

    z = lag_major_to_dft_order(z)
    t = lag_major_to_dft_order(t)
    zp = jnp.pad(z, ((0, 0), (0, LANES - EMB_DIM)))
    w1p = jnp.pad(f_w1, ((0, LANES - EMB_DIM), (0, 0)))
    n_in = f_win.shape[0]
    tl = _tile(L, 512)
    h = pl.pallas_call(
        _filter_mlp_kernel,
        grid=(L // tl,),
        in_specs=[pl.BlockSpec((tl, LANES), lambda i: (i, 0)),
                  pl.BlockSpec((LANES, H), lambda i: (0, 0)),
                  pl.BlockSpec((1, H), lambda i: (0, 0)),
                  pl.BlockSpec((n_in, H, H), lambda i: (0, 0, 0)),
                  pl.BlockSpec((n_in, 1, H), lambda i: (0, 0, 0)),
                  pl.BlockSpec((1, H), lambda i: (0, 0))],
        out_specs=pl.BlockSpec((tl, H), lambda i: (i, 0)),
        out_shape=jax.ShapeDtypeStruct((L, H), F32),
        compiler_params=_params(("parallel",)),
        name="filter_mlp",
    )(zp, w1p, f_b1[None, :], f_win, f_bin[:, None, :], f_freq[None, :])

    deltas = jnp.abs(jnp.linspace(MIN_DECAY, MAX_DECAY, D, dtype=F32))[None, :]
    return h, t, jnp.concatenate([deltas, deltas], axis=1)


def dft_tables(N1, N2):
    N = N1 * N2
    half = N1 // 2
    k1 = jnp.arange(N1, dtype=jnp.int32)
    ang_a = ((k1[:, None] * jnp.arange(half, dtype=jnp.int32)[None, :]) % N1).astype(F32) * (2.0 * math.pi / N1)
    ang_b = ((k1[None, :] * jnp.arange(N2, dtype=jnp.int32)[:, None]) % N).astype(F32) * (2.0 * math.pi / N)
    ca, sa = jnp.cos(ang_a)[None, :, :], jnp.sin(ang_a)[None, :, :]
    cb, sb = jnp.cos(ang_b)[:, :, None], jnp.sin(ang_b)[:, :, None]
    c = ca * cb - sa * sb
    s = sa * cb + ca * sb
    fwd = jnp.concatenate([jnp.concatenate([c, s], axis=2),
                           jnp.concatenate([-s, c], axis=2)], axis=1)
    inv = jnp.swapaxes(fwd, 1, 2)
    fwd_real = jnp.concatenate([c, -s], axis=1)
    k2 = jnp.arange(N2, dtype=jnp.int32)
    ang2 = ((k2[:, None] * k2[None, :]) % N2).astype(F32) * (2.0 * math.pi / N2)
    c2, s2 = jnp.cos(ang2), jnp.sin(ang2)
    f2 = jnp.concatenate([jnp.concatenate([c2, s2], axis=1),
                          jnp.concatenate([-s2, c2], axis=1)], axis=0)
    g2 = f2.T
    return fwd.astype(BF16), inv.astype(BF16), fwd_real.astype(BF16), f2.astype(BF16), g2.astype(BF16)


def _pack_complex(re, im):
    rb = lax.bitcast_convert_type(re.astype(BF16).astype(F32), jnp.uint32)
    ib = lax.bitcast_convert_type(im.astype(BF16).astype(F32), jnp.uint32)
    return (rb >> 16) | (ib & jnp.uint32(0xFFFF0000))


def _unpack_complex(p):
    re = lax.bitcast_convert_type(p << 16, F32)
    im = lax.bitcast_convert_type(p & jnp.uint32(0xFFFF0000), F32)
    return jnp.concatenate([re, im], axis=0).astype(BF16)


def _double_buffered(in_copies, out_copies, compute):
    j = pl.program_id(0)
    n = pl.num_programs(0)
    slot = j % 2

    @pl.when(j == 0)
    def _():
        for cp in in_copies(0, 0):
            cp.start()

    @pl.when(j + 1 < n)
    def _():
        for cp in in_copies(j + 1, 1 - slot):
            cp.start()

    for cp in in_copies(j, slot):
        cp.wait()

    @pl.when(j >= 2)
    def _():
        for cp in out_copies(j - 2, slot):
            cp.wait()

    compute(slot)
    for cp in out_copies(j, slot):
        cp.start()

    @pl.when(j == n - 1)
    def _():
        for cp in out_copies(j, slot):
            cp.wait()

        @pl.when(n >= 2)
        def _():
            for cp in out_copies(j - 1, 1 - slot):
                cp.wait()


def _fft_s1_kernel(z_hbm, m_ref, a_hbm, zbuf, obuf, sem_in, sem_out):
    n_g = zbuf.shape[1]
    half = zbuf.shape[2] // 2
    n1 = obuf.shape[2]

    def in_copies(step, slot):
        return [pltpu.make_async_copy(z_hbm.at[b, :, step * n_g + g, :], zbuf.at[slot, g, pl.ds(b * half, half)],
                                      sem_in.at[slot])
                for g in range(n_g) for b in range(2)]

    def out_copies(step, slot):
        return [pltpu.make_async_copy(obuf.at[slot, g], a_hbm.at[:, step * n_g + g, :], sem_out.at[slot])
                for g in range(n_g)]

    def compute(slot):
        for g in range(n_g):
            a = _dot(m_ref[g], zbuf[slot, g].astype(BF16))
            obuf[slot, g] = _pack_complex(a[:n1], a[n1:])

    _double_buffered(in_copies, out_copies, compute)


def _fft_s1_real_kernel(h_ref, t_ref, w3_ref, dl_ref, m_ref, a_hbm, s_ref, obuf, sem_out):
    n_g = obuf.shape[1]
    n1 = obuf.shape[2]
    j = pl.program_id(0)

    @pl.when(j == 0)
    def _():
        s_ref[...] = jnp.zeros_like(s_ref)

    def out_copies(step, slot):
        return [pltpu.make_async_copy(obuf.at[slot, g], a_hbm.at[:, step * n_g + g, :], sem_out.at[slot])
                for g in range(n_g)]

    def compute(slot):
        w3 = w3_ref[...]
        for g in range(n_g):
            taps = _dot_3pass(h_ref[g], w3) * jnp.exp(-t_ref[g] * dl_ref[...])
            if g == 0:
                row = lax.broadcasted_iota(jnp.int32, taps.shape, 0)
                col = lax.broadcasted_iota(jnp.int32, taps.shape, 1)
                drop = jnp.logical_and(jnp.logical_and(j == 0, row == 0), col >= taps.shape[1] // 2)
                taps = jnp.where(drop, 0.0, taps)
            s_ref[...] += jnp.sum(jnp.abs(taps), axis=0, keepdims=True)
            a = _dot(m_ref[g], taps.astype(BF16))
            obuf[slot, g] = _pack_complex(a[:n1], a[n1:])

    _double_buffered(lambda step, slot: [], out_copies, compute)


def _fft_mid_kernel(a_ref, af_ref, ab_ref, s_ref, f2_ref, g2_ref, c_ref):
    n2 = a_ref.shape[0]
    f2 = f2_ref[...]
    xf = _dot(f2, _unpack_complex(af_ref[...]))
    xb = _dot(f2, _unpack_complex(ab_ref[...]))
    s = s_ref[...]
    kr = (xf[:n2] + xb[:n2]) * s
    ki = (xf[n2:] - xb[n2:]) * s
    x = _dot(f2, _unpack_complex(a_ref[...]))
    xr, xi = x[:n2], x[n2:]
    y = jnp.concatenate([xr * kr - xi * ki, xr * ki + xi * kr], axis=0).astype(BF16)
    c = _dot(g2_ref[...], y)
    c_ref[...] = _pack_complex(c[:n2], c[n2:])


def _fft_out_kernel(c_hbm, m_ref, zv_hbm, x0_hbm, r_hbm, sk_ref, w_ref, o_hbm,
                    cbuf, zbuf, xbuf, rbuf, obuf, sem_in, sem_out):
    n_g = zbuf.shape[1]
    n1 = zbuf.shape[2]
    half = n1 // 2

    def in_copies(step, slot):
        cps = []
        for g in range(n_g):
            n2 = step * n_g + g
            cps.append(pltpu.make_async_copy(c_hbm.at[:, n2, :], cbuf.at[slot, g], sem_in.at[slot]))
            for b in range(2):
                rows = pl.ds(b * half, half)
                for src, dst in ((zv_hbm, zbuf), (x0_hbm, xbuf), (r_hbm, rbuf)):
                    cps.append(pltpu.make_async_copy(src.at[b, :, n2, :], dst.at[slot, g, rows], sem_in.at[slot]))
        return cps

    def out_copies(step, slot):
        return [pltpu.make_async_copy(obuf.at[slot, g, pl.ds(b * half, half)], o_hbm.at[b, :, step * n_g + g, :],
                                      sem_out.at[slot])
                for g in range(n_g) for b in range(2)]

    def compute(slot):
        ys = []
        for g in range(n_g):
            y = _dot(m_ref[g], _unpack_complex(cbuf[slot, g]))
            ys.append(((y + zbuf[slot, g] * sk_ref[...]) * xbuf[slot, g]).astype(BF16))
        proj = _dot(ys[0] if n_g == 1 else jnp.concatenate(ys, axis=0), w_ref[...])
        for g in range(n_g):
            obuf[slot, g] = rbuf[slot, g] + proj[g * n1:(g + 1) * n1]

    _double_buffered(in_copies, out_copies, compute)


def hyena_long_conv(zv, x0, x, w_out, filt, f_w3, skip_b, L):
    D = zv.shape[1]
    N = 2 * L
    N2 = DFT_N2
    N1 = N // N2
    half = N1 // 2
    n_g = max(1, min(N2, MXU_ROWS // N1))
    n_steps = N2 // n_g
    fwd, inv, fwd_real, f2, g2 = dft_tables(N1, N2)
    td2 = _tile(D, 2048)
    nd2 = D // td2
    zv4 = zv.reshape(2, half, N2, D)
    x04 = x0.reshape(2, half, N2, D)
    x4 = x.reshape(2, half, N2, D)
    any_spec = pl.BlockSpec(memory_space=pl.ANY)

    h, t, rates = filt
    H = h.shape[1]
    af, asum = pl.pallas_call(
        _fft_s1_real_kernel,
        grid=(n_steps,),
        in_specs=[pl.BlockSpec((n_g, half, H), lambda j: (j, 0, 0)),
                  pl.BlockSpec((n_g, half, 1), lambda j: (j, 0, 0)),
                  pl.BlockSpec((H, 2 * D), lambda j: (0, 0)),
                  pl.BlockSpec((1, 2 * D), lambda j: (0, 0)),
                  pl.BlockSpec((n_g, 2 * N1, half), lambda j: (j, 0, 0))],
        out_specs=[any_spec, pl.BlockSpec((1, 2 * D), lambda j: (0, 0))],
        out_shape=[jax.ShapeDtypeStruct((N1, N2, 2 * D), jnp.uint32), jax.ShapeDtypeStruct((1, 2 * D), F32)],
        scratch_shapes=[pltpu.VMEM((2, n_g, N1, 2 * D), jnp.uint32), pltpu.SemaphoreType.DMA((2,))],
        compiler_params=_params(("arbitrary",)),
        name="filter_dft_stage1",
    )(h.reshape(N2, half, H), t.reshape(N2, half, 1), f_w3, rates, fwd_real)
    scale = 1.0 / ((asum[:, :D] + asum[:, D:]) * N)

    a = pl.pallas_call(
        _fft_s1_kernel,
        grid=(n_steps,),
        in_specs=[any_spec,
                  pl.BlockSpec((n_g, 2 * N1, N1), lambda j: (j, 0, 0))],
        out_specs=any_spec,
        out_shape=jax.ShapeDtypeStruct((N1, N2, D), jnp.uint32),
        scratch_shapes=[pltpu.VMEM((2, n_g, N1, D), F32), pltpu.VMEM((2, n_g, N1, D), jnp.uint32),
                        pltpu.SemaphoreType.DMA((2,)), pltpu.SemaphoreType.DMA((2,))],
        compiler_params=_params(("arbitrary",)),
        name="conv_dft_stage1",
    )(zv4, fwd)

    c = pl.pallas_call(
        _fft_mid_kernel,
        grid=(N1, nd2),
        in_specs=[pl.BlockSpec((None, N2, td2), lambda k, j: (k, 0, j)),
                  pl.BlockSpec((None, N2, td2), lambda k, j: (k, 0, j)),
                  pl.BlockSpec((None, N2, td2), lambda k, j: (k, 0, j + nd2)),
                  pl.BlockSpec((1, td2), lambda k, j: (0, j)),
                  pl.BlockSpec((2 * N2, 2 * N2), lambda k, j: (0, 0)),
                  pl.BlockSpec((2 * N2, 2 * N2), lambda k, j: (0, 0))],
        out_specs=pl.BlockSpec((None, N2, td2), lambda k, j: (k, 0, j)),
        out_shape=jax.ShapeDtypeStruct((N1, N2, D), jnp.uint32),
        compiler_params=_params(("parallel", "parallel")),
        name="conv_dft_stage2",
    )(a, af, af, scale, f2, g2)

    def buf(dtype):
        return pltpu.VMEM((2, n_g, N1, D), dtype)

    o = pl.pallas_call(
        _fft_out_kernel,
        grid=(n_steps,),
        in_specs=[any_spec,
                  pl.BlockSpec((n_g, N1, 2 * N1), lambda j: (j, 0, 0)),
                  any_spec, any_spec, any_spec,
                  pl.BlockSpec((1, D), lambda j: (0, 0)),
                  pl.BlockSpec(w_out.shape, lambda j: (0, 0))],
        out_specs=any_spec,
        out_shape=jax.ShapeDtypeStruct((2, half, N2, D), F32),
        scratch_shapes=[buf(jnp.uint32), buf(F32), buf(F32), buf(F32), buf(F32),
                        pltpu.SemaphoreType.DMA((2,)), pltpu.SemaphoreType.DMA((2,))],
        compiler_params=_params(("arbitrary",)),
        name="conv_dft_inverse_outproj",
    )(c, inv, zv4, x04, x4, skip_b, w_out)
    return o.reshape(2 * L, D)


VT_ROWS = HEAD_DIM + BF16_SUBLANES


def _qkv_kernel(x_ref, g_ref, w_ref, gain_ref, cos_ref, sin_ref, o_ref, va_ref, vt_ref, hn_ref, *, n_rope_tiles):
    j = pl.program_id(1)

    @pl.when(j == 0)
    def _():
        hn_ref[...] = _norm_rows(x_ref[...], g_ref[...]).astype(BF16)

    u = _dot(hn_ref[...], w_ref[...])
    tn = u.shape[1]

    @pl.when(j < n_rope_tiles)
    def _():
        gain = gain_ref[...]
        cos = cos_ref[...]
        sin = sin_ref[...]
        ones = jnp.ones((HEAD_DIM, HEAD_DIM), BF16)
        for hh in range(tn // HEAD_DIM):
            xh = u[:, hh * HEAD_DIM:(hh + 1) * HEAD_DIM]
            sq = xh * xh
            sq_hi = sq.astype(BF16)
            sq_lo = (sq - sq_hi.astype(F32)).astype(BF16)
            ms = (_dot(sq_hi, ones) + _dot(sq_lo, ones)) * (1.0 / HEAD_DIM)
            y = xh * lax.rsqrt(ms + RMS_EPS) * gain
            sw = pltpu.roll(y, HEAD_DIM // 2, 1)
            o_ref[:, hh * HEAD_DIM:(hh + 1) * HEAD_DIM] = (y * cos + sw * sin).astype(o_ref.dtype)

    @pl.when(j >= n_rope_tiles)
    def _():
        o_ref[...] = u.astype(o_ref.dtype)
        ones = jnp.ones((u.shape[0], HEAD_DIM), va_ref.dtype)
        ones_t = jnp.ones((VT_ROWS - HEAD_DIM, u.shape[0]), vt_ref.dtype)
        for hh in range(tn // HEAD_DIM):
            vh = u[:, hh * HEAD_DIM:(hh + 1) * HEAD_DIM]
            va_ref[:, 2 * hh * HEAD_DIM:(2 * hh + 1) * HEAD_DIM] = vh.astype(va_ref.dtype)
            va_ref[:, (2 * hh + 1) * HEAD_DIM:(2 * hh + 2) * HEAD_DIM] = ones
            vt_ref[hh, :HEAD_DIM, :] = vh.T.astype(vt_ref.dtype)
            vt_ref[hh, HEAD_DIM:, :] = ones_t


def rope_tables(L):
    rows = L // GRID_W
    r, c = jnp.meshgrid(jnp.arange(rows, dtype=F32), jnp.arange(GRID_W, dtype=F32), indexing="ij")
    r = r.reshape(L)
    c = c.reshape(L)
    axis_dim = HEAD_DIM // 2
    inv = 1.0 / (ROPE_THETA ** (jnp.arange(0, axis_dim, 2, dtype=F32) / axis_dim))
    ang = jnp.concatenate([r[:, None] * inv[None], c[:, None] * inv[None]], axis=-1)
    cos = jnp.concatenate([jnp.cos(ang), jnp.cos(ang)], axis=1)
    sin = jnp.concatenate([-jnp.sin(ang), jnp.sin(ang)], axis=1)
    return cos, sin


def qkv_project(x, g, w_qkv, q_gain, k_gain, seq_len, *, tm=512):
    T, D = x.shape
    n_out = w_qkv.shape[1]
    nk = N_KV_HEADS * HEAD_DIM
    nq = n_out - 2 * nk
    tn = nk
    tm = _tile(seq_len, tm)
    n_rope = (nq + nk) // tn
    def deinterleave(a):
        lead = a.shape[:-1]
        a = a.reshape(lead + (a.shape[-1] // HEAD_DIM, HEAD_DIM // 2, 2))
        return jnp.swapaxes(a, -1, -2).reshape(lead + (-1,))

    w_qkv = jnp.concatenate([deinterleave(w_qkv[:, :nq + nk]), w_qkv[:, nq + nk:]], axis=1)
    q_gain = deinterleave(q_gain)
    k_gain = deinterleave(k_gain)
    gains = jnp.concatenate([jnp.tile((q_gain * SCORE_SCALE)[None, None, :], (nq // tn, 1, 1)),
                             k_gain[None, None, :],
                             jnp.ones((1, 1, HEAD_DIM), F32)], axis=0)
    cos, sin = rope_tables(seq_len)
    spt = seq_len // tm
    kern = functools.partial(_qkv_kernel, n_rope_tiles=n_rope)
    return pl.pallas_call(
        kern,
        grid=(T // tm, n_out // tn),
        in_specs=[pl.BlockSpec((tm, D), lambda i, j: (i, 0)),
                  pl.BlockSpec((1, D), lambda i, j: (0, 0)),
                  pl.BlockSpec((D, tn), lambda i, j: (0, j)),
                  pl.BlockSpec((None, 1, HEAD_DIM), lambda i, j: (j, 0, 0)),
                  pl.BlockSpec((tm, HEAD_DIM), lambda i, j: (i % spt, 0)),
                  pl.BlockSpec((tm, HEAD_DIM), lambda i, j: (i % spt, 0))],
        out_specs=[pl.BlockSpec((tm, tn), lambda i, j: (i, j)),
                   pl.BlockSpec((tm, 2 * nk), lambda i, j: (i, 0)),
                   pl.BlockSpec((None, N_KV_HEADS, VT_ROWS, tm), lambda i, j: (i // spt, 0, 0, i % spt))],
        out_shape=[jax.ShapeDtypeStruct((T, n_out), BF16), jax.ShapeDtypeStruct((T, 2 * nk), BF16),
                   jax.ShapeDtypeStruct((T // seq_len, N_KV_HEADS, VT_ROWS, seq_len), BF16)],
        scratch_shapes=[pltpu.VMEM((tm, D), BF16)],
        compiler_params=_params(("parallel", "arbitrary")),
        name="qkv_project",
    )(x, g, w_qkv, gains, cos, sin)


def _flash_kernel(q_ref, k_ref, v_ref, o_ref, qs_ref, s0_ref, s1_ref, p0_ref, p1_ref, a0_ref, a1_ref,
                  m_ref, acc_ref, *, group, tq, tk, nk):
    s_refs = (s0_ref, s1_ref)
    p_refs = (p0_ref, p1_ref)
    a_refs = (a0_ref, a1_ref)
    for gq in range(group):
        qs_ref[gq * tq:(gq + 1) * tq, :] = q_ref[:, gq * HEAD_DIM:(gq + 1) * HEAD_DIM]
    m_ref[...] = jnp.full_like(m_ref, -jnp.inf)
    acc_ref[...] = jnp.zeros_like(acc_ref)

    def rows_of(j):
        if isinstance(j, int):
            return pl.ds(j * tk, tk)
        return pl.ds(pl.multiple_of(j * tk, tk), tk)

    def scores(j, slot):
        kb = k_ref[rows_of(j), :]
        s_refs[slot][...] = lax.dot_general(qs_ref[...], kb, (((1,), (1,)), ((), ())),
                                            preferred_element_type=F32)

    def softmax(slot):
        s = s_refs[slot][...]
        m_prev = m_ref[...]
        m_new = jnp.maximum(m_prev, jnp.max(s, axis=-1, keepdims=True))
        a_refs[slot][...] = jnp.exp2(m_prev - m_new)
        p_refs[slot][...] = jnp.exp2(s - jnp.tile(m_new, (1, tk // LANES))).astype(BF16)
        m_ref[...] = m_new

    def accumulate(j, slot):
        pv = _dot(p_refs[slot][...], v_ref[rows_of(j), :])
        acc_ref[...] = acc_ref[...] * jnp.tile(a_refs[slot][...], (1, 2)) + pv

    def step(j, slot, first, last):
        if not last:
            scores(j + 1, 1 - slot)
        softmax(slot)
        if not first:
            accumulate(j - 1, 1 - slot)

    scores(0, 0)
    if nk >= 4 and nk % 2 == 0:
        step(0, 0, True, False)

        def pair(t, c):
            j = 2 * t + 1
            step(j, 1, False, False)
            step(j + 1, 0, False, False)
            return c

        lax.fori_loop(0, (nk - 2) // 2, pair, 0)
        step(nk - 1, 1, False, True)
    else:
        for j in range(nk):
            step(j, j % 2, j == 0, j == nk - 1)
    accumulate(nk - 1, (nk - 1) % 2)

    acc = acc_ref[...]
    o = acc[:, :HEAD_DIM] / acc[:, HEAD_DIM:]
    for gq in range(group):
        o_ref[:, gq * HEAD_DIM:(gq + 1) * HEAD_DIM] = o[gq * tq:(gq + 1) * tq].astype(o_ref.dtype)


def _flash_bounded_kernel(q_ref, k_ref, vt_ref, o_ref, qt_ref, acc_ref, *, group, tq, tk, nk, unroll):
    for gq in range(group):
        qt_ref[:, gq * tq:(gq + 1) * tq] = q_ref[:, gq * HEAD_DIM:(gq + 1) * HEAD_DIM].astype(F32).T.astype(BF16)
    acc_ref[...] = jnp.zeros_like(acc_ref)

    def block(j, c):
        rows = pl.ds(pl.multiple_of(j * tk, tk), tk)
        s_t = _dot(k_ref[rows, :], qt_ref[...])
        acc_ref[...] += _dot(vt_ref[:, rows], jnp.exp2(s_t).astype(BF16))
        return c

    lax.fori_loop(0, nk, block, 0, unroll=unroll)
    acc = acc_ref[...]
    o_t = acc[:HEAD_DIM] / acc[HEAD_DIM:HEAD_DIM + 1]
    for gq in range(group):
        o_ref[:, gq * HEAD_DIM:(gq + 1) * HEAD_DIM] = o_t[:, gq * tq:(gq + 1) * tq].T.astype(o_ref.dtype)


SAFE_SCORE_BOUND = 80.0


def flash_attention(qkv, v_aug, v_t, score_bound, B, L, n_heads, *, tq=256, tq_bounded=512, tk=512):
    group = n_heads // N_KV_HEADS
    tk = _tile(L, tk)
    nk = L // tk
    qw = group * HEAD_DIM
    k_col = n_heads
    k_spec = pl.BlockSpec((None, L, HEAD_DIM), lambda b, h, qi: (b, 0, k_col + h))

    def call(kernel, name, tq, last_spec, scratch):
        tq = _tile(L, tq)
        q_spec = pl.BlockSpec((None, tq, qw), lambda b, h, qi: (b, qi, h))
        return pl.pallas_call(
            functools.partial(kernel, group=group, tq=tq, tk=tk, nk=nk),
            grid=(B, N_KV_HEADS, L // tq),
            in_specs=[q_spec, k_spec, last_spec],
            out_specs=q_spec,
            out_shape=jax.ShapeDtypeStruct((B, L, n_heads * HEAD_DIM), BF16),
            scratch_shapes=scratch(group * tq),
            compiler_params=_params(("parallel", "parallel", "parallel")),
            name=name)

    general = call(
        _flash_kernel, "flash_attention", tq,
        pl.BlockSpec((None, L, 2 * HEAD_DIM), lambda b, h, qi: (b, 0, h)),
        lambda rows: [pltpu.VMEM((rows, HEAD_DIM), BF16),
                      pltpu.VMEM((rows, tk), F32), pltpu.VMEM((rows, tk), F32),
                      pltpu.VMEM((rows, tk), BF16), pltpu.VMEM((rows, tk), BF16),
                      pltpu.VMEM((rows, LANES), F32), pltpu.VMEM((rows, LANES), F32),
                      pltpu.VMEM((rows, LANES), F32),
                      pltpu.VMEM((rows, 2 * HEAD_DIM), F32)])
    bounded = call(
        functools.partial(_flash_bounded_kernel, unroll=math.gcd(nk, 32)), "flash_attention_bounded", tq_bounded,
        pl.BlockSpec((None, None, VT_ROWS, L), lambda b, h, qi: (b, h, 0, 0)),
        lambda rows: [pltpu.VMEM((HEAD_DIM, rows), BF16), pltpu.VMEM((VT_ROWS, rows), F32)])
    return lax.cond(score_bound <= SAFE_SCORE_BOUND,
                    lambda ops: bounded(ops[0], ops[0], ops[2]),
                    lambda ops: general(ops[0], ops[0], ops[1]),
                    (qkv, v_aug, v_t))


def _pack_bf16_pairs(x):
    half = x.shape[1] // 2
    bits = lax.bitcast_convert_type(x.astype(BF16).astype(F32), jnp.uint32)
    return (bits[:, :half] >> 16) | (bits[:, half:] & jnp.uint32(0xFFFF0000))


def _unpack_bf16_pairs(p):
    lo = lax.bitcast_convert_type(p << 16, F32).astype(BF16)
    hi = lax.bitcast_convert_type(p & jnp.uint32(0xFFFF0000), F32).astype(BF16)
    return lo, hi


def _router_kernel(xa_ref, xb_ref, g_ref, rw_ref, h_ref, info_ref, cnt_ref, *, n_a_tiles):
    i = pl.program_id(0)

    def route(x_ref):
        hf = _norm_rows(x_ref[...], g_ref[...])
        h_ref[...] = _pack_bf16_pairs(hf)
        h_hi = hf.astype(BF16)
        h_lo = (hf - h_hi.astype(F32)).astype(BF16)
        rw = rw_ref[...]
        hw = _dot(h_hi, rw)
        logits = hw[:, :LANES] + hw[:, LANES:] + _dot(h_lo, rw[:, :LANES])
        shape = logits.shape
        lane = lax.broadcasted_iota(jnp.int32, shape, 1)
        lanef = lane.astype(F32)
        neg = -jnp.inf
        big = float(LANES)
        is_g = lane < N_GROUPS
        gl = jnp.where(is_g, logits, neg)
        gmax = jnp.max(gl, axis=-1, keepdims=True)
        gidx = jnp.min(jnp.where(gl == gmax, lanef, big), axis=-1, keepdims=True)
        p_g = 1.0 / jnp.sum(jnp.where(is_g, jnp.exp(logits - gmax), 0.0), axis=-1, keepdims=True)
        lo = N_GROUPS + gidx * EXPERTS_PER_GROUP
        el = jnp.where(lanef >= lo, jnp.where(lanef < lo + EXPERTS_PER_GROUP, logits, neg), neg)
        v1 = jnp.max(el, axis=-1, keepdims=True)
        i1 = jnp.min(jnp.where(el == v1, lanef, big), axis=-1, keepdims=True)
        el2 = jnp.where(lanef == i1, neg, el)
        v2 = jnp.max(el2, axis=-1, keepdims=True)
        i2 = jnp.min(jnp.where(el2 == v2, lanef, big), axis=-1, keepdims=True)
        t = jnp.exp(v2 - v1)
        w1 = p_g / (1.0 + t)
        w2 = p_g * t / (1.0 + t)
        e1 = i1 - N_GROUPS
        e2 = i2 - N_GROUPS
        oh1 = jnp.where(lanef == e1, 1.0, 0.0)
        oh2 = jnp.where(lanef == e2, 1.0, 0.0)
        oh = oh1 + oh2
        tm = shape[0]
        tri = jnp.where(lax.broadcasted_iota(jnp.int32, (tm, tm), 1) < lax.broadcasted_iota(jnp.int32, (tm, tm), 0),
                        1.0, 0.0).astype(BF16)
        before = _dot(tri, oh.astype(BF16)) + cnt_ref[...]
        r1 = jnp.sum(before * oh1, axis=-1, keepdims=True)
        r2 = jnp.sum(before * oh2, axis=-1, keepdims=True)
        cnt_ref[...] += jnp.sum(oh, axis=0, keepdims=True)
        vals = (e1, e2, w1, w2, r1, r2)
        info = jnp.zeros(shape, F32)
        for idx, val in enumerate(vals):
            info = jnp.where(lane == idx, val, info)
        info_ref[...] = info

    @pl.when(i == 0)
    def _():
        cnt_ref[...] = jnp.zeros_like(cnt_ref)

    @pl.when(i < n_a_tiles)
    def _():
        route(xa_ref)

    @pl.when(i >= n_a_tiles)
    def _():
        route(xb_ref)


def moe_route(xa, xb, g, router_group, router_expert, *, tm=512):
    Ta, D = xa.shape
    Tb = xb.shape[0]
    tm = _tile(math.gcd(Ta, Tb), tm)
    na, nb = Ta // tm, Tb // tm
    n_e = N_GROUPS * EXPERTS_PER_GROUP
    rw = jnp.concatenate([router_group, jnp.transpose(router_expert, (1, 0, 2)).reshape(D, n_e)], axis=1)
    rw = jnp.pad(rw, ((0, 0), (0, LANES - rw.shape[1])))
    rw_hi = rw.astype(BF16)
    rw = jnp.concatenate([rw_hi, (rw - rw_hi.astype(F32)).astype(BF16)], axis=1)
    kern = functools.partial(_router_kernel, n_a_tiles=na)
    return pl.pallas_call(
        kern,
        grid=(na + nb,),
        in_specs=[pl.BlockSpec((tm, D), lambda i: (jnp.minimum(i, na - 1), 0)),
                  pl.BlockSpec((tm, D), lambda i: (jnp.maximum(i - na, 0), 0)),
                  pl.BlockSpec((1, D), lambda i: (0, 0)),
                  pl.BlockSpec((D, 2 * LANES), lambda i: (0, 0))],
        out_specs=[pl.BlockSpec((tm, D // 2), lambda i: (i, 0)),
                   pl.BlockSpec((tm, LANES), lambda i: (i, 0)),
                   pl.BlockSpec((1, LANES), lambda i: (0, 0))],
        out_shape=[jax.ShapeDtypeStruct((Ta + Tb, D // 2), jnp.uint32),
                   jax.ShapeDtypeStruct((Ta + Tb, LANES), F32),
                   jax.ShapeDtypeStruct((1, LANES), F32)],
        compiler_params=_params(("arbitrary",)),
        name="moe_route",
    )(xa, xb, g, rw)


def moe_plan(info, cnt, tm, n_experts):
    T = info.shape[0]
    n_tiles = (2 * T) // tm + n_experts
    counts = cnt[0, :n_experts].astype(jnp.int32)
    padded = ((counts + tm - 1) // tm) * tm
    ends = jnp.cumsum(padded)
    off = ends - padded
    e = info[:, 0:2].T.astype(jnp.int32)
    rank = info[:, 4:6].T.astype(jnp.int32)
    k = jnp.arange(n_experts, dtype=jnp.int32)[:, None, None]
    pos = rank + jnp.sum(jnp.where(e[None] == k, off[:, None, None], 0), axis=0)
    pos = pos.T.reshape(-1)
    tile_start = jnp.arange(n_tiles, dtype=jnp.int32) * tm
    tile_e = jnp.sum((tile_start[:, None] >= ends[None, :]).astype(jnp.int32), axis=1)
    tile_e = jnp.minimum(tile_e, n_experts - 1)
    tile_valid = jnp.clip(counts[tile_e] - (tile_start - off[tile_e]), 0, tm).astype(jnp.int32)
    n_used = (ends[-1:] // tm).astype(jnp.int32)
    expert_last = jnp.where(padded > 0, ends - tm, -1)
    tail_tile = n_used[0] + jnp.arange(n_experts, dtype=jnp.int32)
    tail = jnp.where(tail_tile < n_tiles, tail_tile * tm, -1)
    zero_tiles = jnp.concatenate([expert_last, tail]).astype(jnp.int32)[None, :]
    return pos, tile_e, tile_valid, n_used, zero_tiles


DMA_UNROLL = 8


def _dispatch_kernel(pos_ref, ztile_ref, h_ref, hs_hbm, zbuf, sem, zsem):
    tm = h_ref.shape[0]

    @pl.when(pl.program_id(0) == 0)
    def _():
        zbuf[...] = jnp.zeros_like(zbuf)
        zrows = zbuf.shape[0]

        def zero_copy(k):
            return pltpu.make_async_copy(zbuf, hs_hbm.at[pl.ds(pl.multiple_of(ztile_ref[0, k], zrows), zrows)], zsem)

        for k in range(ztile_ref.shape[1]):
            @pl.when(ztile_ref[0, k] >= 0)
            def _():
                zero_copy(k).start()
        for k in range(ztile_ref.shape[1]):
            @pl.when(ztile_ref[0, k] >= 0)
            def _():
                zero_copy(k).wait()

    def row_copy(r, slot):
        return pltpu.make_async_copy(h_ref.at[pl.ds(r, 1)], hs_hbm.at[pl.ds(pos_ref[0, 0, 2 * r + slot], 1)], sem)

    def start(r, c):
        row_copy(r, 0).start()
        row_copy(r, 1).start()
        return c

    def wait(r, c):
        row_copy(r, 0).wait()
        row_copy(r, 1).wait()
        return c

    lax.fori_loop(0, tm, start, 0, unroll=DMA_UNROLL)
    lax.fori_loop(0, tm, wait, 0, unroll=DMA_UNROLL)


def moe_dispatch(h, pos, zero_tiles, n_sorted_rows, tile_rows, *, tm=512):
    T, C = h.shape
    tm = _tile(T, tm)
    return pl.pallas_call(
        _dispatch_kernel,
        grid=(T // tm,),
        in_specs=[pl.BlockSpec((1, 1, 2 * tm), lambda i: (i, 0, 0), memory_space=pltpu.SMEM),
                  pl.BlockSpec(zero_tiles.shape, lambda i: (0, 0), memory_space=pltpu.SMEM),
                  pl.BlockSpec((tm, C), lambda i: (i, 0))],
        out_specs=pl.BlockSpec(memory_space=pl.ANY),
        out_shape=jax.ShapeDtypeStruct((n_sorted_rows, C), h.dtype),
        scratch_shapes=[pltpu.VMEM((tile_rows, C), h.dtype), pltpu.SemaphoreType.DMA, pltpu.SemaphoreType.DMA],
        compiler_params=_params(("arbitrary",)),
        name="moe_dispatch",
    )(pos.reshape(T // tm, 1, 2 * tm), zero_tiles, h)


def _expert_kernel(te_ref, tv_ref, nu_ref, x_ref, wg_ref, wu_ref, wd_ref, y_ref):
    del te_ref, nu_ref
    valid = tv_ref[pl.program_id(0)]

    @pl.when(valid == 0)
    def _():
        y_ref[...] = jnp.zeros_like(y_ref)

    @pl.when(valid > 0)
    def _():
        x_lo, x_hi = _unpack_bf16_pairs(x_ref[...])
        half = x_ref.shape[1]
        gate = (_dot(x_lo, wg_ref[:half, :].astype(BF16)) + _dot(x_hi, wg_ref[half:, :].astype(BF16)))
        up = (_dot(x_lo, wu_ref[:half, :].astype(BF16)) + _dot(x_hi, wu_ref[half:, :].astype(BF16)))
        a = (gate * jax.nn.sigmoid(gate) * up).astype(BF16)
        y_ref[...] = _dot(a, wd_ref[...].astype(BF16))


def moe_experts(hs, tile_e, tile_valid, n_used, w_gate, w_up, w_down, layer, tm):
    P = hs.shape[0]
    D = w_gate.shape[2]
    d_e = w_gate.shape[-1]

    def row_map(i, te, tv, nu):
        return (jnp.minimum(i, nu[0] - 1), 0)

    def out_map(i, te, tv, nu):
        return (i, 0)

    def w_map(i, te, tv, nu):
        return (layer, te[i], 0, 0)

    grid_spec = pltpu.PrefetchScalarGridSpec(
        num_scalar_prefetch=3,
        grid=(P // tm,),
        in_specs=[pl.BlockSpec((tm, D // 2), row_map),
                  pl.BlockSpec((None, None, D, d_e), w_map),
                  pl.BlockSpec((None, None, D, d_e), w_map),
                  pl.BlockSpec((None, None, d_e, D), w_map)],
        out_specs=pl.BlockSpec((tm, D), out_map),
    )
    return pl.pallas_call(
        _expert_kernel,
        grid_spec=grid_spec,
        out_shape=jax.ShapeDtypeStruct((P, D), F32),
        compiler_params=_params(("arbitrary",)),
        name="moe_experts",
    )(tile_e, tile_valid, n_used, hs, w_gate, w_up, w_down)


def _combine_kernel(pos_ref, posn_ref, x_ref, info_ref, g_ref, y_hbm, o_ref, ybuf, sem, *, final_norm):
    tm = x_ref.shape[0]
    i = pl.program_id(0)
    buf = i % 2

    def row_copy(p_ref, b, r, slot):
        return pltpu.make_async_copy(y_hbm.at[pl.ds(p_ref[0, 0, 2 * r + slot], 1)],
                                     ybuf.at[b, slot, pl.ds(r, 1)], sem.at[b])

    def start_tile(p_ref, b):
        def body(r, c):
            row_copy(p_ref, b, r, 0).start()
            row_copy(p_ref, b, r, 1).start()
            return c
        lax.fori_loop(0, tm, body, 0, unroll=DMA_UNROLL)

    @pl.when(i == 0)
    def _():
        start_tile(pos_ref, 0)

    @pl.when(i + 1 < pl.num_programs(0))
    def _():
        start_tile(posn_ref, 1 - buf)

    def wait(r, c):
        row_copy(pos_ref, buf, r, 0).wait()
        row_copy(pos_ref, buf, r, 1).wait()
        return c

    lax.fori_loop(0, tm, wait, 0, unroll=DMA_UNROLL)
    info = info_ref[...]
    y = info[:, 2:3] * ybuf[buf, 0] + info[:, 3:4] * ybuf[buf, 1]
    x = x_ref[...] + y
    if final_norm:
        x = _norm_rows(x, g_ref[...])
    o_ref[...] = x


def moe_combine(x, y, pos, info, row_off, g_final, final_norm, *, tm=256):
    Tg, D = x.shape
    T = info.shape[0]
    tm = _tile(math.gcd(Tg, row_off) if row_off else Tg, tm)
    ob = row_off // tm
    kern = functools.partial(_combine_kernel, final_norm=final_norm)
    n = Tg // tm
    pos3 = pos.reshape(T // tm, 1, 2 * tm)
    return pl.pallas_call(
        kern,
        grid=(n,),
        in_specs=[pl.BlockSpec((1, 1, 2 * tm), lambda i: (i + ob, 0, 0), memory_space=pltpu.SMEM),
                  pl.BlockSpec((1, 1, 2 * tm), lambda i: (jnp.minimum(i + 1, n - 1) + ob, 0, 0),
                               memory_space=pltpu.SMEM),
                  pl.BlockSpec((tm, D), lambda i: (i, 0)),
                  pl.BlockSpec((tm, LANES), lambda i: (i + ob, 0)),
                  pl.BlockSpec((1, D), lambda i: (0, 0)),
                  pl.BlockSpec(memory_space=pl.ANY)],
        out_specs=pl.BlockSpec((tm, D), lambda i: (i, 0)),
        out_shape=jax.ShapeDtypeStruct((Tg, D), F32),
        scratch_shapes=[pltpu.VMEM((2, 2, tm, D), F32), pltpu.SemaphoreType.DMA((2,))],
        compiler_params=_params(("arbitrary",)),
        name="moe_combine",
    )(pos3, pos3, x, info, g_final, y)


MOE_TILE = 512


def hier_moe_block(xs, g, router_group, router_expert, w_gate, w_up, w_down, layer, g_final, final_norm):
    xa, xb = xs
    n_experts = w_gate.shape[1]
    h, info, cnt = moe_route(xa, xb, g, router_group, router_expert)
    T = h.shape[0]
    tm = _tile(2 * T, MOE_TILE)
    pos, tile_e, tile_valid, n_used, zero_tiles = moe_plan(info, cnt, tm, n_experts)
    hs = moe_dispatch(h, pos, zero_tiles, tile_e.shape[0] * tm, tm)
    y = moe_experts(hs, tile_e, tile_valid, n_used, w_gate, w_up, w_down, layer, tm)
    oa = moe_combine(xa, y, pos, info, 0, g_final, final_norm)
    ob = moe_combine(xb, y, pos, info, xa.shape[0], g_final, final_norm)
    return oa, ob


def hyena_block(x, L, g, w_in, conv_w, conv_b, f_w1, f_b1, f_win, f_bin, f_freq, f_w3, skip_b, w_out):
    D = x.shape[1]
    x0, zv = hyena_inproj(x, g, w_in, conv_w, conv_b[None, :], L)
    filt = hyena_filter_features(L, w_out.shape[0], f_w1, f_b1, f_win, f_bin, f_freq)
    return hyena_long_conv(zv, x0, x, w_out, filt, f_w3, skip_b[None, :], L)


def attention_block(x, B, L, g, w_qkv, q_gain, k_gain, w_o):
    D = x.shape[1]
    n_heads = w_o.shape[0] // HEAD_DIM
    qkv, v_aug, v_t = qkv_project(x, g, w_qkv, q_gain, k_gain, L)
    score_bound = 1.02 * HEAD_DIM * SCORE_SCALE * jnp.max(jnp.abs(q_gain)) * jnp.max(jnp.abs(k_gain))
    o = flash_attention(qkv.reshape(B, L, qkv.shape[1]), v_aug.reshape(B, L, v_aug.shape[1]), v_t, score_bound,
                        B, L, n_heads)
    return matmul_residual(o.reshape(B * L, n_heads * HEAD_DIM), w_o, x)


def kernel(x_prompt, x_sample, norm_mix, norm_ffn, norm_final, hy_w_in, hy_conv_w, hy_conv_b, hy_f_w1, hy_f_b1, hy_f_win, hy_f_bin, hy_f_freq, hy_f_w3, hy_skip_b, hy_w_out, at_w_qkv, at_q_gain, at_k_gain, at_w_o, moe_router_group, moe_router_expert, moe_w_gate, moe_w_up, moe_w_down):
    D = x_prompt.shape[-1]
    depth = norm_mix.shape[0]
    shapes = [x_prompt.shape, x_sample.shape]
    for s in shapes:
        assert s[0] == 2, "the long convolution packs exactly two sequences per group"
    xs = [x_prompt.reshape(-1, D), x_sample.reshape(-1, D)]
    hy_w_in_b = hy_w_in.astype(BF16)
    hy_w_out_b = hy_w_out.astype(BF16)
    at_w_qkv_b = at_w_qkv.astype(BF16)
    at_w_o_b = at_w_o.astype(BF16)
    g_final = norm_final[None, :]
    for i in range(depth):
        j = i // N_MIXERS
        g = norm_mix[i][None, :]
        if i % N_MIXERS == 0:
            xs = [hyena_block(x, s[1], g, hy_w_in_b[j], hy_conv_w[j], hy_conv_b[j], hy_f_w1[j], hy_f_b1[j],
                              hy_f_win[j], hy_f_bin[j], hy_f_freq[j], hy_f_w3[j], hy_skip_b[j], hy_w_out_b[j])
                  for x, s in zip(xs, shapes)]
        else:
            xs = [attention_block(x, s[0], s[1], g, at_w_qkv_b[j], at_q_gain[j], at_k_gain[j], at_w_o_b[j])
                  for x, s in zip(xs, shapes)]
        xs = hier_moe_block(xs, norm_ffn[i][None, :], moe_router_group[i], moe_router_expert[i],
                            moe_w_gate, moe_w_up, moe_w_down, i, g_final, i == depth - 1)
    return (xs[0].reshape(shapes[0]), xs[1].reshape(shapes[1]))
```
